```python
import jax, jax.numpy as jnp
from jax import lax
import numpy as np

D_MODEL = 1024
BATCH = 1
SEQ = 16384
DEPTH = 2
DEC_BATCH = 128
DEC_SEQ = 1
PAST_LEN = 16384
PAGE_SIZE = 128

A_HEADS = 8
A_KV_HEADS = 2
HEAD_DIM = 64
WINDOW = 128
ROPE_THETA = 10000.0
A_WIDTH = A_HEADS * HEAD_DIM
A_KV_WIDTH = A_KV_HEADS * HEAD_DIM
A_COLS = A_WIDTH + 2 * A_KV_WIDTH
R_HEADS = 4
R_DK = 64
R_DV = 64
R_CHUNK = 128
R_WIDTH = R_HEADS * R_DV
R_COLS = 2 * R_HEADS * R_DK + 2 * R_WIDTH
C_HEADS = 4
C_HEAD = 64
C_WIDTH = C_HEADS * C_HEAD
C_DECAY_LORA = 64
C_ICLR_LORA = 64
C_GATE_LORA = 160
C_COLS = 3 * C_WIDTH + C_DECAY_LORA + C_ICLR_LORA + C_GATE_LORA
C_GN_EPS = 64e-5
IN_COLS = A_COLS + R_COLS + C_COLS
MIX_WIDTH = A_WIDTH + R_WIDTH + C_WIDTH
N_GROUPS = 4
EXPERTS_PER_GROUP = 8
N_EXPERTS = N_GROUPS * EXPERTS_PER_GROUP
TOP_K = 2
EXPERT_FF = 512
MOE_BLOCK = 128
EPS = 1e-6

kernel_name = "hymba_swa_retnet_rwkv7_hmoe_step"


def rmsnorm(x, g):
    xf = x.astype(jnp.float32)
    y = xf * lax.rsqrt(jnp.mean(xf * xf, -1, keepdims=True) + EPS)
    return (y * g.astype(jnp.float32)).astype(x.dtype)


def head_rms(x):
    return x * lax.rsqrt(jnp.mean(x * x, -1, keepdims=True) + EPS)


def rope(x, pos):
    half = x.shape[-1] // 2
    inv = ROPE_THETA ** (-jnp.arange(half, dtype=jnp.float32) / half)
    ang = pos.astype(jnp.float32)[:, None] * inv[None, :]
    cos, sin = jnp.cos(ang)[:, None, :], jnp.sin(ang)[:, None, :]
    xf = x.astype(jnp.float32)
    x1, x2 = xf[..., :half], xf[..., half:]
    return jnp.concatenate([x1 * cos - x2 * sin, x2 * cos + x1 * sin], -1).astype(x.dtype)


def sink_attend(q, k, v, q_pos, k_pos, sinks):
    g = A_HEADS // A_KV_HEADS
    qg = q.reshape(q.shape[:-2] + (A_KV_HEADS, g, HEAD_DIM))
    s = jnp.einsum("...qkgd,...skd->...kgqs", qg, k, preferred_element_type=jnp.float32) * (HEAD_DIM ** -0.5)
    dist = q_pos[..., :, None] - k_pos[..., None, :]
    ok = (dist >= 0) & (dist <= WINDOW) & (k_pos[..., None, :] >= 0)
    s = jnp.where(ok[..., None, None, :, :], s, -jnp.inf)
    sink = sinks.astype(jnp.float32).reshape(A_KV_HEADS, g)[:, :, None, None]
    m = jnp.maximum(jnp.max(s, -1, keepdims=True), sink)
    p = jnp.exp(s - m)
    p = p / (jnp.sum(p, -1, keepdims=True) + jnp.exp(sink - m))
    o = jnp.einsum("...kgqs,...skd->...qkgd", p.astype(v.dtype), v)
    return o.reshape(o.shape[:-3] + (A_WIDTH,))


def swa_prompt(q, k, v, pos, sinks):
    b, l = q.shape[:2]
    nb = l // WINDOW
    qb = q.reshape(b, nb, WINDOW, A_HEADS, HEAD_DIM)
    kb = k.reshape(b, nb, WINDOW, A_KV_HEADS, HEAD_DIM)
    vb = v.reshape(b, nb, WINDOW, A_KV_HEADS, HEAD_DIM)
    pad = ((0, 0), (1, 0), (0, 0), (0, 0), (0, 0))
    kk = jnp.concatenate([jnp.pad(kb, pad)[:, :-1], kb], 2)
    vv = jnp.concatenate([jnp.pad(vb, pad)[:, :-1], vb], 2)
    pb = pos.reshape(nb, WINDOW)
    kpos = jnp.concatenate([pb - WINDOW, pb], 1)
    return sink_attend(qb, kk, vv, pb, kpos, sinks).reshape(b, l, A_WIDTH)


def swa_sample(q, k_new, v_new, k_buf, v_buf, pos, sinks):
    wb = k_buf.shape[1]
    kk = jnp.concatenate([k_buf.astype(k_new.dtype), k_new], 1)
    vv = jnp.concatenate([v_buf.astype(v_new.dtype), v_new], 1)
    kpos = jnp.concatenate([pos[0] - wb + jnp.arange(wb, dtype=jnp.int32), pos])
    o = sink_attend(q, kk, vv, pos, kpos, sinks)
    return o, kk[:, -wb:], vv[:, -wb:]


def retention(q, k, v, s0):
    b, l = q.shape[:2]
    c = R_CHUNK if l % R_CHUNK == 0 else l
    nc = l // c
    f32 = jnp.float32
    log_g = jnp.log1p(-jnp.exp2(-5.0 - jnp.arange(R_HEADS, dtype=f32)))
    idx = jnp.arange(c, dtype=f32)
    diff = idx[:, None] - idx[None, :]
    intra = jnp.where(diff >= 0, jnp.exp(log_g[:, None, None] * jnp.maximum(diff, 0.0)), 0.0)
    q_dec = jnp.exp(log_g[None, :] * (idx[:, None] + 1.0))
    k_dec = jnp.exp(log_g[None, :] * (c - 1.0 - idx[:, None]))
    c_dec = jnp.exp(log_g * c)

    def to_chunks(t):
        return jnp.moveaxis(t.astype(f32).reshape(b, nc, c, R_HEADS, t.shape[-1]), 1, 0)

    def step(s, xs):
        qc, kc, vc = xs
        att = jnp.einsum("bqhd,bshd->bhqs", qc, kc) * intra
        o = jnp.einsum("bhqs,bshe->bqhe", att, vc) + jnp.einsum("bqhd,bhde->bqhe", qc, s) * q_dec[None, :, :, None]
        s = s * c_dec[None, :, None, None] + jnp.einsum("bshd,bshe->bhde", kc * k_dec[None, :, :, None], vc)
        return s, o

    s, o = lax.scan(step, s0.astype(f32), (to_chunks(q), to_chunks(k), to_chunks(v)))
    return jnp.moveaxis(o, 0, 1).reshape(b, l, R_HEADS, R_DV), s


def rwkv_scan(r, w, k, v, a, bb, s0):
    def step(s, xs):
        rt, wt, kt, vt, at, bt = xs
        sa = jnp.einsum("bhij,bhj->bhi", s, at)
        s = s * wt[:, :, None, :] + sa[..., None] * bt[:, :, None, :] + vt[..., None] * kt[:, :, None, :]
        return s, jnp.einsum("bhij,bhj->bhi", s, rt)

    xs = tuple(jnp.moveaxis(t, 1, 0) for t in (r, w, k, v, a, bb))
    s, y = lax.scan(step, s0, xs)
    return jnp.moveaxis(y, 0, 1), s


def rwkv_mix(zc, shift_prev, s_wkv, lp):
    b, l, _ = zc.shape
    f32 = jnp.float32
    prev = jnp.concatenate([shift_prev[:, None, :].astype(zc.dtype), zc[:, :-1]], 1)
    zs = zc + lp["rwkv_mu"] * (prev - zc)
    o1, o2, o3 = C_WIDTH, 2 * C_WIDTH, 3 * C_WIDTH
    o4, o5 = o3 + C_DECAY_LORA, o3 + C_DECAY_LORA + C_ICLR_LORA
    r, k, v = zs[..., :o1], zs[..., o1:o2], zs[..., o2:o3]
    wl, al, gl = zs[..., o3:o4], zs[..., o4:o5], zs[..., o5:]
    w = -jax.nn.softplus(-(lp["rwkv_w0"] + jnp.tanh(wl) @ lp["rwkv_w2"]).astype(f32)) - 0.5
    decay = jnp.exp(-jnp.exp(w))
    a = jax.nn.sigmoid((lp["rwkv_a0"] + al @ lp["rwkv_a2"]).astype(f32))
    g = jax.nn.sigmoid(gl) @ lp["rwkv_g2"]

    def hs(t):
        return t.astype(f32).reshape(b, l, C_HEADS, C_HEAD)

    kk = hs(k * lp["rwkv_k_k"])
    kk = kk / jnp.maximum(jnp.sqrt(jnp.sum(kk * kk, -1, keepdims=True)), 1e-12)
    k_mod = hs(k.astype(f32) * (1.0 + (a - 1.0) * lp["rwkv_k_a"].astype(f32)))
    rh, vh = hs(r), hs(v)
    y, s_new = rwkv_scan(rh, hs(decay), k_mod, vh, -kk, kk * hs(a), s_wkv.astype(f32))
    mu = jnp.mean(y, -1, keepdims=True)
    var = jnp.mean(jnp.square(y - mu), -1, keepdims=True)
    y = ((y - mu) * lax.rsqrt(var + C_GN_EPS)).reshape(b, l, C_WIDTH)
    y = y * lp["rwkv_ln_w"].astype(f32) + lp["rwkv_ln_b"].astype(f32)
    bonus = jnp.sum(rh * k_mod * lp["rwkv_r_k"].astype(f32), -1, keepdims=True) * vh
    y = y + bonus.reshape(b, l, C_WIDTH)
    return y.astype(zc.dtype) * g, s_new, zc[:, -1]


def routed_experts(hf, eidx, gate, w_gate, w_up, w_down):
    t, d = hf.shape
    na = t * TOP_K
    flat_e = eidx.reshape(na)
    flat_t = jnp.arange(na, dtype=jnp.int32) // TOP_K
    flat_w = gate.reshape(na)
    order = jnp.argsort(flat_e)
    se = flat_e[order]
    counts = jnp.bincount(flat_e, length=N_EXPERTS)
    padded = (counts + MOE_BLOCK - 1) // MOE_BLOCK * MOE_BLOCK
    start = jnp.cumsum(counts) - counts
    pend = jnp.cumsum(padded)
    pstart = pend - padded
    dest = pstart[se] + jnp.arange(na, dtype=jnp.int32) - start[se]
    n_blocks = -(-na // MOE_BLOCK) + N_EXPERTS
    n_rows = n_blocks * MOE_BLOCK
    row_tok = jnp.full((n_rows,), t, jnp.int32).at[dest].set(flat_t[order])
    row_w = jnp.zeros((n_rows,), jnp.float32).at[dest].set(flat_w[order])
    blk_exp = jnp.minimum(jnp.searchsorted(pend, jnp.arange(n_blocks) * MOE_BLOCK, side="right"), N_EXPERTS - 1)
    xp = jnp.concatenate([hf, jnp.zeros((1, d), hf.dtype)], 0)[row_tok].reshape(n_blocks, MOE_BLOCK, d)

    def expert_block(args):
        xb, e = args
        hid = jax.nn.silu(xb @ w_gate[e]) * (xb @ w_up[e])
        return hid @ w_down[e]

    yb = lax.map(expert_block, (xp, blk_exp)).reshape(n_rows, d)
    y = jax.ops.segment_sum(yb * row_w[:, None].astype(yb.dtype), row_tok, num_segments=t + 1)
    return y[:t]


def hier_moe(h, lp):
    b, l, d = h.shape
    hf = h.reshape(b * l, d)
    lg = (hf @ lp["router_g"]).astype(jnp.float32) + lp["router_g_b"].astype(jnp.float32)
    grp = jnp.argmax(lg, -1)
    p_grp = jnp.take_along_axis(jax.nn.softmax(lg, -1), grp[:, None], -1)
    le = (hf @ lp["router_e"]).astype(jnp.float32) + lp["router_e_b"].astype(jnp.float32)
    le = jnp.take_along_axis(le.reshape(-1, N_GROUPS, EXPERTS_PER_GROUP), grp[:, None, None], 1)[:, 0]
    top_l, top_i = lax.top_k(le, TOP_K)
    gate = jax.nn.softmax(top_l, -1) * p_grp
    eidx = (grp[:, None] * EXPERTS_PER_GROUP + top_i).astype(jnp.int32)
    y = routed_experts(hf, eidx, gate, lp["w_gate"], lp["w_up"], lp["w_down"])
    return y.reshape(b, l, d).astype(h.dtype)


def layer(x, pos, lp, k_buf, v_buf, s_ret, s_wkv, s_shift):
    b, l, _ = x.shape
    dt = x.dtype
    h = rmsnorm(x, lp["norm_mix"])
    z = jnp.einsum("bld,de->ble", h, lp["w_in"])
    za, zr, zc = z[..., :A_COLS], z[..., A_COLS:A_COLS + R_COLS], z[..., A_COLS + R_COLS:]
    q = rope(za[..., :A_WIDTH].reshape(b, l, A_HEADS, HEAD_DIM), pos)
    k = rope(za[..., A_WIDTH:A_WIDTH + A_KV_WIDTH].reshape(b, l, A_KV_HEADS, HEAD_DIM), pos)
    v = za[..., A_WIDTH + A_KV_WIDTH:].reshape(b, l, A_KV_HEADS, HEAD_DIM)
    if k_buf is None:
        o_a = swa_prompt(q, k, v, pos, lp["sinks"])
        nkeep = min(WINDOW, l)
        new_k, new_v = k[:, l - nkeep:], v[:, l - nkeep:]
    else:
        o_a, new_k, new_v = swa_sample(q, k, v, k_buf, v_buf, pos, lp["sinks"])
    qk = R_HEADS * R_DK
    rq = rope(zr[..., :qk].reshape(b, l, R_HEADS, R_DK), pos)
    rk = rope(zr[..., qk:2 * qk].reshape(b, l, R_HEADS, R_DK), pos) * (R_DK ** -0.5)
    rv = zr[..., 2 * qk:2 * qk + R_WIDTH].reshape(b, l, R_HEADS, R_DV)
    rg = zr[..., 2 * qk + R_WIDTH:]
    o_r, new_ret = retention(rq, rk, rv, s_ret)
    o_r = head_rms(o_r).reshape(b, l, R_WIDTH).astype(dt) * jax.nn.silu(rg)
    o_c, new_wkv, new_shift = rwkv_mix(zc, s_shift, s_wkv, lp)
    mix = jnp.concatenate([o_a.astype(dt), o_r, o_c.astype(dt)], -1)
    x = x + jnp.einsum("ble,ed->bld", mix, lp["w_out"])
    x = x + hier_moe(rmsnorm(x, lp["norm_ffn"]), lp)
    return x, new_k, new_v, new_ret, new_wkv, new_shift


def setup_inputs(seed: int = 0) -> dict:
    key = jax.random.key(seed)
    ks = iter(jax.random.split(key, 40))
    f32 = jnp.float32

    def nrm(shape, scale):
        return jax.random.normal(next(ks), shape, f32) * scale

    wb = min(WINDOW, PAST_LEN)
    return {
        "x_prompt": nrm((BATCH, SEQ, D_MODEL), 1.0),
        "x_sample": nrm((DEC_BATCH, DEC_SEQ, D_MODEL), 1.0),
        "cache_swa_k": nrm((DEPTH, DEC_BATCH, wb, A_KV_HEADS, HEAD_DIM), 1.0),
        "cache_swa_v": nrm((DEPTH, DEC_BATCH, wb, A_KV_HEADS, HEAD_DIM), 1.0),
        "state_ret": nrm((DEPTH, DEC_BATCH, R_HEADS, R_DK, R_DV), 0.5),
        "state_wkv": nrm((DEPTH, DEC_BATCH, C_HEADS, C_HEAD, C_HEAD), 0.3),
        "state_shift": nrm((DEPTH, DEC_BATCH, C_COLS), 1.0),
        "norm_mix": 1.0 + nrm((DEPTH, D_MODEL), 0.02),
        "w_in": nrm((DEPTH, D_MODEL, IN_COLS), D_MODEL ** -0.5),
        "sinks": nrm((DEPTH, A_HEADS), 0.5),
        "rwkv_mu": jax.random.uniform(next(ks), (DEPTH, C_COLS), f32),
        "rwkv_w0": jax.random.uniform(next(ks), (DEPTH, C_WIDTH), f32, -2.0, 1.0),
        "rwkv_w2": nrm((DEPTH, C_DECAY_LORA, C_WIDTH), 0.1),
        "rwkv_a0": nrm((DEPTH, C_WIDTH), 0.1),
        "rwkv_a2": nrm((DEPTH, C_ICLR_LORA, C_WIDTH), 0.1),
        "rwkv_g2": nrm((DEPTH, C_GATE_LORA, C_WIDTH), C_GATE_LORA ** -0.5),
        "rwkv_k_k": 0.85 + nrm((DEPTH, C_WIDTH), 0.05),
        "rwkv_k_a": 1.0 + nrm((DEPTH, C_WIDTH), 0.05),
        "rwkv_r_k": nrm((DEPTH, C_HEADS, C_HEAD), 0.1),
        "rwkv_ln_w": 1.0 + nrm((DEPTH, C_WIDTH), 0.02),
        "rwkv_ln_b": nrm((DEPTH, C_WIDTH), 0.02),
        "w_out": nrm((DEPTH, MIX_WIDTH, D_MODEL), MIX_WIDTH ** -0.5),
        "norm_ffn": 1.0 + nrm((DEPTH, D_MODEL), 0.02),
        "router_g": nrm((DEPTH, D_MODEL, N_GROUPS), D_MODEL ** -0.5),
        "router_g_b": nrm((DEPTH, N_GROUPS), 0.01),
        "router_e": nrm((DEPTH, D_MODEL, N_EXPERTS), D_MODEL ** -0.5),
        "router_e_b": nrm((DEPTH, N_EXPERTS), 0.01),
        "expert_w_gate": nrm((DEPTH, N_EXPERTS, D_MODEL, EXPERT_FF), D_MODEL ** -0.5),
        "expert_w_up": nrm((DEPTH, N_EXPERTS, D_MODEL, EXPERT_FF), D_MODEL ** -0.5),
        "expert_w_down": nrm((DEPTH, N_EXPERTS, EXPERT_FF, D_MODEL), EXPERT_FF ** -0.5),
        "norm_final": 1.0 + nrm((D_MODEL,), 0.02),
    }


def reference(x_prompt, x_sample, cache_swa_k, cache_swa_v, state_ret, state_wkv, state_shift,
              norm_mix, w_in, sinks, rwkv_mu, rwkv_w0, rwkv_w2, rwkv_a0, rwkv_a2, rwkv_g2,
              rwkv_k_k, rwkv_k_a, rwkv_r_k, rwkv_ln_w, rwkv_ln_b, w_out, norm_ffn,
              router_g, router_g_b, router_e, router_e_b, expert_w_gate, expert_w_up, expert_w_down,
              norm_final):
    bp, lpr = x_prompt.shape[:2]
    pos_p = jnp.arange(lpr, dtype=jnp.int32)
    pos_s = PAST_LEN + jnp.arange(x_sample.shape[1], dtype=jnp.int32)
    xp, xs = x_prompt, x_sample
    outs_p = [[], [], [], [], []]
    outs_s = [[], [], [], [], []]
    for i in range(DEPTH):
        lp = {
            "norm_mix": norm_mix[i], "w_in": w_in[i], "sinks": sinks[i],
            "rwkv_mu": rwkv_mu[i], "rwkv_w0": rwkv_w0[i], "rwkv_w2": rwkv_w2[i],
            "rwkv_a0": rwkv_a0[i], "rwkv_a2": rwkv_a2[i], "rwkv_g2": rwkv_g2[i],
            "rwkv_k_k": rwkv_k_k[i], "rwkv_k_a": rwkv_k_a[i], "rwkv_r_k": rwkv_r_k[i],
            "rwkv_ln_w": rwkv_ln_w[i], "rwkv_ln_b": rwkv_ln_b[i], "w_out": w_out[i],
            "norm_ffn": norm_ffn[i], "router_g": router_g[i], "router_g_b": router_g_b[i],
            "router_e": router_e[i], "router_e_b": router_e_b[i],
            "w_gate": expert_w_gate[i], "w_up": expert_w_up[i], "w_down": expert_w_down[i],
        }
        zr = jnp.zeros((bp, R_HEADS, R_DK, R_DV), jnp.float32)
        zw = jnp.zeros((bp, C_HEADS, C_HEAD, C_HEAD), jnp.float32)
        zsft = jnp.zeros((bp, C_COLS), xp.dtype)
        xp, *new_p = layer(xp, pos_p, lp, None, None, zr, zw, zsft)
        xs, *new_s = layer(xs, pos_s, lp, cache_swa_k[i], cache_swa_v[i], state_ret[i], state_wkv[i], state_shift[i])
        for j in range(5):
            outs_p[j].append(new_p[j])
            outs_s[j].append(new_s[j])
    dts = (cache_swa_k.dtype, cache_swa_v.dtype, state_ret.dtype, state_wkv.dtype, state_shift.dtype)
    sp = [jnp.stack(outs_p[j]).astype(dts[j]) for j in range(5)]
    ss = [jnp.stack(outs_s[j]).astype(dts[j]) for j in range(5)]
    y_prompt = rmsnorm(xp, norm_final)
    y_sample = rmsnorm(xs, norm_final)
    return (y_prompt, y_sample, sp[0], sp[1], sp[2], sp[3], sp[4], ss[0], ss[1], ss[2], ss[3], ss[4])
```

```python
import functools

import jax
import jax.numpy as jnp
from jax import lax
from jax.experimental import pallas as pl
from jax.experimental.pallas import tpu as pltpu

F32 = jnp.float32
BF16 = jnp.bfloat16

D_MODEL = 1024
DEPTH = 2
PAST_LEN = 16384
A_HEADS, A_KV_HEADS, HEAD_DIM, WINDOW = 8, 2, 64, 128
ROPE_THETA = 10000.0
A_WIDTH = A_HEADS * HEAD_DIM
A_KV_WIDTH = A_KV_HEADS * HEAD_DIM
A_COLS = A_WIDTH + 2 * A_KV_WIDTH
R_HEADS, R_DK, R_DV, R_CHUNK = 4, 64, 64, 128
R_WIDTH = R_HEADS * R_DV
R_COLS = 2 * R_HEADS * R_DK + 2 * R_WIDTH
C_HEADS, C_HEAD = 4, 64
C_WIDTH = C_HEADS * C_HEAD
C_DECAY_LORA, C_ICLR_LORA, C_GATE_LORA = 64, 64, 160
C_COLS = 3 * C_WIDTH + C_DECAY_LORA + C_ICLR_LORA + C_GATE_LORA
C_PAD = 1152
C_GN_EPS = 64e-5
N_GROUPS, EXPERTS_PER_GROUP, TOP_K, EXPERT_FF = 4, 8, 2, 512
N_EXPERTS = N_GROUPS * EXPERTS_PER_GROUP
EPS = 1e-6

LANES = 128
SUBLANES = 8
ROW_TILES = D_MODEL // LANES
MOE_ROWS = 128
VMEM_LIMIT = 56 * 1024 * 1024


def _cp(sem, vmem=VMEM_LIMIT):
    return pltpu.CompilerParams(dimension_semantics=sem, vmem_limit_bytes=vmem)


def _full(shape):
    n = len(shape)
    return pl.BlockSpec(shape, lambda *_: (0,) * n)


def _rmsnorm(x, g):
    return x * lax.rsqrt(jnp.mean(x * x, -1, keepdims=True) + EPS) * g


def _silu(x):
    return x * (1.0 / (1.0 + jnp.exp(-x)))


def _sigmoid(x):
    return 1.0 / (1.0 + jnp.exp(-x))


def _rope(x, cos, sin_signed):
    w = x.shape[-1]
    lane = lax.broadcasted_iota(jnp.int32, x.shape, x.ndim - 1)
    first = (lane % HEAD_DIM) < (HEAD_DIM // 2)
    swapped = jnp.where(first, pltpu.roll(x, w - HEAD_DIM // 2, axis=x.ndim - 1),
                        pltpu.roll(x, HEAD_DIM // 2, axis=x.ndim - 1))
    return x * cos + swapped * sin_signed


def _tile_lanes(x, n):
    return jnp.concatenate([x] * n, axis=-1) if n > 1 else x


def _seg_sum(x, seg=64):
    outs = []
    for h in range(x.shape[-1] // seg):
        s = jnp.sum(x[:, h * seg:(h + 1) * seg], axis=-1, keepdims=True)
        outs.append(jnp.broadcast_to(s, (x.shape[0], seg)))
    return jnp.concatenate(outs, axis=-1)


def _read_rows(ref, tm, first, stride):
    return jnp.concatenate([ref[pl.ds(first + s, tm, stride=stride), :] for s in range(ROW_TILES)], axis=1)


def _colbcast(x):
    z = jnp.concatenate([jnp.broadcast_to(x[:, :LANES], (64, LANES)),
                         jnp.broadcast_to(x[:, LANES:], (64, LANES))], axis=0)
    return z.T


def _rowpair(x):
    top = jnp.concatenate([x[:, 0:64], x[:, 128:192]], axis=1)
    bot = jnp.concatenate([x[:, 64:128], x[:, 192:256]], axis=1)
    return top, bot


def _unpair(top, bot):
    return jnp.concatenate([top[:, :64], bot[:, :64], top[:, 64:], bot[:, 64:]], axis=1)


def _rowbcast(top, bot):
    return jnp.concatenate([jnp.broadcast_to(top, (64, LANES)), jnp.broadcast_to(bot, (64, LANES))], axis=0)


def _halfsums(p):
    return jnp.sum(p[:64], axis=0, keepdims=True), jnp.sum(p[64:], axis=0, keepdims=True)


def _wkv_step(st, ab, wb, bb, kb, rb, vtop, vbot):
    sa_t, sa_b = _halfsums(st * ab)
    st = st * wb + _rowbcast(sa_t, sa_b) * bb + _rowbcast(vtop, vbot) * kb
    y_t, y_b = _halfsums(st * rb)
    return st, y_t, y_b


def _wkv_post(y, lnw, lnb, bonus, g):
    mu = _seg_sum(y) * (1.0 / C_HEAD)
    d = y - mu
    var = _seg_sum(d * d) * (1.0 / C_HEAD)
    yn = d * lax.rsqrt(var + C_GN_EPS)
    return (yn * lnw + lnb + bonus) * g


def _heads_to_t(s4, transpose):
    if transpose:
        m = jnp.concatenate([jnp.concatenate([s4[0], s4[1]], axis=1),
                             jnp.concatenate([s4[2], s4[3]], axis=1)], axis=0)
        return m.T
    return jnp.concatenate([jnp.concatenate([s4[0], s4[2]], axis=1),
                            jnp.concatenate([s4[1], s4[3]], axis=1)], axis=0)


def _t_to_heads(st, transpose):
    if transpose:
        m = st.T
        return [m[0:64, 0:64], m[0:64, 64:128], m[64:128, 0:64], m[64:128, 64:128]]
    return [st[0:64, 0:64], st[64:128, 0:64], st[0:64, 64:128], st[64:128, 64:128]]


def _inproj_kernel(*refs, combine, tm):
    if combine:
        x_ref, yw_ref, route_ref, g_ref, wa_ref, wr_ref, wc_ref, xo_ref, za_ref, zr_ref, zc_ref = refs
        route = route_ref[...]
        y0 = _read_rows(yw_ref, tm, 0, 2 * ROW_TILES)
        y1 = _read_rows(yw_ref, tm, ROW_TILES, 2 * ROW_TILES)
        x = x_ref[...] + (route[:, 2:3] * y0 + route[:, 3:4] * y1)
        xo_ref[...] = x
    else:
        x_ref, g_ref, wa_ref, wr_ref, wc_ref, za_ref, zr_ref, zc_ref = refs
        x = x_ref[...]
    h = _rmsnorm(x, g_ref[...]).astype(BF16)
    za_ref[...] = jnp.dot(h, wa_ref[...], preferred_element_type=F32)
    zr_ref[...] = jnp.dot(h, wr_ref[...], preferred_element_type=F32)
    zc_ref[...] = jnp.dot(h, wc_ref[...], preferred_element_type=F32)


def _inproj(x, moe, g, wa, wr, wc, tm):
    t = x.shape[0]
    row = lambda w: pl.BlockSpec((tm, w), lambda i: (i, 0))
    wspecs = [_full((1, D_MODEL)), _full(wa.shape), _full(wr.shape), _full(wc.shape)]
    zshapes = [jax.ShapeDtypeStruct((t, A_COLS), F32), jax.ShapeDtypeStruct((t, R_COLS), F32),
               jax.ShapeDtypeStruct((t, C_PAD), F32)]
    zspecs = [row(A_COLS), row(R_COLS), row(C_PAD)]
    if moe is None:
        return (None,) + tuple(pl.pallas_call(
            functools.partial(_inproj_kernel, combine=False, tm=tm),
            grid=(t // tm,), in_specs=[row(D_MODEL)] + wspecs, out_specs=zspecs, out_shape=zshapes,
            compiler_params=_cp(("parallel",)))(x, g, wa, wr, wc))
    yw, route = moe
    return pl.pallas_call(
        functools.partial(_inproj_kernel, combine=True, tm=tm),
        grid=(t // tm,),
        in_specs=[row(D_MODEL), pl.BlockSpec((tm * 2 * ROW_TILES, LANES), lambda i: (i, 0)), row(LANES)] + wspecs,
        out_specs=[row(D_MODEL)] + zspecs,
        out_shape=[jax.ShapeDtypeStruct((t, D_MODEL), F32)] + zshapes,
        compiler_params=_cp(("parallel",)))(x, yw, route, g, wa, wr, wc)


def _final_kernel(x_ref, yw_ref, route_ref, g_ref, o_ref, *, tm):
    route = route_ref[...]
    y0 = _read_rows(yw_ref, tm, 0, 2 * ROW_TILES)
    y1 = _read_rows(yw_ref, tm, ROW_TILES, 2 * ROW_TILES)
    x = x_ref[...] + (route[:, 2:3] * y0 + route[:, 3:4] * y1)
    o_ref[...] = _rmsnorm(x, g_ref[...])


def _final(x, yw, route, g, tm):
    t = x.shape[0]
    row = lambda w: pl.BlockSpec((tm, w), lambda i: (i, 0))
    return pl.pallas_call(
        functools.partial(_final_kernel, tm=tm), grid=(t // tm,),
        in_specs=[row(D_MODEL), pl.BlockSpec((tm * 2 * ROW_TILES, LANES), lambda i: (i, 0)), row(LANES),
                  _full((1, D_MODEL))],
        out_specs=row(D_MODEL), out_shape=jax.ShapeDtypeStruct((t, D_MODEL), F32),
        compiler_params=_cp(("parallel",)))(x, yw, route, g)


def _swa_prompt_kernel(sinks_ref, q_ref, kc_ref, kp_ref, vc_ref, vp_ref, cosc_ref, sinc_ref, cosp_ref, sinp_ref,
                       o_ref, ko_ref):
    i = pl.program_id(0)
    cosc, sinc = cosc_ref[...], sinc_ref[...]
    q = _rope(q_ref[...], _tile_lanes(cosc, 4), _tile_lanes(sinc, 4))
    kc = _rope(kc_ref[...], cosc, sinc)
    kp = _rope(kp_ref[...], cosp_ref[...], sinp_ref[...])
    ko_ref[...] = kc
    vc, vp = vc_ref[...], vp_ref[...]
    g = A_HEADS // A_KV_HEADS
    nq = g * WINDOW
    r = lax.broadcasted_iota(jnp.int32, (nq, 2 * WINDOW), 0) % WINDOW
    c = lax.broadcasted_iota(jnp.int32, (nq, 2 * WINDOW), 1)
    ok = (c >= r) & (c <= r + WINDOW) & ((c >= WINDOW) | (i > 0))
    heads = []
    for kv in range(A_KV_HEADS):
        lo = kv * HEAD_DIM
        q4 = jnp.concatenate([q[:, (kv * g + j) * HEAD_DIM:(kv * g + j + 1) * HEAD_DIM] for j in range(g)], axis=0)
        kk = jnp.concatenate([kp[:, lo:lo + HEAD_DIM], kc[:, lo:lo + HEAD_DIM]], axis=0)
        vv = jnp.concatenate([vp[:, lo:lo + HEAD_DIM], vc[:, lo:lo + HEAD_DIM]], axis=0)
        s = lax.dot_general(q4, kk, (((1,), (1,)), ((), ())), preferred_element_type=F32) * (HEAD_DIM ** -0.5)
        s = jnp.where(ok, s, -jnp.inf)
        sink = jnp.concatenate([jnp.full((WINDOW, 1), sinks_ref[kv * g + j], F32) for j in range(g)], axis=0)
        m = jnp.maximum(jnp.max(s, -1, keepdims=True), sink)
        p = jnp.exp(s - m)
        p = p / (jnp.sum(p, -1, keepdims=True) + jnp.exp(sink - m))
        o4 = jnp.dot(p, vv, preferred_element_type=F32)
        heads += [o4[j * WINDOW:(j + 1) * WINDOW] for j in range(g)]
    o_ref[...] = jnp.concatenate(heads, axis=1)


def _swa_prompt(za, cos, sin, sinks):
    l = za.shape[0]
    nb = l // WINDOW
    cur = lambda w, c: pl.BlockSpec((WINDOW, w), lambda i: (i, c))
    prv = lambda w, c: pl.BlockSpec((WINDOW, w), lambda i: (jnp.maximum(i - 1, 0), c))
    kcol, vcol = A_WIDTH // A_KV_WIDTH, A_WIDTH // A_KV_WIDTH + 1
    return pl.pallas_call(
        _swa_prompt_kernel, grid=(nb,),
        in_specs=[pl.BlockSpec(memory_space=pltpu.SMEM),
                  cur(A_WIDTH, 0), cur(A_KV_WIDTH, kcol), prv(A_KV_WIDTH, kcol),
                  cur(A_KV_WIDTH, vcol), prv(A_KV_WIDTH, vcol),
                  cur(LANES, 0), cur(LANES, 0), prv(LANES, 0), prv(LANES, 0)],
        out_specs=[cur(A_WIDTH, 0), _full((WINDOW, A_KV_WIDTH))],
        out_shape=[jax.ShapeDtypeStruct((l, A_WIDTH), F32), jax.ShapeDtypeStruct((WINDOW, A_KV_WIDTH), F32)],
        compiler_params=_cp(("arbitrary",)))(sinks, za, za, za, za, za, cos, sin, cos, sin)


def _ret_prompt_kernel(cdec_ref, zr_ref, cos_ref, sin_ref, intra_ref, qdec_ref, kdec_ref, o_ref, so_ref, s_scr):
    @pl.when(pl.program_id(0) == 0)
    def _():
        s_scr[...] = jnp.zeros_like(s_scr)

    z = zr_ref[...]
    cos, sin = _tile_lanes(cos_ref[...], 2), _tile_lanes(sin_ref[...], 2)
    qk = R_HEADS * R_DK
    q = _rope(z[:, :qk], cos, sin)
    k = _rope(z[:, qk:2 * qk], cos, sin) * (R_DK ** -0.5)
    v = z[:, 2 * qk:2 * qk + R_WIDTH]
    gate = z[:, 2 * qk + R_WIDTH:]
    qdec = qdec_ref[...]
    kd = k * kdec_ref[...]
    outs = []
    for h in range(R_HEADS):
        sl = slice(h * R_DK, (h + 1) * R_DK)
        qh, kh, vh = q[:, sl], k[:, sl], v[:, sl]
        att = lax.dot_general(qh, kh, (((1,), (1,)), ((), ())), preferred_element_type=F32) * intra_ref[h]
        s = s_scr[h]
        o = jnp.dot(att, vh, preferred_element_type=F32) + jnp.dot(qh, s, preferred_element_type=F32) * qdec[:, sl]
        s_scr[h] = s * cdec_ref[h] + jnp.dot(kd[:, sl].T, vh, preferred_element_type=F32)
        outs.append(o * lax.rsqrt(jnp.mean(o * o, -1, keepdims=True) + EPS))
    o_ref[...] = jnp.concatenate(outs, axis=1) * _silu(gate)
    so_ref[...] = s_scr[...]


def _ret_tables(c):
    log_g = jnp.log1p(-jnp.exp2(-5.0 - jnp.arange(R_HEADS, dtype=F32)))
    idx = jnp.arange(c, dtype=F32)
    diff = idx[:, None] - idx[None, :]
    intra = jnp.where(diff >= 0, jnp.exp(log_g[:, None, None] * jnp.maximum(diff, 0.0)), 0.0)
    q_dec = jnp.exp(log_g[None, :] * (idx[:, None] + 1.0))
    k_dec = jnp.exp(log_g[None, :] * (c - 1.0 - idx[:, None]))
    c_dec = jnp.exp(log_g * c)
    return intra, jnp.repeat(q_dec, R_DK, axis=1), jnp.repeat(k_dec, R_DK, axis=1), c_dec


def _ret_prompt(zr, cos, sin):
    l = zr.shape[0]
    c = R_CHUNK
    intra, qdec, kdec, cdec = _ret_tables(c)
    blk = lambda w: pl.BlockSpec((c, w), lambda i: (i, 0))
    return pl.pallas_call(
        _ret_prompt_kernel, grid=(l // c,),
        in_specs=[pl.BlockSpec(memory_space=pltpu.SMEM), blk(R_COLS), blk(LANES), blk(LANES),
                  _full((R_HEADS, c, c)), _full((c, R_HEADS * R_DK)), _full((c, R_HEADS * R_DK))],
        out_specs=[blk(R_WIDTH), _full((R_HEADS, R_DK, R_DV))],
        out_shape=[jax.ShapeDtypeStruct((l, R_WIDTH), F32), jax.ShapeDtypeStruct((R_HEADS, R_DK, R_DV), F32)],
        scratch_shapes=[pltpu.VMEM((R_HEADS, R_DK, R_DV), F32)],
        compiler_params=_cp(("arbitrary",)))(cdec, zr, cos, sin, intra, qdec, kdec)


def _rwkv_prep_kernel(zc_ref, prev_ref, shift0_ref, mu_ref, w0_ref, w2_ref, a0_ref, a2_ref, g2_ref,
                      kk_ref, ka_ref, rk_ref,
                      r_ref, w_ref, k_ref, v_ref, a_ref, b_ref, g_ref, bonus_ref, *, sequence, tm):
    zc = zc_ref[...]
    if sequence:
        row = lax.broadcasted_iota(jnp.int32, zc.shape, 0)
        boundary = jnp.where(pl.program_id(0) == 0, shift0_ref[...], prev_ref[SUBLANES - 1:SUBLANES, :])
        prev = jnp.where(row == 0, jnp.broadcast_to(boundary, zc.shape), pltpu.roll(zc, 1, axis=0))
    else:
        prev = prev_ref[...]
    zs = zc + mu_ref[...] * (prev - zc)
    o1, o2, o3 = C_WIDTH, 2 * C_WIDTH, 3 * C_WIDTH
    r, k, v = zs[:, :o1], zs[:, o1:o2], zs[:, o2:o3]
    lora = zs[:, o3:o3 + LANES]
    gl = zs[:, o3 + LANES:]
    w = -jax.nn.softplus(-(w0_ref[...] + jnp.dot(jnp.tanh(lora), w2_ref[...], preferred_element_type=F32))) - 0.5
    decay = jnp.exp(-jnp.exp(w))
    a = _sigmoid(a0_ref[...] + jnp.dot(lora, a2_ref[...], preferred_element_type=F32))
    g = jnp.dot(_sigmoid(gl), g2_ref[...], preferred_element_type=F32)
    kk = k * kk_ref[...]
    kk = kk / jnp.maximum(jnp.sqrt(_seg_sum(kk * kk)), 1e-12)
    k_mod = k * (1.0 + (a - 1.0) * ka_ref[...])
    r_ref[...] = r
    w_ref[...] = decay
    k_ref[...] = k_mod
    v_ref[...] = v
    a_ref[...] = -kk
    b_ref[...] = kk * a
    g_ref[...] = g
    bonus_ref[...] = _seg_sum(r * k_mod * rk_ref[...]) * v


def _rwkv_prep(zc, prev, shift0, cw, sequence, tm):
    t = zc.shape[0]
    row = lambda w: pl.BlockSpec((tm, w), lambda i: (i, 0))
    if sequence:
        per = tm // SUBLANES
        prev_spec = pl.BlockSpec((SUBLANES, C_PAD), lambda i: (jnp.maximum(i * per - 1, 0), 0))
        prev = zc
    else:
        prev_spec = row(C_PAD)
    vec = _full((1, C_WIDTH))
    return pl.pallas_call(
        functools.partial(_rwkv_prep_kernel, sequence=sequence, tm=tm), grid=(t // tm,),
        in_specs=[row(C_PAD), prev_spec, _full((1, C_PAD)), _full((1, C_PAD)), vec, _full((LANES, C_WIDTH)),
                  vec, _full((LANES, C_WIDTH)), _full((2 * LANES, C_WIDTH)), vec, vec, vec],
        out_specs=[row(C_WIDTH)] * 8,
        out_shape=[jax.ShapeDtypeStruct((t, C_WIDTH), F32)] * 8,
        compiler_params=_cp(("parallel",)))(zc, prev, shift0, cw["mu"], cw["w0"], cw["w2"], cw["a0"], cw["a2"],
                                           cw["g2"], cw["k_k"], cw["k_a"], cw["r_k"])


SCAN_BLOCK = 256
SCAN_SUB = 16


def _rwkv_scan_kernel(r_ref, w_ref, k_ref, v_ref, a_ref, b_ref, g_ref, bonus_ref, lnw_ref, lnb_ref,
                      o_ref, so_ref, st_scr, xb_scr, vt_scr, vb_scr, yt_scr, yb_scr):
    @pl.when(pl.program_id(0) == 0)
    def _():
        st_scr[...] = jnp.zeros_like(st_scr)

    vtop, vbot = _rowpair(v_ref[...])
    vt_scr[...] = vtop
    vb_scr[...] = vbot
    ops = (a_ref, w_ref, b_ref, k_ref, r_ref)

    def sub(sc, st):
        base = pl.multiple_of(sc * SCAN_SUB, SCAN_SUB)

        def stage(u, _):
            for n, ref in enumerate(ops):
                xb_scr[n, u] = _colbcast(ref[pl.ds(base + u, 1), :])
            return 0

        lax.fori_loop(0, SCAN_SUB, stage, 0)

        def step(u, st):
            t = base + u
            st, y_t, y_b = _wkv_step(st, xb_scr[0, u], xb_scr[1, u], xb_scr[2, u], xb_scr[3, u], xb_scr[4, u],
                                     vt_scr[pl.ds(t, 1), :], vb_scr[pl.ds(t, 1), :])
            yt_scr[pl.ds(t, 1), :] = y_t
            yb_scr[pl.ds(t, 1), :] = y_b
            return st

        return lax.fori_loop(0, SCAN_SUB, step, st)

    st = lax.fori_loop(0, SCAN_BLOCK // SCAN_SUB, sub, st_scr[...])
    st_scr[...] = st
    y = _unpair(yt_scr[...], yb_scr[...])
    o_ref[...] = _wkv_post(y, lnw_ref[...], lnb_ref[...], bonus_ref[...], g_ref[...])
    heads = _t_to_heads(st, transpose=True)
    for h in range(C_HEADS):
        so_ref[h] = heads[h]


def _rwkv_scan(prep, lnw, lnb):
    r, w, k, v, a, b, g, bonus = prep
    l = r.shape[0]
    blk = pl.BlockSpec((SCAN_BLOCK, C_WIDTH), lambda i: (i, 0))
    vec = _full((1, C_WIDTH))
    return pl.pallas_call(
        _rwkv_scan_kernel, grid=(l // SCAN_BLOCK,),
        in_specs=[blk] * 8 + [vec, vec],
        out_specs=[blk, _full((C_HEADS, C_HEAD, C_HEAD))],
        out_shape=[jax.ShapeDtypeStruct((l, C_WIDTH), F32), jax.ShapeDtypeStruct((C_HEADS, C_HEAD, C_HEAD), F32)],
        scratch_shapes=[pltpu.VMEM((LANES, LANES), F32), pltpu.VMEM((5, SCAN_SUB, LANES, LANES), F32),
                        pltpu.VMEM((SCAN_BLOCK, LANES), F32), pltpu.VMEM((SCAN_BLOCK, LANES), F32),
                        pltpu.VMEM((SCAN_BLOCK, LANES), F32), pltpu.VMEM((SCAN_BLOCK, LANES), F32)],
        compiler_params=_cp(("arbitrary",)))(r, w, k, v, a, b, g, bonus, lnw, lnb)


def _sample_kernel(sinks_ref, gam_ref, za_ref, zr_ref, r_ref, w_ref, k_ref, v_ref, a_ref, b_ref, g_ref, bonus_ref,
                   kc_ref, vc_ref, sret_ref, swkv_ref, cos_ref, sin_ref, lnw_ref, lnb_ref,
                   oa_ref, or_ref, oc_ref, kco_ref, vco_ref, sreto_ref, swkvo_ref):
    cos, sin = cos_ref[...], sin_ref[...]
    lane = lax.broadcasted_iota(jnp.int32, (1, LANES), 1)
    row8 = lax.broadcasted_iota(jnp.int32, (SUBLANES, LANES), 0)

    za = za_ref[0]
    q = _rope(za[:, :A_WIDTH], _tile_lanes(cos, 4), _tile_lanes(sin, 4))
    knew = _rope(za[:, A_WIDTH:A_WIDTH + A_KV_WIDTH], cos, sin)
    vnew = za[:, A_WIDTH + A_KV_WIDTH:]
    kc, vc = kc_ref[0], vc_ref[0]
    g = A_HEADS // A_KV_HEADS
    zero = jnp.zeros((1, HEAD_DIM), F32)
    qrows = []
    for h in range(A_HEADS):
        qh = q[:, h * HEAD_DIM:(h + 1) * HEAD_DIM]
        qrows.append(jnp.concatenate([qh, zero] if h // g == 0 else [zero, qh], axis=1))
    qm = jnp.concatenate(qrows + [jnp.zeros((LANES - A_HEADS, LANES), F32)], axis=0)
    scale = HEAD_DIM ** -0.5
    s = lax.dot_general(kc, qm, (((1,), (1,)), ((), ())), preferred_element_type=F32) * scale
    kn8 = jnp.where(row8 == 0, jnp.broadcast_to(knew, (SUBLANES, LANES)), 0.0)
    s_new = lax.dot_general(kn8, qm, (((1,), (1,)), ((), ())), preferred_element_type=F32)[0:1] * scale
    sink = jnp.zeros((1, LANES), F32)
    for h in range(A_HEADS):
        sink = jnp.where(lane == h, sinks_ref[h], sink)
    m = jnp.maximum(jnp.maximum(jnp.max(s, axis=0, keepdims=True), s_new), sink)
    p = jnp.exp(s - m)
    p_new = jnp.exp(s_new - m)
    denom = jnp.sum(p, axis=0, keepdims=True) + p_new + jnp.exp(sink - m)
    p = p / denom
    p_new = p_new / denom
    pn_col = jnp.broadcast_to(p_new, (LANES, LANES)).T[:, 0:1]
    o_full = jnp.dot(p.T, vc, preferred_element_type=F32) + pn_col * vnew
    oa_ref[0] = jnp.concatenate(
        [o_full[h:h + 1, (h // g) * HEAD_DIM:(h // g + 1) * HEAD_DIM] for h in range(A_HEADS)], axis=1)
    rowk = lax.broadcasted_iota(jnp.int32, kc.shape, 0)
    last = rowk == kc.shape[0] - 1
    kco_ref[0] = jnp.where(last, jnp.broadcast_to(knew, kc.shape), pltpu.roll(kc, kc.shape[0] - 1, axis=0))
    vco_ref[0] = jnp.where(last, jnp.broadcast_to(vnew, vc.shape), pltpu.roll(vc, vc.shape[0] - 1, axis=0))

    zr = zr_ref[0]
    qk = R_HEADS * R_DK
    cos2, sin2 = _tile_lanes(cos, 2), _tile_lanes(sin, 2)
    rq = _rope(zr[:, :qk], cos2, sin2)
    rk = _rope(zr[:, qk:2 * qk], cos2, sin2) * (R_DK ** -0.5)
    rv = zr[:, 2 * qk:2 * qk + R_WIDTH]
    rg = zr[:, 2 * qk + R_WIDTH:]
    st = _heads_to_t(sret_ref[0], transpose=False)
    rr = lax.broadcasted_iota(jnp.int32, (LANES, LANES), 0) >= 64
    cc = lax.broadcasted_iota(jnp.int32, (LANES, LANES), 1) >= 64
    gamma = jnp.where(rr, jnp.where(cc, gam_ref[3], gam_ref[1]), jnp.where(cc, gam_ref[2], gam_ref[0]))
    vt, vb = _rowpair(rv)
    st = st * gamma + _colbcast(rk) * _rowbcast(vt, vb)
    o_t, o_b = _halfsums(st * _colbcast(rq))
    o = _unpair(o_t, o_b)
    o = o * lax.rsqrt(_seg_sum(o * o) * (1.0 / R_DV) + EPS)
    or_ref[0] = o * _silu(rg)
    heads = _t_to_heads(st, transpose=False)
    for h in range(R_HEADS):
        sreto_ref[0, h] = heads[h]

    st = _heads_to_t(swkv_ref[0], transpose=True)
    vt, vb = _rowpair(v_ref[0])
    st, y_t, y_b = _wkv_step(st, _colbcast(a_ref[0]), _colbcast(w_ref[0]), _colbcast(b_ref[0]),
                             _colbcast(k_ref[0]), _colbcast(r_ref[0]), vt, vb)
    oc_ref[0] = _wkv_post(_unpair(y_t, y_b), lnw_ref[...], lnb_ref[...], bonus_ref[0], g_ref[0])
    heads = _t_to_heads(st, transpose=True)
    for h in range(C_HEADS):
        swkvo_ref[0, h] = heads[h]


def _sample_mixers(za, zr, prep, kc, vc, sret, swkv, cos, sin, sinks, lnw, lnb):
    b = za.shape[0]
    wb = kc.shape[1]
    assert wb <= WINDOW and PAST_LEN >= wb
    gam = jnp.exp(jnp.log1p(-jnp.exp2(-5.0 - jnp.arange(R_HEADS, dtype=F32))) * 1.0)
    tok = lambda w: pl.BlockSpec((1, 1, w), lambda i: (i, 0, 0))
    cache = pl.BlockSpec((1, wb, A_KV_WIDTH), lambda i: (i, 0, 0))
    state = pl.BlockSpec((1, 4, 64, 64), lambda i: (i, 0, 0, 0))
    smem = pl.BlockSpec(memory_space=pltpu.SMEM)
    vec = _full((1, C_WIDTH))
    r3 = lambda x: x.reshape(b, 1, x.shape[-1])
    outs = pl.pallas_call(
        _sample_kernel, grid=(b,),
        in_specs=[smem, smem, tok(A_COLS), tok(R_COLS)] + [tok(C_WIDTH)] * 8 + [cache, cache, state, state,
                  _full((1, LANES)), _full((1, LANES)), vec, vec],
        out_specs=[tok(A_WIDTH), tok(R_WIDTH), tok(C_WIDTH), cache, cache, state, state],
        out_shape=[jax.ShapeDtypeStruct((b, 1, A_WIDTH), F32), jax.ShapeDtypeStruct((b, 1, R_WIDTH), F32),
                   jax.ShapeDtypeStruct((b, 1, C_WIDTH), F32), jax.ShapeDtypeStruct(kc.shape, F32),
                   jax.ShapeDtypeStruct(vc.shape, F32), jax.ShapeDtypeStruct(sret.shape, F32),
                   jax.ShapeDtypeStruct(swkv.shape, F32)],
        compiler_params=_cp(("parallel",)))(sinks, gam, r3(za), r3(zr), *[r3(x) for x in prep],
                                           kc, vc, sret, swkv, cos, sin, lnw, lnb)
    oa, orr, oc = (x.reshape(b, x.shape[-1]) for x in outs[:3])
    return (oa, orr, oc) + tuple(outs[3:])


def _outproj_kernel(oa_ref, or_ref, oc_ref, x_ref, wo_ref, gf_ref, wrt_ref, brt_ref, x1_ref, h2_ref, route_ref, *, tm):
    mix = jnp.concatenate([oa_ref[...], or_ref[...], oc_ref[...]], axis=1).astype(BF16)
    x1 = x_ref[...] + jnp.dot(mix, wo_ref[...], preferred_element_type=F32)
    x1_ref[...] = x1
    h2 = _rmsnorm(x1, gf_ref[...])
    for s in range(ROW_TILES):
        h2_ref[pl.ds(s, tm, stride=ROW_TILES), :] = h2[:, s * LANES:(s + 1) * LANES]
    logits = jnp.dot(h2, wrt_ref[...], preferred_element_type=F32, precision=lax.Precision.HIGHEST) + brt_ref[...]
    lane = lax.broadcasted_iota(jnp.int32, logits.shape, 1).astype(F32)
    big = float(LANES)
    is_g = lane < N_GROUPS
    lg = jnp.where(is_g, logits, -jnp.inf)
    m_g = jnp.max(lg, -1, keepdims=True)
    grp = jnp.min(jnp.where(lg == m_g, lane, big), -1, keepdims=True)
    p_grp = 1.0 / jnp.sum(jnp.where(is_g, jnp.exp(lg - m_g), 0.0), -1, keepdims=True)
    lo = N_GROUPS + EXPERTS_PER_GROUP * grp
    in_grp = (lane >= lo) & (lane < lo + EXPERTS_PER_GROUP)
    le = jnp.where(in_grp, logits, -jnp.inf)
    l1 = jnp.max(le, -1, keepdims=True)
    i1 = jnp.min(jnp.where(le == l1, lane, big), -1, keepdims=True)
    le2 = jnp.where(lane == i1, -jnp.inf, le)
    l2 = jnp.max(le2, -1, keepdims=True)
    i2 = jnp.min(jnp.where(le2 == l2, lane, big), -1, keepdims=True)
    e = jnp.exp(l2 - l1)
    g1 = p_grp / (1.0 + e)
    g2 = p_grp * e / (1.0 + e)
    route = jnp.where(lane == 0, i1 - N_GROUPS, jnp.where(lane == 1, i2 - N_GROUPS,
                      jnp.where(lane == 2, g1, jnp.where(lane == 3, g2, 0.0))))
    route_ref[...] = route


def _outproj(oa, orr, oc, x, wo, gf, wrt, brt, tm):
    t = x.shape[0]
    row = lambda w: pl.BlockSpec((tm, w), lambda i: (i, 0))
    return pl.pallas_call(
        functools.partial(_outproj_kernel, tm=tm), grid=(t // tm,),
        in_specs=[row(A_WIDTH), row(R_WIDTH), row(C_WIDTH), row(D_MODEL), _full(wo.shape), _full((1, D_MODEL)),
                  _full(wrt.shape), _full((1, LANES))],
        out_specs=[row(D_MODEL), pl.BlockSpec((tm * ROW_TILES, LANES), lambda i: (i, 0)), row(LANES)],
        out_shape=[jax.ShapeDtypeStruct((t, D_MODEL), F32), jax.ShapeDtypeStruct((t * ROW_TILES, LANES), F32),
                   jax.ShapeDtypeStruct((t, LANES), F32)],
        compiler_params=_cp(("parallel",)))(oa, orr, oc, x, wo, gf, wrt, brt)


def _moe_kernel(blk_exp_ref, row_id_ref, nused_ref, h2_hbm, wg_ref, wu_ref, wd_ref, yw_hbm,
                gbuf, sbuf, xbuf, gsem, ssem):
    i = pl.program_id(0)
    nb = pl.num_programs(0)
    nused = nused_ref[0]
    slot = i % 2

    def gather(blk, sl):
        def body(r, _):
            tok = jnp.maximum(row_id_ref[blk * MOE_ROWS + r], 0) // TOP_K
            pltpu.make_async_copy(h2_hbm.at[pl.ds(pl.multiple_of(tok * ROW_TILES, ROW_TILES), ROW_TILES), :],
                                  gbuf.at[sl, pl.ds(pl.multiple_of(r * ROW_TILES, ROW_TILES), ROW_TILES), :],
                                  gsem.at[sl]).start()
            return 0
        lax.fori_loop(0, MOE_ROWS, body, 0)

    def gather_wait(sl):
        def body(r, _):
            pltpu.make_async_copy(h2_hbm.at[pl.ds(0, ROW_TILES), :], gbuf.at[sl, pl.ds(0, ROW_TILES), :],
                                  gsem.at[sl]).wait()
            return 0
        lax.fori_loop(0, MOE_ROWS, body, 0)

    def scatter(blk):
        def body(r, _):
            rid = row_id_ref[blk * MOE_ROWS + r]

            @pl.when(rid >= 0)
            def _():
                pltpu.make_async_copy(sbuf.at[pl.ds(pl.multiple_of(r * ROW_TILES, ROW_TILES), ROW_TILES), :],
                                      yw_hbm.at[pl.ds(pl.multiple_of(rid * ROW_TILES, ROW_TILES), ROW_TILES), :],
                                      ssem).start()
            return 0
        lax.fori_loop(0, MOE_ROWS, body, 0)

    def scatter_wait(blk):
        def body(r, _):
            @pl.when(row_id_ref[blk * MOE_ROWS + r] >= 0)
            def _():
                pltpu.make_async_copy(sbuf.at[pl.ds(0, ROW_TILES), :], yw_hbm.at[pl.ds(0, ROW_TILES), :], ssem).wait()
            return 0
        lax.fori_loop(0, MOE_ROWS, body, 0)

    @pl.when((i == 0) & (nused > 0))
    def _():
        gather(0, 0)

    @pl.when(i < nused)
    def _():
        gather_wait(slot)

        @pl.when(i + 1 < nused)
        def _():
            gather(i + 1, 1 - slot)

        for s in range(ROW_TILES):
            xbuf[:, s * LANES:(s + 1) * LANES] = gbuf[slot, pl.ds(s, MOE_ROWS, stride=ROW_TILES), :].astype(BF16)
        x = xbuf[...]
        hid = _silu(jnp.dot(x, wg_ref[0], preferred_element_type=F32)) * jnp.dot(x, wu_ref[0], preferred_element_type=F32)
        y = jnp.dot(hid.astype(BF16), wd_ref[0], preferred_element_type=F32)

        @pl.when(i > 0)
        def _():
            scatter_wait(i - 1)

        for s in range(ROW_TILES):
            sbuf[pl.ds(s, MOE_ROWS, stride=ROW_TILES), :] = y[:, s * LANES:(s + 1) * LANES]
        scatter(i)

    @pl.when((i == nb - 1) & (nused > 0))
    def _():
        scatter_wait(nused - 1)


def _moe(h2, route, wg, wu, wd):
    t = route.shape[0]
    na = t * TOP_K
    flat_e = route[:, :TOP_K].astype(jnp.int32).reshape(na)
    order = jnp.argsort(flat_e).astype(jnp.int32)
    counts = jnp.sum(flat_e[:, None] == jnp.arange(N_EXPERTS, dtype=jnp.int32)[None, :], axis=0, dtype=jnp.int32)
    padded = (counts + MOE_ROWS - 1) // MOE_ROWS * MOE_ROWS
    start = jnp.cumsum(counts) - counts
    pend = jnp.cumsum(padded)
    pstart = pend - padded
    n_blocks = -(-na // MOE_ROWS) + N_EXPERTS
    n_rows = n_blocks * MOE_ROWS
    blk_exp = jnp.minimum(jnp.searchsorted(pend, jnp.arange(n_blocks, dtype=jnp.int32) * MOE_ROWS, side="right"),
                          N_EXPERTS - 1).astype(jnp.int32)
    pos = jnp.arange(n_rows, dtype=jnp.int32)
    e_p = blk_exp[pos // MOE_ROWS]
    off = pos - pstart[e_p]
    valid = (off < counts[e_p]) & (pos < pend[-1])
    row_id = jnp.where(valid, order[jnp.clip(start[e_p] + off, 0, na - 1)], -1).astype(jnp.int32)
    nused = (pend[-1] // MOE_ROWS).astype(jnp.int32).reshape(1)
    wspec = lambda a, b: pl.BlockSpec((1, a, b), lambda i, be, rid, nu: (be[i], 0, 0))
    return pl.pallas_call(
        _moe_kernel,
        grid_spec=pltpu.PrefetchScalarGridSpec(
            num_scalar_prefetch=3, grid=(n_blocks,),
            in_specs=[pl.BlockSpec(memory_space=pl.ANY), wspec(D_MODEL, EXPERT_FF), wspec(D_MODEL, EXPERT_FF),
                      wspec(EXPERT_FF, D_MODEL)],
            out_specs=pl.BlockSpec(memory_space=pl.ANY),
            scratch_shapes=[pltpu.VMEM((2, MOE_ROWS * ROW_TILES, LANES), F32),
                            pltpu.VMEM((MOE_ROWS * ROW_TILES, LANES), F32),
                            pltpu.VMEM((MOE_ROWS, D_MODEL), BF16),
                            pltpu.SemaphoreType.DMA((2,)), pltpu.SemaphoreType.DMA(())]),
        out_shape=jax.ShapeDtypeStruct((na * ROW_TILES, LANES), F32),
        compiler_params=_cp(("arbitrary",)))(blk_exp, row_id, nused, h2, wg, wu, wd)


def _rope_tables(pos):
    half = HEAD_DIM // 2
    inv = ROPE_THETA ** (-jnp.arange(half, dtype=F32) / half)
    ang = pos.astype(F32)[:, None] * inv[None, :]
    c, s = jnp.cos(ang), jnp.sin(ang)
    return jnp.concatenate([c, c, c, c], axis=1), jnp.concatenate([-s, s, -s, s], axis=1)


def _pad_cols(w, n):
    return jnp.pad(w, ((0, 0), (0, n - w.shape[1])))


def _pad_rows(w, lo, n):
    return jnp.pad(w, ((lo, n - lo - w.shape[0]), (0, 0)))


def kernel(x_prompt, x_sample, cache_swa_k, cache_swa_v, state_ret, state_wkv, state_shift, norm_mix, w_in, sinks,
           rwkv_mu, rwkv_w0, rwkv_w2, rwkv_a0, rwkv_a2, rwkv_g2, rwkv_k_k, rwkv_k_a, rwkv_r_k, rwkv_ln_w, rwkv_ln_b,
           w_out, norm_ffn, router_g, router_g_b, router_e, router_e_b, expert_w_gate, expert_w_up, expert_w_down,
           norm_final):
    lp = x_prompt.shape[1]
    bs = x_sample.shape[0]
    wb = cache_swa_k.shape[2]
    tm_p = 512
    xp = x_prompt.reshape(lp, D_MODEL)
    xs = x_sample.reshape(bs, D_MODEL)
    cos_p, sin_p = _rope_tables(jnp.arange(lp, dtype=jnp.int32))
    cos_s, sin_s = _rope_tables(PAST_LEN + jnp.arange(1, dtype=jnp.int32))
    outs_p = [[] for _ in range(5)]
    outs_s = [[] for _ in range(5)]
    moe_p = moe_s = None
    row = lambda v: v.reshape(1, -1)
    for i in range(DEPTH):
        wi = w_in[i].astype(BF16)
        wa, wr, wc = wi[:, :A_COLS], wi[:, A_COLS:A_COLS + R_COLS], _pad_cols(wi[:, A_COLS + R_COLS:], C_PAD)
        cw = {
            "mu": _pad_cols(row(rwkv_mu[i]), C_PAD), "w0": row(rwkv_w0[i]), "a0": row(rwkv_a0[i]),
            "w2": _pad_rows(rwkv_w2[i], 0, LANES), "a2": _pad_rows(rwkv_a2[i], C_DECAY_LORA, LANES),
            "g2": _pad_rows(rwkv_g2[i], 0, 2 * LANES),
            "k_k": row(rwkv_k_k[i]), "k_a": row(rwkv_k_a[i]), "r_k": row(rwkv_r_k[i]),
        }
        lnw, lnb = row(rwkv_ln_w[i]), row(rwkv_ln_b[i])
        g_mix, g_ffn = row(norm_mix[i]), row(norm_ffn[i])
        wo = w_out[i].astype(BF16)
        wrt = _pad_cols(jnp.concatenate([router_g[i], router_e[i]], axis=1), LANES)
        brt = _pad_cols(row(jnp.concatenate([router_g_b[i], router_e_b[i]])), LANES)
        wg, wu, wd = expert_w_gate[i].astype(BF16), expert_w_up[i].astype(BF16), expert_w_down[i].astype(BF16)

        xn, za, zr, zc = _inproj(xp, moe_p, g_mix, wa, wr, wc, tm_p)
        xp = xp if xn is None else xn
        oa, k_last = _swa_prompt(za, cos_p, sin_p, sinks[i])
        orr, s_ret = _ret_prompt(zr, cos_p, sin_p)
        prep = _rwkv_prep(zc, None, jnp.zeros((1, C_PAD), F32), cw, True, tm_p)
        oc, s_wkv = _rwkv_scan(prep, lnw, lnb)
        xp, h2, route = _outproj(oa, orr, oc, xp, wo, g_ffn, wrt, brt, tm_p)
        moe_p = (_moe(h2, route, wg, wu, wd), route)
        nk = min(WINDOW, lp)
        outs_p[0].append(k_last.reshape(1, nk, A_KV_HEADS, HEAD_DIM))
        outs_p[1].append(za[lp - nk:, A_WIDTH + A_KV_WIDTH:].reshape(1, nk, A_KV_HEADS, HEAD_DIM))
        outs_p[2].append(s_ret[None])
        outs_p[3].append(s_wkv[None])
        outs_p[4].append(zc[lp - 1:, :C_COLS])

        xn, za, zr, zc = _inproj(xs, moe_s, g_mix, wa, wr, wc, bs)
        xs = xs if xn is None else xn
        prep = _rwkv_prep(zc, _pad_cols(state_shift[i], C_PAD), jnp.zeros((1, C_PAD), F32), cw, False, bs)
        oa, orr, oc, kc_new, vc_new, sret_new, swkv_new = _sample_mixers(
            za, zr, prep, cache_swa_k[i].reshape(bs, wb, A_KV_WIDTH), cache_swa_v[i].reshape(bs, wb, A_KV_WIDTH),
            state_ret[i], state_wkv[i], cos_s, sin_s, sinks[i], lnw, lnb)
        xs, h2, route = _outproj(oa, orr, oc, xs, wo, g_ffn, wrt, brt, bs)
        moe_s = (_moe(h2, route, wg, wu, wd), route)
        outs_s[0].append(kc_new.reshape(bs, wb, A_KV_HEADS, HEAD_DIM))
        outs_s[1].append(vc_new.reshape(bs, wb, A_KV_HEADS, HEAD_DIM))
        outs_s[2].append(sret_new)
        outs_s[3].append(swkv_new)
        outs_s[4].append(zc[:, :C_COLS])

    gfin = row(norm_final)
    y_prompt = _final(xp, moe_p[0], moe_p[1], gfin, tm_p).reshape(x_prompt.shape)
    y_sample = _final(xs, moe_s[0], moe_s[1], gfin, bs).reshape(x_sample.shape)
    sp = [jnp.stack(o) for o in outs_p]
    ss = [jnp.stack(o) for o in outs_s]
    return (y_prompt, y_sample, sp[0], sp[1], sp[2], sp[3], sp[4], ss[0], ss[1], ss[2], ss[3], ss[4])
```

```python
import functools

import numpy as np
import jax
import jax.numpy as jnp
from jax import lax
from jax.experimental import pallas as pl
from jax.experimental.pallas import tpu as pltpu

F32 = jnp.float32
BF16 = jnp.bfloat16

D_MODEL = 1024
DEPTH = 2
PAST_LEN = 16384
A_HEADS, A_KV_HEADS, HEAD_DIM, WINDOW = 8, 2, 64, 128
ROPE_THETA = 10000.0
A_WIDTH = A_HEADS * HEAD_DIM
A_KV_WIDTH = A_KV_HEADS * HEAD_DIM
A_COLS = A_WIDTH + 2 * A_KV_WIDTH
R_HEADS, R_DK, R_DV, R_CHUNK = 4, 64, 64, 128
R_WIDTH = R_HEADS * R_DV
R_COLS = 2 * R_HEADS * R_DK + 2 * R_WIDTH
C_HEADS, C_HEAD = 4, 64
C_WIDTH = C_HEADS * C_HEAD
C_DECAY_LORA, C_ICLR_LORA, C_GATE_LORA = 64, 64, 160
C_COLS = 3 * C_WIDTH + C_DECAY_LORA + C_ICLR_LORA + C_GATE_LORA
C_PAD = 1152
C_GN_EPS = 64e-5
N_GROUPS, EXPERTS_PER_GROUP, TOP_K, EXPERT_FF = 4, 8, 2, 512
N_EXPERTS = N_GROUPS * EXPERTS_PER_GROUP
EPS = 1e-6

LANES = 128
SUBLANES = 8
ROW_TILES = D_MODEL // LANES
MOE_ROWS = 256
DMA_UNROLL = 8
IDX_BITS = 19
VMEM_LIMIT = 56 * 1024 * 1024


def _cp(sem, vmem=VMEM_LIMIT):
    return pltpu.CompilerParams(dimension_semantics=sem, vmem_limit_bytes=vmem)


def _full(shape):
    n = len(shape)
    return pl.BlockSpec(shape, lambda *_: (0,) * n)


def _rmsnorm(x, g):
    return x * lax.rsqrt(jnp.mean(x * x, -1, keepdims=True) + EPS) * g


def _silu(x):
    return x * (1.0 / (1.0 + jnp.exp(-x)))


def _sigmoid(x):
    return 1.0 / (1.0 + jnp.exp(-x))


def _rope(x, cos, sin_signed):
    w = x.shape[-1]
    lane = lax.broadcasted_iota(jnp.int32, x.shape, x.ndim - 1)
    first = (lane % HEAD_DIM) < (HEAD_DIM // 2)
    swapped = jnp.where(first, pltpu.roll(x, w - HEAD_DIM // 2, axis=x.ndim - 1),
                        pltpu.roll(x, HEAD_DIM // 2, axis=x.ndim - 1))
    return x * cos + swapped * sin_signed


def _tile_lanes(x, n):
    return jnp.concatenate([x] * n, axis=-1) if n > 1 else x


def _seg_sum(x, seg=64):
    outs = []
    for h in range(x.shape[-1] // seg):
        s = jnp.sum(x[:, h * seg:(h + 1) * seg], axis=-1, keepdims=True)
        outs.append(jnp.broadcast_to(s, (x.shape[0], seg)))
    return jnp.concatenate(outs, axis=-1)


def _read_rows(ref, tm, first, stride):
    return jnp.concatenate([ref[pl.ds(first + s, tm, stride=stride), :] for s in range(ROW_TILES)], axis=1)


def _colbcast(x):
    z = jnp.concatenate([jnp.broadcast_to(x[:, :LANES], (64, LANES)),
                         jnp.broadcast_to(x[:, LANES:], (64, LANES))], axis=0)
    return z.T


def _rowpair(x):
    top = jnp.concatenate([x[:, 0:64], x[:, 128:192]], axis=1)
    bot = jnp.concatenate([x[:, 64:128], x[:, 192:256]], axis=1)
    return top, bot


def _unpair(top, bot):
    return jnp.concatenate([top[:, :64], bot[:, :64], top[:, 64:], bot[:, 64:]], axis=1)


def _rowbcast(top, bot):
    return jnp.concatenate([jnp.broadcast_to(top, (64, LANES)), jnp.broadcast_to(bot, (64, LANES))], axis=0)


def _halfsums(p):
    return jnp.sum(p[:64], axis=0, keepdims=True), jnp.sum(p[64:], axis=0, keepdims=True)


def _wkv_step(st, ab, wb, bb, kb, rb, vtop, vbot):
    sa_t, sa_b = _halfsums(st * ab)
    st = st * wb + _rowbcast(sa_t, sa_b) * bb + _rowbcast(vtop, vbot) * kb
    y_t, y_b = _halfsums(st * rb)
    return st, y_t, y_b


def _wkv_post(y, lnw, lnb, bonus, g):
    mu = _seg_sum(y) * (1.0 / C_HEAD)
    d = y - mu
    var = _seg_sum(d * d) * (1.0 / C_HEAD)
    yn = d * lax.rsqrt(var + C_GN_EPS)
    return (yn * lnw + lnb + bonus) * g


def _heads_to_t(s4, transpose):
    if transpose:
        m = jnp.concatenate([jnp.concatenate([s4[0], s4[1]], axis=1),
                             jnp.concatenate([s4[2], s4[3]], axis=1)], axis=0)
        return m.T
    return jnp.concatenate([jnp.concatenate([s4[0], s4[2]], axis=1),
                            jnp.concatenate([s4[1], s4[3]], axis=1)], axis=0)


def _t_to_heads(st, transpose):
    if transpose:
        m = st.T
        return [m[0:64, 0:64], m[0:64, 64:128], m[64:128, 0:64], m[64:128, 64:128]]
    return [st[0:64, 0:64], st[64:128, 0:64], st[0:64, 64:128], st[64:128, 64:128]]


def _inproj_kernel(*refs, combine, tm):
    if combine:
        x_ref, yw_ref, route_ref, g_ref, wa_ref, wr_ref, wc_ref, xo_ref, za_ref, zr_ref, zc_ref = refs
        route = route_ref[...]
        y0 = _read_rows(yw_ref, tm, 0, 2 * ROW_TILES)
        y1 = _read_rows(yw_ref, tm, ROW_TILES, 2 * ROW_TILES)
        x = x_ref[...] + (route[:, 2:3] * y0 + route[:, 3:4] * y1)
        xo_ref[...] = x
    else:
        x_ref, g_ref, wa_ref, wr_ref, wc_ref, za_ref, zr_ref, zc_ref = refs
        x = x_ref[...]
    h = _rmsnorm(x, g_ref[...]).astype(BF16)
    za_ref[...] = jnp.dot(h, wa_ref[...], preferred_element_type=F32)
    zr_ref[...] = jnp.dot(h, wr_ref[...], preferred_element_type=F32)
    zc_ref[...] = jnp.dot(h, wc_ref[...], preferred_element_type=F32)


def _inproj(x, moe, g, wa, wr, wc, tm):
    t = x.shape[0]
    row = lambda w: pl.BlockSpec((tm, w), lambda i: (i, 0))
    wspecs = [_full((1, D_MODEL)), _full(wa.shape), _full(wr.shape), _full(wc.shape)]
    zshapes = [jax.ShapeDtypeStruct((t, A_COLS), F32), jax.ShapeDtypeStruct((t, R_COLS), F32),
               jax.ShapeDtypeStruct((t, C_PAD), F32)]
    zspecs = [row(A_COLS), row(R_COLS), row(C_PAD)]
    if moe is None:
        return (None,) + tuple(pl.pallas_call(
            functools.partial(_inproj_kernel, combine=False, tm=tm), name="inproj",
            grid=(t // tm,), in_specs=[row(D_MODEL)] + wspecs, out_specs=zspecs, out_shape=zshapes,
            compiler_params=_cp(("parallel",)))(x, g, wa, wr, wc))
    yw, route, first_tok = moe
    off = first_tok // tm
    return pl.pallas_call(
        functools.partial(_inproj_kernel, combine=True, tm=tm), name="combine_inproj",
        grid=(t // tm,),
        in_specs=[row(D_MODEL), pl.BlockSpec((tm * 2 * ROW_TILES, LANES), lambda i: (i + off, 0)),
                  pl.BlockSpec((tm, LANES), lambda i: (i + off, 0))] + wspecs,
        out_specs=[row(D_MODEL)] + zspecs,
        out_shape=[jax.ShapeDtypeStruct((t, D_MODEL), F32)] + zshapes,
        compiler_params=_cp(("parallel",)))(x, yw, route, g, wa, wr, wc)


def _final_kernel(x_ref, yw_ref, route_ref, g_ref, o_ref, *, tm):
    route = route_ref[...]
    y0 = _read_rows(yw_ref, tm, 0, 2 * ROW_TILES)
    y1 = _read_rows(yw_ref, tm, ROW_TILES, 2 * ROW_TILES)
    x = x_ref[...] + (route[:, 2:3] * y0 + route[:, 3:4] * y1)
    o_ref[...] = _rmsnorm(x, g_ref[...])


def _final(x, moe, g, tm):
    t = x.shape[0]
    yw, route, first_tok = moe
    off = first_tok // tm
    row = lambda w: pl.BlockSpec((tm, w), lambda i: (i, 0))
    return pl.pallas_call(
        functools.partial(_final_kernel, tm=tm), grid=(t // tm,), name="combine_final",
        in_specs=[row(D_MODEL), pl.BlockSpec((tm * 2 * ROW_TILES, LANES), lambda i: (i + off, 0)),
                  pl.BlockSpec((tm, LANES), lambda i: (i + off, 0)), _full((1, D_MODEL))],
        out_specs=row(D_MODEL), out_shape=jax.ShapeDtypeStruct((t, D_MODEL), F32),
        compiler_params=_cp(("parallel",)))(x, yw, route, g)


def _swa_prompt_kernel(sinks_ref, q_ref, kc_ref, kp_ref, vc_ref, vp_ref, cosc_ref, sinc_ref, cosp_ref, sinp_ref,
                       o_ref, ko_ref):
    i = pl.program_id(0)
    cosc, sinc = cosc_ref[...], sinc_ref[...]
    q = _rope(q_ref[...], _tile_lanes(cosc, 4), _tile_lanes(sinc, 4))
    kc = _rope(kc_ref[...], cosc, sinc)
    kp = _rope(kp_ref[...], cosp_ref[...], sinp_ref[...])
    ko_ref[...] = kc
    vc, vp = vc_ref[...], vp_ref[...]
    g = A_HEADS // A_KV_HEADS
    nq = g * WINDOW
    r = lax.broadcasted_iota(jnp.int32, (nq, 2 * WINDOW), 0) % WINDOW
    c = lax.broadcasted_iota(jnp.int32, (nq, 2 * WINDOW), 1)
    ok = (c >= r) & (c <= r + WINDOW) & ((c >= WINDOW) | (i > 0))
    heads = []
    for kv in range(A_KV_HEADS):
        lo = kv * HEAD_DIM
        q4 = jnp.concatenate([q[:, (kv * g + j) * HEAD_DIM:(kv * g + j + 1) * HEAD_DIM] for j in range(g)], axis=0)
        kk = jnp.concatenate([kp[:, lo:lo + HEAD_DIM], kc[:, lo:lo + HEAD_DIM]], axis=0)
        vv = jnp.concatenate([vp[:, lo:lo + HEAD_DIM], vc[:, lo:lo + HEAD_DIM]], axis=0)
        s = lax.dot_general(q4, kk, (((1,), (1,)), ((), ())), preferred_element_type=F32) * (HEAD_DIM ** -0.5)
        s = jnp.where(ok, s, -jnp.inf)
        sink = jnp.concatenate([jnp.full((WINDOW, 1), sinks_ref[kv * g + j], F32) for j in range(g)], axis=0)
        m = jnp.maximum(jnp.max(s, -1, keepdims=True), sink)
        p = jnp.exp(s - m)
        p = p / (jnp.sum(p, -1, keepdims=True) + jnp.exp(sink - m))
        o4 = jnp.dot(p, vv, preferred_element_type=F32)
        heads += [o4[j * WINDOW:(j + 1) * WINDOW] for j in range(g)]
    o_ref[...] = jnp.concatenate(heads, axis=1)


def _swa_prompt(za, cos, sin, sinks):
    l = za.shape[0]
    nb = l // WINDOW
    cur = lambda w, c: pl.BlockSpec((WINDOW, w), lambda i: (i, c))
    prv = lambda w, c: pl.BlockSpec((WINDOW, w), lambda i: (jnp.maximum(i - 1, 0), c))
    kcol, vcol = A_WIDTH // A_KV_WIDTH, A_WIDTH // A_KV_WIDTH + 1
    return pl.pallas_call(
        _swa_prompt_kernel, grid=(nb,), name="swa_prompt",
        in_specs=[pl.BlockSpec(memory_space=pltpu.SMEM),
                  cur(A_WIDTH, 0), cur(A_KV_WIDTH, kcol), prv(A_KV_WIDTH, kcol),
                  cur(A_KV_WIDTH, vcol), prv(A_KV_WIDTH, vcol),
                  cur(LANES, 0), cur(LANES, 0), prv(LANES, 0), prv(LANES, 0)],
        out_specs=[cur(A_WIDTH, 0), _full((WINDOW, A_KV_WIDTH))],
        out_shape=[jax.ShapeDtypeStruct((l, A_WIDTH), F32), jax.ShapeDtypeStruct((WINDOW, A_KV_WIDTH), F32)],
        compiler_params=_cp(("arbitrary",)))(sinks, za, za, za, za, za, cos, sin, cos, sin)


def _ret_prompt_kernel(cdec_ref, zr_ref, cos_ref, sin_ref, intra_ref, qdec_ref, kdec_ref, o_ref, so_ref, s_scr):
    @pl.when(pl.program_id(0) == 0)
    def _():
        s_scr[...] = jnp.zeros_like(s_scr)

    z = zr_ref[...]
    cos, sin = _tile_lanes(cos_ref[...], 2), _tile_lanes(sin_ref[...], 2)
    qk = R_HEADS * R_DK
    q = _rope(z[:, :qk], cos, sin)
    k = _rope(z[:, qk:2 * qk], cos, sin) * (R_DK ** -0.5)
    v = z[:, 2 * qk:2 * qk + R_WIDTH]
    gate = z[:, 2 * qk + R_WIDTH:]
    qdec = qdec_ref[...]
    kd = k * kdec_ref[...]
    outs = []
    for h in range(R_HEADS):
        sl = slice(h * R_DK, (h + 1) * R_DK)
        qh, kh, vh = q[:, sl], k[:, sl], v[:, sl]
        att = lax.dot_general(qh, kh, (((1,), (1,)), ((), ())), preferred_element_type=F32) * intra_ref[h]
        s = s_scr[h]
        o = jnp.dot(att, vh, preferred_element_type=F32) + jnp.dot(qh, s, preferred_element_type=F32) * qdec[:, sl]
        s_scr[h] = s * cdec_ref[h] + jnp.dot(kd[:, sl].T, vh, preferred_element_type=F32)
        outs.append(o * lax.rsqrt(jnp.mean(o * o, -1, keepdims=True) + EPS))
    o_ref[...] = jnp.concatenate(outs, axis=1) * _silu(gate)
    so_ref[...] = s_scr[...]


def _ret_tables(c):
    log_g = jnp.log1p(-jnp.exp2(-5.0 - jnp.arange(R_HEADS, dtype=F32)))
    idx = jnp.arange(c, dtype=F32)
    diff = idx[:, None] - idx[None, :]
    intra = jnp.where(diff >= 0, jnp.exp(log_g[:, None, None] * jnp.maximum(diff, 0.0)), 0.0)
    q_dec = jnp.exp(log_g[None, :] * (idx[:, None] + 1.0))
    k_dec = jnp.exp(log_g[None, :] * (c - 1.0 - idx[:, None]))
    c_dec = jnp.exp(log_g * c)
    return intra, jnp.repeat(q_dec, R_DK, axis=1), jnp.repeat(k_dec, R_DK, axis=1), c_dec


def _ret_prompt(zr, cos, sin):
    l = zr.shape[0]
    c = R_CHUNK
    intra, qdec, kdec, cdec = _ret_tables(c)
    blk = lambda w: pl.BlockSpec((c, w), lambda i: (i, 0))
    return pl.pallas_call(
        _ret_prompt_kernel, grid=(l // c,), name="ret_prompt",
        in_specs=[pl.BlockSpec(memory_space=pltpu.SMEM), blk(R_COLS), blk(LANES), blk(LANES),
                  _full((R_HEADS, c, c)), _full((c, R_HEADS * R_DK)), _full((c, R_HEADS * R_DK))],
        out_specs=[blk(R_WIDTH), _full((R_HEADS, R_DK, R_DV))],
        out_shape=[jax.ShapeDtypeStruct((l, R_WIDTH), F32), jax.ShapeDtypeStruct((R_HEADS, R_DK, R_DV), F32)],
        scratch_shapes=[pltpu.VMEM((R_HEADS, R_DK, R_DV), F32)],
        compiler_params=_cp(("arbitrary",)))(cdec, zr, cos, sin, intra, qdec, kdec)


def _rwkv_prep_kernel(zc_ref, prev_ref, shift0_ref, mu_ref, w0_ref, w2_ref, a0_ref, a2_ref, g2_ref,
                      kk_ref, ka_ref, rk_ref,
                      r_ref, w_ref, k_ref, v_ref, a_ref, b_ref, g_ref, bonus_ref, *, sequence, tm):
    zc = zc_ref[...]
    if sequence:
        row = lax.broadcasted_iota(jnp.int32, zc.shape, 0)
        boundary = jnp.where(pl.program_id(0) == 0, shift0_ref[...], prev_ref[SUBLANES - 1:SUBLANES, :])
        prev = jnp.where(row == 0, jnp.broadcast_to(boundary, zc.shape), pltpu.roll(zc, 1, axis=0))
    else:
        prev = prev_ref[...]
    zs = zc + mu_ref[...] * (prev - zc)
    o1, o2, o3 = C_WIDTH, 2 * C_WIDTH, 3 * C_WIDTH
    r, k, v = zs[:, :o1], zs[:, o1:o2], zs[:, o2:o3]
    lora = zs[:, o3:o3 + LANES]
    gl = zs[:, o3 + LANES:]
    w = -jax.nn.softplus(-(w0_ref[...] + jnp.dot(jnp.tanh(lora), w2_ref[...], preferred_element_type=F32))) - 0.5
    decay = jnp.exp(-jnp.exp(w))
    a = _sigmoid(a0_ref[...] + jnp.dot(lora, a2_ref[...], preferred_element_type=F32))
    g = jnp.dot(_sigmoid(gl), g2_ref[...], preferred_element_type=F32)
    kk = k * kk_ref[...]
    kk = kk / jnp.maximum(jnp.sqrt(_seg_sum(kk * kk)), 1e-12)
    k_mod = k * (1.0 + (a - 1.0) * ka_ref[...])
    r_ref[...] = r
    w_ref[...] = decay
    k_ref[...] = k_mod
    v_ref[...] = v
    a_ref[...] = -kk
    b_ref[...] = kk * a
    g_ref[...] = g
    bonus_ref[...] = _seg_sum(r * k_mod * rk_ref[...]) * v


def _rwkv_prep(zc, prev, shift0, cw, sequence, tm):
    t = zc.shape[0]
    row = lambda w: pl.BlockSpec((tm, w), lambda i: (i, 0))
    if sequence:
        per = tm // SUBLANES
        prev_spec = pl.BlockSpec((SUBLANES, C_PAD), lambda i: (jnp.maximum(i * per - 1, 0), 0))
        prev = zc
    else:
        prev_spec = row(C_PAD)
    vec = _full((1, C_WIDTH))
    return pl.pallas_call(
        functools.partial(_rwkv_prep_kernel, sequence=sequence, tm=tm), grid=(t // tm,), name="rwkv_prep",
        in_specs=[row(C_PAD), prev_spec, _full((1, C_PAD)), _full((1, C_PAD)), vec, _full((LANES, C_WIDTH)),
                  vec, _full((LANES, C_WIDTH)), _full((2 * LANES, C_WIDTH)), vec, vec, vec],
        out_specs=[row(C_WIDTH)] * 8,
        out_shape=[jax.ShapeDtypeStruct((t, C_WIDTH), F32)] * 8,
        compiler_params=_cp(("parallel",)))(zc, prev, shift0, cw["mu"], cw["w0"], cw["w2"], cw["a0"], cw["a2"],
                                           cw["g2"], cw["k_k"], cw["k_a"], cw["r_k"])


SCAN_BLOCK = 256
SCAN_SUB = 16
SCAN_NSUB = SCAN_BLOCK // SCAN_SUB
SCAN_UNROLL = 4


def _scan_select():
    sel = np.zeros((SCAN_SUB // 2, LANES, 2 * LANES), np.float32)
    for p in range(SCAN_SUB // 2):
        for piece in range(3):
            for rh in range(2):
                for tl in range(2):
                    row = piece * 2 * SCAN_SUB + rh * SCAN_SUB + 2 * p + tl
                    sel[p, row, tl * LANES + rh * 64: tl * LANES + (rh + 1) * 64] = 1.0
    return jnp.asarray(sel, BF16)


def _stage_colbcast(ops, base, sel_ref, xb_scr):
    for n, ref in enumerate(ops):
        xs = ref[pl.ds(base, SCAN_SUB), :]
        hi = xs.astype(BF16).astype(F32)
        r1 = xs - hi
        mid = r1.astype(BF16).astype(F32)
        lo = (r1 - mid).astype(BF16).astype(F32)
        g = jnp.concatenate([hi[:, :LANES], hi[:, LANES:], mid[:, :LANES], mid[:, LANES:],
                             lo[:, :LANES], lo[:, LANES:], jnp.zeros((LANES - 6 * SCAN_SUB, LANES), F32)], axis=0)
        lhs = g.T.astype(BF16)
        for p in range(SCAN_SUB // 2):
            out = jnp.dot(lhs, sel_ref[p], preferred_element_type=F32)
            xb_scr[n, 2 * p] = out[:, :LANES]
            xb_scr[n, 2 * p + 1] = out[:, LANES:]


def _rwkv_scan_kernel(r_ref, w_ref, k_ref, v_ref, a_ref, b_ref, g_ref, bonus_ref, lnw_ref, lnb_ref, sel_ref,
                      o_ref, so_ref, st_scr, xb_scr, vt_scr, vb_scr, yt_scr, yb_scr):
    @pl.when(pl.program_id(0) == 0)
    def _():
        st_scr[...] = jnp.zeros_like(st_scr)

    vtop, vbot = _rowpair(v_ref[...])
    vt_scr[...] = vtop
    vb_scr[...] = vbot
    ops = (a_ref, w_ref, b_ref, k_ref, r_ref)

    def sub(sc, st):
        base = pl.multiple_of(sc * SCAN_SUB, SCAN_SUB)
        _stage_colbcast(ops, base, sel_ref, xb_scr)

        def step(u, st):
            t = base + u
            st, y_t, y_b = _wkv_step(st, xb_scr[0, u], xb_scr[1, u], xb_scr[2, u], xb_scr[3, u], xb_scr[4, u],
                                     vt_scr[pl.ds(t, 1), :], vb_scr[pl.ds(t, 1), :])
            yt_scr[pl.ds(t, 1), :] = y_t
            yb_scr[pl.ds(t, 1), :] = y_b
            return st

        return lax.fori_loop(0, SCAN_SUB, step, st, unroll=SCAN_UNROLL)

    st = lax.fori_loop(0, SCAN_NSUB, sub, st_scr[...])
    st_scr[...] = st
    y = _unpair(yt_scr[...], yb_scr[...])
    o_ref[...] = _wkv_post(y, lnw_ref[...], lnb_ref[...], bonus_ref[...], g_ref[...])
    heads = _t_to_heads(st, transpose=True)
    for h in range(C_HEADS):
        so_ref[h] = heads[h]


def _rwkv_scan(prep, lnw, lnb):
    r, w, k, v, a, b, g, bonus = prep
    l = r.shape[0]
    blk = pl.BlockSpec((SCAN_BLOCK, C_WIDTH), lambda i: (i, 0))
    vec = _full((1, C_WIDTH))
    sel = _scan_select()
    return pl.pallas_call(
        _rwkv_scan_kernel, grid=(l // SCAN_BLOCK,), name="rwkv_scan",
        in_specs=[blk] * 8 + [vec, vec, _full(sel.shape)],
        out_specs=[blk, _full((C_HEADS, C_HEAD, C_HEAD))],
        out_shape=[jax.ShapeDtypeStruct((l, C_WIDTH), F32), jax.ShapeDtypeStruct((C_HEADS, C_HEAD, C_HEAD), F32)],
        scratch_shapes=[pltpu.VMEM((LANES, LANES), F32), pltpu.VMEM((5, SCAN_SUB, LANES, LANES), F32),
                        pltpu.VMEM((SCAN_BLOCK, LANES), F32), pltpu.VMEM((SCAN_BLOCK, LANES), F32),
                        pltpu.VMEM((SCAN_BLOCK, LANES), F32), pltpu.VMEM((SCAN_BLOCK, LANES), F32)],
        compiler_params=_cp(("arbitrary",)))(r, w, k, v, a, b, g, bonus, lnw, lnb, sel)


def _sample_kernel(sinks_ref, gam_ref, za_ref, zr_ref, r_ref, w_ref, k_ref, v_ref, a_ref, b_ref, g_ref, bonus_ref,
                   kc_ref, vc_ref, sret_ref, swkv_ref, cos_ref, sin_ref, lnw_ref, lnb_ref,
                   oa_ref, or_ref, oc_ref, kco_ref, vco_ref, sreto_ref, swkvo_ref):
    cos, sin = cos_ref[...], sin_ref[...]
    lane = lax.broadcasted_iota(jnp.int32, (1, LANES), 1)
    row8 = lax.broadcasted_iota(jnp.int32, (SUBLANES, LANES), 0)

    za = za_ref[0]
    q = _rope(za[:, :A_WIDTH], _tile_lanes(cos, 4), _tile_lanes(sin, 4))
    knew = _rope(za[:, A_WIDTH:A_WIDTH + A_KV_WIDTH], cos, sin)
    vnew = za[:, A_WIDTH + A_KV_WIDTH:]
    kc, vc = kc_ref[0], vc_ref[0]
    g = A_HEADS // A_KV_HEADS
    zero = jnp.zeros((1, HEAD_DIM), F32)
    qrows = []
    for h in range(A_HEADS):
        qh = q[:, h * HEAD_DIM:(h + 1) * HEAD_DIM]
        qrows.append(jnp.concatenate([qh, zero] if h // g == 0 else [zero, qh], axis=1))
    qm = jnp.concatenate(qrows + [jnp.zeros((LANES - A_HEADS, LANES), F32)], axis=0)
    scale = HEAD_DIM ** -0.5
    s = lax.dot_general(kc, qm, (((1,), (1,)), ((), ())), preferred_element_type=F32) * scale
    kn8 = jnp.where(row8 == 0, jnp.broadcast_to(knew, (SUBLANES, LANES)), 0.0)
    s_new = lax.dot_general(kn8, qm, (((1,), (1,)), ((), ())), preferred_element_type=F32)[0:1] * scale
    sink = jnp.zeros((1, LANES), F32)
    for h in range(A_HEADS):
        sink = jnp.where(lane == h, sinks_ref[h], sink)
    m = jnp.maximum(jnp.maximum(jnp.max(s, axis=0, keepdims=True), s_new), sink)
    p = jnp.exp(s - m)
    p_new = jnp.exp(s_new - m)
    denom = jnp.sum(p, axis=0, keepdims=True) + p_new + jnp.exp(sink - m)
    p = p / denom
    p_new = p_new / denom
    pn_col = jnp.broadcast_to(p_new, (LANES, LANES)).T[:, 0:1]
    o_full = jnp.dot(p.T, vc, preferred_element_type=F32) + pn_col * vnew
    oa_ref[0] = jnp.concatenate(
        [o_full[h:h + 1, (h // g) * HEAD_DIM:(h // g + 1) * HEAD_DIM] for h in range(A_HEADS)], axis=1)
    rowk = lax.broadcasted_iota(jnp.int32, kc.shape, 0)
    last = rowk == kc.shape[0] - 1
    kco_ref[0] = jnp.where(last, jnp.broadcast_to(knew, kc.shape), pltpu.roll(kc, kc.shape[0] - 1, axis=0))
    vco_ref[0] = jnp.where(last, jnp.broadcast_to(vnew, vc.shape), pltpu.roll(vc, vc.shape[0] - 1, axis=0))

    zr = zr_ref[0]
    qk = R_HEADS * R_DK
    cos2, sin2 = _tile_lanes(cos, 2), _tile_lanes(sin, 2)
    rq = _rope(zr[:, :qk], cos2, sin2)
    rk = _rope(zr[:, qk:2 * qk], cos2, sin2) * (R_DK ** -0.5)
    rv = zr[:, 2 * qk:2 * qk + R_WIDTH]
    rg = zr[:, 2 * qk + R_WIDTH:]
    st = _heads_to_t(sret_ref[0], transpose=False)
    rr = lax.broadcasted_iota(jnp.int32, (LANES, LANES), 0) >= 64
    cc = lax.broadcasted_iota(jnp.int32, (LANES, LANES), 1) >= 64
    gamma = jnp.where(rr, jnp.where(cc, gam_ref[3], gam_ref[1]), jnp.where(cc, gam_ref[2], gam_ref[0]))
    vt, vb = _rowpair(rv)
    st = st * gamma + _colbcast(rk) * _rowbcast(vt, vb)
    o_t, o_b = _halfsums(st * _colbcast(rq))
    o = _unpair(o_t, o_b)
    o = o * lax.rsqrt(_seg_sum(o * o) * (1.0 / R_DV) + EPS)
    or_ref[0] = o * _silu(rg)
    heads = _t_to_heads(st, transpose=False)
    for h in range(R_HEADS):
        sreto_ref[0, h] = heads[h]

    st = _heads_to_t(swkv_ref[0], transpose=True)
    vt, vb = _rowpair(v_ref[0])
    st, y_t, y_b = _wkv_step(st, _colbcast(a_ref[0]), _colbcast(w_ref[0]), _colbcast(b_ref[0]),
                             _colbcast(k_ref[0]), _colbcast(r_ref[0]), vt, vb)
    oc_ref[0] = _wkv_post(_unpair(y_t, y_b), lnw_ref[...], lnb_ref[...], bonus_ref[0], g_ref[0])
    heads = _t_to_heads(st, transpose=True)
    for h in range(C_HEADS):
        swkvo_ref[0, h] = heads[h]


def _sample_mixers(za, zr, prep, kc, vc, sret, swkv, cos, sin, sinks, lnw, lnb):
    b = za.shape[0]
    wb = kc.shape[1]
    assert wb <= WINDOW and PAST_LEN >= wb
    gam = jnp.exp(jnp.log1p(-jnp.exp2(-5.0 - jnp.arange(R_HEADS, dtype=F32))) * 1.0)
    tok = lambda w: pl.BlockSpec((1, 1, w), lambda i: (i, 0, 0))
    cache = pl.BlockSpec((1, wb, A_KV_WIDTH), lambda i: (i, 0, 0))
    state = pl.BlockSpec((1, 4, 64, 64), lambda i: (i, 0, 0, 0))
    smem = pl.BlockSpec(memory_space=pltpu.SMEM)
    vec = _full((1, C_WIDTH))
    r3 = lambda x: x.reshape(b, 1, x.shape[-1])
    outs = pl.pallas_call(
        _sample_kernel, grid=(b,), name="sample_mixers",
        in_specs=[smem, smem, tok(A_COLS), tok(R_COLS)] + [tok(C_WIDTH)] * 8 + [cache, cache, state, state,
                  _full((1, LANES)), _full((1, LANES)), vec, vec],
        out_specs=[tok(A_WIDTH), tok(R_WIDTH), tok(C_WIDTH), cache, cache, state, state],
        out_shape=[jax.ShapeDtypeStruct((b, 1, A_WIDTH), F32), jax.ShapeDtypeStruct((b, 1, R_WIDTH), F32),
                   jax.ShapeDtypeStruct((b, 1, C_WIDTH), F32), jax.ShapeDtypeStruct(kc.shape, F32),
                   jax.ShapeDtypeStruct(vc.shape, F32), jax.ShapeDtypeStruct(sret.shape, F32),
                   jax.ShapeDtypeStruct(swkv.shape, F32)],
        compiler_params=_cp(("parallel",)))(sinks, gam, r3(za), r3(zr), *[r3(x) for x in prep],
                                           kc, vc, sret, swkv, cos, sin, lnw, lnb)
    oa, orr, oc = (x.reshape(b, x.shape[-1]) for x in outs[:3])
    return (oa, orr, oc) + tuple(outs[3:])


def _outproj_kernel(oa_ref, or_ref, oc_ref, x_ref, wo_ref, gf_ref, wrt_ref, brt_ref, *rest, tm):
    x1_ref, h2_ref, route_ref = rest[-3:]
    mix =jnp.concatenate([oa_ref[...], or_ref[...], oc_ref[...]], axis=1).astype(BF16)
    x1 = x_ref[...] + jnp.dot(mix, wo_ref[...], preferred_element_type=F32)
    x1_ref[...] = x1
    h2 = _rmsnorm(x1, gf_ref[...])
    for s in range(ROW_TILES):
        h2_ref[pl.ds(s, tm, stride=ROW_TILES), :] = h2[:, s * LANES:(s + 1) * LANES]
    logits = jnp.dot(h2, wrt_ref[...], preferred_element_type=F32, precision=lax.Precision.HIGHEST) + brt_ref[...]
    lane = lax.broadcasted_iota(jnp.int32, logits.shape, 1).astype(F32)
    big = float(LANES)
    is_g = lane < N_GROUPS
    lg = jnp.where(is_g, logits, -jnp.inf)
    m_g = jnp.max(lg, -1, keepdims=True)
    grp = jnp.min(jnp.where(lg == m_g, lane, big), -1, keepdims=True)
    p_grp = 1.0 / jnp.sum(jnp.where(is_g, jnp.exp(lg - m_g), 0.0), -1, keepdims=True)
    lo = N_GROUPS + EXPERTS_PER_GROUP * grp
    in_grp = (lane >= lo) & (lane < lo + EXPERTS_PER_GROUP)
    le = jnp.where(in_grp, logits, -jnp.inf)
    l1 = jnp.max(le, -1, keepdims=True)
    i1 = jnp.min(jnp.where(le == l1, lane, big), -1, keepdims=True)
    le2 = jnp.where(lane == i1, -jnp.inf, le)
    l2 = jnp.max(le2, -1, keepdims=True)
    i2 = jnp.min(jnp.where(le2 == l2, lane, big), -1, keepdims=True)
    e = jnp.exp(l2 - l1)
    g1 = p_grp / (1.0 + e)
    g2 = p_grp * e / (1.0 + e)
    route = jnp.where(lane == 0, i1 - N_GROUPS, jnp.where(lane == 1, i2 - N_GROUPS,
                      jnp.where(lane == 2, g1, jnp.where(lane == 3, g2, 0.0))))
    route_ref[...] = route


def _outproj(oa, orr, oc, x, wo, gf, wrt, brt, tm, t_all, first_tok, shared=None):
    t = x.shape[0]
    off = first_tok // tm
    nblk = t // tm
    steps = nblk + (1 if shared is None else 0)
    assert shared is not None or t_all == t + tm
    row = lambda w: pl.BlockSpec((tm, w), lambda i: (jnp.minimum(i, nblk - 1), 0))
    in_specs = [row(A_WIDTH), row(R_WIDTH), row(C_WIDTH), row(D_MODEL), _full(wo.shape), _full((1, D_MODEL)),
                _full(wrt.shape), _full((1, LANES))]
    args = [oa, orr, oc, x, wo, gf, wrt, brt]
    aliases = {}
    if shared is not None:
        aliases = {len(args): 1, len(args) + 1: 2}
        in_specs += [pl.BlockSpec(memory_space=pl.ANY)] * 2
        args += list(shared)
    return pl.pallas_call(
        functools.partial(_outproj_kernel, tm=tm), grid=(steps,), name="outproj_router",
        in_specs=in_specs,
        out_specs=[row(D_MODEL), pl.BlockSpec((tm * ROW_TILES, LANES), lambda i: (i + off, 0)),
                   pl.BlockSpec((tm, LANES), lambda i: (i + off, 0))],
        out_shape=[jax.ShapeDtypeStruct((t, D_MODEL), F32), jax.ShapeDtypeStruct((t_all * ROW_TILES, LANES), F32),
                   jax.ShapeDtypeStruct((t_all, LANES), F32)],
        input_output_aliases=aliases,
        compiler_params=_cp(("arbitrary",)))(*args)


def _moe_kernel(blk_exp_ref, src_ref, dst_ref, nused_ref, h2_hbm, wg_ref, wu_ref, wd_ref, yw_hbm,
                gbuf, sbuf, xbuf, wgb, wub, wdb, gsem, ssem):
    i = pl.program_id(0)
    nb = pl.num_programs(0)
    nused = nused_ref[0]
    slot = i % 2

    def tile(off):
        return pl.ds(pl.multiple_of(off, ROW_TILES), ROW_TILES)

    def gather(blk, sl):
        def body(r, _):
            pltpu.make_async_copy(h2_hbm.at[tile(src_ref[blk * MOE_ROWS + r]), :],
                                  gbuf.at[sl, tile(r * ROW_TILES), :], gsem.at[sl]).start()
            return 0
        lax.fori_loop(0, MOE_ROWS, body, 0, unroll=DMA_UNROLL)

    def gather_wait(sl):
        def body(r, _):
            pltpu.make_async_copy(h2_hbm.at[tile(0), :], gbuf.at[sl, tile(0), :], gsem.at[sl]).wait()
            return 0
        lax.fori_loop(0, MOE_ROWS, body, 0, unroll=DMA_UNROLL)

    def scatter(blk):
        def body(r, _):
            pltpu.make_async_copy(sbuf.at[tile(r * ROW_TILES), :],
                                  yw_hbm.at[tile(dst_ref[blk * MOE_ROWS + r]), :], ssem).start()
            return 0
        lax.fori_loop(0, MOE_ROWS, body, 0, unroll=DMA_UNROLL)

    def scatter_wait():
        def body(r, _):
            pltpu.make_async_copy(sbuf.at[tile(0), :], yw_hbm.at[tile(0), :], ssem).wait()
            return 0
        lax.fori_loop(0, MOE_ROWS, body, 0, unroll=DMA_UNROLL)

    @pl.when((i == 0) & (nused > 0))
    def _():
        gather(0, 0)

    @pl.when(i < nused)
    def _():
        @pl.when((i == 0) | (blk_exp_ref[i] != blk_exp_ref[jnp.maximum(i - 1, 0)]))
        def _():
            wgb[...] = wg_ref[0].astype(BF16)
            wub[...] = wu_ref[0].astype(BF16)
            wdb[...] = wd_ref[0].astype(BF16)

        gather_wait(slot)

        @pl.when(i + 1 < nused)
        def _():
            gather(i + 1, 1 - slot)

        for s in range(ROW_TILES):
            xbuf[:, s * LANES:(s + 1) * LANES] = gbuf[slot, pl.ds(s, MOE_ROWS, stride=ROW_TILES), :].astype(BF16)
        x = xbuf[...]
        hid = _silu(jnp.dot(x, wgb[...], preferred_element_type=F32)) * jnp.dot(x, wub[...], preferred_element_type=F32)
        y = jnp.dot(hid.astype(BF16), wdb[...], preferred_element_type=F32)

        @pl.when(i > 0)
        def _():
            scatter_wait()

        for s in range(ROW_TILES):
            sbuf[pl.ds(s, MOE_ROWS, stride=ROW_TILES), :] = y[:, s * LANES:(s + 1) * LANES]
        scatter(i)

    @pl.when((i == nb - 1) & (nused > 0))
    def _():
        scatter_wait()
        spare = pltpu.make_async_copy(sbuf, yw_hbm.at[pl.ds(yw_hbm.shape[0] - MOE_ROWS * ROW_TILES,
                                                            MOE_ROWS * ROW_TILES), :], ssem)
        spare.start()
        spare.wait()


def _moe(h2, route, t, wg, wu, wd):
    na = t * TOP_K
    assert na % MOE_ROWS == 0 and na < (1 << IDX_BITS)
    flat_e = route[:t, :TOP_K].astype(jnp.int32).reshape(na)
    experts = jnp.arange(N_EXPERTS, dtype=jnp.int32)
    counts = jnp.sum(flat_e[:, None] == experts[None, :], axis=0, dtype=jnp.int32)
    npad = (-counts) % MOE_ROWS
    real_keys = (flat_e << (IDX_BITS + 1)) | jnp.arange(na, dtype=jnp.int32)
    q = jnp.arange(MOE_ROWS, dtype=jnp.int32)[None, :]
    pad_keys = jnp.where(q < npad[:, None], (experts[:, None] << (IDX_BITS + 1)) | (1 << IDX_BITS) | q,
                         jnp.iinfo(jnp.int32).max)
    keys = jnp.sort(jnp.concatenate([real_keys, pad_keys.reshape(-1)]))
    n_rows = na + N_EXPERTS * MOE_ROWS
    n_blocks = n_rows // MOE_ROWS
    is_pad = ((keys >> IDX_BITS) & 1) == 1
    idx = keys & ((1 << IDX_BITS) - 1)
    pos = jnp.arange(n_rows, dtype=jnp.int32)
    src = jnp.where(is_pad, 0, (idx // TOP_K) * ROW_TILES).astype(jnp.int32)
    dst = jnp.where(is_pad, na + pos % MOE_ROWS, idx).astype(jnp.int32) * ROW_TILES
    blk_exp = jnp.minimum(keys[::MOE_ROWS] >> (IDX_BITS + 1), N_EXPERTS - 1).astype(jnp.int32)
    nused = (jnp.sum(counts + npad) // MOE_ROWS).astype(jnp.int32).reshape(1)
    wspec = lambda a, b: pl.BlockSpec((1, a, b), lambda i, be, s, d, nu: (be[i], 0, 0))
    return pl.pallas_call(
        _moe_kernel, name="moe_experts",
        grid_spec=pltpu.PrefetchScalarGridSpec(
            num_scalar_prefetch=4, grid=(n_blocks,),
            in_specs=[pl.BlockSpec(memory_space=pl.ANY), wspec(D_MODEL, EXPERT_FF), wspec(D_MODEL, EXPERT_FF),
                      wspec(EXPERT_FF, D_MODEL)],
            out_specs=pl.BlockSpec(memory_space=pl.ANY),
            scratch_shapes=[pltpu.VMEM((2, MOE_ROWS * ROW_TILES, LANES), F32),
                            pltpu.VMEM((MOE_ROWS * ROW_TILES, LANES), F32),
                            pltpu.VMEM((MOE_ROWS, D_MODEL), BF16),
                            pltpu.VMEM((D_MODEL, EXPERT_FF), BF16), pltpu.VMEM((D_MODEL, EXPERT_FF), BF16),
                            pltpu.VMEM((EXPERT_FF, D_MODEL), BF16),
                            pltpu.SemaphoreType.DMA((2,)), pltpu.SemaphoreType.DMA(())]),
        out_shape=jax.ShapeDtypeStruct(((na + MOE_ROWS) * ROW_TILES, LANES), F32),
        compiler_params=_cp(("arbitrary",)))(blk_exp, src, dst, nused, h2, wg, wu, wd)


def _rope_tables(pos):
    half = HEAD_DIM // 2
    inv = ROPE_THETA ** (-jnp.arange(half, dtype=F32) / half)
    ang = pos.astype(F32)[:, None] * inv[None, :]
    c, s = jnp.cos(ang), jnp.sin(ang)
    return jnp.concatenate([c, c, c, c], axis=1), jnp.concatenate([-s, s, -s, s], axis=1)


def _pad_cols(w, n):
    return jnp.pad(w, ((0, 0), (0, n - w.shape[1])))


def _pad_rows(w, lo, n):
    return jnp.pad(w, ((lo, n - lo - w.shape[0]), (0, 0)))


def kernel(x_prompt, x_sample, cache_swa_k, cache_swa_v, state_ret, state_wkv, state_shift, norm_mix, w_in, sinks,
           rwkv_mu, rwkv_w0, rwkv_w2, rwkv_a0, rwkv_a2, rwkv_g2, rwkv_k_k, rwkv_k_a, rwkv_r_k, rwkv_ln_w, rwkv_ln_b,
           w_out, norm_ffn, router_g, router_g_b, router_e, router_e_b, expert_w_gate, expert_w_up, expert_w_down,
           norm_final):
    lp = x_prompt.shape[1]
    bs = x_sample.shape[0]
    wb = cache_swa_k.shape[2]
    tm_p = 512
    xp = x_prompt.reshape(lp, D_MODEL)
    xs = x_sample.reshape(bs, D_MODEL)
    cos_p, sin_p = _rope_tables(jnp.arange(lp, dtype=jnp.int32))
    cos_s, sin_s = _rope_tables(PAST_LEN + jnp.arange(1, dtype=jnp.int32))
    outs_p = [[] for _ in range(5)]
    outs_s = [[] for _ in range(5)]
    moe_p = moe_s = None
    row = lambda v: v.reshape(1, -1)
    for i in range(DEPTH):
        wi = w_in[i].astype(BF16)
        wa, wr, wc = wi[:, :A_COLS], wi[:, A_COLS:A_COLS + R_COLS], _pad_cols(wi[:, A_COLS + R_COLS:], C_PAD)
        cw = {
            "mu": _pad_cols(row(rwkv_mu[i]), C_PAD), "w0": row(rwkv_w0[i]), "a0": row(rwkv_a0[i]),
            "w2": _pad_rows(rwkv_w2[i], 0, LANES), "a2": _pad_rows(rwkv_a2[i], C_DECAY_LORA, LANES),
            "g2": _pad_rows(rwkv_g2[i], 0, 2 * LANES),
            "k_k": row(rwkv_k_k[i]), "k_a": row(rwkv_k_a[i]), "r_k": row(rwkv_r_k[i]),
        }
        lnw, lnb = row(rwkv_ln_w[i]), row(rwkv_ln_b[i])
        g_mix, g_ffn = row(norm_mix[i]), row(norm_ffn[i])
        wo = w_out[i].astype(BF16)
        wrt = _pad_cols(jnp.concatenate([router_g[i], router_e[i]], axis=1), LANES)
        brt = _pad_cols(row(jnp.concatenate([router_g_b[i], router_e_b[i]])), LANES)

        xn, za, zr, zc = _inproj(xp, moe_p, g_mix, wa, wr, wc, tm_p)
        xp = xp if xn is None else xn
        oa, k_last = _swa_prompt(za, cos_p, sin_p, sinks[i])
        orr, s_ret = _ret_prompt(zr, cos_p, sin_p)
        prep = _rwkv_prep(zc, None, jnp.zeros((1, C_PAD), F32), cw, True, tm_p)
        oc, s_wkv = _rwkv_scan(prep, lnw, lnb)
        xp, h2, route = _outproj(oa, orr, oc, xp, wo, g_ffn, wrt, brt, tm_p, lp + tm_p, 0)
        nk = min(WINDOW, lp)
        outs_p[0].append(k_last.reshape(1, nk, A_KV_HEADS, HEAD_DIM))
        outs_p[1].append(za[lp - nk:, A_WIDTH + A_KV_WIDTH:].reshape(1, nk, A_KV_HEADS, HEAD_DIM))
        outs_p[2].append(s_ret[None])
        outs_p[3].append(s_wkv[None])
        outs_p[4].append(zc[lp - 1:, :C_COLS])

        xn, za, zr, zc = _inproj(xs, moe_s, g_mix, wa, wr, wc, bs)
        xs = xs if xn is None else xn
        prep = _rwkv_prep(zc, _pad_cols(state_shift[i], C_PAD), jnp.zeros((1, C_PAD), F32), cw, False, bs)
        oa, orr, oc, kc_new, vc_new, sret_new, swkv_new = _sample_mixers(
            za, zr, prep, cache_swa_k[i].reshape(bs, wb, A_KV_WIDTH), cache_swa_v[i].reshape(bs, wb, A_KV_WIDTH),
            state_ret[i], state_wkv[i], cos_s, sin_s, sinks[i], lnw, lnb)
        xs, h2, route = _outproj(oa, orr, oc, xs, wo, g_ffn, wrt, brt, bs, lp + tm_p, lp, shared=(h2, route))
        yw = _moe(h2, route, lp + bs, expert_w_gate[i], expert_w_up[i], expert_w_down[i])
        moe_p, moe_s = (yw, route, 0), (yw, route, lp)
        outs_s[0].append(kc_new.reshape(bs, wb, A_KV_HEADS, HEAD_DIM))
        outs_s[1].append(vc_new.reshape(bs, wb, A_KV_HEADS, HEAD_DIM))
        outs_s[2].append(sret_new)
        outs_s[3].append(swkv_new)
        outs_s[4].append(zc[:, :C_COLS])

    gfin = row(norm_final)
    y_prompt = _final(xp, moe_p, gfin, tm_p).reshape(x_prompt.shape)
    y_sample = _final(xs, moe_s, gfin, bs).reshape(x_sample.shape)
    sp = [jnp.stack(o) for o in outs_p]
    ss = [jnp.stack(o) for o in outs_s]
    return (y_prompt, y_sample, sp[0], sp[1], sp[2], sp[3], sp[4], ss[0], ss[1], ss[2], ss[3], ss[4])
```

```python
import functools

import numpy as np
import jax
import jax.numpy as jnp
from jax import lax
from jax.experimental import pallas as pl
from jax.experimental.pallas import tpu as pltpu

F32 = jnp.float32
BF16 = jnp.bfloat16

D_MODEL = 1024
DEPTH = 2
PAST_LEN = 16384
A_HEADS, A_KV_HEADS, HEAD_DIM, WINDOW = 8, 2, 64, 128
ROPE_THETA = 10000.0
A_WIDTH = A_HEADS * HEAD_DIM
A_KV_WIDTH = A_KV_HEADS * HEAD_DIM
A_COLS = A_WIDTH + 2 * A_KV_WIDTH
R_HEADS, R_DK, R_DV, R_CHUNK = 4, 64, 64, 128
R_WIDTH = R_HEADS * R_DV
R_COLS = 2 * R_HEADS * R_DK + 2 * R_WIDTH
C_HEADS, C_HEAD = 4, 64
C_WIDTH = C_HEADS * C_HEAD
C_DECAY_LORA, C_ICLR_LORA, C_GATE_LORA = 64, 64, 160
C_COLS = 3 * C_WIDTH + C_DECAY_LORA + C_ICLR_LORA + C_GATE_LORA
C_PAD = 1152
C_GN_EPS = 64e-5
N_GROUPS, EXPERTS_PER_GROUP, TOP_K, EXPERT_FF = 4, 8, 2, 512
N_EXPERTS = N_GROUPS * EXPERTS_PER_GROUP
EPS = 1e-6

LANES = 128
SUBLANES = 8
ROW_TILES = D_MODEL // LANES
MOE_ROWS = 256
DMA_UNROLL = 8
IDX_BITS = 19
VMEM_LIMIT = 56 * 1024 * 1024


def _cp(sem, vmem=VMEM_LIMIT):
    return pltpu.CompilerParams(dimension_semantics=sem, vmem_limit_bytes=vmem)


def _full(shape):
    n = len(shape)
    return pl.BlockSpec(shape, lambda *_: (0,) * n)


def _rmsnorm(x, g):
    return x * lax.rsqrt(jnp.mean(x * x, -1, keepdims=True) + EPS) * g


def _silu(x):
    return x * (1.0 / (1.0 + jnp.exp(-x)))


def _sigmoid(x):
    return 1.0 / (1.0 + jnp.exp(-x))


def _rope(x, cos, sin_signed):
    w = x.shape[-1]
    lane = lax.broadcasted_iota(jnp.int32, x.shape, x.ndim - 1)
    first = (lane % HEAD_DIM) < (HEAD_DIM // 2)
    swapped = jnp.where(first, pltpu.roll(x, w - HEAD_DIM // 2, axis=x.ndim - 1),
                        pltpu.roll(x, HEAD_DIM // 2, axis=x.ndim - 1))
    return x * cos + swapped * sin_signed


def _tile_lanes(x, n):
    return jnp.concatenate([x] * n, axis=-1) if n > 1 else x


def _seg_sum(x, seg=64):
    outs = []
    for h in range(x.shape[-1] // seg):
        s = jnp.sum(x[:, h * seg:(h + 1) * seg], axis=-1, keepdims=True)
        outs.append(jnp.broadcast_to(s, (x.shape[0], seg)))
    return jnp.concatenate(outs, axis=-1)


def _read_rows(ref, tm, first, stride):
    return jnp.concatenate([ref[pl.ds(first + s, tm, stride=stride), :] for s in range(ROW_TILES)], axis=1)


def _colbcast(x):
    z = jnp.concatenate([jnp.broadcast_to(x[:, :LANES], (64, LANES)),
                         jnp.broadcast_to(x[:, LANES:], (64, LANES))], axis=0)
    return z.T


def _rowpair(x):
    top = jnp.concatenate([x[:, 0:64], x[:, 128:192]], axis=1)
    bot = jnp.concatenate([x[:, 64:128], x[:, 192:256]], axis=1)
    return top, bot


def _unpair(top, bot):
    return jnp.concatenate([top[:, :64], bot[:, :64], top[:, 64:], bot[:, 64:]], axis=1)


def _rowbcast(top, bot):
    return jnp.concatenate([jnp.broadcast_to(top, (64, LANES)), jnp.broadcast_to(bot, (64, LANES))], axis=0)


def _halfsums(p):
    return jnp.sum(p[:64], axis=0, keepdims=True), jnp.sum(p[64:], axis=0, keepdims=True)


def _wkv_step(st, ab, wb, bb, kb, rb, vtop, vbot):
    sa_t, sa_b = _halfsums(st * ab)
    st = st * wb + _rowbcast(sa_t, sa_b) * bb + _rowbcast(vtop, vbot) * kb
    y_t, y_b = _halfsums(st * rb)
    return st, y_t, y_b


def _wkv_post(y, lnw, lnb, bonus, g):
    mu = _seg_sum(y) * (1.0 / C_HEAD)
    d = y - mu
    var = _seg_sum(d * d) * (1.0 / C_HEAD)
    yn = d * lax.rsqrt(var + C_GN_EPS)
    return (yn * lnw + lnb + bonus) * g


def _heads_to_t(s4, transpose):
    if transpose:
        m = jnp.concatenate([jnp.concatenate([s4[0], s4[1]], axis=1),
                             jnp.concatenate([s4[2], s4[3]], axis=1)], axis=0)
        return m.T
    return jnp.concatenate([jnp.concatenate([s4[0], s4[2]], axis=1),
                            jnp.concatenate([s4[1], s4[3]], axis=1)], axis=0)


def _t_to_heads(st, transpose):
    if transpose:
        m = st.T
        return [m[0:64, 0:64], m[0:64, 64:128], m[64:128, 0:64], m[64:128, 64:128]]
    return [st[0:64, 0:64], st[64:128, 0:64], st[0:64, 64:128], st[64:128, 64:128]]


def _inproj_kernel(*refs, combine, tm):
    if combine:
        x_ref, yw_ref, route_ref, g_ref, wa_ref, wr_ref, wc_ref, xo_ref, za_ref, zr_ref, zc_ref = refs
        route = route_ref[...]
        y0 = _read_rows(yw_ref, tm, 0, 2 * ROW_TILES)
        y1 = _read_rows(yw_ref, tm, ROW_TILES, 2 * ROW_TILES)
        x = x_ref[...] + (route[:, 2:3] * y0 + route[:, 3:4] * y1)
        xo_ref[...] = x
    else:
        x_ref, g_ref, wa_ref, wr_ref, wc_ref, za_ref, zr_ref, zc_ref = refs
        x = x_ref[...]
    h = _rmsnorm(x, g_ref[...]).astype(BF16)
    za_ref[...] = jnp.dot(h, wa_ref[...], preferred_element_type=F32)
    zr_ref[...] = jnp.dot(h, wr_ref[...], preferred_element_type=F32)
    zc_ref[...] = jnp.dot(h, wc_ref[...], preferred_element_type=F32)


def _inproj(x, moe, g, wa, wr, wc, tm):
    t = x.shape[0]
    row = lambda w: pl.BlockSpec((tm, w), lambda i: (i, 0))
    wspecs = [_full((1, D_MODEL)), _full(wa.shape), _full(wr.shape), _full(wc.shape)]
    zshapes = [jax.ShapeDtypeStruct((t, A_COLS), F32), jax.ShapeDtypeStruct((t, R_COLS), F32),
               jax.ShapeDtypeStruct((t, C_PAD), F32)]
    zspecs = [row(A_COLS), row(R_COLS), row(C_PAD)]
    if moe is None:
        return (None,) + tuple(pl.pallas_call(
            functools.partial(_inproj_kernel, combine=False, tm=tm), name="inproj",
            grid=(t // tm,), in_specs=[row(D_MODEL)] + wspecs, out_specs=zspecs, out_shape=zshapes,
            compiler_params=_cp(("parallel",)))(x, g, wa, wr, wc))
    yw, route, first_tok = moe
    off = first_tok // tm
    return pl.pallas_call(
        functools.partial(_inproj_kernel, combine=True, tm=tm), name="combine_inproj",
        grid=(t // tm,),
        in_specs=[row(D_MODEL), pl.BlockSpec((tm * 2 * ROW_TILES, LANES), lambda i: (i + off, 0)),
                  pl.BlockSpec((tm, LANES), lambda i: (i + off, 0))] + wspecs,
        out_specs=[row(D_MODEL)] + zspecs,
        out_shape=[jax.ShapeDtypeStruct((t, D_MODEL), F32)] + zshapes,
        compiler_params=_cp(("parallel",)))(x, yw, route, g, wa, wr, wc)


def _final_kernel(x_ref, yw_ref, route_ref, g_ref, o_ref, *, tm):
    route = route_ref[...]
    y0 = _read_rows(yw_ref, tm, 0, 2 * ROW_TILES)
    y1 = _read_rows(yw_ref, tm, ROW_TILES, 2 * ROW_TILES)
    x = x_ref[...] + (route[:, 2:3] * y0 + route[:, 3:4] * y1)
    o_ref[...] = _rmsnorm(x, g_ref[...])


def _final(x, moe, g, tm):
    t = x.shape[0]
    yw, route, first_tok = moe
    off = first_tok // tm
    row = lambda w: pl.BlockSpec((tm, w), lambda i: (i, 0))
    return pl.pallas_call(
        functools.partial(_final_kernel, tm=tm), grid=(t // tm,), name="combine_final",
        in_specs=[row(D_MODEL), pl.BlockSpec((tm * 2 * ROW_TILES, LANES), lambda i: (i + off, 0)),
                  pl.BlockSpec((tm, LANES), lambda i: (i + off, 0)), _full((1, D_MODEL))],
        out_specs=row(D_MODEL), out_shape=jax.ShapeDtypeStruct((t, D_MODEL), F32),
        compiler_params=_cp(("parallel",)))(x, yw, route, g)


def _swa_prompt_kernel(sinks_ref, q_ref, kc_ref, kp_ref, vc_ref, vp_ref, cosc_ref, sinc_ref, cosp_ref, sinp_ref,
                       o_ref, ko_ref):
    i = pl.program_id(0)
    cosc, sinc = cosc_ref[...], sinc_ref[...]
    q = _rope(q_ref[...], _tile_lanes(cosc, 4), _tile_lanes(sinc, 4))
    kc = _rope(kc_ref[...], cosc, sinc)
    kp = _rope(kp_ref[...], cosp_ref[...], sinp_ref[...])
    ko_ref[...] = kc
    vc, vp = vc_ref[...], vp_ref[...]
    g = A_HEADS // A_KV_HEADS
    nq = g * WINDOW
    r = lax.broadcasted_iota(jnp.int32, (nq, 2 * WINDOW), 0) % WINDOW
    c = lax.broadcasted_iota(jnp.int32, (nq, 2 * WINDOW), 1)
    ok = (c >= r) & (c <= r + WINDOW) & ((c >= WINDOW) | (i > 0))
    heads = []
    for kv in range(A_KV_HEADS):
        lo = kv * HEAD_DIM
        q4 = jnp.concatenate([q[:, (kv * g + j) * HEAD_DIM:(kv * g + j + 1) * HEAD_DIM] for j in range(g)], axis=0)
        kk = jnp.concatenate([kp[:, lo:lo + HEAD_DIM], kc[:, lo:lo + HEAD_DIM]], axis=0)
        vv = jnp.concatenate([vp[:, lo:lo + HEAD_DIM], vc[:, lo:lo + HEAD_DIM]], axis=0)
        s = lax.dot_general(q4, kk, (((1,), (1,)), ((), ())), preferred_element_type=F32) * (HEAD_DIM ** -0.5)
        s = jnp.where(ok, s, -jnp.inf)
        sink = jnp.concatenate([jnp.full((WINDOW, 1), sinks_ref[kv * g + j], F32) for j in range(g)], axis=0)
        m = jnp.maximum(jnp.max(s, -1, keepdims=True), sink)
        p = jnp.exp(s - m)
        p = p / (jnp.sum(p, -1, keepdims=True) + jnp.exp(sink - m))
        o4 = jnp.dot(p, vv, preferred_element_type=F32)
        heads += [o4[j * WINDOW:(j + 1) * WINDOW] for j in range(g)]
    o_ref[...] = jnp.concatenate(heads, axis=1)


def _swa_prompt(za, cos, sin, sinks):
    l = za.shape[0]
    nb = l // WINDOW
    cur = lambda w, c: pl.BlockSpec((WINDOW, w), lambda i: (i, c))
    prv = lambda w, c: pl.BlockSpec((WINDOW, w), lambda i: (jnp.maximum(i - 1, 0), c))
    kcol, vcol = A_WIDTH // A_KV_WIDTH, A_WIDTH // A_KV_WIDTH + 1
    return pl.pallas_call(
        _swa_prompt_kernel, grid=(nb,), name="swa_prompt",
        in_specs=[pl.BlockSpec(memory_space=pltpu.SMEM),
                  cur(A_WIDTH, 0), cur(A_KV_WIDTH, kcol), prv(A_KV_WIDTH, kcol),
                  cur(A_KV_WIDTH, vcol), prv(A_KV_WIDTH, vcol),
                  cur(LANES, 0), cur(LANES, 0), prv(LANES, 0), prv(LANES, 0)],
        out_specs=[cur(A_WIDTH, 0), _full((WINDOW, A_KV_WIDTH))],
        out_shape=[jax.ShapeDtypeStruct((l, A_WIDTH), F32), jax.ShapeDtypeStruct((WINDOW, A_KV_WIDTH), F32)],
        compiler_params=_cp(("arbitrary",)))(sinks, za, za, za, za, za, cos, sin, cos, sin)


def _ret_prompt_kernel(cdec_ref, zr_ref, cos_ref, sin_ref, intra_ref, qdec_ref, kdec_ref, o_ref, so_ref, s_scr):
    @pl.when(pl.program_id(0) == 0)
    def _():
        s_scr[...] = jnp.zeros_like(s_scr)

    z = zr_ref[...]
    cos, sin = _tile_lanes(cos_ref[...], 2), _tile_lanes(sin_ref[...], 2)
    qk = R_HEADS * R_DK
    q = _rope(z[:, :qk], cos, sin)
    k = _rope(z[:, qk:2 * qk], cos, sin) * (R_DK ** -0.5)
    v = z[:, 2 * qk:2 * qk + R_WIDTH]
    gate = z[:, 2 * qk + R_WIDTH:]
    qdec = qdec_ref[...]
    kd = k * kdec_ref[...]
    outs = []
    for h in range(R_HEADS):
        sl = slice(h * R_DK, (h + 1) * R_DK)
        qh, kh, vh = q[:, sl], k[:, sl], v[:, sl]
        att = lax.dot_general(qh, kh, (((1,), (1,)), ((), ())), preferred_element_type=F32) * intra_ref[h]
        s = s_scr[h]
        o = jnp.dot(att, vh, preferred_element_type=F32) + jnp.dot(qh, s, preferred_element_type=F32) * qdec[:, sl]
        s_scr[h] = s * cdec_ref[h] + jnp.dot(kd[:, sl].T, vh, preferred_element_type=F32)
        outs.append(o * lax.rsqrt(jnp.mean(o * o, -1, keepdims=True) + EPS))
    o_ref[...] = jnp.concatenate(outs, axis=1) * _silu(gate)
    so_ref[...] = s_scr[...]


def _ret_tables(c):
    log_g = jnp.log1p(-jnp.exp2(-5.0 - jnp.arange(R_HEADS, dtype=F32)))
    idx = jnp.arange(c, dtype=F32)
    diff = idx[:, None] - idx[None, :]
    intra = jnp.where(diff >= 0, jnp.exp(log_g[:, None, None] * jnp.maximum(diff, 0.0)), 0.0)
    q_dec = jnp.exp(log_g[None, :] * (idx[:, None] + 1.0))
    k_dec = jnp.exp(log_g[None, :] * (c - 1.0 - idx[:, None]))
    c_dec = jnp.exp(log_g * c)
    return intra, jnp.repeat(q_dec, R_DK, axis=1), jnp.repeat(k_dec, R_DK, axis=1), c_dec


def _ret_prompt(zr, cos, sin):
    l = zr.shape[0]
    c = R_CHUNK
    intra, qdec, kdec, cdec = _ret_tables(c)
    blk = lambda w: pl.BlockSpec((c, w), lambda i: (i, 0))
    return pl.pallas_call(
        _ret_prompt_kernel, grid=(l // c,), name="ret_prompt",
        in_specs=[pl.BlockSpec(memory_space=pltpu.SMEM), blk(R_COLS), blk(LANES), blk(LANES),
                  _full((R_HEADS, c, c)), _full((c, R_HEADS * R_DK)), _full((c, R_HEADS * R_DK))],
        out_specs=[blk(R_WIDTH), _full((R_HEADS, R_DK, R_DV))],
        out_shape=[jax.ShapeDtypeStruct((l, R_WIDTH), F32), jax.ShapeDtypeStruct((R_HEADS, R_DK, R_DV), F32)],
        scratch_shapes=[pltpu.VMEM((R_HEADS, R_DK, R_DV), F32)],
        compiler_params=_cp(("arbitrary",)))(cdec, zr, cos, sin, intra, qdec, kdec)


def _rwkv_prep_kernel(zc_ref, prev_ref, shift0_ref, mu_ref, w0_ref, w2_ref, a0_ref, a2_ref, g2_ref,
                      kk_ref, ka_ref, rk_ref,
                      r_ref, w_ref, k_ref, v_ref, a_ref, b_ref, g_ref, bonus_ref, *, sequence, tm):
    zc = zc_ref[...]
    if sequence:
        row = lax.broadcasted_iota(jnp.int32, zc.shape, 0)
        boundary = jnp.where(pl.program_id(0) == 0, shift0_ref[...], prev_ref[SUBLANES - 1:SUBLANES, :])
        prev = jnp.where(row == 0, jnp.broadcast_to(boundary, zc.shape), pltpu.roll(zc, 1, axis=0))
    else:
        prev = prev_ref[...]
    zs = zc + mu_ref[...] * (prev - zc)
    o1, o2, o3 = C_WIDTH, 2 * C_WIDTH, 3 * C_WIDTH
    r, k, v = zs[:, :o1], zs[:, o1:o2], zs[:, o2:o3]
    lora = zs[:, o3:o3 + LANES]
    gl = zs[:, o3 + LANES:]
    w = -jax.nn.softplus(-(w0_ref[...] + jnp.dot(jnp.tanh(lora), w2_ref[...], preferred_element_type=F32))) - 0.5
    decay = jnp.exp(-jnp.exp(w))
    a = _sigmoid(a0_ref[...] + jnp.dot(lora, a2_ref[...], preferred_element_type=F32))
    g = jnp.dot(_sigmoid(gl), g2_ref[...], preferred_element_type=F32)
    kk = k * kk_ref[...]
    kk = kk / jnp.maximum(jnp.sqrt(_seg_sum(kk * kk)), 1e-12)
    k_mod = k * (1.0 + (a - 1.0) * ka_ref[...])
    r_ref[...] = r
    w_ref[...] = decay
    k_ref[...] = k_mod
    v_ref[...] = v
    a_ref[...] = -kk
    b_ref[...] = kk * a
    g_ref[...] = g
    bonus_ref[...] = _seg_sum(r * k_mod * rk_ref[...]) * v


def _rwkv_prep(zc, prev, shift0, cw, sequence, tm):
    t = zc.shape[0]
    row = lambda w: pl.BlockSpec((tm, w), lambda i: (i, 0))
    if sequence:
        per = tm // SUBLANES
        prev_spec = pl.BlockSpec((SUBLANES, C_PAD), lambda i: (jnp.maximum(i * per - 1, 0), 0))
        prev = zc
    else:
        prev_spec = row(C_PAD)
    vec = _full((1, C_WIDTH))
    return pl.pallas_call(
        functools.partial(_rwkv_prep_kernel, sequence=sequence, tm=tm), grid=(t // tm,), name="rwkv_prep",
        in_specs=[row(C_PAD), prev_spec, _full((1, C_PAD)), _full((1, C_PAD)), vec, _full((LANES, C_WIDTH)),
                  vec, _full((LANES, C_WIDTH)), _full((2 * LANES, C_WIDTH)), vec, vec, vec],
        out_specs=[row(C_WIDTH)] * 8,
        out_shape=[jax.ShapeDtypeStruct((t, C_WIDTH), F32)] * 8,
        compiler_params=_cp(("parallel",)))(zc, prev, shift0, cw["mu"], cw["w0"], cw["w2"], cw["a0"], cw["a2"],
                                           cw["g2"], cw["k_k"], cw["k_a"], cw["r_k"])


SCAN_BLOCK = 256
SCAN_SUB = 16
SCAN_NSUB = SCAN_BLOCK // SCAN_SUB
SCAN_UNROLL = 16


def _scan_select():
    sel = np.zeros((SCAN_SUB // 2, LANES, 2 * LANES), np.float32)
    for p in range(SCAN_SUB // 2):
        for piece in range(3):
            for rh in range(2):
                for tl in range(2):
                    row = piece * 2 * SCAN_SUB + rh * SCAN_SUB + 2 * p + tl
                    sel[p, row, tl * LANES + rh * 64: tl * LANES + (rh + 1) * 64] = 1.0
    return jnp.asarray(sel, BF16)


def _stage_lhs(ops, base, lhs_scr):
    for n, ref in enumerate(ops):
        xs = ref[pl.ds(base, SCAN_SUB), :]
        hi = xs.astype(BF16).astype(F32)
        r1 = xs - hi
        mid = r1.astype(BF16).astype(F32)
        lo = (r1 - mid).astype(BF16).astype(F32)
        g = jnp.concatenate([hi[:, :LANES], hi[:, LANES:], mid[:, :LANES], mid[:, LANES:],
                             lo[:, :LANES], lo[:, LANES:], jnp.zeros((LANES - 6 * SCAN_SUB, LANES), F32)], axis=0)
        lhs_scr[n] = g.T.astype(BF16)


def _stage_tiles(lhs_scr, sel_ref, xb_scr, p):
    sel = sel_ref[p]
    for n in range(5):
        out = jnp.dot(lhs_scr[n], sel, preferred_element_type=F32)
        xb_scr[n, 2 * p] = out[:, :LANES]
        xb_scr[n, 2 * p + 1] = out[:, LANES:]


def _rwkv_scan_kernel(r_ref, w_ref, k_ref, v_ref, a_ref, b_ref, g_ref, bonus_ref, lnw_ref, lnb_ref, sel_ref,
                      o_ref, so_ref, st_scr, lhs_a, lhs_b, xb_a, xb_b, vt_scr, vb_scr, yt_scr, yb_scr):
    @pl.when(pl.program_id(0) == 0)
    def _():
        st_scr[...] = jnp.zeros_like(st_scr)

    vtop, vbot = _rowpair(v_ref[...])
    vt_scr[...] = vtop
    vb_scr[...] = vbot
    ops = (a_ref, w_ref, b_ref, k_ref, r_ref)
    pairs = SCAN_UNROLL // 2
    _stage_lhs(ops, 0, lhs_a)
    for p in range(SCAN_SUB // 2):
        _stage_tiles(lhs_a, sel_ref, xb_a, p)
    _stage_lhs(ops, SCAN_SUB, lhs_b)

    def half(sc, st, lhs_cur, xb_cur, lhs_nxt, xb_nxt):
        base = pl.multiple_of(sc * SCAN_SUB, SCAN_SUB)

        def quad(q, st):
            for pp in range(pairs):
                _stage_tiles(lhs_nxt, sel_ref, xb_nxt, q * pairs + pp)
            for uu in range(SCAN_UNROLL):
                u = q * SCAN_UNROLL + uu
                t = base + u
                st, y_t, y_b = _wkv_step(st, xb_cur[0, u], xb_cur[1, u], xb_cur[2, u], xb_cur[3, u], xb_cur[4, u],
                                         vt_scr[pl.ds(t, 1), :], vb_scr[pl.ds(t, 1), :])
                yt_scr[pl.ds(t, 1), :] = y_t
                yb_scr[pl.ds(t, 1), :] = y_b
            return st

        st = lax.fori_loop(0, SCAN_SUB // SCAN_UNROLL, quad, st)
        nxt2 = pl.multiple_of(jnp.minimum(sc + 2, SCAN_NSUB - 1) * SCAN_SUB, SCAN_SUB)
        _stage_lhs(ops, nxt2, lhs_cur)
        return st

    def sub2(s2, st):
        st = half(2 * s2, st, lhs_a, xb_a, lhs_b, xb_b)
        return half(2 * s2 + 1, st, lhs_b, xb_b, lhs_a, xb_a)

    st = lax.fori_loop(0, SCAN_NSUB // 2, sub2, st_scr[...])
    st_scr[...] = st
    y = _unpair(yt_scr[...], yb_scr[...])
    o_ref[...] = _wkv_post(y, lnw_ref[...], lnb_ref[...], bonus_ref[...], g_ref[...])
    heads = _t_to_heads(st, transpose=True)
    for h in range(C_HEADS):
        so_ref[h] = heads[h]


def _rwkv_scan(prep, lnw, lnb):
    r, w, k, v, a, b, g, bonus = prep
    l = r.shape[0]
    blk = pl.BlockSpec((SCAN_BLOCK, C_WIDTH), lambda i: (i, 0))
    vec = _full((1, C_WIDTH))
    sel = _scan_select()
    return pl.pallas_call(
        _rwkv_scan_kernel, grid=(l // SCAN_BLOCK,), name="rwkv_scan",
        in_specs=[blk] * 8 + [vec, vec, _full(sel.shape)],
        out_specs=[blk, _full((C_HEADS, C_HEAD, C_HEAD))],
        out_shape=[jax.ShapeDtypeStruct((l, C_WIDTH), F32), jax.ShapeDtypeStruct((C_HEADS, C_HEAD, C_HEAD), F32)],
        scratch_shapes=[pltpu.VMEM((LANES, LANES), F32),
                        pltpu.VMEM((5, LANES, LANES), BF16), pltpu.VMEM((5, LANES, LANES), BF16),
                        pltpu.VMEM((5, SCAN_SUB, LANES, LANES), F32), pltpu.VMEM((5, SCAN_SUB, LANES, LANES), F32),
                        pltpu.VMEM((SCAN_BLOCK, LANES), F32), pltpu.VMEM((SCAN_BLOCK, LANES), F32),
                        pltpu.VMEM((SCAN_BLOCK, LANES), F32), pltpu.VMEM((SCAN_BLOCK, LANES), F32)],
        compiler_params=_cp(("arbitrary",)))(r, w, k, v, a, b, g, bonus, lnw, lnb, sel)


def _sample_kernel(sinks_ref, gam_ref, za_ref, zr_ref, r_ref, w_ref, k_ref, v_ref, a_ref, b_ref, g_ref, bonus_ref,
                   kc_ref, vc_ref, sret_ref, swkv_ref, cos_ref, sin_ref, lnw_ref, lnb_ref,
                   oa_ref, or_ref, oc_ref, kco_ref, vco_ref, sreto_ref, swkvo_ref):
    cos, sin = cos_ref[...], sin_ref[...]
    lane = lax.broadcasted_iota(jnp.int32, (1, LANES), 1)
    row8 = lax.broadcasted_iota(jnp.int32, (SUBLANES, LANES), 0)

    za = za_ref[0]
    q = _rope(za[:, :A_WIDTH], _tile_lanes(cos, 4), _tile_lanes(sin, 4))
    knew = _rope(za[:, A_WIDTH:A_WIDTH + A_KV_WIDTH], cos, sin)
    vnew = za[:, A_WIDTH + A_KV_WIDTH:]
    kc, vc = kc_ref[0], vc_ref[0]
    g = A_HEADS // A_KV_HEADS
    zero = jnp.zeros((1, HEAD_DIM), F32)
    qrows = []
    for h in range(A_HEADS):
        qh = q[:, h * HEAD_DIM:(h + 1) * HEAD_DIM]
        qrows.append(jnp.concatenate([qh, zero] if h // g == 0 else [zero, qh], axis=1))
    qm = jnp.concatenate(qrows + [jnp.zeros((LANES - A_HEADS, LANES), F32)], axis=0)
    scale = HEAD_DIM ** -0.5
    s = lax.dot_general(kc, qm, (((1,), (1,)), ((), ())), preferred_element_type=F32) * scale
    kn8 = jnp.where(row8 == 0, jnp.broadcast_to(knew, (SUBLANES, LANES)), 0.0)
    s_new = lax.dot_general(kn8, qm, (((1,), (1,)), ((), ())), preferred_element_type=F32)[0:1] * scale
    sink = jnp.zeros((1, LANES), F32)
    for h in range(A_HEADS):
        sink = jnp.where(lane == h, sinks_ref[h], sink)
    m = jnp.maximum(jnp.maximum(jnp.max(s, axis=0, keepdims=True), s_new), sink)
    p = jnp.exp(s - m)
    p_new = jnp.exp(s_new - m)
    denom = jnp.sum(p, axis=0, keepdims=True) + p_new + jnp.exp(sink - m)
    p = p / denom
    p_new = p_new / denom
    pn_col = jnp.broadcast_to(p_new, (LANES, LANES)).T[:, 0:1]
    o_full = jnp.dot(p.T, vc, preferred_element_type=F32) + pn_col * vnew
    oa_ref[0] = jnp.concatenate(
        [o_full[h:h + 1, (h // g) * HEAD_DIM:(h // g + 1) * HEAD_DIM] for h in range(A_HEADS)], axis=1)
    rowk = lax.broadcasted_iota(jnp.int32, kc.shape, 0)
    last = rowk == kc.shape[0] - 1
    kco_ref[0] = jnp.where(last, jnp.broadcast_to(knew, kc.shape), pltpu.roll(kc, kc.shape[0] - 1, axis=0))
    vco_ref[0] = jnp.where(last, jnp.broadcast_to(vnew, vc.shape), pltpu.roll(vc, vc.shape[0] - 1, axis=0))

    zr = zr_ref[0]
    qk = R_HEADS * R_DK
    cos2, sin2 = _tile_lanes(cos, 2), _tile_lanes(sin, 2)
    rq = _rope(zr[:, :qk], cos2, sin2)
    rk = _rope(zr[:, qk:2 * qk], cos2, sin2) * (R_DK ** -0.5)
    rv = zr[:, 2 * qk:2 * qk + R_WIDTH]
    rg = zr[:, 2 * qk + R_WIDTH:]
    st = _heads_to_t(sret_ref[0], transpose=False)
    rr = lax.broadcasted_iota(jnp.int32, (LANES, LANES), 0) >= 64
    cc = lax.broadcasted_iota(jnp.int32, (LANES, LANES), 1) >= 64
    gamma = jnp.where(rr, jnp.where(cc, gam_ref[3], gam_ref[1]), jnp.where(cc, gam_ref[2], gam_ref[0]))
    vt, vb = _rowpair(rv)
    st = st * gamma + _colbcast(rk) * _rowbcast(vt, vb)
    o_t, o_b = _halfsums(st * _colbcast(rq))
    o = _unpair(o_t, o_b)
    o = o * lax.rsqrt(_seg_sum(o * o) * (1.0 / R_DV) + EPS)
    or_ref[0] = o * _silu(rg)
    heads = _t_to_heads(st, transpose=False)
    for h in range(R_HEADS):
        sreto_ref[0, h] = heads[h]

    st = _heads_to_t(swkv_ref[0], transpose=True)
    vt, vb = _rowpair(v_ref[0])
    st, y_t, y_b = _wkv_step(st, _colbcast(a_ref[0]), _colbcast(w_ref[0]), _colbcast(b_ref[0]),
                             _colbcast(k_ref[0]), _colbcast(r_ref[0]), vt, vb)
    oc_ref[0] = _wkv_post(_unpair(y_t, y_b), lnw_ref[...], lnb_ref[...], bonus_ref[0], g_ref[0])
    heads = _t_to_heads(st, transpose=True)
    for h in range(C_HEADS):
        swkvo_ref[0, h] = heads[h]


def _sample_mixers(za, zr, prep, kc, vc, sret, swkv, cos, sin, sinks, lnw, lnb):
    b = za.shape[0]
    wb = kc.shape[1]
    assert wb <= WINDOW and PAST_LEN >= wb
    gam = jnp.exp(jnp.log1p(-jnp.exp2(-5.0 - jnp.arange(R_HEADS, dtype=F32))) * 1.0)
    tok = lambda w: pl.BlockSpec((1, 1, w), lambda i: (i, 0, 0))
    cache = pl.BlockSpec((1, wb, A_KV_WIDTH), lambda i: (i, 0, 0))
    state = pl.BlockSpec((1, 4, 64, 64), lambda i: (i, 0, 0, 0))
    smem = pl.BlockSpec(memory_space=pltpu.SMEM)
    vec = _full((1, C_WIDTH))
    r3 = lambda x: x.reshape(b, 1, x.shape[-1])
    outs = pl.pallas_call(
        _sample_kernel, grid=(b,), name="sample_mixers",
        in_specs=[smem, smem, tok(A_COLS), tok(R_COLS)] + [tok(C_WIDTH)] * 8 + [cache, cache, state, state,
                  _full((1, LANES)), _full((1, LANES)), vec, vec],
        out_specs=[tok(A_WIDTH), tok(R_WIDTH), tok(C_WIDTH), cache, cache, state, state],
        out_shape=[jax.ShapeDtypeStruct((b, 1, A_WIDTH), F32), jax.ShapeDtypeStruct((b, 1, R_WIDTH), F32),
                   jax.ShapeDtypeStruct((b, 1, C_WIDTH), F32), jax.ShapeDtypeStruct(kc.shape, F32),
                   jax.ShapeDtypeStruct(vc.shape, F32), jax.ShapeDtypeStruct(sret.shape, F32),
                   jax.ShapeDtypeStruct(swkv.shape, F32)],
        compiler_params=_cp(("parallel",)))(sinks, gam, r3(za), r3(zr), *[r3(x) for x in prep],
                                           kc, vc, sret, swkv, cos, sin, lnw, lnb)
    oa, orr, oc = (x.reshape(b, x.shape[-1]) for x in outs[:3])
    return (oa, orr, oc) + tuple(outs[3:])


def _outproj_kernel(oa_ref, or_ref, oc_ref, x_ref, wo_ref, gf_ref, wrt_ref, brt_ref, *rest, tm):
    x1_ref, h2_ref, route_ref = rest[-3:]
    mix =jnp.concatenate([oa_ref[...], or_ref[...], oc_ref[...]], axis=1).astype(BF16)
    x1 = x_ref[...] + jnp.dot(mix, wo_ref[...], preferred_element_type=F32)
    x1_ref[...] = x1
    h2 = _rmsnorm(x1, gf_ref[...])
    for s in range(ROW_TILES):
        h2_ref[pl.ds(s, tm, stride=ROW_TILES), :] = h2[:, s * LANES:(s + 1) * LANES]
    logits = jnp.dot(h2, wrt_ref[...], preferred_element_type=F32, precision=lax.Precision.HIGHEST) + brt_ref[...]
    lane = lax.broadcasted_iota(jnp.int32, logits.shape, 1).astype(F32)
    big = float(LANES)
    is_g = lane < N_GROUPS
    lg = jnp.where(is_g, logits, -jnp.inf)
    m_g = jnp.max(lg, -1, keepdims=True)
    grp = jnp.min(jnp.where(lg == m_g, lane, big), -1, keepdims=True)
    p_grp = 1.0 / jnp.sum(jnp.where(is_g, jnp.exp(lg - m_g), 0.0), -1, keepdims=True)
    lo = N_GROUPS + EXPERTS_PER_GROUP * grp
    in_grp = (lane >= lo) & (lane < lo + EXPERTS_PER_GROUP)
    le = jnp.where(in_grp, logits, -jnp.inf)
    l1 = jnp.max(le, -1, keepdims=True)
    i1 = jnp.min(jnp.where(le == l1, lane, big), -1, keepdims=True)
    le2 = jnp.where(lane == i1, -jnp.inf, le)
    l2 = jnp.max(le2, -1, keepdims=True)
    i2 = jnp.min(jnp.where(le2 == l2, lane, big), -1, keepdims=True)
    e = jnp.exp(l2 - l1)
    g1 = p_grp / (1.0 + e)
    g2 = p_grp * e / (1.0 + e)
    route = jnp.where(lane == 0, i1 - N_GROUPS, jnp.where(lane == 1, i2 - N_GROUPS,
                      jnp.where(lane == 2, g1, jnp.where(lane == 3, g2, 0.0))))
    route_ref[...] = route


def _outproj(oa, orr, oc, x, wo, gf, wrt, brt, tm, t_all, first_tok, shared=None):
    t = x.shape[0]
    off = first_tok // tm
    nblk = t // tm
    steps = nblk + (1 if shared is None else 0)
    assert shared is not None or t_all == t + tm
    row = lambda w: pl.BlockSpec((tm, w), lambda i: (jnp.minimum(i, nblk - 1), 0))
    in_specs = [row(A_WIDTH), row(R_WIDTH), row(C_WIDTH), row(D_MODEL), _full(wo.shape), _full((1, D_MODEL)),
                _full(wrt.shape), _full((1, LANES))]
    args = [oa, orr, oc, x, wo, gf, wrt, brt]
    aliases = {}
    if shared is not None:
        aliases = {len(args): 1, len(args) + 1: 2}
        in_specs += [pl.BlockSpec(memory_space=pl.ANY)] * 2
        args += list(shared)
    return pl.pallas_call(
        functools.partial(_outproj_kernel, tm=tm), grid=(steps,), name="outproj_router",
        in_specs=in_specs,
        out_specs=[row(D_MODEL), pl.BlockSpec((tm * ROW_TILES, LANES), lambda i: (i + off, 0)),
                   pl.BlockSpec((tm, LANES), lambda i: (i + off, 0))],
        out_shape=[jax.ShapeDtypeStruct((t, D_MODEL), F32), jax.ShapeDtypeStruct((t_all * ROW_TILES, LANES), F32),
                   jax.ShapeDtypeStruct((t_all, LANES), F32)],
        input_output_aliases=aliases,
        compiler_params=_cp(("arbitrary",)))(*args)


def _moe_kernel(blk_exp_ref, src_ref, dst_ref, nused_ref, h2_hbm, wg_ref, wu_ref, wd_ref, yw_hbm,
                gbuf, sbuf, xbuf, wgb, wub, wdb, gsem, ssem):
    i = pl.program_id(0)
    nb = pl.num_programs(0)
    nused = nused_ref[0]
    slot = i % 2

    def tile(off):
        return pl.ds(pl.multiple_of(off, ROW_TILES), ROW_TILES)

    def gather(blk, sl):
        def body(r, _):
            pltpu.make_async_copy(h2_hbm.at[tile(src_ref[blk * MOE_ROWS + r]), :],
                                  gbuf.at[sl, tile(r * ROW_TILES), :], gsem.at[sl]).start()
            return 0
        lax.fori_loop(0, MOE_ROWS, body, 0, unroll=DMA_UNROLL)

    def gather_wait(sl):
        def body(r, _):
            pltpu.make_async_copy(h2_hbm.at[tile(0), :], gbuf.at[sl, tile(0), :], gsem.at[sl]).wait()
            return 0
        lax.fori_loop(0, MOE_ROWS, body, 0, unroll=DMA_UNROLL)

    def scatter(blk):
        def body(r, _):
            pltpu.make_async_copy(sbuf.at[tile(r * ROW_TILES), :],
                                  yw_hbm.at[tile(dst_ref[blk * MOE_ROWS + r]), :], ssem).start()
            return 0
        lax.fori_loop(0, MOE_ROWS, body, 0, unroll=DMA_UNROLL)

    def scatter_wait():
        def body(r, _):
            pltpu.make_async_copy(sbuf.at[tile(0), :], yw_hbm.at[tile(0), :], ssem).wait()
            return 0
        lax.fori_loop(0, MOE_ROWS, body, 0, unroll=DMA_UNROLL)

    @pl.when((i == 0) & (nused > 0))
    def _():
        gather(0, 0)

    @pl.when(i < nused)
    def _():
        @pl.when((i == 0) | (blk_exp_ref[i] != blk_exp_ref[jnp.maximum(i - 1, 0)]))
        def _():
            wgb[...] = wg_ref[0].astype(BF16)
            wub[...] = wu_ref[0].astype(BF16)
            wdb[...] = wd_ref[0].astype(BF16)

        gather_wait(slot)

        @pl.when(i + 1 < nused)
        def _():
            gather(i + 1, 1 - slot)

        for s in range(ROW_TILES):
            xbuf[:, s * LANES:(s + 1) * LANES] = gbuf[slot, pl.ds(s, MOE_ROWS, stride=ROW_TILES), :].astype(BF16)
        x = xbuf[...]
        hid = _silu(jnp.dot(x, wgb[...], preferred_element_type=F32)) * jnp.dot(x, wub[...], preferred_element_type=F32)
        y = jnp.dot(hid.astype(BF16), wdb[...], preferred_element_type=F32)

        @pl.when(i > 0)
        def _():
            scatter_wait()

        for s in range(ROW_TILES):
            sbuf[pl.ds(s, MOE_ROWS, stride=ROW_TILES), :] = y[:, s * LANES:(s + 1) * LANES]
        scatter(i)

    @pl.when((i == nb - 1) & (nused > 0))
    def _():
        scatter_wait()
        spare = pltpu.make_async_copy(sbuf, yw_hbm.at[pl.ds(yw_hbm.shape[0] - MOE_ROWS * ROW_TILES,
                                                            MOE_ROWS * ROW_TILES), :], ssem)
        spare.start()
        spare.wait()


def _moe(h2, route, t, wg, wu, wd):
    na = t * TOP_K
    assert na % MOE_ROWS == 0 and na < (1 << IDX_BITS)
    flat_e = route[:t, :TOP_K].astype(jnp.int32).reshape(na)
    experts = jnp.arange(N_EXPERTS, dtype=jnp.int32)
    counts = jnp.sum(flat_e[:, None] == experts[None, :], axis=0, dtype=jnp.int32)
    npad = (-counts) % MOE_ROWS
    real_keys = (flat_e << (IDX_BITS + 1)) | jnp.arange(na, dtype=jnp.int32)
    q = jnp.arange(MOE_ROWS, dtype=jnp.int32)[None, :]
    pad_keys = jnp.where(q < npad[:, None], (experts[:, None] << (IDX_BITS + 1)) | (1 << IDX_BITS) | q,
                         jnp.iinfo(jnp.int32).max)
    keys = jnp.sort(jnp.concatenate([real_keys, pad_keys.reshape(-1)]))
    n_rows = na + N_EXPERTS * MOE_ROWS
    n_blocks = n_rows // MOE_ROWS
    is_pad = ((keys >> IDX_BITS) & 1) == 1
    idx = keys & ((1 << IDX_BITS) - 1)
    pos = jnp.arange(n_rows, dtype=jnp.int32)
    src = jnp.where(is_pad, 0, (idx // TOP_K) * ROW_TILES).astype(jnp.int32)
    dst = jnp.where(is_pad, na + pos % MOE_ROWS, idx).astype(jnp.int32) * ROW_TILES
    blk_exp = jnp.minimum(keys[::MOE_ROWS] >> (IDX_BITS + 1), N_EXPERTS - 1).astype(jnp.int32)
    nused = (jnp.sum(counts + npad) // MOE_ROWS).astype(jnp.int32).reshape(1)
    wspec = lambda a, b: pl.BlockSpec((1, a, b), lambda i, be, s, d, nu: (be[i], 0, 0))
    return pl.pallas_call(
        _moe_kernel, name="moe_experts",
        grid_spec=pltpu.PrefetchScalarGridSpec(
            num_scalar_prefetch=4, grid=(n_blocks,),
            in_specs=[pl.BlockSpec(memory_space=pl.ANY), wspec(D_MODEL, EXPERT_FF), wspec(D_MODEL, EXPERT_FF),
                      wspec(EXPERT_FF, D_MODEL)],
            out_specs=pl.BlockSpec(memory_space=pl.ANY),
            scratch_shapes=[pltpu.VMEM((2, MOE_ROWS * ROW_TILES, LANES), F32),
                            pltpu.VMEM((MOE_ROWS * ROW_TILES, LANES), F32),
                            pltpu.VMEM((MOE_ROWS, D_MODEL), BF16),
                            pltpu.VMEM((D_MODEL, EXPERT_FF), BF16), pltpu.VMEM((D_MODEL, EXPERT_FF), BF16),
                            pltpu.VMEM((EXPERT_FF, D_MODEL), BF16),
                            pltpu.SemaphoreType.DMA((2,)), pltpu.SemaphoreType.DMA(())]),
        out_shape=jax.ShapeDtypeStruct(((na + MOE_ROWS) * ROW_TILES, LANES), F32),
        compiler_params=_cp(("arbitrary",)))(blk_exp, src, dst, nused, h2, wg, wu, wd)


def _rope_tables(pos):
    half = HEAD_DIM // 2
    inv = ROPE_THETA ** (-jnp.arange(half, dtype=F32) / half)
    ang = pos.astype(F32)[:, None] * inv[None, :]
    c, s = jnp.cos(ang), jnp.sin(ang)
    return jnp.concatenate([c, c, c, c], axis=1), jnp.concatenate([-s, s, -s, s], axis=1)


def _pad_cols(w, n):
    return jnp.pad(w, ((0, 0), (0, n - w.shape[1])))


def _pad_rows(w, lo, n):
    return jnp.pad(w, ((lo, n - lo - w.shape[0]), (0, 0)))


def kernel(x_prompt, x_sample, cache_swa_k, cache_swa_v, state_ret, state_wkv, state_shift, norm_mix, w_in, sinks,
           rwkv_mu, rwkv_w0, rwkv_w2, rwkv_a0, rwkv_a2, rwkv_g2, rwkv_k_k, rwkv_k_a, rwkv_r_k, rwkv_ln_w, rwkv_ln_b,
           w_out, norm_ffn, router_g, router_g_b, router_e, router_e_b, expert_w_gate, expert_w_up, expert_w_down,
           norm_final):
    lp = x_prompt.shape[1]
    bs = x_sample.shape[0]
    wb = cache_swa_k.shape[2]
    tm_p = 512
    xp = x_prompt.reshape(lp, D_MODEL)
    xs = x_sample.reshape(bs, D_MODEL)
    cos_p, sin_p = _rope_tables(jnp.arange(lp, dtype=jnp.int32))
    cos_s, sin_s = _rope_tables(PAST_LEN + jnp.arange(1, dtype=jnp.int32))
    outs_p = [[] for _ in range(5)]
    outs_s = [[] for _ in range(5)]
    moe_p = moe_s = None
    row = lambda v: v.reshape(1, -1)
    for i in range(DEPTH):
        wi = w_in[i].astype(BF16)
        wa, wr, wc = wi[:, :A_COLS], wi[:, A_COLS:A_COLS + R_COLS], _pad_cols(wi[:, A_COLS + R_COLS:], C_PAD)
        cw = {
            "mu": _pad_cols(row(rwkv_mu[i]), C_PAD), "w0": row(rwkv_w0[i]), "a0": row(rwkv_a0[i]),
            "w2": _pad_rows(rwkv_w2[i], 0, LANES), "a2": _pad_rows(rwkv_a2[i], C_DECAY_LORA, LANES),
            "g2": _pad_rows(rwkv_g2[i], 0, 2 * LANES),
            "k_k": row(rwkv_k_k[i]), "k_a": row(rwkv_k_a[i]), "r_k": row(rwkv_r_k[i]),
        }
        lnw, lnb = row(rwkv_ln_w[i]), row(rwkv_ln_b[i])
        g_mix, g_ffn = row(norm_mix[i]), row(norm_ffn[i])
        wo = w_out[i].astype(BF16)
        wrt = _pad_cols(jnp.concatenate([router_g[i], router_e[i]], axis=1), LANES)
        brt = _pad_cols(row(jnp.concatenate([router_g_b[i], router_e_b[i]])), LANES)

        xn, za, zr, zc = _inproj(xp, moe_p, g_mix, wa, wr, wc, tm_p)
        xp = xp if xn is None else xn
        oa, k_last = _swa_prompt(za, cos_p, sin_p, sinks[i])
        orr, s_ret = _ret_prompt(zr, cos_p, sin_p)
        prep = _rwkv_prep(zc, None, jnp.zeros((1, C_PAD), F32), cw, True, tm_p)
        oc, s_wkv = _rwkv_scan(prep, lnw, lnb)
        xp, h2, route = _outproj(oa, orr, oc, xp, wo, g_ffn, wrt, brt, tm_p, lp + tm_p, 0)
        nk = min(WINDOW, lp)
        outs_p[0].append(k_last.reshape(1, nk, A_KV_HEADS, HEAD_DIM))
        outs_p[1].append(za[lp - nk:, A_WIDTH + A_KV_WIDTH:].reshape(1, nk, A_KV_HEADS, HEAD_DIM))
        outs_p[2].append(s_ret[None])
        outs_p[3].append(s_wkv[None])
        outs_p[4].append(zc[lp - 1:, :C_COLS])

        xn, za, zr, zc = _inproj(xs, moe_s, g_mix, wa, wr, wc, bs)
        xs = xs if xn is None else xn
        prep = _rwkv_prep(zc, _pad_cols(state_shift[i], C_PAD), jnp.zeros((1, C_PAD), F32), cw, False, bs)
        oa, orr, oc, kc_new, vc_new, sret_new, swkv_new = _sample_mixers(
            za, zr, prep, cache_swa_k[i].reshape(bs, wb, A_KV_WIDTH), cache_swa_v[i].reshape(bs, wb, A_KV_WIDTH),
            state_ret[i], state_wkv[i], cos_s, sin_s, sinks[i], lnw, lnb)
        xs, h2, route = _outproj(oa, orr, oc, xs, wo, g_ffn, wrt, brt, bs, lp + tm_p, lp, shared=(h2, route))
        yw = _moe(h2, route, lp + bs, expert_w_gate[i], expert_w_up[i], expert_w_down[i])
        moe_p, moe_s = (yw, route, 0), (yw, route, lp)
        outs_s[0].append(kc_new.reshape(bs, wb, A_KV_HEADS, HEAD_DIM))
        outs_s[1].append(vc_new.reshape(bs, wb, A_KV_HEADS, HEAD_DIM))
        outs_s[2].append(sret_new)
        outs_s[3].append(swkv_new)
        outs_s[4].append(zc[:, :C_COLS])

    gfin = row(norm_final)
    y_prompt = _final(xp, moe_p, gfin, tm_p).reshape(x_prompt.shape)
    y_sample = _final(xs, moe_s, gfin, bs).reshape(x_sample.shape)
    sp = [jnp.stack(o) for o in outs_p]
    ss = [jnp.stack(o) for o in outs_s]
    return (y_prompt, y_sample, sp[0], sp[1], sp[2], sp[3], sp[4], ss[0], ss[1], ss[2], ss[3], ss[4])
```

```python
import functools

import numpy as np
import jax
import jax.numpy as jnp
from jax import lax
from jax.experimental import pallas as pl
from jax.experimental.pallas import tpu as pltpu

F32 = jnp.float32
BF16 = jnp.bfloat16

D_MODEL = 1024
DEPTH = 2
PAST_LEN = 16384
A_HEADS, A_KV_HEADS, HEAD_DIM, WINDOW = 8, 2, 64, 128
ROPE_THETA = 10000.0
A_WIDTH = A_HEADS * HEAD_DIM
A_KV_WIDTH = A_KV_HEADS * HEAD_DIM
A_COLS = A_WIDTH + 2 * A_KV_WIDTH
R_HEADS, R_DK, R_DV, R_CHUNK = 4, 64, 64, 128
R_WIDTH = R_HEADS * R_DV
R_COLS = 2 * R_HEADS * R_DK + 2 * R_WIDTH
C_HEADS, C_HEAD = 4, 64
C_WIDTH = C_HEADS * C_HEAD
C_DECAY_LORA, C_ICLR_LORA, C_GATE_LORA = 64, 64, 160
C_COLS = 3 * C_WIDTH + C_DECAY_LORA + C_ICLR_LORA + C_GATE_LORA
C_PAD = 1152
C_GN_EPS = 64e-5
N_GROUPS, EXPERTS_PER_GROUP, TOP_K, EXPERT_FF = 4, 8, 2, 512
N_EXPERTS = N_GROUPS * EXPERTS_PER_GROUP
EPS = 1e-6

LANES = 128
SUBLANES = 8
ROW_TILES = D_MODEL // LANES
MOE_ROWS = 256
DMA_UNROLL = 8
IDX_BITS = 19
VMEM_LIMIT = 56 * 1024 * 1024


def _cp(sem, vmem=VMEM_LIMIT):
    return pltpu.CompilerParams(dimension_semantics=sem, vmem_limit_bytes=vmem)


def _full(shape):
    n = len(shape)
    return pl.BlockSpec(shape, lambda *_: (0,) * n)


def _rmsnorm(x, g):
    return x * lax.rsqrt(jnp.mean(x * x, -1, keepdims=True) + EPS) * g


def _silu(x):
    return x * (1.0 / (1.0 + jnp.exp(-x)))


def _sigmoid(x):
    return 1.0 / (1.0 + jnp.exp(-x))


def _rope(x, cos, sin_signed):
    w = x.shape[-1]
    lane = lax.broadcasted_iota(jnp.int32, x.shape, x.ndim - 1)
    first = (lane % HEAD_DIM) < (HEAD_DIM // 2)
    swapped = jnp.where(first, pltpu.roll(x, w - HEAD_DIM // 2, axis=x.ndim - 1),
                        pltpu.roll(x, HEAD_DIM // 2, axis=x.ndim - 1))
    return x * cos + swapped * sin_signed


def _tile_lanes(x, n):
    return jnp.concatenate([x] * n, axis=-1) if n > 1 else x


def _seg_sum(x, seg=64):
    outs = []
    for h in range(x.shape[-1] // seg):
        s = jnp.sum(x[:, h * seg:(h + 1) * seg], axis=-1, keepdims=True)
        outs.append(jnp.broadcast_to(s, (x.shape[0], seg)))
    return jnp.concatenate(outs, axis=-1)


def _read_rows(ref, tm, first, stride):
    return jnp.concatenate([ref[pl.ds(first + s, tm, stride=stride), :] for s in range(ROW_TILES)], axis=1)


def _colbcast(x):
    z = jnp.concatenate([jnp.broadcast_to(x[:, :LANES], (64, LANES)),
                         jnp.broadcast_to(x[:, LANES:], (64, LANES))], axis=0)
    return z.T


def _rowpair(x):
    top = jnp.concatenate([x[:, 0:64], x[:, 128:192]], axis=1)
    bot = jnp.concatenate([x[:, 64:128], x[:, 192:256]], axis=1)
    return top, bot


def _unpair(top, bot):
    return jnp.concatenate([top[:, :64], bot[:, :64], top[:, 64:], bot[:, 64:]], axis=1)


def _rowbcast(top, bot):
    return jnp.concatenate([jnp.broadcast_to(top, (64, LANES)), jnp.broadcast_to(bot, (64, LANES))], axis=0)


def _halfsums(p):
    return jnp.sum(p[:64], axis=0, keepdims=True), jnp.sum(p[64:], axis=0, keepdims=True)


def _wkv_step(st, ab, wb, bb, kb, rb, vtop, vbot):
    sa_t, sa_b = _halfsums(st * ab)
    st = st * wb + _rowbcast(sa_t, sa_b) * bb + _rowbcast(vtop, vbot) * kb
    y_t, y_b = _halfsums(st * rb)
    return st, y_t, y_b


def _wkv_post(y, lnw, lnb, bonus, g):
    mu = _seg_sum(y) * (1.0 / C_HEAD)
    d = y - mu
    var = _seg_sum(d * d) * (1.0 / C_HEAD)
    yn = d * lax.rsqrt(var + C_GN_EPS)
    return (yn * lnw + lnb + bonus) * g


def _heads_to_t(s4, transpose):
    if transpose:
        m = jnp.concatenate([jnp.concatenate([s4[0], s4[1]], axis=1),
                             jnp.concatenate([s4[2], s4[3]], axis=1)], axis=0)
        return m.T
    return jnp.concatenate([jnp.concatenate([s4[0], s4[2]], axis=1),
                            jnp.concatenate([s4[1], s4[3]], axis=1)], axis=0)


def _t_to_heads(st, transpose):
    if transpose:
        m = st.T
        return [m[0:64, 0:64], m[0:64, 64:128], m[64:128, 0:64], m[64:128, 64:128]]
    return [st[0:64, 0:64], st[64:128, 0:64], st[0:64, 64:128], st[64:128, 64:128]]


def _inproj_kernel(*refs, combine, tm):
    if combine:
        x_ref, yw_ref, route_ref, g_ref, wa_ref, wr_ref, wc_ref, cos_ref, sin_ref, xo_ref, za_ref, zr_ref, zc_ref = refs
        route = route_ref[...]
        y0 = _read_rows(yw_ref, tm, 0, 2 * ROW_TILES)
        y1 = _read_rows(yw_ref, tm, ROW_TILES, 2 * ROW_TILES)
        x = x_ref[...] + (route[:, 2:3] * y0 + route[:, 3:4] * y1)
        xo_ref[...] = x
    else:
        x_ref, g_ref, wa_ref, wr_ref, wc_ref, cos_ref, sin_ref, za_ref, zr_ref, zc_ref = refs
        x = x_ref[...]
    h = _rmsnorm(x, g_ref[...]).astype(BF16)
    cos, sin = cos_ref[...], sin_ref[...]
    na = (A_WIDTH + A_KV_WIDTH) // LANES
    nr = 2 * R_HEADS * R_DK // LANES
    za = jnp.dot(h, wa_ref[...], preferred_element_type=F32)
    za_ref[:, :na * LANES] = _rope(za[:, :na * LANES], _tile_lanes(cos, na), _tile_lanes(sin, na))
    za_ref[:, na * LANES:] = za[:, na * LANES:]
    zr = jnp.dot(h, wr_ref[...], preferred_element_type=F32)
    zr_ref[:, :nr * LANES] = _rope(zr[:, :nr * LANES], _tile_lanes(cos, nr), _tile_lanes(sin, nr))
    zr_ref[:, nr * LANES:] = zr[:, nr * LANES:]
    zc_ref[...] = jnp.dot(h, wc_ref[...], preferred_element_type=F32)


def _inproj(x, moe, g, wa, wr, wc, cos, sin, tm):
    t = x.shape[0]
    row = lambda w: pl.BlockSpec((tm, w), lambda i: (i, 0))
    wspecs = [_full((1, D_MODEL)), _full(wa.shape), _full(wr.shape), _full(wc.shape), row(LANES), row(LANES)]
    zshapes = [jax.ShapeDtypeStruct((t, A_COLS), F32), jax.ShapeDtypeStruct((t, R_COLS), F32),
               jax.ShapeDtypeStruct((t, C_PAD), F32)]
    zspecs = [row(A_COLS), row(R_COLS), row(C_PAD)]
    if moe is None:
        return (None,) + tuple(pl.pallas_call(
            functools.partial(_inproj_kernel, combine=False, tm=tm), name="inproj",
            grid=(t // tm,), in_specs=[row(D_MODEL)] + wspecs, out_specs=zspecs, out_shape=zshapes,
            compiler_params=_cp(("parallel",)))(x, g, wa, wr, wc, cos, sin))
    yw, route, first_tok = moe
    off = first_tok // tm
    return pl.pallas_call(
        functools.partial(_inproj_kernel, combine=True, tm=tm), name="combine_inproj",
        grid=(t // tm,),
        in_specs=[row(D_MODEL), pl.BlockSpec((tm * 2 * ROW_TILES, LANES), lambda i: (i + off, 0)),
                  pl.BlockSpec((tm, LANES), lambda i: (i + off, 0))] + wspecs,
        out_specs=[row(D_MODEL)] + zspecs,
        out_shape=[jax.ShapeDtypeStruct((t, D_MODEL), F32)] + zshapes,
        compiler_params=_cp(("parallel",)))(x, yw, route, g, wa, wr, wc, cos, sin)


def _final_kernel(x_ref, yw_ref, route_ref, g_ref, o_ref, *, tm):
    route = route_ref[...]
    y0 = _read_rows(yw_ref, tm, 0, 2 * ROW_TILES)
    y1 = _read_rows(yw_ref, tm, ROW_TILES, 2 * ROW_TILES)
    x = x_ref[...] + (route[:, 2:3] * y0 + route[:, 3:4] * y1)
    o_ref[...] = _rmsnorm(x, g_ref[...])


def _final(x, moe, g, tm):
    t = x.shape[0]
    yw, route, first_tok = moe
    off = first_tok // tm
    row = lambda w: pl.BlockSpec((tm, w), lambda i: (i, 0))
    return pl.pallas_call(
        functools.partial(_final_kernel, tm=tm), grid=(t // tm,), name="combine_final",
        in_specs=[row(D_MODEL), pl.BlockSpec((tm * 2 * ROW_TILES, LANES), lambda i: (i + off, 0)),
                  pl.BlockSpec((tm, LANES), lambda i: (i + off, 0)), _full((1, D_MODEL))],
        out_specs=row(D_MODEL), out_shape=jax.ShapeDtypeStruct((t, D_MODEL), F32),
        compiler_params=_cp(("parallel",)))(x, yw, route, g)


def _swa_prompt_kernel(sinks_ref, q_ref, kc_ref, kp_ref, vc_ref, vp_ref, o_ref):
    i = pl.program_id(0)
    g = A_HEADS // A_KV_HEADS
    nq = g * WINDOW
    kcat = jnp.concatenate([kp_ref[...], kc_ref[...]], axis=0)
    vcat = jnp.concatenate([vp_ref[...], vc_ref[...]], axis=0)
    kswap = pltpu.roll(kcat, HEAD_DIM, axis=1)
    vswap = pltpu.roll(vcat, HEAD_DIM, axis=1)
    upper_k = lax.broadcasted_iota(jnp.int32, kcat.shape, 1) >= HEAD_DIM
    upper_q = lax.broadcasted_iota(jnp.int32, (WINDOW, LANES), 1) >= HEAD_DIM
    r = lax.broadcasted_iota(jnp.int32, (nq, 2 * WINDOW), 0) % WINDOW
    c = lax.broadcasted_iota(jnp.int32, (nq, 2 * WINDOW), 1)
    ok = (c >= r) & (c <= r + WINDOW) & ((c >= WINDOW) | (i > 0))
    slabs = []
    for kv in range(A_KV_HEADS):
        kboth = jnp.where(upper_k, kswap, kcat) if kv == 0 else jnp.where(upper_k, kcat, kswap)
        v_lo = jnp.where(upper_k, 0.0, vcat if kv == 0 else vswap)
        v_hi = jnp.where(upper_k, vswap if kv == 0 else vcat, 0.0)
        vpair = jnp.concatenate([v_lo, v_hi], axis=0)
        qs = []
        for j in range(g):
            h = kv * g + j
            slab = q_ref[:, (h // 2) * LANES:(h // 2 + 1) * LANES]
            qs.append(jnp.where(upper_q, slab, 0.0) if h % 2 else jnp.where(upper_q, 0.0, slab))
        q4 = jnp.concatenate(qs, axis=0)
        s = lax.dot_general(q4, kboth, (((1,), (1,)), ((), ())), preferred_element_type=F32) * (HEAD_DIM ** -0.5)
        s = jnp.where(ok, s, -jnp.inf)
        sink = jnp.concatenate([jnp.full((WINDOW, 1), sinks_ref[kv * g + j], F32) for j in range(g)], axis=0)
        m = jnp.maximum(jnp.max(s, -1, keepdims=True), sink)
        p = jnp.exp(s - m)
        p = p / (jnp.sum(p, -1, keepdims=True) + jnp.exp(sink - m))
        for j in range(0, g, 2):
            pp = jnp.concatenate([p[j * WINDOW:(j + 1) * WINDOW], p[(j + 1) * WINDOW:(j + 2) * WINDOW]], axis=1)
            slabs.append(jnp.dot(pp, vpair, preferred_element_type=F32))
    o_ref[...] = jnp.concatenate(slabs, axis=1)


def _swa_prompt(za, sinks):
    l = za.shape[0]
    nb = l // WINDOW
    cur = lambda w, c: pl.BlockSpec((WINDOW, w), lambda i: (i, c))
    prv = lambda w, c: pl.BlockSpec((WINDOW, w), lambda i: (jnp.maximum(i - 1, 0), c))
    kcol, vcol = A_WIDTH // A_KV_WIDTH, A_WIDTH // A_KV_WIDTH + 1
    return pl.pallas_call(
        _swa_prompt_kernel, grid=(nb,), name="swa_prompt",
        in_specs=[pl.BlockSpec(memory_space=pltpu.SMEM),
                  cur(A_WIDTH, 0), cur(A_KV_WIDTH, kcol), prv(A_KV_WIDTH, kcol),
                  cur(A_KV_WIDTH, vcol), prv(A_KV_WIDTH, vcol)],
        out_specs=cur(A_WIDTH, 0),
        out_shape=jax.ShapeDtypeStruct((l, A_WIDTH), F32),
        compiler_params=_cp(("parallel",)))(sinks, za, za, za, za, za)


def _ret_prompt_kernel(cdec_ref, zr_ref, intra_ref, qdec_ref, kdec_ref, o_ref, so_ref, s_scr):
    @pl.when(pl.program_id(0) == 0)
    def _():
        s_scr[...] = jnp.zeros_like(s_scr)

    z = zr_ref[...]
    qk = R_HEADS * R_DK
    q = z[:, :qk]
    k = z[:, qk:2 * qk] * (R_DK ** -0.5)
    v = z[:, 2 * qk:2 * qk + R_WIDTH]
    gate = z[:, 2 * qk + R_WIDTH:]
    qdec = qdec_ref[...]
    kd = k * kdec_ref[...]
    outs = []
    for h in range(R_HEADS):
        sl = slice(h * R_DK, (h + 1) * R_DK)
        qh, kh, vh = q[:, sl], k[:, sl], v[:, sl]
        att = lax.dot_general(qh, kh, (((1,), (1,)), ((), ())), preferred_element_type=F32) * intra_ref[h]
        s = s_scr[h]
        o = jnp.dot(att, vh, preferred_element_type=F32) + jnp.dot(qh, s, preferred_element_type=F32) * qdec[:, sl]
        s_scr[h] = s * cdec_ref[h] + jnp.dot(kd[:, sl].T, vh, preferred_element_type=F32)
        outs.append(o * lax.rsqrt(jnp.mean(o * o, -1, keepdims=True) + EPS))
    o_ref[...] = jnp.concatenate(outs, axis=1) * _silu(gate)
    so_ref[...] = s_scr[...]


def _ret_tables(c):
    log_g = jnp.log1p(-jnp.exp2(-5.0 - jnp.arange(R_HEADS, dtype=F32)))
    idx = jnp.arange(c, dtype=F32)
    diff = idx[:, None] - idx[None, :]
    intra = jnp.where(diff >= 0, jnp.exp(log_g[:, None, None] * jnp.maximum(diff, 0.0)), 0.0)
    q_dec = jnp.exp(log_g[None, :] * (idx[:, None] + 1.0))
    k_dec = jnp.exp(log_g[None, :] * (c - 1.0 - idx[:, None]))
    c_dec = jnp.exp(log_g * c)
    return intra, jnp.repeat(q_dec, R_DK, axis=1), jnp.repeat(k_dec, R_DK, axis=1), c_dec


def _ret_prompt(zr):
    l = zr.shape[0]
    c = R_CHUNK
    intra, qdec, kdec, cdec = _ret_tables(c)
    blk = lambda w: pl.BlockSpec((c, w), lambda i: (i, 0))
    return pl.pallas_call(
        _ret_prompt_kernel, grid=(l // c,), name="ret_prompt",
        in_specs=[pl.BlockSpec(memory_space=pltpu.SMEM), blk(R_COLS),
                  _full((R_HEADS, c, c)), _full((c, R_HEADS * R_DK)), _full((c, R_HEADS * R_DK))],
        out_specs=[blk(R_WIDTH), _full((R_HEADS, R_DK, R_DV))],
        out_shape=[jax.ShapeDtypeStruct((l, R_WIDTH), F32), jax.ShapeDtypeStruct((R_HEADS, R_DK, R_DV), F32)],
        scratch_shapes=[pltpu.VMEM((R_HEADS, R_DK, R_DV), F32)],
        compiler_params=_cp(("arbitrary",)))(cdec, zr, intra, qdec, kdec)


def _rwkv_prep_kernel(zc_ref, prev_ref, shift0_ref, mu_ref, w0_ref, w2_ref, a0_ref, a2_ref, g2_ref,
                      kk_ref, ka_ref, rk_ref,
                      r_ref, w_ref, k_ref, v_ref, a_ref, b_ref, g_ref, bonus_ref, *, sequence, tm):
    zc = zc_ref[...]
    if sequence:
        row = lax.broadcasted_iota(jnp.int32, zc.shape, 0)
        boundary = jnp.where(pl.program_id(0) == 0, shift0_ref[...], prev_ref[SUBLANES - 1:SUBLANES, :])
        prev = jnp.where(row == 0, jnp.broadcast_to(boundary, zc.shape), pltpu.roll(zc, 1, axis=0))
    else:
        prev = prev_ref[...]
    zs = zc + mu_ref[...] * (prev - zc)
    o1, o2, o3 = C_WIDTH, 2 * C_WIDTH, 3 * C_WIDTH
    r, k, v = zs[:, :o1], zs[:, o1:o2], zs[:, o2:o3]
    lora = zs[:, o3:o3 + LANES]
    gl = zs[:, o3 + LANES:]
    w = -jax.nn.softplus(-(w0_ref[...] + jnp.dot(jnp.tanh(lora), w2_ref[...], preferred_element_type=F32))) - 0.5
    decay = jnp.exp(-jnp.exp(w))
    a = _sigmoid(a0_ref[...] + jnp.dot(lora, a2_ref[...], preferred_element_type=F32))
    g = jnp.dot(_sigmoid(gl), g2_ref[...], preferred_element_type=F32)
    kk = k * kk_ref[...]
    kk = kk / jnp.maximum(jnp.sqrt(_seg_sum(kk * kk)), 1e-12)
    k_mod = k * (1.0 + (a - 1.0) * ka_ref[...])
    r_ref[...] = r
    w_ref[...] = decay
    k_ref[...] = k_mod
    v_ref[...] = v
    a_ref[...] = -kk
    b_ref[...] = kk * a
    g_ref[...] = g
    bonus_ref[...] = _seg_sum(r * k_mod * rk_ref[...]) * v


def _rwkv_prep(zc, prev, shift0, cw, sequence, tm):
    t = zc.shape[0]
    row = lambda w: pl.BlockSpec((tm, w), lambda i: (i, 0))
    if sequence:
        per = tm // SUBLANES
        prev_spec = pl.BlockSpec((SUBLANES, C_PAD), lambda i: (jnp.maximum(i * per - 1, 0), 0))
        prev = zc
    else:
        prev_spec = row(C_PAD)
    vec = _full((1, C_WIDTH))
    return pl.pallas_call(
        functools.partial(_rwkv_prep_kernel, sequence=sequence, tm=tm), grid=(t // tm,), name="rwkv_prep",
        in_specs=[row(C_PAD), prev_spec, _full((1, C_PAD)), _full((1, C_PAD)), vec, _full((LANES, C_WIDTH)),
                  vec, _full((LANES, C_WIDTH)), _full((2 * LANES, C_WIDTH)), vec, vec, vec],
        out_specs=[row(C_WIDTH)] * 8,
        out_shape=[jax.ShapeDtypeStruct((t, C_WIDTH), F32)] * 8,
        compiler_params=_cp(("parallel",)))(zc, prev, shift0, cw["mu"], cw["w0"], cw["w2"], cw["a0"], cw["a2"],
                                           cw["g2"], cw["k_k"], cw["k_a"], cw["r_k"])


SCAN_BLOCK = 256
SCAN_SUB = 16
SCAN_NSUB = SCAN_BLOCK // SCAN_SUB
SCAN_UNROLL = 16


def _scan_select():
    sel = np.zeros((SCAN_SUB // 2, LANES, 2 * LANES), np.float32)
    for p in range(SCAN_SUB // 2):
        for piece in range(3):
            for rh in range(2):
                for tl in range(2):
                    row = piece * 2 * SCAN_SUB + rh * SCAN_SUB + 2 * p + tl
                    sel[p, row, tl * LANES + rh * 64: tl * LANES + (rh + 1) * 64] = 1.0
    return jnp.asarray(sel, BF16)


def _stage_lhs(ops, base, lhs_scr):
    for n, ref in enumerate(ops):
        xs = ref[pl.ds(base, SCAN_SUB), :]
        hi = xs.astype(BF16).astype(F32)
        r1 = xs - hi
        mid = r1.astype(BF16).astype(F32)
        lo = (r1 - mid).astype(BF16).astype(F32)
        g = jnp.concatenate([hi[:, :LANES], hi[:, LANES:], mid[:, :LANES], mid[:, LANES:],
                             lo[:, :LANES], lo[:, LANES:], jnp.zeros((LANES - 6 * SCAN_SUB, LANES), F32)], axis=0)
        lhs_scr[n] = g.T.astype(BF16)


def _stage_tiles(lhs_scr, sel_ref, xb_scr, p):
    sel = sel_ref[p]
    for n in range(5):
        out = jnp.dot(lhs_scr[n], sel, preferred_element_type=F32)
        xb_scr[n, 2 * p] = out[:, :LANES]
        xb_scr[n, 2 * p + 1] = out[:, LANES:]


def _rwkv_scan_kernel(r_ref, w_ref, k_ref, v_ref, a_ref, b_ref, g_ref, bonus_ref, lnw_ref, lnb_ref, sel_ref,
                      o_ref, so_ref, st_scr, lhs_a, lhs_b, xb_a, xb_b, vt_scr, vb_scr, yt_scr, yb_scr):
    @pl.when(pl.program_id(0) == 0)
    def _():
        st_scr[...] = jnp.zeros_like(st_scr)

    vtop, vbot = _rowpair(v_ref[...])
    vt_scr[...] = vtop
    vb_scr[...] = vbot
    ops = (a_ref, w_ref, b_ref, k_ref, r_ref)
    pairs = SCAN_UNROLL // 2
    _stage_lhs(ops, 0, lhs_a)
    for p in range(SCAN_SUB // 2):
        _stage_tiles(lhs_a, sel_ref, xb_a, p)
    _stage_lhs(ops, SCAN_SUB, lhs_b)

    def half(sc, st, lhs_cur, xb_cur, lhs_nxt, xb_nxt):
        base = pl.multiple_of(sc * SCAN_SUB, SCAN_SUB)

        def quad(q, st):
            for pp in range(pairs):
                _stage_tiles(lhs_nxt, sel_ref, xb_nxt, q * pairs + pp)
            for uu in range(SCAN_UNROLL):
                u = q * SCAN_UNROLL + uu
                t = base + u
                st, y_t, y_b = _wkv_step(st, xb_cur[0, u], xb_cur[1, u], xb_cur[2, u], xb_cur[3, u], xb_cur[4, u],
                                         vt_scr[pl.ds(t, 1), :], vb_scr[pl.ds(t, 1), :])
                yt_scr[pl.ds(t, 1), :] = y_t
                yb_scr[pl.ds(t, 1), :] = y_b
            return st

        st = lax.fori_loop(0, SCAN_SUB // SCAN_UNROLL, quad, st)
        nxt2 = pl.multiple_of(jnp.minimum(sc + 2, SCAN_NSUB - 1) * SCAN_SUB, SCAN_SUB)
        _stage_lhs(ops, nxt2, lhs_cur)
        return st

    def sub2(s2, st):
        st = half(2 * s2, st, lhs_a, xb_a, lhs_b, xb_b)
        return half(2 * s2 + 1, st, lhs_b, xb_b, lhs_a, xb_a)

    st = lax.fori_loop(0, SCAN_NSUB // 2, sub2, st_scr[...])
    st_scr[...] = st
    y = _unpair(yt_scr[...], yb_scr[...])
    o_ref[...] = _wkv_post(y, lnw_ref[...], lnb_ref[...], bonus_ref[...], g_ref[...])
    heads = _t_to_heads(st, transpose=True)
    for h in range(C_HEADS):
        so_ref[h] = heads[h]


def _rwkv_scan(prep, lnw, lnb):
    r, w, k, v, a, b, g, bonus = prep
    l = r.shape[0]
    blk = pl.BlockSpec((SCAN_BLOCK, C_WIDTH), lambda i: (i, 0))
    vec = _full((1, C_WIDTH))
    sel = _scan_select()
    return pl.pallas_call(
        _rwkv_scan_kernel, grid=(l // SCAN_BLOCK,), name="rwkv_scan",
        in_specs=[blk] * 8 + [vec, vec, _full(sel.shape)],
        out_specs=[blk, _full((C_HEADS, C_HEAD, C_HEAD))],
        out_shape=[jax.ShapeDtypeStruct((l, C_WIDTH), F32), jax.ShapeDtypeStruct((C_HEADS, C_HEAD, C_HEAD), F32)],
        scratch_shapes=[pltpu.VMEM((LANES, LANES), F32),
                        pltpu.VMEM((5, LANES, LANES), BF16), pltpu.VMEM((5, LANES, LANES), BF16),
                        pltpu.VMEM((5, SCAN_SUB, LANES, LANES), F32), pltpu.VMEM((5, SCAN_SUB, LANES, LANES), F32),
                        pltpu.VMEM((SCAN_BLOCK, LANES), F32), pltpu.VMEM((SCAN_BLOCK, LANES), F32),
                        pltpu.VMEM((SCAN_BLOCK, LANES), F32), pltpu.VMEM((SCAN_BLOCK, LANES), F32)],
        compiler_params=_cp(("arbitrary",)))(r, w, k, v, a, b, g, bonus, lnw, lnb, sel)


def _sample_kernel(sinks_ref, gam_ref, za_ref, zr_ref, r_ref, w_ref, k_ref, v_ref, a_ref, b_ref, g_ref, bonus_ref,
                   kc_ref, vc_ref, sret_ref, swkv_ref, lnw_ref, lnb_ref,
                   oa_ref, or_ref, oc_ref, kco_ref, vco_ref, sreto_ref, swkvo_ref):
    lane =lax.broadcasted_iota(jnp.int32, (1, LANES), 1)
    row8 = lax.broadcasted_iota(jnp.int32, (SUBLANES, LANES), 0)

    za = za_ref[0]
    q = za[:, :A_WIDTH]
    knew = za[:, A_WIDTH:A_WIDTH + A_KV_WIDTH]
    vnew = za[:, A_WIDTH + A_KV_WIDTH:]
    kc, vc = kc_ref[0], vc_ref[0]
    g = A_HEADS // A_KV_HEADS
    zero = jnp.zeros((1, HEAD_DIM), F32)
    qrows = []
    for h in range(A_HEADS):
        qh = q[:, h * HEAD_DIM:(h + 1) * HEAD_DIM]
        qrows.append(jnp.concatenate([qh, zero] if h // g == 0 else [zero, qh], axis=1))
    qm = jnp.concatenate(qrows + [jnp.zeros((LANES - A_HEADS, LANES), F32)], axis=0)
    scale = HEAD_DIM ** -0.5
    s = lax.dot_general(kc, qm, (((1,), (1,)), ((), ())), preferred_element_type=F32) * scale
    kn8 = jnp.where(row8 == 0, jnp.broadcast_to(knew, (SUBLANES, LANES)), 0.0)
    s_new = lax.dot_general(kn8, qm, (((1,), (1,)), ((), ())), preferred_element_type=F32)[0:1] * scale
    sink = jnp.zeros((1, LANES), F32)
    for h in range(A_HEADS):
        sink = jnp.where(lane == h, sinks_ref[h], sink)
    m = jnp.maximum(jnp.maximum(jnp.max(s, axis=0, keepdims=True), s_new), sink)
    p = jnp.exp(s - m)
    p_new = jnp.exp(s_new - m)
    denom = jnp.sum(p, axis=0, keepdims=True) + p_new + jnp.exp(sink - m)
    p = p / denom
    p_new = p_new / denom
    pn_col = jnp.broadcast_to(p_new, (LANES, LANES)).T[:, 0:1]
    o_full = jnp.dot(p.T, vc, preferred_element_type=F32) + pn_col * vnew
    oa_ref[0] = jnp.concatenate(
        [o_full[h:h + 1, (h // g) * HEAD_DIM:(h // g + 1) * HEAD_DIM] for h in range(A_HEADS)], axis=1)
    rowk = lax.broadcasted_iota(jnp.int32, kc.shape, 0)
    last = rowk == kc.shape[0] - 1
    kco_ref[0] = jnp.where(last, jnp.broadcast_to(knew, kc.shape), pltpu.roll(kc, kc.shape[0] - 1, axis=0))
    vco_ref[0] = jnp.where(last, jnp.broadcast_to(vnew, vc.shape), pltpu.roll(vc, vc.shape[0] - 1, axis=0))

    zr = zr_ref[0]
    qk = R_HEADS * R_DK
    rq = zr[:, :qk]
    rk = zr[:, qk:2 * qk] * (R_DK ** -0.5)
    rv = zr[:, 2 * qk:2 * qk + R_WIDTH]
    rg = zr[:, 2 * qk + R_WIDTH:]
    st = _heads_to_t(sret_ref[0], transpose=False)
    rr = lax.broadcasted_iota(jnp.int32, (LANES, LANES), 0) >= 64
    cc = lax.broadcasted_iota(jnp.int32, (LANES, LANES), 1) >= 64
    gamma = jnp.where(rr, jnp.where(cc, gam_ref[3], gam_ref[1]), jnp.where(cc, gam_ref[2], gam_ref[0]))
    vt, vb = _rowpair(rv)
    st = st * gamma + _colbcast(rk) * _rowbcast(vt, vb)
    o_t, o_b = _halfsums(st * _colbcast(rq))
    o = _unpair(o_t, o_b)
    o = o * lax.rsqrt(_seg_sum(o * o) * (1.0 / R_DV) + EPS)
    or_ref[0] = o * _silu(rg)
    heads = _t_to_heads(st, transpose=False)
    for h in range(R_HEADS):
        sreto_ref[0, h] = heads[h]

    st = _heads_to_t(swkv_ref[0], transpose=True)
    vt, vb = _rowpair(v_ref[0])
    st, y_t, y_b = _wkv_step(st, _colbcast(a_ref[0]), _colbcast(w_ref[0]), _colbcast(b_ref[0]),
                             _colbcast(k_ref[0]), _colbcast(r_ref[0]), vt, vb)
    oc_ref[0] = _wkv_post(_unpair(y_t, y_b), lnw_ref[...], lnb_ref[...], bonus_ref[0], g_ref[0])
    heads = _t_to_heads(st, transpose=True)
    for h in range(C_HEADS):
        swkvo_ref[0, h] = heads[h]


def _sample_mixers(za, zr, prep, kc, vc, sret, swkv, sinks, lnw, lnb):
    b = za.shape[0]
    wb = kc.shape[1]
    assert wb <= WINDOW and PAST_LEN >= wb
    gam = jnp.exp(jnp.log1p(-jnp.exp2(-5.0 - jnp.arange(R_HEADS, dtype=F32))) * 1.0)
    tok = lambda w: pl.BlockSpec((1, 1, w), lambda i: (i, 0, 0))
    cache = pl.BlockSpec((1, wb, A_KV_WIDTH), lambda i: (i, 0, 0))
    state = pl.BlockSpec((1, 4, 64, 64), lambda i: (i, 0, 0, 0))
    smem = pl.BlockSpec(memory_space=pltpu.SMEM)
    vec = _full((1, C_WIDTH))
    r3 = lambda x: x.reshape(b, 1, x.shape[-1])
    outs = pl.pallas_call(
        _sample_kernel, grid=(b,), name="sample_mixers",
        in_specs=[smem, smem, tok(A_COLS), tok(R_COLS)] + [tok(C_WIDTH)] * 8 + [cache, cache, state, state,
                  vec, vec],
        out_specs=[tok(A_WIDTH), tok(R_WIDTH), tok(C_WIDTH), cache, cache, state, state],
        out_shape=[jax.ShapeDtypeStruct((b, 1, A_WIDTH), F32), jax.ShapeDtypeStruct((b, 1, R_WIDTH), F32),
                   jax.ShapeDtypeStruct((b, 1, C_WIDTH), F32), jax.ShapeDtypeStruct(kc.shape, F32),
                   jax.ShapeDtypeStruct(vc.shape, F32), jax.ShapeDtypeStruct(sret.shape, F32),
                   jax.ShapeDtypeStruct(swkv.shape, F32)],
        compiler_params=_cp(("parallel",)))(sinks, gam, r3(za), r3(zr), *[r3(x) for x in prep],
                                           kc, vc, sret, swkv, lnw, lnb)
    oa, orr, oc = (x.reshape(b, x.shape[-1]) for x in outs[:3])
    return (oa, orr, oc) + tuple(outs[3:])


def _outproj_kernel(oa_ref, or_ref, oc_ref, x_ref, wo_ref, gf_ref, wrt_ref, brt_ref, *rest, tm):
    x1_ref, h2_ref, route_ref = rest[-3:]
    mix =jnp.concatenate([oa_ref[...], or_ref[...], oc_ref[...]], axis=1).astype(BF16)
    x1 = x_ref[...] + jnp.dot(mix, wo_ref[...], preferred_element_type=F32)
    x1_ref[...] = x1
    h2 = _rmsnorm(x1, gf_ref[...])
    for s in range(ROW_TILES):
        h2_ref[pl.ds(s, tm, stride=ROW_TILES), :] = h2[:, s * LANES:(s + 1) * LANES]
    logits = jnp.dot(h2, wrt_ref[...], preferred_element_type=F32, precision=lax.Precision.HIGHEST) + brt_ref[...]
    lane = lax.broadcasted_iota(jnp.int32, logits.shape, 1).astype(F32)
    big = float(LANES)
    is_g = lane < N_GROUPS
    lg = jnp.where(is_g, logits, -jnp.inf)
    m_g = jnp.max(lg, -1, keepdims=True)
    grp = jnp.min(jnp.where(lg == m_g, lane, big), -1, keepdims=True)
    p_grp = 1.0 / jnp.sum(jnp.where(is_g, jnp.exp(lg - m_g), 0.0), -1, keepdims=True)
    lo = N_GROUPS + EXPERTS_PER_GROUP * grp
    in_grp = (lane >= lo) & (lane < lo + EXPERTS_PER_GROUP)
    le = jnp.where(in_grp, logits, -jnp.inf)
    l1 = jnp.max(le, -1, keepdims=True)
    i1 = jnp.min(jnp.where(le == l1, lane, big), -1, keepdims=True)
    le2 = jnp.where(lane == i1, -jnp.inf, le)
    l2 = jnp.max(le2, -1, keepdims=True)
    i2 = jnp.min(jnp.where(le2 == l2, lane, big), -1, keepdims=True)
    e = jnp.exp(l2 - l1)
    g1 = p_grp / (1.0 + e)
    g2 = p_grp * e / (1.0 + e)
    route = jnp.where(lane == 0, i1 - N_GROUPS, jnp.where(lane == 1, i2 - N_GROUPS,
                      jnp.where(lane == 2, g1, jnp.where(lane == 3, g2, 0.0))))
    route_ref[...] = route


def _outproj(oa, orr, oc, x, wo, gf, wrt, brt, tm, t_all, first_tok, shared=None):
    t = x.shape[0]
    off = first_tok // tm
    nblk = t // tm
    steps = nblk + (1 if shared is None else 0)
    assert shared is not None or t_all == t + tm
    row = lambda w: pl.BlockSpec((tm, w), lambda i: (jnp.minimum(i, nblk - 1), 0))
    in_specs = [row(A_WIDTH), row(R_WIDTH), row(C_WIDTH), row(D_MODEL), _full(wo.shape), _full((1, D_MODEL)),
                _full(wrt.shape), _full((1, LANES))]
    args = [oa, orr, oc, x, wo, gf, wrt, brt]
    aliases = {}
    if shared is not None:
        aliases = {len(args): 1, len(args) + 1: 2}
        in_specs += [pl.BlockSpec(memory_space=pl.ANY)] * 2
        args += list(shared)
    return pl.pallas_call(
        functools.partial(_outproj_kernel, tm=tm), grid=(steps,), name="outproj_router",
        in_specs=in_specs,
        out_specs=[row(D_MODEL), pl.BlockSpec((tm * ROW_TILES, LANES), lambda i: (i + off, 0)),
                   pl.BlockSpec((tm, LANES), lambda i: (i + off, 0))],
        out_shape=[jax.ShapeDtypeStruct((t, D_MODEL), F32), jax.ShapeDtypeStruct((t_all * ROW_TILES, LANES), F32),
                   jax.ShapeDtypeStruct((t_all, LANES), F32)],
        input_output_aliases=aliases,
        compiler_params=_cp(("arbitrary",)))(*args)


def _moe_kernel(blk_exp_ref, src_ref, dst_ref, nused_ref, h2_hbm, wg_ref, wu_ref, wd_ref, yw_hbm,
                gbuf, sbuf, xbuf, hbuf, wgb, wub, wdb, gsem, ssem):
    i = pl.program_id(0)
    nb = pl.num_programs(0)
    nused = nused_ref[0]
    slot = i % 2

    def tile(off):
        return pl.ds(pl.multiple_of(off, ROW_TILES), ROW_TILES)

    def gather(blk, sl):
        def body(r, _):
            pltpu.make_async_copy(h2_hbm.at[tile(src_ref[blk * MOE_ROWS + r]), :],
                                  gbuf.at[sl, tile(r * ROW_TILES), :], gsem.at[sl]).start()
            return 0
        lax.fori_loop(0, MOE_ROWS, body, 0, unroll=DMA_UNROLL)

    def gather_wait(sl):
        def body(r, _):
            pltpu.make_async_copy(h2_hbm.at[tile(0), :], gbuf.at[sl, tile(0), :], gsem.at[sl]).wait()
            return 0
        lax.fori_loop(0, MOE_ROWS, body, 0, unroll=DMA_UNROLL)

    def scatter(blk):
        def body(r, _):
            pltpu.make_async_copy(sbuf.at[tile(r * ROW_TILES), :],
                                  yw_hbm.at[tile(dst_ref[blk * MOE_ROWS + r]), :], ssem).start()
            return 0
        lax.fori_loop(0, MOE_ROWS, body, 0, unroll=DMA_UNROLL)

    def scatter_wait():
        def body(r, _):
            pltpu.make_async_copy(sbuf.at[tile(0), :], yw_hbm.at[tile(0), :], ssem).wait()
            return 0
        lax.fori_loop(0, MOE_ROWS, body, 0, unroll=DMA_UNROLL)

    @pl.when((i == 0) & (nused > 0))
    def _():
        gather(0, 0)

    @pl.when(i < nused)
    def _():
        @pl.when((i == 0) | (blk_exp_ref[i] != blk_exp_ref[jnp.maximum(i - 1, 0)]))
        def _():
            wgb[...] = wg_ref[0].astype(BF16)
            wub[...] = wu_ref[0].astype(BF16)
            wdb[...] = wd_ref[0].astype(BF16)

        gather_wait(slot)
        for s in range(ROW_TILES):
            xbuf[:, s * LANES:(s + 1) * LANES] = gbuf[slot, pl.ds(s, MOE_ROWS, stride=ROW_TILES), :].astype(BF16)
        x = xbuf[...]
        hid = _silu(jnp.dot(x, wgb[...], preferred_element_type=F32)) * jnp.dot(x, wub[...], preferred_element_type=F32)
        hbuf[...] = hid.astype(BF16)

        @pl.when(i + 1 < nused)
        def _():
            gather(i + 1, 1 - slot)

        y = jnp.dot(hbuf[...], wdb[...], preferred_element_type=F32)

        @pl.when(i > 0)
        def _():
            scatter_wait()

        for s in range(ROW_TILES):
            sbuf[pl.ds(s, MOE_ROWS, stride=ROW_TILES), :] = y[:, s * LANES:(s + 1) * LANES]
        scatter(i)

    @pl.when((i == nb - 1) & (nused > 0))
    def _():
        scatter_wait()
        spare = pltpu.make_async_copy(sbuf, yw_hbm.at[pl.ds(yw_hbm.shape[0] - MOE_ROWS * ROW_TILES,
                                                            MOE_ROWS * ROW_TILES), :], ssem)
        spare.start()
        spare.wait()


def _moe(h2, route, t, wg, wu, wd):
    na = t * TOP_K
    assert na % MOE_ROWS == 0 and na < (1 << IDX_BITS)
    flat_e = route[:t, :TOP_K].astype(jnp.int32).reshape(na)
    experts = jnp.arange(N_EXPERTS, dtype=jnp.int32)
    counts = jnp.sum(flat_e[:, None] == experts[None, :], axis=0, dtype=jnp.int32)
    npad = (-counts) % MOE_ROWS
    real_keys = (flat_e << (IDX_BITS + 1)) | jnp.arange(na, dtype=jnp.int32)
    q = jnp.arange(MOE_ROWS, dtype=jnp.int32)[None, :]
    pad_keys = jnp.where(q < npad[:, None], (experts[:, None] << (IDX_BITS + 1)) | (1 << IDX_BITS) | q,
                         jnp.iinfo(jnp.int32).max)
    keys = jnp.sort(jnp.concatenate([real_keys, pad_keys.reshape(-1)]))
    n_rows = na + N_EXPERTS * MOE_ROWS
    n_blocks = n_rows // MOE_ROWS
    is_pad = ((keys >> IDX_BITS) & 1) == 1
    idx = keys & ((1 << IDX_BITS) - 1)
    pos = jnp.arange(n_rows, dtype=jnp.int32)
    src = jnp.where(is_pad, 0, (idx // TOP_K) * ROW_TILES).astype(jnp.int32)
    dst = jnp.where(is_pad, na + pos % MOE_ROWS, idx).astype(jnp.int32) * ROW_TILES
    blk_exp = jnp.minimum(keys[::MOE_ROWS] >> (IDX_BITS + 1), N_EXPERTS - 1).astype(jnp.int32)
    nused = (jnp.sum(counts + npad) // MOE_ROWS).astype(jnp.int32).reshape(1)
    wspec = lambda a, b: pl.BlockSpec((1, a, b), lambda i, be, s, d, nu: (be[i], 0, 0))
    return pl.pallas_call(
        _moe_kernel, name="moe_experts",
        grid_spec=pltpu.PrefetchScalarGridSpec(
            num_scalar_prefetch=4, grid=(n_blocks,),
            in_specs=[pl.BlockSpec(memory_space=pl.ANY), wspec(D_MODEL, EXPERT_FF), wspec(D_MODEL, EXPERT_FF),
                      wspec(EXPERT_FF, D_MODEL)],
            out_specs=pl.BlockSpec(memory_space=pl.ANY),
            scratch_shapes=[pltpu.VMEM((2, MOE_ROWS * ROW_TILES, LANES), F32),
                            pltpu.VMEM((MOE_ROWS * ROW_TILES, LANES), F32),
                            pltpu.VMEM((MOE_ROWS, D_MODEL), BF16), pltpu.VMEM((MOE_ROWS, EXPERT_FF), BF16),
                            pltpu.VMEM((D_MODEL, EXPERT_FF), BF16), pltpu.VMEM((D_MODEL, EXPERT_FF), BF16),
                            pltpu.VMEM((EXPERT_FF, D_MODEL), BF16),
                            pltpu.SemaphoreType.DMA((2,)), pltpu.SemaphoreType.DMA(())]),
        out_shape=jax.ShapeDtypeStruct(((na + MOE_ROWS) * ROW_TILES, LANES), F32),
        compiler_params=_cp(("arbitrary",)))(blk_exp, src, dst, nused, h2, wg, wu, wd)


def _rope_tables(pos):
    half = HEAD_DIM // 2
    inv = ROPE_THETA ** (-jnp.arange(half, dtype=F32) / half)
    ang = pos.astype(F32)[:, None] * inv[None, :]
    c, s = jnp.cos(ang), jnp.sin(ang)
    return jnp.concatenate([c, c, c, c], axis=1), jnp.concatenate([-s, s, -s, s], axis=1)


def _pad_cols(w, n):
    return jnp.pad(w, ((0, 0), (0, n - w.shape[1])))


def _pad_rows(w, lo, n):
    return jnp.pad(w, ((lo, n - lo - w.shape[0]), (0, 0)))


def kernel(x_prompt, x_sample, cache_swa_k, cache_swa_v, state_ret, state_wkv, state_shift, norm_mix, w_in, sinks,
           rwkv_mu, rwkv_w0, rwkv_w2, rwkv_a0, rwkv_a2, rwkv_g2, rwkv_k_k, rwkv_k_a, rwkv_r_k, rwkv_ln_w, rwkv_ln_b,
           w_out, norm_ffn, router_g, router_g_b, router_e, router_e_b, expert_w_gate, expert_w_up, expert_w_down,
           norm_final):
    lp = x_prompt.shape[1]
    bs = x_sample.shape[0]
    wb = cache_swa_k.shape[2]
    tm_p = 512
    xp = x_prompt.reshape(lp, D_MODEL)
    xs = x_sample.reshape(bs, D_MODEL)
    cos_p, sin_p = _rope_tables(jnp.arange(lp, dtype=jnp.int32))
    cos_s, sin_s = (jnp.broadcast_to(t, (bs, LANES))
                    for t in _rope_tables(PAST_LEN + jnp.arange(x_sample.shape[1], dtype=jnp.int32)))
    outs_p = [[] for _ in range(5)]
    outs_s = [[] for _ in range(5)]
    moe_p = moe_s = None
    row = lambda v: v.reshape(1, -1)
    for i in range(DEPTH):
        wi = w_in[i].astype(BF16)
        wa, wr, wc = wi[:, :A_COLS], wi[:, A_COLS:A_COLS + R_COLS], _pad_cols(wi[:, A_COLS + R_COLS:], C_PAD)
        cw = {
            "mu": _pad_cols(row(rwkv_mu[i]), C_PAD), "w0": row(rwkv_w0[i]), "a0": row(rwkv_a0[i]),
            "w2": _pad_rows(rwkv_w2[i], 0, LANES), "a2": _pad_rows(rwkv_a2[i], C_DECAY_LORA, LANES),
            "g2": _pad_rows(rwkv_g2[i], 0, 2 * LANES),
            "k_k": row(rwkv_k_k[i]), "k_a": row(rwkv_k_a[i]), "r_k": row(rwkv_r_k[i]),
        }
        lnw, lnb = row(rwkv_ln_w[i]), row(rwkv_ln_b[i])
        g_mix, g_ffn = row(norm_mix[i]), row(norm_ffn[i])
        wo = w_out[i].astype(BF16)
        wrt = _pad_cols(jnp.concatenate([router_g[i], router_e[i]], axis=1), LANES)
        brt = _pad_cols(row(jnp.concatenate([router_g_b[i], router_e_b[i]])), LANES)

        xn, za, zr, zc = _inproj(xp, moe_p, g_mix, wa, wr, wc, cos_p, sin_p, tm_p)
        xp = xp if xn is None else xn
        oa = _swa_prompt(za, sinks[i])
        orr, s_ret = _ret_prompt(zr)
        prep = _rwkv_prep(zc, None, jnp.zeros((1, C_PAD), F32), cw, True, tm_p)
        oc, s_wkv = _rwkv_scan(prep, lnw, lnb)
        xp, h2, route = _outproj(oa, orr, oc, xp, wo, g_ffn, wrt, brt, tm_p, lp + tm_p, 0)
        nk = min(WINDOW, lp)
        outs_p[0].append(za[lp - nk:, A_WIDTH:A_WIDTH + A_KV_WIDTH].reshape(1, nk, A_KV_HEADS, HEAD_DIM))
        outs_p[1].append(za[lp - nk:, A_WIDTH + A_KV_WIDTH:].reshape(1, nk, A_KV_HEADS, HEAD_DIM))
        outs_p[2].append(s_ret[None])
        outs_p[3].append(s_wkv[None])
        outs_p[4].append(zc[lp - 1:, :C_COLS])

        xn, za, zr, zc = _inproj(xs, moe_s, g_mix, wa, wr, wc, cos_s, sin_s, bs)
        xs = xs if xn is None else xn
        prep = _rwkv_prep(zc, _pad_cols(state_shift[i], C_PAD), jnp.zeros((1, C_PAD), F32), cw, False, bs)
        oa, orr, oc, kc_new, vc_new, sret_new, swkv_new = _sample_mixers(
            za, zr, prep, cache_swa_k[i].reshape(bs, wb, A_KV_WIDTH), cache_swa_v[i].reshape(bs, wb, A_KV_WIDTH),
            state_ret[i], state_wkv[i], sinks[i], lnw, lnb)
        xs, h2, route = _outproj(oa, orr, oc, xs, wo, g_ffn, wrt, brt, bs, lp + tm_p, lp, shared=(h2, route))
        yw = _moe(h2, route, lp + bs, expert_w_gate[i], expert_w_up[i], expert_w_down[i])
        moe_p, moe_s = (yw, route, 0), (yw, route, lp)
        outs_s[0].append(kc_new.reshape(bs, wb, A_KV_HEADS, HEAD_DIM))
        outs_s[1].append(vc_new.reshape(bs, wb, A_KV_HEADS, HEAD_DIM))
        outs_s[2].append(sret_new)
        outs_s[3].append(swkv_new)
        outs_s[4].append(zc[:, :C_COLS])

    gfin = row(norm_final)
    y_prompt = _final(xp, moe_p, gfin, tm_p).reshape(x_prompt.shape)
    y_sample = _final(xs, moe_s, gfin, bs).reshape(x_sample.shape)
    sp = [jnp.stack(o) for o in outs_p]
    ss = [jnp.stack(o) for o in outs_s]
    return (y_prompt, y_sample, sp[0], sp[1], sp[2], sp[3], sp[4], ss[0], ss[1], ss[2], ss[3], ss[4])
```

```python
import functools

import numpy as np
import jax
import jax.numpy as jnp
from jax import lax
from jax.experimental import pallas as pl
from jax.experimental.pallas import tpu as pltpu

F32 = jnp.float32
BF16 = jnp.bfloat16

D_MODEL = 1024
DEPTH = 2
PAST_LEN = 16384
A_HEADS, A_KV_HEADS, HEAD_DIM, WINDOW = 8, 2, 64, 128
ROPE_THETA = 10000.0
A_WIDTH = A_HEADS * HEAD_DIM
A_KV_WIDTH = A_KV_HEADS * HEAD_DIM
A_COLS = A_WIDTH + 2 * A_KV_WIDTH
R_HEADS, R_DK, R_DV, R_CHUNK = 4, 64, 64, 128
R_WIDTH = R_HEADS * R_DV
R_COLS = 2 * R_HEADS * R_DK + 2 * R_WIDTH
C_HEADS, C_HEAD = 4, 64
C_WIDTH = C_HEADS * C_HEAD
C_DECAY_LORA, C_ICLR_LORA, C_GATE_LORA = 64, 64, 160
C_COLS = 3 * C_WIDTH + C_DECAY_LORA + C_ICLR_LORA + C_GATE_LORA
C_PAD = 1152
C_GN_EPS = 64e-5
N_GROUPS, EXPERTS_PER_GROUP, TOP_K, EXPERT_FF = 4, 8, 2, 512
N_EXPERTS = N_GROUPS * EXPERTS_PER_GROUP
EPS = 1e-6

LANES = 128
SUBLANES = 8
ROW_TILES = D_MODEL // LANES
MOE_ROWS = 256
DMA_UNROLL = 8
IDX_BITS = 19
VMEM_LIMIT = 56 * 1024 * 1024


def _cp(sem, vmem=VMEM_LIMIT):
    return pltpu.CompilerParams(dimension_semantics=sem, vmem_limit_bytes=vmem)


def _full(shape):
    n = len(shape)
    return pl.BlockSpec(shape, lambda *_: (0,) * n)


def _rmsnorm(x, g):
    return x * lax.rsqrt(jnp.mean(x * x, -1, keepdims=True) + EPS) * g


def _silu(x):
    return x * (1.0 / (1.0 + jnp.exp(-x)))


def _sigmoid(x):
    return 1.0 / (1.0 + jnp.exp(-x))


def _rope(x, cos, sin_signed):
    w = x.shape[-1]
    lane = lax.broadcasted_iota(jnp.int32, x.shape, x.ndim - 1)
    first = (lane % HEAD_DIM) < (HEAD_DIM // 2)
    swapped = jnp.where(first, pltpu.roll(x, w - HEAD_DIM // 2, axis=x.ndim - 1),
                        pltpu.roll(x, HEAD_DIM // 2, axis=x.ndim - 1))
    return x * cos + swapped * sin_signed


def _tile_lanes(x, n):
    return jnp.concatenate([x] * n, axis=-1) if n > 1 else x


def _seg_sum(x, seg=64):
    outs = []
    for h in range(x.shape[-1] // seg):
        s = jnp.sum(x[:, h * seg:(h + 1) * seg], axis=-1, keepdims=True)
        outs.append(jnp.broadcast_to(s, (x.shape[0], seg)))
    return jnp.concatenate(outs, axis=-1)


def _read_rows(ref, tm, first, stride):
    return jnp.concatenate([ref[pl.ds(first + s, tm, stride=stride), :] for s in range(ROW_TILES)], axis=1)


def _colbcast(x):
    z = jnp.concatenate([jnp.broadcast_to(x[:, :LANES], (64, LANES)),
                         jnp.broadcast_to(x[:, LANES:], (64, LANES))], axis=0)
    return z.T


def _rowpair(x):
    top = jnp.concatenate([x[:, 0:64], x[:, 128:192]], axis=1)
    bot = jnp.concatenate([x[:, 64:128], x[:, 192:256]], axis=1)
    return top, bot


def _unpair(top, bot):
    return jnp.concatenate([top[:, :64], bot[:, :64], top[:, 64:], bot[:, 64:]], axis=1)


def _rowbcast(top, bot):
    return jnp.concatenate([jnp.broadcast_to(top, (64, LANES)), jnp.broadcast_to(bot, (64, LANES))], axis=0)


def _halfsums(p):
    return jnp.sum(p[:64], axis=0, keepdims=True), jnp.sum(p[64:], axis=0, keepdims=True)


def _wkv_step(st, ab, wb, bb, kb, rb, vtop, vbot):
    sa_t, sa_b = _halfsums(st * ab)
    st = st * wb + _rowbcast(sa_t, sa_b) * bb + _rowbcast(vtop, vbot) * kb
    y_t, y_b = _halfsums(st * rb)
    return st, y_t, y_b


def _wkv_post(y, lnw, lnb, bonus, g):
    mu = _seg_sum(y) * (1.0 / C_HEAD)
    d = y - mu
    var = _seg_sum(d * d) * (1.0 / C_HEAD)
    yn = d * lax.rsqrt(var + C_GN_EPS)
    return (yn * lnw + lnb + bonus) * g


def _heads_to_t(s4, transpose):
    if transpose:
        m = jnp.concatenate([jnp.concatenate([s4[0], s4[1]], axis=1),
                             jnp.concatenate([s4[2], s4[3]], axis=1)], axis=0)
        return m.T
    return jnp.concatenate([jnp.concatenate([s4[0], s4[2]], axis=1),
                            jnp.concatenate([s4[1], s4[3]], axis=1)], axis=0)


def _t_to_heads(st, transpose):
    if transpose:
        m = st.T
        return [m[0:64, 0:64], m[0:64, 64:128], m[64:128, 0:64], m[64:128, 64:128]]
    return [st[0:64, 0:64], st[64:128, 0:64], st[0:64, 64:128], st[64:128, 64:128]]


def _inproj_kernel(*refs, combine, tm):
    if combine:
        x_ref, yw_ref, route_ref, g_ref, wa_ref, wr_ref, wc_ref, cos_ref, sin_ref, xo_ref, za_ref, zr_ref, zc_ref = refs
        route = route_ref[...]
        y0 = _read_rows(yw_ref, tm, 0, 2 * ROW_TILES)
        y1 = _read_rows(yw_ref, tm, ROW_TILES, 2 * ROW_TILES)
        x = x_ref[...] + (route[:, 2:3] * y0 + route[:, 3:4] * y1)
        xo_ref[...] = x
    else:
        x_ref, g_ref, wa_ref, wr_ref, wc_ref, cos_ref, sin_ref, za_ref, zr_ref, zc_ref = refs
        x = x_ref[...]
    h = _rmsnorm(x, g_ref[...]).astype(BF16)
    cos, sin = cos_ref[...], sin_ref[...]
    na = (A_WIDTH + A_KV_WIDTH) // LANES
    nr = 2 * R_HEADS * R_DK // LANES
    za = jnp.dot(h, wa_ref[...], preferred_element_type=F32)
    za_ref[:, :na * LANES] = _rope(za[:, :na * LANES], _tile_lanes(cos, na), _tile_lanes(sin, na))
    za_ref[:, na * LANES:] = za[:, na * LANES:]
    zr = jnp.dot(h, wr_ref[...], preferred_element_type=F32)
    zr_ref[:, :nr * LANES] = _rope(zr[:, :nr * LANES], _tile_lanes(cos, nr), _tile_lanes(sin, nr))
    zr_ref[:, nr * LANES:] = zr[:, nr * LANES:]
    zc_ref[...] = jnp.dot(h, wc_ref[...], preferred_element_type=F32)


def _inproj(x, moe, g, wa, wr, wc, cos, sin, tm):
    t = x.shape[0]
    row = lambda w: pl.BlockSpec((tm, w), lambda i: (i, 0))
    wspecs = [_full((1, D_MODEL)), _full(wa.shape), _full(wr.shape), _full(wc.shape), row(LANES), row(LANES)]
    zshapes = [jax.ShapeDtypeStruct((t, A_COLS), F32), jax.ShapeDtypeStruct((t, R_COLS), F32),
               jax.ShapeDtypeStruct((t, C_PAD), F32)]
    zspecs = [row(A_COLS), row(R_COLS), row(C_PAD)]
    if moe is None:
        return (None,) + tuple(pl.pallas_call(
            functools.partial(_inproj_kernel, combine=False, tm=tm), name="inproj",
            grid=(t // tm,), in_specs=[row(D_MODEL)] + wspecs, out_specs=zspecs, out_shape=zshapes,
            compiler_params=_cp(("parallel",)))(x, g, wa, wr, wc, cos, sin))
    yw, route, first_tok = moe
    off = first_tok // tm
    return pl.pallas_call(
        functools.partial(_inproj_kernel, combine=True, tm=tm), name="combine_inproj",
        grid=(t // tm,),
        in_specs=[row(D_MODEL), pl.BlockSpec((tm * 2 * ROW_TILES, LANES), lambda i: (i + off, 0)),
                  pl.BlockSpec((tm, LANES), lambda i: (i + off, 0))] + wspecs,
        out_specs=[row(D_MODEL)] + zspecs,
        out_shape=[jax.ShapeDtypeStruct((t, D_MODEL), F32)] + zshapes,
        compiler_params=_cp(("parallel",)))(x, yw, route, g, wa, wr, wc, cos, sin)


def _final_kernel(x_ref, yw_ref, route_ref, g_ref, o_ref, *, tm):
    route = route_ref[...]
    y0 = _read_rows(yw_ref, tm, 0, 2 * ROW_TILES)
    y1 = _read_rows(yw_ref, tm, ROW_TILES, 2 * ROW_TILES)
    x = x_ref[...] + (route[:, 2:3] * y0 + route[:, 3:4] * y1)
    o_ref[...] = _rmsnorm(x, g_ref[...])


def _final(x, moe, g, tm):
    t = x.shape[0]
    yw, route, first_tok = moe
    off = first_tok // tm
    row = lambda w: pl.BlockSpec((tm, w), lambda i: (i, 0))
    return pl.pallas_call(
        functools.partial(_final_kernel, tm=tm), grid=(t // tm,), name="combine_final",
        in_specs=[row(D_MODEL), pl.BlockSpec((tm * 2 * ROW_TILES, LANES), lambda i: (i + off, 0)),
                  pl.BlockSpec((tm, LANES), lambda i: (i + off, 0)), _full((1, D_MODEL))],
        out_specs=row(D_MODEL), out_shape=jax.ShapeDtypeStruct((t, D_MODEL), F32),
        compiler_params=_cp(("parallel",)))(x, yw, route, g)


def _swa_prompt_kernel(sinks_ref, q_ref, kc_ref, kp_ref, vc_ref, vp_ref, o_ref):
    i = pl.program_id(0)
    g = A_HEADS // A_KV_HEADS
    nq = g * WINDOW
    kcat = jnp.concatenate([kp_ref[...], kc_ref[...]], axis=0)
    vcat = jnp.concatenate([vp_ref[...], vc_ref[...]], axis=0)
    kswap = pltpu.roll(kcat, HEAD_DIM, axis=1)
    vswap = pltpu.roll(vcat, HEAD_DIM, axis=1)
    upper_k = lax.broadcasted_iota(jnp.int32, kcat.shape, 1) >= HEAD_DIM
    upper_q = lax.broadcasted_iota(jnp.int32, (WINDOW, LANES), 1) >= HEAD_DIM
    r = lax.broadcasted_iota(jnp.int32, (nq, 2 * WINDOW), 0) % WINDOW
    c = lax.broadcasted_iota(jnp.int32, (nq, 2 * WINDOW), 1)
    ok = (c >= r) & (c <= r + WINDOW) & ((c >= WINDOW) | (i > 0))
    slabs = []
    for kv in range(A_KV_HEADS):
        kboth = jnp.where(upper_k, kswap, kcat) if kv == 0 else jnp.where(upper_k, kcat, kswap)
        v_lo = jnp.where(upper_k, 0.0, vcat if kv == 0 else vswap)
        v_hi = jnp.where(upper_k, vswap if kv == 0 else vcat, 0.0)
        vpair = jnp.concatenate([v_lo, v_hi], axis=0)
        qs = []
        for j in range(g):
            h = kv * g + j
            slab = q_ref[:, (h // 2) * LANES:(h // 2 + 1) * LANES]
            qs.append(jnp.where(upper_q, slab, 0.0) if h % 2 else jnp.where(upper_q, 0.0, slab))
        q4 = jnp.concatenate(qs, axis=0)
        s = lax.dot_general(q4, kboth, (((1,), (1,)), ((), ())), preferred_element_type=F32) * (HEAD_DIM ** -0.5)
        s = jnp.where(ok, s, -jnp.inf)
        sink = jnp.concatenate([jnp.full((WINDOW, 1), sinks_ref[kv * g + j], F32) for j in range(g)], axis=0)
        m = jnp.maximum(jnp.max(s, -1, keepdims=True), sink)
        p = jnp.exp(s - m)
        p = p / (jnp.sum(p, -1, keepdims=True) + jnp.exp(sink - m))
        for j in range(0, g, 2):
            pp = jnp.concatenate([p[j * WINDOW:(j + 1) * WINDOW], p[(j + 1) * WINDOW:(j + 2) * WINDOW]], axis=1)
            slabs.append(jnp.dot(pp, vpair, preferred_element_type=F32))
    o_ref[...] = jnp.concatenate(slabs, axis=1)


def _swa_prompt(za, sinks):
    l = za.shape[0]
    nb = l // WINDOW
    cur = lambda w, c: pl.BlockSpec((WINDOW, w), lambda i: (i, c))
    prv = lambda w, c: pl.BlockSpec((WINDOW, w), lambda i: (jnp.maximum(i - 1, 0), c))
    kcol, vcol = A_WIDTH // A_KV_WIDTH, A_WIDTH // A_KV_WIDTH + 1
    return pl.pallas_call(
        _swa_prompt_kernel, grid=(nb,), name="swa_prompt",
        in_specs=[pl.BlockSpec(memory_space=pltpu.SMEM),
                  cur(A_WIDTH, 0), cur(A_KV_WIDTH, kcol), prv(A_KV_WIDTH, kcol),
                  cur(A_KV_WIDTH, vcol), prv(A_KV_WIDTH, vcol)],
        out_specs=cur(A_WIDTH, 0),
        out_shape=jax.ShapeDtypeStruct((l, A_WIDTH), F32),
        compiler_params=_cp(("parallel",)))(sinks, za, za, za, za, za)


def _ret_prompt_kernel(cdec_ref, zr_ref, intra_ref, qdec_ref, kdec_ref, o_ref, so_ref, s_scr):
    @pl.when(pl.program_id(0) == 0)
    def _():
        s_scr[...] = jnp.zeros_like(s_scr)

    z = zr_ref[...]
    qk = R_HEADS * R_DK
    q = z[:, :qk]
    k = z[:, qk:2 * qk] * (R_DK ** -0.5)
    v = z[:, 2 * qk:2 * qk + R_WIDTH]
    gate = z[:, 2 * qk + R_WIDTH:]
    qdec = qdec_ref[...]
    kd = k * kdec_ref[...]
    outs = []
    for h in range(R_HEADS):
        sl = slice(h * R_DK, (h + 1) * R_DK)
        qh, kh, vh = q[:, sl], k[:, sl], v[:, sl]
        att = lax.dot_general(qh, kh, (((1,), (1,)), ((), ())), preferred_element_type=F32) * intra_ref[h]
        s = s_scr[h]
        o = jnp.dot(att, vh, preferred_element_type=F32) + jnp.dot(qh, s, preferred_element_type=F32) * qdec[:, sl]
        s_scr[h] = s * cdec_ref[h] + jnp.dot(kd[:, sl].T, vh, preferred_element_type=F32)
        outs.append(o * lax.rsqrt(jnp.mean(o * o, -1, keepdims=True) + EPS))
    o_ref[...] = jnp.concatenate(outs, axis=1) * _silu(gate)
    so_ref[...] = s_scr[...]


def _ret_tables(c):
    log_g = jnp.log1p(-jnp.exp2(-5.0 - jnp.arange(R_HEADS, dtype=F32)))
    idx = jnp.arange(c, dtype=F32)
    diff = idx[:, None] - idx[None, :]
    intra = jnp.where(diff >= 0, jnp.exp(log_g[:, None, None] * jnp.maximum(diff, 0.0)), 0.0)
    q_dec = jnp.exp(log_g[None, :] * (idx[:, None] + 1.0))
    k_dec = jnp.exp(log_g[None, :] * (c - 1.0 - idx[:, None]))
    c_dec = jnp.exp(log_g * c)
    return intra, jnp.repeat(q_dec, R_DK, axis=1), jnp.repeat(k_dec, R_DK, axis=1), c_dec


def _ret_prompt(zr):
    l = zr.shape[0]
    c = R_CHUNK
    intra, qdec, kdec, cdec = _ret_tables(c)
    blk = lambda w: pl.BlockSpec((c, w), lambda i: (i, 0))
    return pl.pallas_call(
        _ret_prompt_kernel, grid=(l // c,), name="ret_prompt",
        in_specs=[pl.BlockSpec(memory_space=pltpu.SMEM), blk(R_COLS),
                  _full((R_HEADS, c, c)), _full((c, R_HEADS * R_DK)), _full((c, R_HEADS * R_DK))],
        out_specs=[blk(R_WIDTH), _full((R_HEADS, R_DK, R_DV))],
        out_shape=[jax.ShapeDtypeStruct((l, R_WIDTH), F32), jax.ShapeDtypeStruct((R_HEADS, R_DK, R_DV), F32)],
        scratch_shapes=[pltpu.VMEM((R_HEADS, R_DK, R_DV), F32)],
        compiler_params=_cp(("arbitrary",)))(cdec, zr, intra, qdec, kdec)


def _rwkv_prep_kernel(zc_ref, prev_ref, shift0_ref, mu_ref, w0_ref, w2_ref, a0_ref, a2_ref, g2_ref,
                      kk_ref, ka_ref, rk_ref,
                      r_ref, w_ref, k_ref, v_ref, a_ref, b_ref, g_ref, bonus_ref, *scratch, sequence, tm):
    zc = zc_ref[...]
    if sequence:
        row = lax.broadcasted_iota(jnp.int32, zc.shape, 0)
        boundary = jnp.where(pl.program_id(0) == 0, shift0_ref[...], prev_ref[SUBLANES - 1:SUBLANES, :])
        prev = jnp.where(row == 0, jnp.broadcast_to(boundary, zc.shape), pltpu.roll(zc, 1, axis=0))
    else:
        prev = prev_ref[...]
    zs = zc + mu_ref[...] * (prev - zc)
    o1, o2, o3 = C_WIDTH, 2 * C_WIDTH, 3 * C_WIDTH
    r, k, v = zs[:, :o1], zs[:, o1:o2], zs[:, o2:o3]
    lora = zs[:, o3:o3 + LANES]
    gl = zs[:, o3 + LANES:]
    w = -jax.nn.softplus(-(w0_ref[...] + jnp.dot(jnp.tanh(lora), w2_ref[...], preferred_element_type=F32))) - 0.5
    decay = jnp.exp(-jnp.exp(w))
    a = _sigmoid(a0_ref[...] + jnp.dot(lora, a2_ref[...], preferred_element_type=F32))
    g = jnp.dot(_sigmoid(gl), g2_ref[...], preferred_element_type=F32)
    kk = k * kk_ref[...]
    kk = kk / jnp.maximum(jnp.sqrt(_seg_sum(kk * kk)), 1e-12)
    k_mod = k * (1.0 + (a - 1.0) * ka_ref[...])
    v_ref[...] = v
    g_ref[...] = g
    bonus_ref[...] = _seg_sum(r * k_mod * rk_ref[...]) * v
    if not sequence:
        r_ref[...] = r
        w_ref[...] = decay
        k_ref[...] = k_mod
        a_ref[...] = -kk
        b_ref[...] = kk * a
        return
    sub =lax.broadcasted_iota(jnp.int32, decay.shape, 0) % SCAN_SUB
    cp = decay
    shift = 1
    while shift < SCAN_SUB:
        cp = jnp.where(sub >= shift, cp * pltpu.roll(cp, shift, axis=0), cp)
        shift *= 2
    cpx = jnp.where(sub == 0, 1.0, pltpu.roll(cp, 1, axis=0))
    inv = 1.0 / cp
    r_ref[...] = r * cp
    k_ref[...] = k_mod * inv
    a_ref[...] = -kk * cpx
    b_ref[...] = kk * a * inv
    for n, scr in enumerate(scratch):
        scr[...] = cp[:, n * LANES:(n + 1) * LANES]
        w_ref[:, n * LANES:(n + 1) * LANES] = scr[pl.ds(SCAN_SUB - 1, tm // SCAN_SUB, stride=SCAN_SUB), :]


def _rwkv_prep(zc, prev, shift0, cw, sequence, tm):
    t = zc.shape[0]
    row = lambda w: pl.BlockSpec((tm, w), lambda i: (i, 0))
    if sequence:
        per = tm // SUBLANES
        prev_spec = pl.BlockSpec((SUBLANES, C_PAD), lambda i: (jnp.maximum(i * per - 1, 0), 0))
        prev = zc
    else:
        prev_spec = row(C_PAD)
    vec = _full((1, C_WIDTH))
    out_specs = [row(C_WIDTH)] * 8
    out_shape = [jax.ShapeDtypeStruct((t, C_WIDTH), F32)] * 8
    if sequence:
        out_specs[1] = pl.BlockSpec((tm // SCAN_SUB, C_WIDTH), lambda i: (i, 0))
        out_shape[1] = jax.ShapeDtypeStruct((t // SCAN_SUB, C_WIDTH), F32)
    return pl.pallas_call(
        functools.partial(_rwkv_prep_kernel, sequence=sequence, tm=tm), grid=(t // tm,), name="rwkv_prep",
        in_specs=[row(C_PAD), prev_spec, _full((1, C_PAD)), _full((1, C_PAD)), vec, _full((LANES, C_WIDTH)),
                  vec, _full((LANES, C_WIDTH)), _full((2 * LANES, C_WIDTH)), vec, vec, vec],
        out_specs=out_specs, out_shape=out_shape,
        scratch_shapes=[pltpu.VMEM((tm, LANES), F32)] * (C_WIDTH // LANES) if sequence else [],
        compiler_params=_cp(("parallel",)))(zc, prev, shift0, cw["mu"], cw["w0"], cw["w2"], cw["a0"], cw["a2"],
                                           cw["g2"], cw["k_k"], cw["k_a"], cw["r_k"])


SCAN_BLOCK = 256
SCAN_SUB = 16
SCAN_NSUB = SCAN_BLOCK // SCAN_SUB
SCAN_UNROLL = 16
SCAN_OPS = 4


def _scan_select():
    sel = np.zeros((SCAN_SUB // 2, LANES, 2 * LANES), np.float32)
    for p in range(SCAN_SUB // 2):
        for piece in range(3):
            for rh in range(2):
                for tl in range(2):
                    row = piece * 2 * SCAN_SUB + rh * SCAN_SUB + 2 * p + tl
                    sel[p, row, tl * LANES + rh * 64: tl * LANES + (rh + 1) * 64] = 1.0
    return jnp.asarray(sel, BF16)


def _stage_lhs(ops, base, lhs_scr):
    for n, ref in enumerate(ops):
        xs = ref[pl.ds(base, SCAN_SUB), :]
        hi = xs.astype(BF16).astype(F32)
        r1 = xs - hi
        mid = r1.astype(BF16).astype(F32)
        lo = (r1 - mid).astype(BF16).astype(F32)
        g = jnp.concatenate([hi[:, :LANES], hi[:, LANES:], mid[:, :LANES], mid[:, LANES:],
                             lo[:, :LANES], lo[:, LANES:], jnp.zeros((LANES - 6 * SCAN_SUB, LANES), F32)], axis=0)
        lhs_scr[n] = g.T.astype(BF16)


def _stage_tiles(lhs_scr, sel_ref, xb_scr, p):
    sel = sel_ref[p]
    for n in range(SCAN_OPS):
        out = jnp.dot(lhs_scr[n], sel, preferred_element_type=F32)
        xb_scr[n, 2 * p] = out[:, :LANES]
        xb_scr[n, 2 * p + 1] = out[:, LANES:]


def _rwkv_scan_kernel(r_ref, pc_ref, k_ref, v_ref, a_ref, b_ref, g_ref, bonus_ref, lnw_ref, lnb_ref, sel_ref,
                      o_ref, so_ref, st_scr, lhs_a, lhs_b, xb_a, xb_b, vt_scr, vb_scr, yt_scr, yb_scr):
    @pl.when(pl.program_id(0) == 0)
    def _():
        st_scr[...] = jnp.zeros_like(st_scr)

    vtop, vbot = _rowpair(v_ref[...])
    vt_scr[...] = vtop
    vb_scr[...] = vbot
    ops = (a_ref, b_ref, k_ref, r_ref)
    pairs = SCAN_UNROLL // 2
    _stage_lhs(ops, 0, lhs_a)
    for p in range(SCAN_SUB // 2):
        _stage_tiles(lhs_a, sel_ref, xb_a, p)
    _stage_lhs(ops, SCAN_SUB, lhs_b)

    def half(sc, st, lhs_cur, xb_cur, lhs_nxt, xb_nxt):
        base = pl.multiple_of(sc * SCAN_SUB, SCAN_SUB)

        def quad(q, st):
            for pp in range(pairs):
                _stage_tiles(lhs_nxt, sel_ref, xb_nxt, q * pairs + pp)
            for uu in range(SCAN_UNROLL):
                u = q * SCAN_UNROLL + uu
                t = base + u
                sa_t, sa_b = _halfsums(st * xb_cur[0, u])
                st = (st + _rowbcast(sa_t, sa_b) * xb_cur[1, u]
                      + _rowbcast(vt_scr[pl.ds(t, 1), :], vb_scr[pl.ds(t, 1), :]) * xb_cur[2, u])
                y_t, y_b = _halfsums(st * xb_cur[3, u])
                yt_scr[pl.ds(t, 1), :] = y_t
                yb_scr[pl.ds(t, 1), :] = y_b
            return st

        st = lax.fori_loop(0, SCAN_SUB // SCAN_UNROLL, quad, st) * _colbcast(pc_ref[pl.ds(sc, 1), :])
        nxt2 = pl.multiple_of(jnp.minimum(sc + 2, SCAN_NSUB - 1) * SCAN_SUB, SCAN_SUB)
        _stage_lhs(ops, nxt2, lhs_cur)
        return st

    def sub2(s2, st):
        st = half(2 * s2, st, lhs_a, xb_a, lhs_b, xb_b)
        return half(2 * s2 + 1, st, lhs_b, xb_b, lhs_a, xb_a)

    st = lax.fori_loop(0, SCAN_NSUB // 2, sub2, st_scr[...])
    st_scr[...] = st
    y = _unpair(yt_scr[...], yb_scr[...])
    o_ref[...] = _wkv_post(y, lnw_ref[...], lnb_ref[...], bonus_ref[...], g_ref[...])
    heads = _t_to_heads(st, transpose=True)
    for h in range(C_HEADS):
        so_ref[h] = heads[h]


def _rwkv_scan(prep, lnw, lnb):
    r, pc, k, v, a, b, g, bonus = prep
    l = r.shape[0]
    blk = pl.BlockSpec((SCAN_BLOCK, C_WIDTH), lambda i: (i, 0))
    vec = _full((1, C_WIDTH))
    sel = _scan_select()
    return pl.pallas_call(
        _rwkv_scan_kernel, grid=(l // SCAN_BLOCK,), name="rwkv_scan",
        in_specs=[blk, pl.BlockSpec((SCAN_NSUB, C_WIDTH), lambda i: (i, 0))] + [blk] * 6 + [vec, vec, _full(sel.shape)],
        out_specs=[blk, _full((C_HEADS, C_HEAD, C_HEAD))],
        out_shape=[jax.ShapeDtypeStruct((l, C_WIDTH), F32), jax.ShapeDtypeStruct((C_HEADS, C_HEAD, C_HEAD), F32)],
        scratch_shapes=[pltpu.VMEM((LANES, LANES), F32),
                        pltpu.VMEM((SCAN_OPS, LANES, LANES), BF16), pltpu.VMEM((SCAN_OPS, LANES, LANES), BF16),
                        pltpu.VMEM((SCAN_OPS, SCAN_SUB, LANES, LANES), F32),
                        pltpu.VMEM((SCAN_OPS, SCAN_SUB, LANES, LANES), F32),
                        pltpu.VMEM((SCAN_BLOCK, LANES), F32), pltpu.VMEM((SCAN_BLOCK, LANES), F32),
                        pltpu.VMEM((SCAN_BLOCK, LANES), F32), pltpu.VMEM((SCAN_BLOCK, LANES), F32)],
        compiler_params=_cp(("arbitrary",)))(r, pc, k, v, a, b, g, bonus, lnw, lnb, sel)


def _sample_kernel(sinks_ref, gam_ref, za_ref, zr_ref, r_ref, w_ref, k_ref, v_ref, a_ref, b_ref, g_ref, bonus_ref,
                   kc_ref, vc_ref, sret_ref, swkv_ref, lnw_ref, lnb_ref,
                   oa_ref, or_ref, oc_ref, kco_ref, vco_ref, sreto_ref, swkvo_ref):
    lane =lax.broadcasted_iota(jnp.int32, (1, LANES), 1)
    row8 = lax.broadcasted_iota(jnp.int32, (SUBLANES, LANES), 0)

    za = za_ref[0]
    q = za[:, :A_WIDTH]
    knew = za[:, A_WIDTH:A_WIDTH + A_KV_WIDTH]
    vnew = za[:, A_WIDTH + A_KV_WIDTH:]
    kc, vc = kc_ref[0], vc_ref[0]
    g = A_HEADS // A_KV_HEADS
    zero = jnp.zeros((1, HEAD_DIM), F32)
    qrows = []
    for h in range(A_HEADS):
        qh = q[:, h * HEAD_DIM:(h + 1) * HEAD_DIM]
        qrows.append(jnp.concatenate([qh, zero] if h // g == 0 else [zero, qh], axis=1))
    qm = jnp.concatenate(qrows + [jnp.zeros((LANES - A_HEADS, LANES), F32)], axis=0)
    scale = HEAD_DIM ** -0.5
    s = lax.dot_general(kc, qm, (((1,), (1,)), ((), ())), preferred_element_type=F32) * scale
    kn8 = jnp.where(row8 == 0, jnp.broadcast_to(knew, (SUBLANES, LANES)), 0.0)
    s_new = lax.dot_general(kn8, qm, (((1,), (1,)), ((), ())), preferred_element_type=F32)[0:1] * scale
    sink = jnp.zeros((1, LANES), F32)
    for h in range(A_HEADS):
        sink = jnp.where(lane == h, sinks_ref[h], sink)
    m = jnp.maximum(jnp.maximum(jnp.max(s, axis=0, keepdims=True), s_new), sink)
    p = jnp.exp(s - m)
    p_new = jnp.exp(s_new - m)
    denom = jnp.sum(p, axis=0, keepdims=True) + p_new + jnp.exp(sink - m)
    p = p / denom
    p_new = p_new / denom
    pn_col = jnp.broadcast_to(p_new, (LANES, LANES)).T[:, 0:1]
    o_full = jnp.dot(p.T, vc, preferred_element_type=F32) + pn_col * vnew
    oa_ref[0] = jnp.concatenate(
        [o_full[h:h + 1, (h // g) * HEAD_DIM:(h // g + 1) * HEAD_DIM] for h in range(A_HEADS)], axis=1)
    rowk = lax.broadcasted_iota(jnp.int32, kc.shape, 0)
    last = rowk == kc.shape[0] - 1
    kco_ref[0] = jnp.where(last, jnp.broadcast_to(knew, kc.shape), pltpu.roll(kc, kc.shape[0] - 1, axis=0))
    vco_ref[0] = jnp.where(last, jnp.broadcast_to(vnew, vc.shape), pltpu.roll(vc, vc.shape[0] - 1, axis=0))

    zr = zr_ref[0]
    qk = R_HEADS * R_DK
    rq = zr[:, :qk]
    rk = zr[:, qk:2 * qk] * (R_DK ** -0.5)
    rv = zr[:, 2 * qk:2 * qk + R_WIDTH]
    rg = zr[:, 2 * qk + R_WIDTH:]
    st = _heads_to_t(sret_ref[0], transpose=False)
    rr = lax.broadcasted_iota(jnp.int32, (LANES, LANES), 0) >= 64
    cc = lax.broadcasted_iota(jnp.int32, (LANES, LANES), 1) >= 64
    gamma = jnp.where(rr, jnp.where(cc, gam_ref[3], gam_ref[1]), jnp.where(cc, gam_ref[2], gam_ref[0]))
    vt, vb = _rowpair(rv)
    st = st * gamma + _colbcast(rk) * _rowbcast(vt, vb)
    o_t, o_b = _halfsums(st * _colbcast(rq))
    o = _unpair(o_t, o_b)
    o = o * lax.rsqrt(_seg_sum(o * o) * (1.0 / R_DV) + EPS)
    or_ref[0] = o * _silu(rg)
    heads = _t_to_heads(st, transpose=False)
    for h in range(R_HEADS):
        sreto_ref[0, h] = heads[h]

    st = _heads_to_t(swkv_ref[0], transpose=True)
    vt, vb = _rowpair(v_ref[0])
    st, y_t, y_b = _wkv_step(st, _colbcast(a_ref[0]), _colbcast(w_ref[0]), _colbcast(b_ref[0]),
                             _colbcast(k_ref[0]), _colbcast(r_ref[0]), vt, vb)
    oc_ref[0] = _wkv_post(_unpair(y_t, y_b), lnw_ref[...], lnb_ref[...], bonus_ref[0], g_ref[0])
    heads = _t_to_heads(st, transpose=True)
    for h in range(C_HEADS):
        swkvo_ref[0, h] = heads[h]


def _sample_mixers(za, zr, prep, kc, vc, sret, swkv, sinks, lnw, lnb):
    b = za.shape[0]
    wb = kc.shape[1]
    assert wb <= WINDOW and PAST_LEN >= wb
    gam = jnp.exp(jnp.log1p(-jnp.exp2(-5.0 - jnp.arange(R_HEADS, dtype=F32))) * 1.0)
    tok = lambda w: pl.BlockSpec((1, 1, w), lambda i: (i, 0, 0))
    cache = pl.BlockSpec((1, wb, A_KV_WIDTH), lambda i: (i, 0, 0))
    state = pl.BlockSpec((1, 4, 64, 64), lambda i: (i, 0, 0, 0))
    smem = pl.BlockSpec(memory_space=pltpu.SMEM)
    vec = _full((1, C_WIDTH))
    r3 = lambda x: x.reshape(b, 1, x.shape[-1])
    outs = pl.pallas_call(
        _sample_kernel, grid=(b,), name="sample_mixers",
        in_specs=[smem, smem, tok(A_COLS), tok(R_COLS)] + [tok(C_WIDTH)] * 8 + [cache, cache, state, state,
                  vec, vec],
        out_specs=[tok(A_WIDTH), tok(R_WIDTH), tok(C_WIDTH), cache, cache, state, state],
        out_shape=[jax.ShapeDtypeStruct((b, 1, A_WIDTH), F32), jax.ShapeDtypeStruct((b, 1, R_WIDTH), F32),
                   jax.ShapeDtypeStruct((b, 1, C_WIDTH), F32), jax.ShapeDtypeStruct(kc.shape, F32),
                   jax.ShapeDtypeStruct(vc.shape, F32), jax.ShapeDtypeStruct(sret.shape, F32),
                   jax.ShapeDtypeStruct(swkv.shape, F32)],
        compiler_params=_cp(("parallel",)))(sinks, gam, r3(za), r3(zr), *[r3(x) for x in prep],
                                           kc, vc, sret, swkv, lnw, lnb)
    oa, orr, oc = (x.reshape(b, x.shape[-1]) for x in outs[:3])
    return (oa, orr, oc) + tuple(outs[3:])


def _outproj_kernel(oa_ref, or_ref, oc_ref, x_ref, wo_ref, gf_ref, wrt_ref, brt_ref, *rest, tm):
    x1_ref, h2_ref, route_ref = rest[-3:]
    mix =jnp.concatenate([oa_ref[...], or_ref[...], oc_ref[...]], axis=1).astype(BF16)
    x1 = x_ref[...] + jnp.dot(mix, wo_ref[...], preferred_element_type=F32)
    x1_ref[...] = x1
    h2 = _rmsnorm(x1, gf_ref[...])
    for s in range(ROW_TILES):
        h2_ref[pl.ds(s, tm, stride=ROW_TILES), :] = h2[:, s * LANES:(s + 1) * LANES]
    h_hi = h2.astype(BF16)
    h_lo = (h2 - h_hi.astype(F32)).astype(BF16)
    w_hi, w_lo = wrt_ref[0], wrt_ref[1]
    logits = (jnp.dot(h_hi, w_hi, preferred_element_type=F32) + jnp.dot(h_lo, w_hi, preferred_element_type=F32)
              + jnp.dot(h_hi, w_lo, preferred_element_type=F32)) + brt_ref[...]
    lane = lax.broadcasted_iota(jnp.int32, logits.shape, 1).astype(F32)
    big = float(LANES)
    is_g = lane < N_GROUPS
    lg = jnp.where(is_g, logits, -jnp.inf)
    m_g = jnp.max(lg, -1, keepdims=True)
    grp = jnp.min(jnp.where(lg == m_g, lane, big), -1, keepdims=True)
    p_grp = 1.0 / jnp.sum(jnp.where(is_g, jnp.exp(lg - m_g), 0.0), -1, keepdims=True)
    lo = N_GROUPS + EXPERTS_PER_GROUP * grp
    in_grp = (lane >= lo) & (lane < lo + EXPERTS_PER_GROUP)
    le = jnp.where(in_grp, logits, -jnp.inf)
    l1 = jnp.max(le, -1, keepdims=True)
    i1 = jnp.min(jnp.where(le == l1, lane, big), -1, keepdims=True)
    le2 = jnp.where(lane == i1, -jnp.inf, le)
    l2 = jnp.max(le2, -1, keepdims=True)
    i2 = jnp.min(jnp.where(le2 == l2, lane, big), -1, keepdims=True)
    e = jnp.exp(l2 - l1)
    g1 = p_grp / (1.0 + e)
    g2 = p_grp * e / (1.0 + e)
    route = jnp.where(lane == 0, i1 - N_GROUPS, jnp.where(lane == 1, i2 - N_GROUPS,
                      jnp.where(lane == 2, g1, jnp.where(lane == 3, g2, 0.0))))
    route_ref[...] = route


def _outproj(oa, orr, oc, x, wo, gf, wrt, brt, tm, t_all, first_tok, shared=None):
    t = x.shape[0]
    off = first_tok // tm
    nblk = t // tm
    steps = nblk + (1 if shared is None else 0)
    assert shared is not None or t_all == t + tm
    row = lambda w: pl.BlockSpec((tm, w), lambda i: (jnp.minimum(i, nblk - 1), 0))
    in_specs = [row(A_WIDTH), row(R_WIDTH), row(C_WIDTH), row(D_MODEL), _full(wo.shape), _full((1, D_MODEL)),
                _full(wrt.shape), _full((1, LANES))]
    args = [oa, orr, oc, x, wo, gf, wrt, brt]
    aliases = {}
    if shared is not None:
        aliases = {len(args): 1, len(args) + 1: 2}
        in_specs += [pl.BlockSpec(memory_space=pl.ANY)] * 2
        args += list(shared)
    return pl.pallas_call(
        functools.partial(_outproj_kernel, tm=tm), grid=(steps,), name="outproj_router",
        in_specs=in_specs,
        out_specs=[row(D_MODEL), pl.BlockSpec((tm * ROW_TILES, LANES), lambda i: (i + off, 0)),
                   pl.BlockSpec((tm, LANES), lambda i: (i + off, 0))],
        out_shape=[jax.ShapeDtypeStruct((t, D_MODEL), F32), jax.ShapeDtypeStruct((t_all * ROW_TILES, LANES), F32),
                   jax.ShapeDtypeStruct((t_all, LANES), F32)],
        input_output_aliases=aliases,
        compiler_params=_cp(("arbitrary",)))(*args)


def _moe_kernel(blk_exp_ref, src_ref, dst_ref, nused_ref, h2_hbm, wg_ref, wu_ref, wd_ref, yw_hbm,
                gbuf, sbuf, xbuf, wgb, wub, wdb, gsem, ssem):
    i = pl.program_id(0)
    nb = pl.num_programs(0)
    nused = nused_ref[0]
    slot = i % 2

    def tile(off):
        return pl.ds(pl.multiple_of(off, ROW_TILES), ROW_TILES)

    def gather(blk, sl):
        def body(r, _):
            pltpu.make_async_copy(h2_hbm.at[tile(src_ref[blk * MOE_ROWS + r]), :],
                                  gbuf.at[sl, tile(r * ROW_TILES), :], gsem.at[sl]).start()
            return 0
        lax.fori_loop(0, MOE_ROWS, body, 0, unroll=DMA_UNROLL)

    def gather_wait(sl):
        def body(r, _):
            pltpu.make_async_copy(h2_hbm.at[tile(0), :], gbuf.at[sl, tile(0), :], gsem.at[sl]).wait()
            return 0
        lax.fori_loop(0, MOE_ROWS, body, 0, unroll=DMA_UNROLL)

    def scatter(blk):
        def body(r, _):
            pltpu.make_async_copy(sbuf.at[tile(r * ROW_TILES), :],
                                  yw_hbm.at[tile(dst_ref[blk * MOE_ROWS + r]), :], ssem).start()
            return 0
        lax.fori_loop(0, MOE_ROWS, body, 0, unroll=DMA_UNROLL)

    def scatter_wait():
        def body(r, _):
            pltpu.make_async_copy(sbuf.at[tile(0), :], yw_hbm.at[tile(0), :], ssem).wait()
            return 0
        lax.fori_loop(0, MOE_ROWS, body, 0, unroll=DMA_UNROLL)

    @pl.when((i == 0) & (nused > 0))
    def _():
        gather(0, 0)

    @pl.when(i < nused)
    def _():
        @pl.when((i == 0) | (blk_exp_ref[i] != blk_exp_ref[jnp.maximum(i - 1, 0)]))
        def _():
            wgb[...] = wg_ref[0].astype(BF16)
            wub[...] = wu_ref[0].astype(BF16)
            wdb[...] = wd_ref[0].astype(BF16)

        gather_wait(slot)

        @pl.when(i + 1 < nused)
        def _():
            gather(i + 1, 1 - slot)

        for s in range(ROW_TILES):
            xbuf[:, s * LANES:(s + 1) * LANES] = gbuf[slot, pl.ds(s, MOE_ROWS, stride=ROW_TILES), :].astype(BF16)
        x = xbuf[...]
        hid = _silu(jnp.dot(x, wgb[...], preferred_element_type=F32)) * jnp.dot(x, wub[...], preferred_element_type=F32)
        y = jnp.dot(hid.astype(BF16), wdb[...], preferred_element_type=F32)

        @pl.when(i > 0)
        def _():
            scatter_wait()

        for s in range(ROW_TILES):
            sbuf[pl.ds(s, MOE_ROWS, stride=ROW_TILES), :] = y[:, s * LANES:(s + 1) * LANES]
        scatter(i)

    @pl.when((i == nb - 1) & (nused > 0))
    def _():
        scatter_wait()
        spare = pltpu.make_async_copy(sbuf, yw_hbm.at[pl.ds(yw_hbm.shape[0] - MOE_ROWS * ROW_TILES,
                                                            MOE_ROWS * ROW_TILES), :], ssem)
        spare.start()
        spare.wait()


def _moe(h2, route, t, wg, wu, wd):
    na = t * TOP_K
    assert na % MOE_ROWS == 0 and na < (1 << IDX_BITS)
    flat_e = route[:t, :TOP_K].astype(jnp.int32).reshape(na)
    experts = jnp.arange(N_EXPERTS, dtype=jnp.int32)
    counts = jnp.sum(flat_e[:, None] == experts[None, :], axis=0, dtype=jnp.int32)
    npad = (-counts) % MOE_ROWS
    real_keys = (flat_e << (IDX_BITS + 1)) | jnp.arange(na, dtype=jnp.int32)
    q = jnp.arange(MOE_ROWS, dtype=jnp.int32)[None, :]
    pad_keys = jnp.where(q < npad[:, None], (experts[:, None] << (IDX_BITS + 1)) | (1 << IDX_BITS) | q,
                         jnp.iinfo(jnp.int32).max)
    keys = jnp.sort(jnp.concatenate([real_keys, pad_keys.reshape(-1)]))
    n_rows = na + N_EXPERTS * MOE_ROWS
    n_blocks = n_rows // MOE_ROWS
    is_pad = ((keys >> IDX_BITS) & 1) == 1
    idx = keys & ((1 << IDX_BITS) - 1)
    pos = jnp.arange(n_rows, dtype=jnp.int32)
    src = jnp.where(is_pad, 0, (idx // TOP_K) * ROW_TILES).astype(jnp.int32)
    dst = jnp.where(is_pad, na + pos % MOE_ROWS, idx).astype(jnp.int32) * ROW_TILES
    blk_exp = jnp.minimum(keys[::MOE_ROWS] >> (IDX_BITS + 1), N_EXPERTS - 1).astype(jnp.int32)
    nused = (jnp.sum(counts + npad) // MOE_ROWS).astype(jnp.int32).reshape(1)
    wspec = lambda a, b: pl.BlockSpec((1, a, b), lambda i, be, s, d, nu: (be[i], 0, 0))
    return pl.pallas_call(
        _moe_kernel, name="moe_experts",
        grid_spec=pltpu.PrefetchScalarGridSpec(
            num_scalar_prefetch=4, grid=(n_blocks,),
            in_specs=[pl.BlockSpec(memory_space=pl.ANY), wspec(D_MODEL, EXPERT_FF), wspec(D_MODEL, EXPERT_FF),
                      wspec(EXPERT_FF, D_MODEL)],
            out_specs=pl.BlockSpec(memory_space=pl.ANY),
            scratch_shapes=[pltpu.VMEM((2, MOE_ROWS * ROW_TILES, LANES), F32),
                            pltpu.VMEM((MOE_ROWS * ROW_TILES, LANES), F32),
                            pltpu.VMEM((MOE_ROWS, D_MODEL), BF16),
                            pltpu.VMEM((D_MODEL, EXPERT_FF), BF16), pltpu.VMEM((D_MODEL, EXPERT_FF), BF16),
                            pltpu.VMEM((EXPERT_FF, D_MODEL), BF16),
                            pltpu.SemaphoreType.DMA((2,)), pltpu.SemaphoreType.DMA(())]),
        out_shape=jax.ShapeDtypeStruct(((na + MOE_ROWS) * ROW_TILES, LANES), F32),
        compiler_params=_cp(("arbitrary",)))(blk_exp, src, dst, nused, h2, wg, wu, wd)


def _rope_tables(pos):
    half = HEAD_DIM // 2
    inv = ROPE_THETA ** (-jnp.arange(half, dtype=F32) / half)
    ang = pos.astype(F32)[:, None] * inv[None, :]
    c, s = jnp.cos(ang), jnp.sin(ang)
    return jnp.concatenate([c, c, c, c], axis=1), jnp.concatenate([-s, s, -s, s], axis=1)


def _pad_cols(w, n):
    return jnp.pad(w, ((0, 0), (0, n - w.shape[1])))


def _pad_rows(w, lo, n):
    return jnp.pad(w, ((lo, n - lo - w.shape[0]), (0, 0)))


def kernel(x_prompt, x_sample, cache_swa_k, cache_swa_v, state_ret, state_wkv, state_shift, norm_mix, w_in, sinks,
           rwkv_mu, rwkv_w0, rwkv_w2, rwkv_a0, rwkv_a2, rwkv_g2, rwkv_k_k, rwkv_k_a, rwkv_r_k, rwkv_ln_w, rwkv_ln_b,
           w_out, norm_ffn, router_g, router_g_b, router_e, router_e_b, expert_w_gate, expert_w_up, expert_w_down,
           norm_final):
    lp = x_prompt.shape[1]
    bs = x_sample.shape[0]
    wb = cache_swa_k.shape[2]
    tm_p = 512
    xp = x_prompt.reshape(lp, D_MODEL)
    xs = x_sample.reshape(bs, D_MODEL)
    cos_p, sin_p = _rope_tables(jnp.arange(lp, dtype=jnp.int32))
    cos_s, sin_s = (jnp.broadcast_to(t, (bs, LANES))
                    for t in _rope_tables(PAST_LEN + jnp.arange(x_sample.shape[1], dtype=jnp.int32)))
    outs_p = [[] for _ in range(5)]
    outs_s = [[] for _ in range(5)]
    moe_p = moe_s = None
    row = lambda v: v.reshape(1, -1)
    for i in range(DEPTH):
        wi = w_in[i].astype(BF16)
        wa, wr, wc = wi[:, :A_COLS], wi[:, A_COLS:A_COLS + R_COLS], _pad_cols(wi[:, A_COLS + R_COLS:], C_PAD)
        cw = {
            "mu": _pad_cols(row(rwkv_mu[i]), C_PAD), "w0": row(rwkv_w0[i]), "a0": row(rwkv_a0[i]),
            "w2": _pad_rows(rwkv_w2[i], 0, LANES), "a2": _pad_rows(rwkv_a2[i], C_DECAY_LORA, LANES),
            "g2": _pad_rows(rwkv_g2[i], 0, 2 * LANES),
            "k_k": row(rwkv_k_k[i]), "k_a": row(rwkv_k_a[i]), "r_k": row(rwkv_r_k[i]),
        }
        lnw, lnb = row(rwkv_ln_w[i]), row(rwkv_ln_b[i])
        g_mix, g_ffn = row(norm_mix[i]), row(norm_ffn[i])
        wo = w_out[i].astype(BF16)
        wrt = _pad_cols(jnp.concatenate([router_g[i], router_e[i]], axis=1), LANES)
        wrt_hi = wrt.astype(BF16)
        wrt = jnp.stack([wrt_hi, (wrt - wrt_hi.astype(F32)).astype(BF16)])
        brt = _pad_cols(row(jnp.concatenate([router_g_b[i], router_e_b[i]])), LANES)

        xn, za, zr, zc = _inproj(xp, moe_p, g_mix, wa, wr, wc, cos_p, sin_p, tm_p)
        xp = xp if xn is None else xn
        oa = _swa_prompt(za, sinks[i])
        orr, s_ret = _ret_prompt(zr)
        prep = _rwkv_prep(zc, None, jnp.zeros((1, C_PAD), F32), cw, True, tm_p)
        oc, s_wkv = _rwkv_scan(prep, lnw, lnb)
        xp, h2, route = _outproj(oa, orr, oc, xp, wo, g_ffn, wrt, brt, tm_p, lp + tm_p, 0)
        nk = min(WINDOW, lp)
        outs_p[0].append(za[lp - nk:, A_WIDTH:A_WIDTH + A_KV_WIDTH].reshape(1, nk, A_KV_HEADS, HEAD_DIM))
        outs_p[1].append(za[lp - nk:, A_WIDTH + A_KV_WIDTH:].reshape(1, nk, A_KV_HEADS, HEAD_DIM))
        outs_p[2].append(s_ret[None])
        outs_p[3].append(s_wkv[None])
        outs_p[4].append(zc[lp - 1:, :C_COLS])

        xn, za, zr, zc = _inproj(xs, moe_s, g_mix, wa, wr, wc, cos_s, sin_s, bs)
        xs = xs if xn is None else xn
        prep = _rwkv_prep(zc, _pad_cols(state_shift[i], C_PAD), jnp.zeros((1, C_PAD), F32), cw, False, bs)
        oa, orr, oc, kc_new, vc_new, sret_new, swkv_new = _sample_mixers(
            za, zr, prep, cache_swa_k[i].reshape(bs, wb, A_KV_WIDTH), cache_swa_v[i].reshape(bs, wb, A_KV_WIDTH),
            state_ret[i], state_wkv[i], sinks[i], lnw, lnb)
        xs, h2, route = _outproj(oa, orr, oc, xs, wo, g_ffn, wrt, brt, bs, lp + tm_p, lp, shared=(h2, route))
        yw = _moe(h2, route, lp + bs, expert_w_gate[i], expert_w_up[i], expert_w_down[i])
        moe_p, moe_s = (yw, route, 0), (yw, route, lp)
        outs_s[0].append(kc_new.reshape(bs, wb, A_KV_HEADS, HEAD_DIM))
        outs_s[1].append(vc_new.reshape(bs, wb, A_KV_HEADS, HEAD_DIM))
        outs_s[2].append(sret_new)
        outs_s[3].append(swkv_new)
        outs_s[4].append(zc[:, :C_COLS])

    gfin = row(norm_final)
    y_prompt = _final(xp, moe_p, gfin, tm_p).reshape(x_prompt.shape)
    y_sample = _final(xs, moe_s, gfin, bs).reshape(x_sample.shape)
    sp = [jnp.stack(o) for o in outs_p]
    ss = [jnp.stack(o) for o in outs_s]
    return (y_prompt, y_sample, sp[0], sp[1], sp[2], sp[3], sp[4], ss[0], ss[1], ss[2], ss[3], ss[4])
```

```python
import functools

import numpy as np
import jax
import jax.numpy as jnp
from jax import lax
from jax.experimental import pallas as pl
from jax.experimental.pallas import tpu as pltpu

F32 = jnp.float32
BF16 = jnp.bfloat16

D_MODEL = 1024
DEPTH = 2
PAST_LEN = 16384
A_HEADS, A_KV_HEADS, HEAD_DIM, WINDOW = 8, 2, 64, 128
ROPE_THETA = 10000.0
A_WIDTH = A_HEADS * HEAD_DIM
A_KV_WIDTH = A_KV_HEADS * HEAD_DIM
A_COLS = A_WIDTH + 2 * A_KV_WIDTH
R_HEADS, R_DK, R_DV, R_CHUNK = 4, 64, 64, 128
R_WIDTH = R_HEADS * R_DV
R_COLS = 2 * R_HEADS * R_DK + 2 * R_WIDTH
C_HEADS, C_HEAD = 4, 64
C_WIDTH = C_HEADS * C_HEAD
C_DECAY_LORA, C_ICLR_LORA, C_GATE_LORA = 64, 64, 160
C_COLS = 3 * C_WIDTH + C_DECAY_LORA + C_ICLR_LORA + C_GATE_LORA
C_PAD = 1152
C_GN_EPS = 64e-5
N_GROUPS, EXPERTS_PER_GROUP, TOP_K, EXPERT_FF = 4, 8, 2, 512
N_EXPERTS = N_GROUPS * EXPERTS_PER_GROUP
EPS = 1e-6

LANES = 128
SUBLANES = 8
ROW_TILES = D_MODEL // LANES
MOE_ROWS = 256
DMA_UNROLL = 8
IDX_BITS = 19
VMEM_LIMIT = 56 * 1024 * 1024


def _cp(sem, vmem=VMEM_LIMIT):
    return pltpu.CompilerParams(dimension_semantics=sem, vmem_limit_bytes=vmem)


def _full(shape):
    n = len(shape)
    return pl.BlockSpec(shape, lambda *_: (0,) * n)


def _rmsnorm(x, g):
    return x * lax.rsqrt(jnp.mean(x * x, -1, keepdims=True) + EPS) * g


def _silu(x):
    return x * (1.0 / (1.0 + jnp.exp(-x)))


def _sigmoid(x):
    return 1.0 / (1.0 + jnp.exp(-x))


def _rope(x, cos, sin_signed):
    w = x.shape[-1]
    lane = lax.broadcasted_iota(jnp.int32, x.shape, x.ndim - 1)
    first = (lane % HEAD_DIM) < (HEAD_DIM // 2)
    swapped = jnp.where(first, pltpu.roll(x, w - HEAD_DIM // 2, axis=x.ndim - 1),
                        pltpu.roll(x, HEAD_DIM // 2, axis=x.ndim - 1))
    return x * cos + swapped * sin_signed


def _tile_lanes(x, n):
    return jnp.concatenate([x] * n, axis=-1) if n > 1 else x


def _seg_sum(x, seg=64):
    outs = []
    for h in range(x.shape[-1] // seg):
        s = jnp.sum(x[:, h * seg:(h + 1) * seg], axis=-1, keepdims=True)
        outs.append(jnp.broadcast_to(s, (x.shape[0], seg)))
    return jnp.concatenate(outs, axis=-1)


def _read_rows(ref, tm, first, stride):
    return jnp.concatenate([ref[pl.ds(first + s, tm, stride=stride), :] for s in range(ROW_TILES)], axis=1)


def _colbcast(x):
    z = jnp.concatenate([jnp.broadcast_to(x[:, :LANES], (64, LANES)),
                         jnp.broadcast_to(x[:, LANES:], (64, LANES))], axis=0)
    return z.T


def _rowpair(x):
    top = jnp.concatenate([x[:, 0:64], x[:, 128:192]], axis=1)
    bot = jnp.concatenate([x[:, 64:128], x[:, 192:256]], axis=1)
    return top, bot


def _unpair(top, bot):
    return jnp.concatenate([top[:, :64], bot[:, :64], top[:, 64:], bot[:, 64:]], axis=1)


def _rowbcast(top, bot):
    return jnp.concatenate([jnp.broadcast_to(top, (64, LANES)), jnp.broadcast_to(bot, (64, LANES))], axis=0)


def _halfsums(p):
    return jnp.sum(p[:64], axis=0, keepdims=True), jnp.sum(p[64:], axis=0, keepdims=True)


def _wkv_step(st, ab, wb, bb, kb, rb, vtop, vbot):
    sa_t, sa_b = _halfsums(st * ab)
    st = st * wb + _rowbcast(sa_t, sa_b) * bb + _rowbcast(vtop, vbot) * kb
    y_t, y_b = _halfsums(st * rb)
    return st, y_t, y_b


def _wkv_post(y, lnw, lnb, bonus, g):
    mu = _seg_sum(y) * (1.0 / C_HEAD)
    d = y - mu
    var = _seg_sum(d * d) * (1.0 / C_HEAD)
    yn = d * lax.rsqrt(var + C_GN_EPS)
    return (yn * lnw + lnb + bonus) * g


def _heads_to_t(s4, transpose):
    if transpose:
        m = jnp.concatenate([jnp.concatenate([s4[0], s4[1]], axis=1),
                             jnp.concatenate([s4[2], s4[3]], axis=1)], axis=0)
        return m.T
    return jnp.concatenate([jnp.concatenate([s4[0], s4[2]], axis=1),
                            jnp.concatenate([s4[1], s4[3]], axis=1)], axis=0)


def _t_to_heads(st, transpose):
    if transpose:
        m = st.T
        return [m[0:64, 0:64], m[0:64, 64:128], m[64:128, 0:64], m[64:128, 64:128]]
    return [st[0:64, 0:64], st[64:128, 0:64], st[0:64, 64:128], st[64:128, 64:128]]


def _inproj_kernel(*refs, combine, tm):
    if combine:
        x_ref, yw_ref, route_ref, g_ref, wa_ref, wr_ref, wc_ref, cos_ref, sin_ref, xo_ref, za_ref, zr_ref, zc_ref = refs
        route = route_ref[...]
        y0 = _read_rows(yw_ref, tm, 0, 2 * ROW_TILES)
        y1 = _read_rows(yw_ref, tm, ROW_TILES, 2 * ROW_TILES)
        x = x_ref[...] + (route[:, 2:3] * y0 + route[:, 3:4] * y1)
        xo_ref[...] = x
    else:
        x_ref, g_ref, wa_ref, wr_ref, wc_ref, cos_ref, sin_ref, za_ref, zr_ref, zc_ref = refs
        x = x_ref[...]
    h = _rmsnorm(x, g_ref[...]).astype(BF16)
    cos, sin = cos_ref[...], sin_ref[...]
    na = (A_WIDTH + A_KV_WIDTH) // LANES
    nr = 2 * R_HEADS * R_DK // LANES
    za = jnp.dot(h, wa_ref[...], preferred_element_type=F32)
    za_ref[:, :na * LANES] = _rope(za[:, :na * LANES], _tile_lanes(cos, na), _tile_lanes(sin, na))
    za_ref[:, na * LANES:] = za[:, na * LANES:]
    zr = jnp.dot(h, wr_ref[...], preferred_element_type=F32)
    zr_ref[:, :nr * LANES] = _rope(zr[:, :nr * LANES], _tile_lanes(cos, nr), _tile_lanes(sin, nr))
    zr_ref[:, nr * LANES:] = zr[:, nr * LANES:]
    zc_ref[...] = jnp.dot(h, wc_ref[...], preferred_element_type=F32)


def _inproj(x, moe, g, wa, wr, wc, cos, sin, tm):
    t = x.shape[0]
    row = lambda w: pl.BlockSpec((tm, w), lambda i: (i, 0))
    wspecs = [_full((1, D_MODEL)), _full(wa.shape), _full(wr.shape), _full(wc.shape), row(LANES), row(LANES)]
    zshapes = [jax.ShapeDtypeStruct((t, A_COLS), F32), jax.ShapeDtypeStruct((t, R_COLS), F32),
               jax.ShapeDtypeStruct((t, C_PAD), F32)]
    zspecs = [row(A_COLS), row(R_COLS), row(C_PAD)]
    if moe is None:
        return (None,) + tuple(pl.pallas_call(
            functools.partial(_inproj_kernel, combine=False, tm=tm), name="inproj",
            grid=(t // tm,), in_specs=[row(D_MODEL)] + wspecs, out_specs=zspecs, out_shape=zshapes,
            compiler_params=_cp(("parallel",)))(x, g, wa, wr, wc, cos, sin))
    yw, route, first_tok = moe
    off = first_tok // tm
    return pl.pallas_call(
        functools.partial(_inproj_kernel, combine=True, tm=tm), name="combine_inproj",
        grid=(t // tm,),
        in_specs=[row(D_MODEL), pl.BlockSpec((tm * 2 * ROW_TILES, LANES), lambda i: (i + off, 0)),
                  pl.BlockSpec((tm, LANES), lambda i: (i + off, 0))] + wspecs,
        out_specs=[row(D_MODEL)] + zspecs,
        out_shape=[jax.ShapeDtypeStruct((t, D_MODEL), F32)] + zshapes,
        compiler_params=_cp(("parallel",)))(x, yw, route, g, wa, wr, wc, cos, sin)


def _final_kernel(x_ref, yw_ref, route_ref, g_ref, o_ref, *, tm):
    route = route_ref[...]
    y0 = _read_rows(yw_ref, tm, 0, 2 * ROW_TILES)
    y1 = _read_rows(yw_ref, tm, ROW_TILES, 2 * ROW_TILES)
    x = x_ref[...] + (route[:, 2:3] * y0 + route[:, 3:4] * y1)
    o_ref[...] = _rmsnorm(x, g_ref[...])


def _final(x, moe, g, tm):
    t = x.shape[0]
    yw, route, first_tok = moe
    off = first_tok // tm
    row = lambda w: pl.BlockSpec((tm, w), lambda i: (i, 0))
    return pl.pallas_call(
        functools.partial(_final_kernel, tm=tm), grid=(t // tm,), name="combine_final",
        in_specs=[row(D_MODEL), pl.BlockSpec((tm * 2 * ROW_TILES, LANES), lambda i: (i + off, 0)),
                  pl.BlockSpec((tm, LANES), lambda i: (i + off, 0)), _full((1, D_MODEL))],
        out_specs=row(D_MODEL), out_shape=jax.ShapeDtypeStruct((t, D_MODEL), F32),
        compiler_params=_cp(("parallel",)))(x, yw, route, g)


def _swa_prompt_kernel(sinks_ref, q_ref, kc_ref, kp_ref, vc_ref, vp_ref, o_ref):
    i = pl.program_id(0)
    g = A_HEADS // A_KV_HEADS
    nq = g * WINDOW
    kcat = jnp.concatenate([kp_ref[...], kc_ref[...]], axis=0)
    vcat = jnp.concatenate([vp_ref[...], vc_ref[...]], axis=0)
    kswap = pltpu.roll(kcat, HEAD_DIM, axis=1)
    vswap = pltpu.roll(vcat, HEAD_DIM, axis=1)
    upper_k = lax.broadcasted_iota(jnp.int32, kcat.shape, 1) >= HEAD_DIM
    upper_q = lax.broadcasted_iota(jnp.int32, (WINDOW, LANES), 1) >= HEAD_DIM
    r = lax.broadcasted_iota(jnp.int32, (nq, 2 * WINDOW), 0) % WINDOW
    c = lax.broadcasted_iota(jnp.int32, (nq, 2 * WINDOW), 1)
    ok = (c >= r) & (c <= r + WINDOW) & ((c >= WINDOW) | (i > 0))
    sink_col4 = c == (r + WINDOW + 1) % (2 * WINDOW)
    sink_col = sink_col4[:WINDOW]
    slabs = []
    for kv in range(A_KV_HEADS):
        kboth = jnp.where(upper_k, kswap, kcat) if kv == 0 else jnp.where(upper_k, kcat, kswap)
        v_lo = jnp.where(upper_k, 0.0, vcat if kv == 0 else vswap)
        v_hi = jnp.where(upper_k, vswap if kv == 0 else vcat, 0.0)
        vpair = jnp.concatenate([v_lo, v_hi], axis=0)
        qs = []
        for j in range(g):
            h = kv * g + j
            slab = q_ref[:, (h // 2) * LANES:(h // 2 + 1) * LANES]
            qs.append(jnp.where(upper_q, slab, 0.0) if h % 2 else jnp.where(upper_q, 0.0, slab))
        q4 = jnp.concatenate(qs, axis=0)
        s = lax.dot_general(q4, kboth, (((1,), (1,)), ((), ())), preferred_element_type=F32) * (HEAD_DIM ** -0.5)
        sink = jnp.concatenate([jnp.where(sink_col, sinks_ref[kv * g + j], -jnp.inf) for j in range(g)], axis=0)
        s = jnp.where(ok, s, sink)
        m = jnp.max(s, -1, keepdims=True)
        p = jnp.exp(s - m)
        p = jnp.where(sink_col4, 0.0, p / jnp.sum(p, -1, keepdims=True))
        for j in range(0, g, 2):
            pp = jnp.concatenate([p[j * WINDOW:(j + 1) * WINDOW], p[(j + 1) * WINDOW:(j + 2) * WINDOW]], axis=1)
            slabs.append(jnp.dot(pp, vpair, preferred_element_type=F32))
    o_ref[...] = jnp.concatenate(slabs, axis=1)


def _swa_prompt(za, sinks):
    l = za.shape[0]
    nb = l // WINDOW
    cur = lambda w, c: pl.BlockSpec((WINDOW, w), lambda i: (i, c))
    prv = lambda w, c: pl.BlockSpec((WINDOW, w), lambda i: (jnp.maximum(i - 1, 0), c))
    kcol, vcol = A_WIDTH // A_KV_WIDTH, A_WIDTH // A_KV_WIDTH + 1
    return pl.pallas_call(
        _swa_prompt_kernel, grid=(nb,), name="swa_prompt",
        in_specs=[pl.BlockSpec(memory_space=pltpu.SMEM),
                  cur(A_WIDTH, 0), cur(A_KV_WIDTH, kcol), prv(A_KV_WIDTH, kcol),
                  cur(A_KV_WIDTH, vcol), prv(A_KV_WIDTH, vcol)],
        out_specs=cur(A_WIDTH, 0),
        out_shape=jax.ShapeDtypeStruct((l, A_WIDTH), F32),
        compiler_params=_cp(("parallel",)))(sinks, za, za, za, za, za)


def _ret_prompt_kernel(cdec_ref, zr_ref, intra_ref, qdec_ref, kdec_ref, o_ref, so_ref, s_scr):
    @pl.when(pl.program_id(0) == 0)
    def _():
        s_scr[...] = jnp.zeros_like(s_scr)

    z = zr_ref[...]
    qk = R_HEADS * R_DK
    q = z[:, :qk]
    k = z[:, qk:2 * qk] * (R_DK ** -0.5)
    v = z[:, 2 * qk:2 * qk + R_WIDTH]
    gate = z[:, 2 * qk + R_WIDTH:]
    qdec = qdec_ref[...]
    kd = k * kdec_ref[...]
    outs = []
    for h in range(R_HEADS):
        sl = slice(h * R_DK, (h + 1) * R_DK)
        qh, kh, vh = q[:, sl], k[:, sl], v[:, sl]
        att = lax.dot_general(qh, kh, (((1,), (1,)), ((), ())), preferred_element_type=F32) * intra_ref[h]
        s = s_scr[h]
        o = jnp.dot(att, vh, preferred_element_type=F32) + jnp.dot(qh, s, preferred_element_type=F32) * qdec[:, sl]
        s_scr[h] = s * cdec_ref[h] + jnp.dot(kd[:, sl].T, vh, preferred_element_type=F32)
        outs.append(o * lax.rsqrt(jnp.mean(o * o, -1, keepdims=True) + EPS))
    o_ref[...] = jnp.concatenate(outs, axis=1) * _silu(gate)
    so_ref[...] = s_scr[...]


def _ret_tables(c):
    log_g = jnp.log1p(-jnp.exp2(-5.0 - jnp.arange(R_HEADS, dtype=F32)))
    idx = jnp.arange(c, dtype=F32)
    diff = idx[:, None] - idx[None, :]
    intra = jnp.where(diff >= 0, jnp.exp(log_g[:, None, None] * jnp.maximum(diff, 0.0)), 0.0)
    q_dec = jnp.exp(log_g[None, :] * (idx[:, None] + 1.0))
    k_dec = jnp.exp(log_g[None, :] * (c - 1.0 - idx[:, None]))
    c_dec = jnp.exp(log_g * c)
    return intra, jnp.repeat(q_dec, R_DK, axis=1), jnp.repeat(k_dec, R_DK, axis=1), c_dec


def _ret_prompt(zr):
    l = zr.shape[0]
    c = R_CHUNK
    intra, qdec, kdec, cdec = _ret_tables(c)
    blk = lambda w: pl.BlockSpec((c, w), lambda i: (i, 0))
    return pl.pallas_call(
        _ret_prompt_kernel, grid=(l // c,), name="ret_prompt",
        in_specs=[pl.BlockSpec(memory_space=pltpu.SMEM), blk(R_COLS),
                  _full((R_HEADS, c, c)), _full((c, R_HEADS * R_DK)), _full((c, R_HEADS * R_DK))],
        out_specs=[blk(R_WIDTH), _full((R_HEADS, R_DK, R_DV))],
        out_shape=[jax.ShapeDtypeStruct((l, R_WIDTH), F32), jax.ShapeDtypeStruct((R_HEADS, R_DK, R_DV), F32)],
        scratch_shapes=[pltpu.VMEM((R_HEADS, R_DK, R_DV), F32)],
        compiler_params=_cp(("arbitrary",)))(cdec, zr, intra, qdec, kdec)


def _rwkv_prep_kernel(zc_ref, prev_ref, shift0_ref, mu_ref, w0_ref, w2_ref, a0_ref, a2_ref, g2_ref,
                      kk_ref, ka_ref, rk_ref,
                      r_ref, w_ref, k_ref, v_ref, a_ref, b_ref, g_ref, bonus_ref, *scratch, sequence, tm):
    zc = zc_ref[...]
    if sequence:
        row = lax.broadcasted_iota(jnp.int32, zc.shape, 0)
        boundary = jnp.where(pl.program_id(0) == 0, shift0_ref[...], prev_ref[SUBLANES - 1:SUBLANES, :])
        prev = jnp.where(row == 0, jnp.broadcast_to(boundary, zc.shape), pltpu.roll(zc, 1, axis=0))
    else:
        prev = prev_ref[...]
    zs = zc + mu_ref[...] * (prev - zc)
    o1, o2, o3 = C_WIDTH, 2 * C_WIDTH, 3 * C_WIDTH
    r, k, v = zs[:, :o1], zs[:, o1:o2], zs[:, o2:o3]
    lora = zs[:, o3:o3 + LANES]
    gl = zs[:, o3 + LANES:]
    w = -jax.nn.softplus(-(w0_ref[...] + jnp.dot(jnp.tanh(lora), w2_ref[...], preferred_element_type=F32))) - 0.5
    decay = jnp.exp(-jnp.exp(w))
    a = _sigmoid(a0_ref[...] + jnp.dot(lora, a2_ref[...], preferred_element_type=F32))
    g = jnp.dot(_sigmoid(gl), g2_ref[...], preferred_element_type=F32)
    kk = k * kk_ref[...]
    kk = kk / jnp.maximum(jnp.sqrt(_seg_sum(kk * kk)), 1e-12)
    k_mod = k * (1.0 + (a - 1.0) * ka_ref[...])
    v_ref[...] = v
    g_ref[...] = g
    bonus_ref[...] = _seg_sum(r * k_mod * rk_ref[...]) * v
    if not sequence:
        r_ref[...] = r
        w_ref[...] = decay
        k_ref[...] = k_mod
        a_ref[...] = -kk
        b_ref[...] = kk * a
        return
    sub =lax.broadcasted_iota(jnp.int32, decay.shape, 0) % SCAN_SUB
    cp = decay
    shift = 1
    while shift < SCAN_SUB:
        cp = jnp.where(sub >= shift, cp * pltpu.roll(cp, shift, axis=0), cp)
        shift *= 2
    cpx = jnp.where(sub == 0, 1.0, pltpu.roll(cp, 1, axis=0))
    inv = 1.0 / cp
    r_ref[...] = r * cp
    k_ref[...] = k_mod * inv
    a_ref[...] = -kk * cpx
    b_ref[...] = kk * a * inv
    for n, scr in enumerate(scratch):
        scr[...] = cp[:, n * LANES:(n + 1) * LANES]
        w_ref[:, n * LANES:(n + 1) * LANES] = scr[pl.ds(SCAN_SUB - 1, tm // SCAN_SUB, stride=SCAN_SUB), :]


def _rwkv_prep(zc, prev, shift0, cw, sequence, tm):
    t = zc.shape[0]
    row = lambda w: pl.BlockSpec((tm, w), lambda i: (i, 0))
    if sequence:
        per = tm // SUBLANES
        prev_spec = pl.BlockSpec((SUBLANES, C_PAD), lambda i: (jnp.maximum(i * per - 1, 0), 0))
        prev = zc
    else:
        prev_spec = row(C_PAD)
    vec = _full((1, C_WIDTH))
    out_specs = [row(C_WIDTH)] * 8
    out_shape = [jax.ShapeDtypeStruct((t, C_WIDTH), F32)] * 8
    if sequence:
        out_specs[1] = pl.BlockSpec((tm // SCAN_SUB, C_WIDTH), lambda i: (i, 0))
        out_shape[1] = jax.ShapeDtypeStruct((t // SCAN_SUB, C_WIDTH), F32)
    return pl.pallas_call(
        functools.partial(_rwkv_prep_kernel, sequence=sequence, tm=tm), grid=(t // tm,), name="rwkv_prep",
        in_specs=[row(C_PAD), prev_spec, _full((1, C_PAD)), _full((1, C_PAD)), vec, _full((LANES, C_WIDTH)),
                  vec, _full((LANES, C_WIDTH)), _full((2 * LANES, C_WIDTH)), vec, vec, vec],
        out_specs=out_specs, out_shape=out_shape,
        scratch_shapes=[pltpu.VMEM((tm, LANES), F32)] * (C_WIDTH // LANES) if sequence else [],
        compiler_params=_cp(("parallel",)))(zc, prev, shift0, cw["mu"], cw["w0"], cw["w2"], cw["a0"], cw["a2"],
                                           cw["g2"], cw["k_k"], cw["k_a"], cw["r_k"])


SCAN_BLOCK = 256
SCAN_SUB = 16
SCAN_NSUB = SCAN_BLOCK // SCAN_SUB
SCAN_UNROLL = 16
SCAN_OPS = 3


def _scan_select():
    sel = np.zeros((SCAN_SUB // 2, LANES, 2 * LANES), np.float32)
    for p in range(SCAN_SUB // 2):
        for piece in range(3):
            for rh in range(2):
                for tl in range(2):
                    row = piece * 2 * SCAN_SUB + rh * SCAN_SUB + 2 * p + tl
                    sel[p, row, tl * LANES + rh * 64: tl * LANES + (rh + 1) * 64] = 1.0
    return jnp.asarray(sel, BF16)


def _stage_lhs(ops, base, lhs_scr):
    for n, ref in enumerate(ops):
        xs = ref[pl.ds(base, SCAN_SUB), :]
        hi = xs.astype(BF16).astype(F32)
        r1 = xs - hi
        mid = r1.astype(BF16).astype(F32)
        lo = (r1 - mid).astype(BF16).astype(F32)
        g = jnp.concatenate([hi[:, :LANES], hi[:, LANES:], mid[:, :LANES], mid[:, LANES:],
                             lo[:, :LANES], lo[:, LANES:], jnp.zeros((LANES - 6 * SCAN_SUB, LANES), F32)], axis=0)
        lhs_scr[n] = g.T.astype(BF16)


def _stage_tiles(lhs_scr, sel_ref, xb_scr, p):
    sel = sel_ref[p]
    for n in range(SCAN_OPS):
        out = jnp.dot(lhs_scr[n], sel, preferred_element_type=F32)
        xb_scr[n, 2 * p] = out[:, :LANES]
        xb_scr[n, 2 * p + 1] = out[:, LANES:]


def _rwkv_scan_kernel(r_ref, pc_ref, k_ref, v_ref, a_ref, b_ref, g_ref, bonus_ref, lnw_ref, lnb_ref, sel_ref,
                      o_ref, so_ref, st_scr, lhs_a, lhs_b, xb_a, xb_b, vt_scr, vb_scr, yt_scr, yb_scr):
    @pl.when(pl.program_id(0) == 0)
    def _():
        st_scr[...] = jnp.zeros_like(st_scr)

    vtop, vbot = _rowpair(v_ref[...])
    vt_scr[...] = vtop
    vb_scr[...] = vbot
    ops = (a_ref, b_ref, k_ref)
    pairs = SCAN_UNROLL // 2
    _stage_lhs(ops, 0, lhs_a)
    for p in range(SCAN_SUB // 2):
        _stage_tiles(lhs_a, sel_ref, xb_a, p)
    _stage_lhs(ops, SCAN_SUB, lhs_b)

    def half(sc, st, lhs_cur, xb_cur, lhs_nxt, xb_nxt):
        base = pl.multiple_of(sc * SCAN_SUB, SCAN_SUB)

        def quad(q, st):
            for pp in range(pairs):
                _stage_tiles(lhs_nxt, sel_ref, xb_nxt, q * pairs + pp)
            for uu in range(SCAN_UNROLL):
                u = q * SCAN_UNROLL + uu
                t = base + u
                sa_t, sa_b = _halfsums(st * xb_cur[0, u])
                st = (st + _rowbcast(sa_t, sa_b) * xb_cur[1, u]
                      + _rowbcast(vt_scr[pl.ds(t, 1), :], vb_scr[pl.ds(t, 1), :]) * xb_cur[2, u])
                y_t, y_b = _halfsums(st * _colbcast(r_ref[pl.ds(t, 1), :]))
                yt_scr[pl.ds(t, 1), :] = y_t
                yb_scr[pl.ds(t, 1), :] = y_b
            return st

        st = lax.fori_loop(0, SCAN_SUB // SCAN_UNROLL, quad, st) * _colbcast(pc_ref[pl.ds(sc, 1), :])
        nxt2 = pl.multiple_of(jnp.minimum(sc + 2, SCAN_NSUB - 1) * SCAN_SUB, SCAN_SUB)
        _stage_lhs(ops, nxt2, lhs_cur)
        return st

    def sub2(s2, st):
        st = half(2 * s2, st, lhs_a, xb_a, lhs_b, xb_b)
        return half(2 * s2 + 1, st, lhs_b, xb_b, lhs_a, xb_a)

    st = lax.fori_loop(0, SCAN_NSUB // 2, sub2, st_scr[...])
    st_scr[...] = st
    y = _unpair(yt_scr[...], yb_scr[...])
    o_ref[...] = _wkv_post(y, lnw_ref[...], lnb_ref[...], bonus_ref[...], g_ref[...])
    heads = _t_to_heads(st, transpose=True)
    for h in range(C_HEADS):
        so_ref[h] = heads[h]


def _rwkv_scan(prep, lnw, lnb):
    r, pc, k, v, a, b, g, bonus = prep
    l = r.shape[0]
    blk = pl.BlockSpec((SCAN_BLOCK, C_WIDTH), lambda i: (i, 0))
    vec = _full((1, C_WIDTH))
    sel = _scan_select()
    return pl.pallas_call(
        _rwkv_scan_kernel, grid=(l // SCAN_BLOCK,), name="rwkv_scan",
        in_specs=[blk, pl.BlockSpec((SCAN_NSUB, C_WIDTH), lambda i: (i, 0))] + [blk] * 6 + [vec, vec, _full(sel.shape)],
        out_specs=[blk, _full((C_HEADS, C_HEAD, C_HEAD))],
        out_shape=[jax.ShapeDtypeStruct((l, C_WIDTH), F32), jax.ShapeDtypeStruct((C_HEADS, C_HEAD, C_HEAD), F32)],
        scratch_shapes=[pltpu.VMEM((LANES, LANES), F32),
                        pltpu.VMEM((SCAN_OPS, LANES, LANES), BF16), pltpu.VMEM((SCAN_OPS, LANES, LANES), BF16),
                        pltpu.VMEM((SCAN_OPS, SCAN_SUB, LANES, LANES), F32),
                        pltpu.VMEM((SCAN_OPS, SCAN_SUB, LANES, LANES), F32),
                        pltpu.VMEM((SCAN_BLOCK, LANES), F32), pltpu.VMEM((SCAN_BLOCK, LANES), F32),
                        pltpu.VMEM((SCAN_BLOCK, LANES), F32), pltpu.VMEM((SCAN_BLOCK, LANES), F32)],
        compiler_params=_cp(("arbitrary",)))(r, pc, k, v, a, b, g, bonus, lnw, lnb, sel)


def _sample_kernel(sinks_ref, gam_ref, za_ref, zr_ref, r_ref, w_ref, k_ref, v_ref, a_ref, b_ref, g_ref, bonus_ref,
                   kc_ref, vc_ref, sret_ref, swkv_ref, lnw_ref, lnb_ref,
                   oa_ref, or_ref, oc_ref, kco_ref, vco_ref, sreto_ref, swkvo_ref):
    lane =lax.broadcasted_iota(jnp.int32, (1, LANES), 1)
    row8 = lax.broadcasted_iota(jnp.int32, (SUBLANES, LANES), 0)

    za = za_ref[0]
    q = za[:, :A_WIDTH]
    knew = za[:, A_WIDTH:A_WIDTH + A_KV_WIDTH]
    vnew = za[:, A_WIDTH + A_KV_WIDTH:]
    kc, vc = kc_ref[0], vc_ref[0]
    g = A_HEADS // A_KV_HEADS
    zero = jnp.zeros((1, HEAD_DIM), F32)
    qrows = []
    for h in range(A_HEADS):
        qh = q[:, h * HEAD_DIM:(h + 1) * HEAD_DIM]
        qrows.append(jnp.concatenate([qh, zero] if h // g == 0 else [zero, qh], axis=1))
    qm = jnp.concatenate(qrows + [jnp.zeros((LANES - A_HEADS, LANES), F32)], axis=0)
    scale = HEAD_DIM ** -0.5
    s = lax.dot_general(kc, qm, (((1,), (1,)), ((), ())), preferred_element_type=F32) * scale
    kn8 = jnp.where(row8 == 0, jnp.broadcast_to(knew, (SUBLANES, LANES)), 0.0)
    s_new = lax.dot_general(kn8, qm, (((1,), (1,)), ((), ())), preferred_element_type=F32)[0:1] * scale
    sink = jnp.zeros((1, LANES), F32)
    for h in range(A_HEADS):
        sink = jnp.where(lane == h, sinks_ref[h], sink)
    m = jnp.maximum(jnp.maximum(jnp.max(s, axis=0, keepdims=True), s_new), sink)
    p = jnp.exp(s - m)
    p_new = jnp.exp(s_new - m)
    denom = jnp.sum(p, axis=0, keepdims=True) + p_new + jnp.exp(sink - m)
    p = p / denom
    p_new = p_new / denom
    pn_col = jnp.broadcast_to(p_new, (LANES, LANES)).T[:, 0:1]
    o_full = jnp.dot(p.T, vc, preferred_element_type=F32) + pn_col * vnew
    oa_ref[0] = jnp.concatenate(
        [o_full[h:h + 1, (h // g) * HEAD_DIM:(h // g + 1) * HEAD_DIM] for h in range(A_HEADS)], axis=1)
    rowk = lax.broadcasted_iota(jnp.int32, kc.shape, 0)
    last = rowk == kc.shape[0] - 1
    kco_ref[0] = jnp.where(last, jnp.broadcast_to(knew, kc.shape), pltpu.roll(kc, kc.shape[0] - 1, axis=0))
    vco_ref[0] = jnp.where(last, jnp.broadcast_to(vnew, vc.shape), pltpu.roll(vc, vc.shape[0] - 1, axis=0))

    zr = zr_ref[0]
    qk = R_HEADS * R_DK
    rq = zr[:, :qk]
    rk = zr[:, qk:2 * qk] * (R_DK ** -0.5)
    rv = zr[:, 2 * qk:2 * qk + R_WIDTH]
    rg = zr[:, 2 * qk + R_WIDTH:]
    st = _heads_to_t(sret_ref[0], transpose=False)
    rr = lax.broadcasted_iota(jnp.int32, (LANES, LANES), 0) >= 64
    cc = lax.broadcasted_iota(jnp.int32, (LANES, LANES), 1) >= 64
    gamma = jnp.where(rr, jnp.where(cc, gam_ref[3], gam_ref[1]), jnp.where(cc, gam_ref[2], gam_ref[0]))
    vt, vb = _rowpair(rv)
    st = st * gamma + _colbcast(rk) * _rowbcast(vt, vb)
    o_t, o_b = _halfsums(st * _colbcast(rq))
    o = _unpair(o_t, o_b)
    o = o * lax.rsqrt(_seg_sum(o * o) * (1.0 / R_DV) + EPS)
    or_ref[0] = o * _silu(rg)
    heads = _t_to_heads(st, transpose=False)
    for h in range(R_HEADS):
        sreto_ref[0, h] = heads[h]

    st = _heads_to_t(swkv_ref[0], transpose=True)
    vt, vb = _rowpair(v_ref[0])
    st, y_t, y_b = _wkv_step(st, _colbcast(a_ref[0]), _colbcast(w_ref[0]), _colbcast(b_ref[0]),
                             _colbcast(k_ref[0]), _colbcast(r_ref[0]), vt, vb)
    oc_ref[0] = _wkv_post(_unpair(y_t, y_b), lnw_ref[...], lnb_ref[...], bonus_ref[0], g_ref[0])
    heads = _t_to_heads(st, transpose=True)
    for h in range(C_HEADS):
        swkvo_ref[0, h] = heads[h]


def _sample_mixers(za, zr, prep, kc, vc, sret, swkv, sinks, lnw, lnb):
    b = za.shape[0]
    wb = kc.shape[1]
    assert wb <= WINDOW and PAST_LEN >= wb
    gam = jnp.exp(jnp.log1p(-jnp.exp2(-5.0 - jnp.arange(R_HEADS, dtype=F32))) * 1.0)
    tok = lambda w: pl.BlockSpec((1, 1, w), lambda i: (i, 0, 0))
    cache = pl.BlockSpec((1, wb, A_KV_WIDTH), lambda i: (i, 0, 0))
    state = pl.BlockSpec((1, 4, 64, 64), lambda i: (i, 0, 0, 0))
    smem = pl.BlockSpec(memory_space=pltpu.SMEM)
    vec = _full((1, C_WIDTH))
    r3 = lambda x: x.reshape(b, 1, x.shape[-1])
    outs = pl.pallas_call(
        _sample_kernel, grid=(b,), name="sample_mixers",
        in_specs=[smem, smem, tok(A_COLS), tok(R_COLS)] + [tok(C_WIDTH)] * 8 + [cache, cache, state, state,
                  vec, vec],
        out_specs=[tok(A_WIDTH), tok(R_WIDTH), tok(C_WIDTH), cache, cache, state, state],
        out_shape=[jax.ShapeDtypeStruct((b, 1, A_WIDTH), F32), jax.ShapeDtypeStruct((b, 1, R_WIDTH), F32),
                   jax.ShapeDtypeStruct((b, 1, C_WIDTH), F32), jax.ShapeDtypeStruct(kc.shape, F32),
                   jax.ShapeDtypeStruct(vc.shape, F32), jax.ShapeDtypeStruct(sret.shape, F32),
                   jax.ShapeDtypeStruct(swkv.shape, F32)],
        compiler_params=_cp(("parallel",)))(sinks, gam, r3(za), r3(zr), *[r3(x) for x in prep],
                                           kc, vc, sret, swkv, lnw, lnb)
    oa, orr, oc = (x.reshape(b, x.shape[-1]) for x in outs[:3])
    return (oa, orr, oc) + tuple(outs[3:])


def _outproj_kernel(oa_ref, or_ref, oc_ref, x_ref, wo_ref, gf_ref, wrt_ref, brt_ref, *rest, tm):
    x1_ref, h2_ref, route_ref = rest[-3:]
    mix =jnp.concatenate([oa_ref[...], or_ref[...], oc_ref[...]], axis=1).astype(BF16)
    x1 = x_ref[...] + jnp.dot(mix, wo_ref[...], preferred_element_type=F32)
    x1_ref[...] = x1
    h2 = _rmsnorm(x1, gf_ref[...])
    for s in range(ROW_TILES):
        h2_ref[pl.ds(s, tm, stride=ROW_TILES), :] = h2[:, s * LANES:(s + 1) * LANES]
    h_hi = h2.astype(BF16)
    h_lo = (h2 - h_hi.astype(F32)).astype(BF16)
    w_hi, w_lo = wrt_ref[0], wrt_ref[1]
    logits = (jnp.dot(h_hi, w_hi, preferred_element_type=F32) + jnp.dot(h_lo, w_hi, preferred_element_type=F32)
              + jnp.dot(h_hi, w_lo, preferred_element_type=F32)) + brt_ref[...]
    lane = lax.broadcasted_iota(jnp.int32, logits.shape, 1).astype(F32)
    big = float(LANES)
    is_g = lane < N_GROUPS
    lg = jnp.where(is_g, logits, -jnp.inf)
    m_g = jnp.max(lg, -1, keepdims=True)
    grp = jnp.min(jnp.where(lg == m_g, lane, big), -1, keepdims=True)
    p_grp = 1.0 / jnp.sum(jnp.where(is_g, jnp.exp(lg - m_g), 0.0), -1, keepdims=True)
    lo = N_GROUPS + EXPERTS_PER_GROUP * grp
    in_grp = (lane >= lo) & (lane < lo + EXPERTS_PER_GROUP)
    le = jnp.where(in_grp, logits, -jnp.inf)
    l1 = jnp.max(le, -1, keepdims=True)
    i1 = jnp.min(jnp.where(le == l1, lane, big), -1, keepdims=True)
    le2 = jnp.where(lane == i1, -jnp.inf, le)
    l2 = jnp.max(le2, -1, keepdims=True)
    i2 = jnp.min(jnp.where(le2 == l2, lane, big), -1, keepdims=True)
    e = jnp.exp(l2 - l1)
    g1 = p_grp / (1.0 + e)
    g2 = p_grp * e / (1.0 + e)
    route = jnp.where(lane == 0, i1 - N_GROUPS, jnp.where(lane == 1, i2 - N_GROUPS,
                      jnp.where(lane == 2, g1, jnp.where(lane == 3, g2, 0.0))))
    route_ref[...] = route


def _outproj(oa, orr, oc, x, wo, gf, wrt, brt, tm, t_all, first_tok, shared=None):
    t = x.shape[0]
    off = first_tok // tm
    nblk = t // tm
    steps = nblk + (1 if shared is None else 0)
    assert shared is not None or t_all == t + tm
    row = lambda w: pl.BlockSpec((tm, w), lambda i: (jnp.minimum(i, nblk - 1), 0))
    in_specs = [row(A_WIDTH), row(R_WIDTH), row(C_WIDTH), row(D_MODEL), _full(wo.shape), _full((1, D_MODEL)),
                _full(wrt.shape), _full((1, LANES))]
    args = [oa, orr, oc, x, wo, gf, wrt, brt]
    aliases = {}
    if shared is not None:
        aliases = {len(args): 1, len(args) + 1: 2}
        in_specs += [pl.BlockSpec(memory_space=pl.ANY)] * 2
        args += list(shared)
    return pl.pallas_call(
        functools.partial(_outproj_kernel, tm=tm), grid=(steps,), name="outproj_router",
        in_specs=in_specs,
        out_specs=[row(D_MODEL), pl.BlockSpec((tm * ROW_TILES, LANES), lambda i: (i + off, 0)),
                   pl.BlockSpec((tm, LANES), lambda i: (i + off, 0))],
        out_shape=[jax.ShapeDtypeStruct((t, D_MODEL), F32), jax.ShapeDtypeStruct((t_all * ROW_TILES, LANES), F32),
                   jax.ShapeDtypeStruct((t_all, LANES), F32)],
        input_output_aliases=aliases,
        compiler_params=_cp(("arbitrary",)))(*args)


def _moe_kernel(blk_exp_ref, src_ref, dst_ref, nused_ref, h2_hbm, wg_ref, wu_ref, wd_ref, yw_hbm,
                gbuf, sbuf, xbuf, wgb, wub, wdb, gsem, ssem):
    i = pl.program_id(0)
    nb = pl.num_programs(0)
    nused = nused_ref[0]
    slot = i % 2

    def tile(off):
        return pl.ds(pl.multiple_of(off, ROW_TILES), ROW_TILES)

    def gather(blk, sl):
        def body(r, _):
            pltpu.make_async_copy(h2_hbm.at[tile(src_ref[blk * MOE_ROWS + r]), :],
                                  gbuf.at[sl, tile(r * ROW_TILES), :], gsem.at[sl]).start()
            return 0
        lax.fori_loop(0, MOE_ROWS, body, 0, unroll=DMA_UNROLL)

    def gather_wait(sl):
        def body(r, _):
            pltpu.make_async_copy(h2_hbm.at[tile(0), :], gbuf.at[sl, tile(0), :], gsem.at[sl]).wait()
            return 0
        lax.fori_loop(0, MOE_ROWS, body, 0, unroll=DMA_UNROLL)

    def scatter(blk):
        def body(r, _):
            pltpu.make_async_copy(sbuf.at[tile(r * ROW_TILES), :],
                                  yw_hbm.at[tile(dst_ref[blk * MOE_ROWS + r]), :], ssem).start()
            return 0
        lax.fori_loop(0, MOE_ROWS, body, 0, unroll=DMA_UNROLL)

    def scatter_wait():
        def body(r, _):
            pltpu.make_async_copy(sbuf.at[tile(0), :], yw_hbm.at[tile(0), :], ssem).wait()
            return 0
        lax.fori_loop(0, MOE_ROWS, body, 0, unroll=DMA_UNROLL)

    @pl.when((i == 0) & (nused > 0))
    def _():
        gather(0, 0)

    @pl.when(i < nused)
    def _():
        @pl.when((i == 0) | (blk_exp_ref[i] != blk_exp_ref[jnp.maximum(i - 1, 0)]))
        def _():
            wgb[...] = wg_ref[0].astype(BF16)
            wub[...] = wu_ref[0].astype(BF16)
            wdb[...] = wd_ref[0].astype(BF16)

        gather_wait(slot)

        @pl.when(i + 1 < nused)
        def _():
            gather(i + 1, 1 - slot)

        for s in range(ROW_TILES):
            xbuf[:, s * LANES:(s + 1) * LANES] = gbuf[slot, pl.ds(s, MOE_ROWS, stride=ROW_TILES), :].astype(BF16)
        x = xbuf[...]
        hid = _silu(jnp.dot(x, wgb[...], preferred_element_type=F32)) * jnp.dot(x, wub[...], preferred_element_type=F32)
        y = jnp.dot(hid.astype(BF16), wdb[...], preferred_element_type=F32)

        @pl.when(i > 0)
        def _():
            scatter_wait()

        for s in range(ROW_TILES):
            sbuf[pl.ds(s, MOE_ROWS, stride=ROW_TILES), :] = y[:, s * LANES:(s + 1) * LANES]
        scatter(i)

    @pl.when((i == nb - 1) & (nused > 0))
    def _():
        scatter_wait()
        spare = pltpu.make_async_copy(sbuf, yw_hbm.at[pl.ds(yw_hbm.shape[0] - MOE_ROWS * ROW_TILES,
                                                            MOE_ROWS * ROW_TILES), :], ssem)
        spare.start()
        spare.wait()


def _moe(h2, route, t, wg, wu, wd):
    na = t * TOP_K
    assert na % MOE_ROWS == 0 and na < (1 << IDX_BITS)
    flat_e = route[:t, :TOP_K].astype(jnp.int32).reshape(na)
    experts = jnp.arange(N_EXPERTS, dtype=jnp.int32)
    counts = jnp.sum(flat_e[:, None] == experts[None, :], axis=0, dtype=jnp.int32)
    npad = (-counts) % MOE_ROWS
    real_keys = (flat_e << (IDX_BITS + 1)) | jnp.arange(na, dtype=jnp.int32)
    q = jnp.arange(MOE_ROWS, dtype=jnp.int32)[None, :]
    pad_keys = jnp.where(q < npad[:, None], (experts[:, None] << (IDX_BITS + 1)) | (1 << IDX_BITS) | q,
                         jnp.iinfo(jnp.int32).max)
    keys = jnp.sort(jnp.concatenate([real_keys, pad_keys.reshape(-1)]))
    n_rows = na + N_EXPERTS * MOE_ROWS
    n_blocks = n_rows // MOE_ROWS
    is_pad = ((keys >> IDX_BITS) & 1) == 1
    idx = keys & ((1 << IDX_BITS) - 1)
    pos = jnp.arange(n_rows, dtype=jnp.int32)
    src = jnp.where(is_pad, 0, (idx // TOP_K) * ROW_TILES).astype(jnp.int32)
    dst = jnp.where(is_pad, na + pos % MOE_ROWS, idx).astype(jnp.int32) * ROW_TILES
    blk_exp = jnp.minimum(keys[::MOE_ROWS] >> (IDX_BITS + 1), N_EXPERTS - 1).astype(jnp.int32)
    nused = (jnp.sum(counts + npad) // MOE_ROWS).astype(jnp.int32).reshape(1)
    wspec = lambda a, b: pl.BlockSpec((1, a, b), lambda i, be, s, d, nu: (be[i], 0, 0))
    return pl.pallas_call(
        _moe_kernel, name="moe_experts",
        grid_spec=pltpu.PrefetchScalarGridSpec(
            num_scalar_prefetch=4, grid=(n_blocks,),
            in_specs=[pl.BlockSpec(memory_space=pl.ANY), wspec(D_MODEL, EXPERT_FF), wspec(D_MODEL, EXPERT_FF),
                      wspec(EXPERT_FF, D_MODEL)],
            out_specs=pl.BlockSpec(memory_space=pl.ANY),
            scratch_shapes=[pltpu.VMEM((2, MOE_ROWS * ROW_TILES, LANES), F32),
                            pltpu.VMEM((MOE_ROWS * ROW_TILES, LANES), F32),
                            pltpu.VMEM((MOE_ROWS, D_MODEL), BF16),
                            pltpu.VMEM((D_MODEL, EXPERT_FF), BF16), pltpu.VMEM((D_MODEL, EXPERT_FF), BF16),
                            pltpu.VMEM((EXPERT_FF, D_MODEL), BF16),
                            pltpu.SemaphoreType.DMA((2,)), pltpu.SemaphoreType.DMA(())]),
        out_shape=jax.ShapeDtypeStruct(((na + MOE_ROWS) * ROW_TILES, LANES), F32),
        compiler_params=_cp(("arbitrary",)))(blk_exp, src, dst, nused, h2, wg, wu, wd)


def _rope_tables(pos):
    half = HEAD_DIM // 2
    inv = ROPE_THETA ** (-jnp.arange(half, dtype=F32) / half)
    ang = pos.astype(F32)[:, None] * inv[None, :]
    c, s = jnp.cos(ang), jnp.sin(ang)
    return jnp.concatenate([c, c, c, c], axis=1), jnp.concatenate([-s, s, -s, s], axis=1)


def _pad_cols(w, n):
    return jnp.pad(w, ((0, 0), (0, n - w.shape[1])))


def _pad_rows(w, lo, n):
    return jnp.pad(w, ((lo, n - lo - w.shape[0]), (0, 0)))


def kernel(x_prompt, x_sample, cache_swa_k, cache_swa_v, state_ret, state_wkv, state_shift, norm_mix, w_in, sinks,
           rwkv_mu, rwkv_w0, rwkv_w2, rwkv_a0, rwkv_a2, rwkv_g2, rwkv_k_k, rwkv_k_a, rwkv_r_k, rwkv_ln_w, rwkv_ln_b,
           w_out, norm_ffn, router_g, router_g_b, router_e, router_e_b, expert_w_gate, expert_w_up, expert_w_down,
           norm_final):
    lp = x_prompt.shape[1]
    bs = x_sample.shape[0]
    wb = cache_swa_k.shape[2]
    tm_p = 512
    xp = x_prompt.reshape(lp, D_MODEL)
    xs = x_sample.reshape(bs, D_MODEL)
    cos_p, sin_p = _rope_tables(jnp.arange(lp, dtype=jnp.int32))
    cos_s, sin_s = (jnp.broadcast_to(t, (bs, LANES))
                    for t in _rope_tables(PAST_LEN + jnp.arange(x_sample.shape[1], dtype=jnp.int32)))
    outs_p = [[] for _ in range(5)]
    outs_s = [[] for _ in range(5)]
    moe_p = moe_s = None
    row = lambda v: v.reshape(1, -1)
    for i in range(DEPTH):
        wi = w_in[i].astype(BF16)
        wa, wr, wc = wi[:, :A_COLS], wi[:, A_COLS:A_COLS + R_COLS], _pad_cols(wi[:, A_COLS + R_COLS:], C_PAD)
        cw = {
            "mu": _pad_cols(row(rwkv_mu[i]), C_PAD), "w0": row(rwkv_w0[i]), "a0": row(rwkv_a0[i]),
            "w2": _pad_rows(rwkv_w2[i], 0, LANES), "a2": _pad_rows(rwkv_a2[i], C_DECAY_LORA, LANES),
            "g2": _pad_rows(rwkv_g2[i], 0, 2 * LANES),
            "k_k": row(rwkv_k_k[i]), "k_a": row(rwkv_k_a[i]), "r_k": row(rwkv_r_k[i]),
        }
        lnw, lnb = row(rwkv_ln_w[i]), row(rwkv_ln_b[i])
        g_mix, g_ffn = row(norm_mix[i]), row(norm_ffn[i])
        wo = w_out[i].astype(BF16)
        wrt = _pad_cols(jnp.concatenate([router_g[i], router_e[i]], axis=1), LANES)
        wrt_hi = wrt.astype(BF16)
        wrt = jnp.stack([wrt_hi, (wrt - wrt_hi.astype(F32)).astype(BF16)])
        brt = _pad_cols(row(jnp.concatenate([router_g_b[i], router_e_b[i]])), LANES)

        xn, za, zr, zc = _inproj(xp, moe_p, g_mix, wa, wr, wc, cos_p, sin_p, tm_p)
        xp = xp if xn is None else xn
        oa = _swa_prompt(za, sinks[i])
        orr, s_ret = _ret_prompt(zr)
        prep = _rwkv_prep(zc, None, jnp.zeros((1, C_PAD), F32), cw, True, tm_p)
        oc, s_wkv = _rwkv_scan(prep, lnw, lnb)
        xp, h2, route = _outproj(oa, orr, oc, xp, wo, g_ffn, wrt, brt, tm_p, lp + tm_p, 0)
        nk = min(WINDOW, lp)
        outs_p[0].append(za[lp - nk:, A_WIDTH:A_WIDTH + A_KV_WIDTH].reshape(1, nk, A_KV_HEADS, HEAD_DIM))
        outs_p[1].append(za[lp - nk:, A_WIDTH + A_KV_WIDTH:].reshape(1, nk, A_KV_HEADS, HEAD_DIM))
        outs_p[2].append(s_ret[None])
        outs_p[3].append(s_wkv[None])
        outs_p[4].append(zc[lp - 1:, :C_COLS])

        xn, za, zr, zc = _inproj(xs, moe_s, g_mix, wa, wr, wc, cos_s, sin_s, bs)
        xs = xs if xn is None else xn
        prep = _rwkv_prep(zc, _pad_cols(state_shift[i], C_PAD), jnp.zeros((1, C_PAD), F32), cw, False, bs)
        oa, orr, oc, kc_new, vc_new, sret_new, swkv_new = _sample_mixers(
            za, zr, prep, cache_swa_k[i].reshape(bs, wb, A_KV_WIDTH), cache_swa_v[i].reshape(bs, wb, A_KV_WIDTH),
            state_ret[i], state_wkv[i], sinks[i], lnw, lnb)
        xs, h2, route = _outproj(oa, orr, oc, xs, wo, g_ffn, wrt, brt, bs, lp + tm_p, lp, shared=(h2, route))
        yw = _moe(h2, route, lp + bs, expert_w_gate[i], expert_w_up[i], expert_w_down[i])
        moe_p, moe_s = (yw, route, 0), (yw, route, lp)
        outs_s[0].append(kc_new.reshape(bs, wb, A_KV_HEADS, HEAD_DIM))
        outs_s[1].append(vc_new.reshape(bs, wb, A_KV_HEADS, HEAD_DIM))
        outs_s[2].append(sret_new)
        outs_s[3].append(swkv_new)
        outs_s[4].append(zc[:, :C_COLS])

    gfin = row(norm_final)
    y_prompt = _final(xp, moe_p, gfin, tm_p).reshape(x_prompt.shape)
    y_sample = _final(xs, moe_s, gfin, bs).reshape(x_sample.shape)
    sp = [jnp.stack(o) for o in outs_p]
    ss = [jnp.stack(o) for o in outs_s]
    return (y_prompt, y_sample, sp[0], sp[1], sp[2], sp[3], sp[4], ss[0], ss[1], ss[2], ss[3], ss[4])
```

```python
import functools

import numpy as np
import jax
import jax.numpy as jnp
from jax import lax
from jax.experimental import pallas as pl
from jax.experimental.pallas import tpu as pltpu

F32 = jnp.float32
BF16 = jnp.bfloat16

D_MODEL = 1024
DEPTH = 2
PAST_LEN = 16384
A_HEADS, A_KV_HEADS, HEAD_DIM, WINDOW = 8, 2, 64, 128
ROPE_THETA = 10000.0
A_WIDTH = A_HEADS * HEAD_DIM
A_KV_WIDTH = A_KV_HEADS * HEAD_DIM
A_COLS = A_WIDTH + 2 * A_KV_WIDTH
R_HEADS, R_DK, R_DV, R_CHUNK = 4, 64, 64, 128
R_WIDTH = R_HEADS * R_DV
R_COLS = 2 * R_HEADS * R_DK + 2 * R_WIDTH
C_HEADS, C_HEAD = 4, 64
C_WIDTH = C_HEADS * C_HEAD
C_DECAY_LORA, C_ICLR_LORA, C_GATE_LORA = 64, 64, 160
C_COLS = 3 * C_WIDTH + C_DECAY_LORA + C_ICLR_LORA + C_GATE_LORA
C_PAD = 1152
C_GN_EPS = 64e-5
N_GROUPS, EXPERTS_PER_GROUP, TOP_K, EXPERT_FF = 4, 8, 2, 512
N_EXPERTS = N_GROUPS * EXPERTS_PER_GROUP
EPS = 1e-6

LANES = 128
SUBLANES = 8
ROW_TILES = D_MODEL // LANES
MOE_ROWS = 512
DMA_UNROLL = 8
IDX_BITS = 19
VMEM_LIMIT = 56 * 1024 * 1024


def _cp(sem, vmem=VMEM_LIMIT):
    return pltpu.CompilerParams(dimension_semantics=sem, vmem_limit_bytes=vmem)


def _full(shape):
    n = len(shape)
    return pl.BlockSpec(shape, lambda *_: (0,) * n)


def _rmsnorm(x, g):
    return x * lax.rsqrt(jnp.mean(x * x, -1, keepdims=True) + EPS) * g


def _silu(x):
    return x * (1.0 / (1.0 + jnp.exp(-x)))


def _sigmoid(x):
    return 1.0 / (1.0 + jnp.exp(-x))


def _rope(x, cos, sin_signed):
    w = x.shape[-1]
    lane = lax.broadcasted_iota(jnp.int32, x.shape, x.ndim - 1)
    first = (lane % HEAD_DIM) < (HEAD_DIM // 2)
    swapped = jnp.where(first, pltpu.roll(x, w - HEAD_DIM // 2, axis=x.ndim - 1),
                        pltpu.roll(x, HEAD_DIM // 2, axis=x.ndim - 1))
    return x * cos + swapped * sin_signed


def _tile_lanes(x, n):
    return jnp.concatenate([x] * n, axis=-1) if n > 1 else x


def _seg_sum(x, seg=64):
    outs = []
    for h in range(x.shape[-1] // seg):
        s = jnp.sum(x[:, h * seg:(h + 1) * seg], axis=-1, keepdims=True)
        outs.append(jnp.broadcast_to(s, (x.shape[0], seg)))
    return jnp.concatenate(outs, axis=-1)


def _read_rows(ref, tm, first, stride):
    return jnp.concatenate([ref[pl.ds(first + s, tm, stride=stride), :] for s in range(ROW_TILES)], axis=1)


def _colbcast(x):
    z = jnp.concatenate([jnp.broadcast_to(x[:, :LANES], (64, LANES)),
                         jnp.broadcast_to(x[:, LANES:], (64, LANES))], axis=0)
    return z.T


def _rowpair(x):
    top = jnp.concatenate([x[:, 0:64], x[:, 128:192]], axis=1)
    bot = jnp.concatenate([x[:, 64:128], x[:, 192:256]], axis=1)
    return top, bot


def _unpair(top, bot):
    return jnp.concatenate([top[:, :64], bot[:, :64], top[:, 64:], bot[:, 64:]], axis=1)


def _rowbcast(top, bot):
    return jnp.concatenate([jnp.broadcast_to(top, (64, LANES)), jnp.broadcast_to(bot, (64, LANES))], axis=0)


def _halfsums(p):
    return jnp.sum(p[:64], axis=0, keepdims=True), jnp.sum(p[64:], axis=0, keepdims=True)


def _wkv_step(st, ab, wb, bb, kb, rb, vtop, vbot):
    sa_t, sa_b = _halfsums(st * ab)
    st = st * wb + _rowbcast(sa_t, sa_b) * bb + _rowbcast(vtop, vbot) * kb
    y_t, y_b = _halfsums(st * rb)
    return st, y_t, y_b


def _wkv_post(y, lnw, lnb, bonus, g):
    mu = _seg_sum(y) * (1.0 / C_HEAD)
    d = y - mu
    var = _seg_sum(d * d) * (1.0 / C_HEAD)
    yn = d * lax.rsqrt(var + C_GN_EPS)
    return (yn * lnw + lnb + bonus) * g


def _heads_to_t(s4, transpose):
    if transpose:
        m = jnp.concatenate([jnp.concatenate([s4[0], s4[1]], axis=1),
                             jnp.concatenate([s4[2], s4[3]], axis=1)], axis=0)
        return m.T
    return jnp.concatenate([jnp.concatenate([s4[0], s4[2]], axis=1),
                            jnp.concatenate([s4[1], s4[3]], axis=1)], axis=0)


def _t_to_heads(st, transpose):
    if transpose:
        m = st.T
        return [m[0:64, 0:64], m[0:64, 64:128], m[64:128, 0:64], m[64:128, 64:128]]
    return [st[0:64, 0:64], st[64:128, 0:64], st[0:64, 64:128], st[64:128, 64:128]]


def _inproj_kernel(*refs, combine, tm):
    if combine:
        x_ref, yw_ref, route_ref, g_ref, wa_ref, wr_ref, wc_ref, cos_ref, sin_ref, xo_ref, za_ref, zr_ref, zc_ref = refs
        route = route_ref[...]
        y0 = _read_rows(yw_ref, tm, 0, 2 * ROW_TILES)
        y1 = _read_rows(yw_ref, tm, ROW_TILES, 2 * ROW_TILES)
        x = x_ref[...] + (route[:, 2:3] * y0 + route[:, 3:4] * y1)
        xo_ref[...] = x
    else:
        x_ref, g_ref, wa_ref, wr_ref, wc_ref, cos_ref, sin_ref, za_ref, zr_ref, zc_ref = refs
        x = x_ref[...]
    h = _rmsnorm(x, g_ref[...]).astype(BF16)
    cos, sin = cos_ref[...], sin_ref[...]
    na = (A_WIDTH + A_KV_WIDTH) // LANES
    nr = 2 * R_HEADS * R_DK // LANES
    za = jnp.dot(h, wa_ref[...], preferred_element_type=F32)
    za_ref[:, :na * LANES] = _rope(za[:, :na * LANES], _tile_lanes(cos, na), _tile_lanes(sin, na))
    za_ref[:, na * LANES:] = za[:, na * LANES:]
    zr = jnp.dot(h, wr_ref[...], preferred_element_type=F32)
    zr_ref[:, :nr * LANES] = _rope(zr[:, :nr * LANES], _tile_lanes(cos, nr), _tile_lanes(sin, nr))
    zr_ref[:, nr * LANES:] = zr[:, nr * LANES:]
    zc_ref[...] = jnp.dot(h, wc_ref[...], preferred_element_type=F32)


def _inproj(x, moe, g, wa, wr, wc, cos, sin, tm):
    t = x.shape[0]
    row = lambda w: pl.BlockSpec((tm, w), lambda i: (i, 0))
    wspecs = [_full((1, D_MODEL)), _full(wa.shape), _full(wr.shape), _full(wc.shape), row(LANES), row(LANES)]
    zshapes = [jax.ShapeDtypeStruct((t, A_COLS), F32), jax.ShapeDtypeStruct((t, R_COLS), F32),
               jax.ShapeDtypeStruct((t, C_PAD), F32)]
    zspecs = [row(A_COLS), row(R_COLS), row(C_PAD)]
    if moe is None:
        return (None,) + tuple(pl.pallas_call(
            functools.partial(_inproj_kernel, combine=False, tm=tm), name="inproj",
            grid=(t // tm,), in_specs=[row(D_MODEL)] + wspecs, out_specs=zspecs, out_shape=zshapes,
            compiler_params=_cp(("parallel",)))(x, g, wa, wr, wc, cos, sin))
    yw, route, first_tok = moe
    off = first_tok // tm
    return pl.pallas_call(
        functools.partial(_inproj_kernel, combine=True, tm=tm), name="combine_inproj",
        grid=(t // tm,),
        in_specs=[row(D_MODEL), pl.BlockSpec((tm * 2 * ROW_TILES, LANES), lambda i: (i + off, 0)),
                  pl.BlockSpec((tm, LANES), lambda i: (i + off, 0))] + wspecs,
        out_specs=[row(D_MODEL)] + zspecs,
        out_shape=[jax.ShapeDtypeStruct((t, D_MODEL), F32)] + zshapes,
        compiler_params=_cp(("parallel",)))(x, yw, route, g, wa, wr, wc, cos, sin)


def _final_kernel(x_ref, yw_ref, route_ref, g_ref, o_ref, *, tm):
    route = route_ref[...]
    y0 = _read_rows(yw_ref, tm, 0, 2 * ROW_TILES)
    y1 = _read_rows(yw_ref, tm, ROW_TILES, 2 * ROW_TILES)
    x = x_ref[...] + (route[:, 2:3] * y0 + route[:, 3:4] * y1)
    o_ref[...] = _rmsnorm(x, g_ref[...])


def _final(x, moe, g, tm):
    t = x.shape[0]
    yw, route, first_tok = moe
    off = first_tok // tm
    row = lambda w: pl.BlockSpec((tm, w), lambda i: (i, 0))
    return pl.pallas_call(
        functools.partial(_final_kernel, tm=tm), grid=(t // tm,), name="combine_final",
        in_specs=[row(D_MODEL), pl.BlockSpec((tm * 2 * ROW_TILES, LANES), lambda i: (i + off, 0)),
                  pl.BlockSpec((tm, LANES), lambda i: (i + off, 0)), _full((1, D_MODEL))],
        out_specs=row(D_MODEL), out_shape=jax.ShapeDtypeStruct((t, D_MODEL), F32),
        compiler_params=_cp(("parallel",)))(x, yw, route, g)


def _swa_prompt_kernel(sinks_ref, q_ref, kc_ref, kp_ref, vc_ref, vp_ref, o_ref):
    i = pl.program_id(0)
    g = A_HEADS // A_KV_HEADS
    nq = g * WINDOW
    kcat = jnp.concatenate([kp_ref[...], kc_ref[...]], axis=0)
    vcat = jnp.concatenate([vp_ref[...], vc_ref[...]], axis=0)
    kswap = pltpu.roll(kcat, HEAD_DIM, axis=1)
    vswap = pltpu.roll(vcat, HEAD_DIM, axis=1)
    upper_k = lax.broadcasted_iota(jnp.int32, kcat.shape, 1) >= HEAD_DIM
    upper_q = lax.broadcasted_iota(jnp.int32, (WINDOW, LANES), 1) >= HEAD_DIM
    r = lax.broadcasted_iota(jnp.int32, (nq, 2 * WINDOW), 0) % WINDOW
    c = lax.broadcasted_iota(jnp.int32, (nq, 2 * WINDOW), 1)
    ok = (c >= r) & (c <= r + WINDOW) & ((c >= WINDOW) | (i > 0))
    sink_col4 = c == (r + WINDOW + 1) % (2 * WINDOW)
    sink_col = sink_col4[:WINDOW]
    slabs = []
    for kv in range(A_KV_HEADS):
        kboth = jnp.where(upper_k, kswap, kcat) if kv == 0 else jnp.where(upper_k, kcat, kswap)
        v_lo = jnp.where(upper_k, 0.0, vcat if kv == 0 else vswap)
        v_hi = jnp.where(upper_k, vswap if kv == 0 else vcat, 0.0)
        vpair = jnp.concatenate([v_lo, v_hi], axis=0)
        qs = []
        for j in range(g):
            h = kv * g + j
            slab = q_ref[:, (h // 2) * LANES:(h // 2 + 1) * LANES]
            qs.append(jnp.where(upper_q, slab, 0.0) if h % 2 else jnp.where(upper_q, 0.0, slab))
        q4 = jnp.concatenate(qs, axis=0)
        s = lax.dot_general(q4, kboth, (((1,), (1,)), ((), ())), preferred_element_type=F32) * (HEAD_DIM ** -0.5)
        sink = jnp.concatenate([jnp.where(sink_col, sinks_ref[kv * g + j], -jnp.inf) for j in range(g)], axis=0)
        s = jnp.where(ok, s, sink)
        m = jnp.max(s, -1, keepdims=True)
        p = jnp.exp(s - m)
        p = jnp.where(sink_col4, 0.0, p / jnp.sum(p, -1, keepdims=True))
        for j in range(0, g, 2):
            pp = jnp.concatenate([p[j * WINDOW:(j + 1) * WINDOW], p[(j + 1) * WINDOW:(j + 2) * WINDOW]], axis=1)
            slabs.append(jnp.dot(pp, vpair, preferred_element_type=F32))
    o_ref[...] = jnp.concatenate(slabs, axis=1)


def _swa_prompt(za, sinks):
    l = za.shape[0]
    nb = l // WINDOW
    cur = lambda w, c: pl.BlockSpec((WINDOW, w), lambda i: (i, c))
    prv = lambda w, c: pl.BlockSpec((WINDOW, w), lambda i: (jnp.maximum(i - 1, 0), c))
    kcol, vcol = A_WIDTH // A_KV_WIDTH, A_WIDTH // A_KV_WIDTH + 1
    return pl.pallas_call(
        _swa_prompt_kernel, grid=(nb,), name="swa_prompt",
        in_specs=[pl.BlockSpec(memory_space=pltpu.SMEM),
                  cur(A_WIDTH, 0), cur(A_KV_WIDTH, kcol), prv(A_KV_WIDTH, kcol),
                  cur(A_KV_WIDTH, vcol), prv(A_KV_WIDTH, vcol)],
        out_specs=cur(A_WIDTH, 0),
        out_shape=jax.ShapeDtypeStruct((l, A_WIDTH), F32),
        compiler_params=_cp(("parallel",)))(sinks, za, za, za, za, za)


def _ret_prompt_kernel(cdec_ref, zr_ref, intra_ref, qdec_ref, kdec_ref, o_ref, so_ref, s_scr):
    @pl.when(pl.program_id(0) == 0)
    def _():
        s_scr[...] = jnp.zeros_like(s_scr)

    z = zr_ref[...]
    qk = R_HEADS * R_DK
    q = z[:, :qk]
    k = z[:, qk:2 * qk] * (R_DK ** -0.5)
    v = z[:, 2 * qk:2 * qk + R_WIDTH]
    gate = z[:, 2 * qk + R_WIDTH:]
    qdec = qdec_ref[...]
    kd = k * kdec_ref[...]
    outs = []
    for h in range(R_HEADS):
        sl = slice(h * R_DK, (h + 1) * R_DK)
        qh, kh, vh = q[:, sl], k[:, sl], v[:, sl]
        att = lax.dot_general(qh, kh, (((1,), (1,)), ((), ())), preferred_element_type=F32) * intra_ref[h]
        s = s_scr[h]
        o = jnp.dot(att, vh, preferred_element_type=F32) + jnp.dot(qh, s, preferred_element_type=F32) * qdec[:, sl]
        s_scr[h] = s * cdec_ref[h] + jnp.dot(kd[:, sl].T, vh, preferred_element_type=F32)
        outs.append(o * lax.rsqrt(jnp.mean(o * o, -1, keepdims=True) + EPS))
    o_ref[...] = jnp.concatenate(outs, axis=1) * _silu(gate)
    so_ref[...] = s_scr[...]


def _ret_tables(c):
    log_g = jnp.log1p(-jnp.exp2(-5.0 - jnp.arange(R_HEADS, dtype=F32)))
    idx = jnp.arange(c, dtype=F32)
    diff = idx[:, None] - idx[None, :]
    intra = jnp.where(diff >= 0, jnp.exp(log_g[:, None, None] * jnp.maximum(diff, 0.0)), 0.0)
    q_dec = jnp.exp(log_g[None, :] * (idx[:, None] + 1.0))
    k_dec = jnp.exp(log_g[None, :] * (c - 1.0 - idx[:, None]))
    c_dec = jnp.exp(log_g * c)
    return intra, jnp.repeat(q_dec, R_DK, axis=1), jnp.repeat(k_dec, R_DK, axis=1), c_dec


def _ret_prompt(zr):
    l = zr.shape[0]
    c = R_CHUNK
    intra, qdec, kdec, cdec = _ret_tables(c)
    blk = lambda w: pl.BlockSpec((c, w), lambda i: (i, 0))
    return pl.pallas_call(
        _ret_prompt_kernel, grid=(l // c,), name="ret_prompt",
        in_specs=[pl.BlockSpec(memory_space=pltpu.SMEM), blk(R_COLS),
                  _full((R_HEADS, c, c)), _full((c, R_HEADS * R_DK)), _full((c, R_HEADS * R_DK))],
        out_specs=[blk(R_WIDTH), _full((R_HEADS, R_DK, R_DV))],
        out_shape=[jax.ShapeDtypeStruct((l, R_WIDTH), F32), jax.ShapeDtypeStruct((R_HEADS, R_DK, R_DV), F32)],
        scratch_shapes=[pltpu.VMEM((R_HEADS, R_DK, R_DV), F32)],
        compiler_params=_cp(("arbitrary",)))(cdec, zr, intra, qdec, kdec)


def _rwkv_prep_kernel(zc_ref, prev_ref, shift0_ref, mu_ref, w0_ref, w2_ref, a0_ref, a2_ref, g2_ref,
                      kk_ref, ka_ref, rk_ref,
                      r_ref, w_ref, k_ref, v_ref, a_ref, b_ref, g_ref, bonus_ref, *scratch, sequence, tm):
    zc = zc_ref[...]
    if sequence:
        row = lax.broadcasted_iota(jnp.int32, zc.shape, 0)
        boundary = jnp.where(pl.program_id(0) == 0, shift0_ref[...], prev_ref[SUBLANES - 1:SUBLANES, :])
        prev = jnp.where(row == 0, jnp.broadcast_to(boundary, zc.shape), pltpu.roll(zc, 1, axis=0))
    else:
        prev = prev_ref[...]
    zs = zc + mu_ref[...] * (prev - zc)
    o1, o2, o3 = C_WIDTH, 2 * C_WIDTH, 3 * C_WIDTH
    r, k, v = zs[:, :o1], zs[:, o1:o2], zs[:, o2:o3]
    lora = zs[:, o3:o3 + LANES]
    gl = zs[:, o3 + LANES:]
    w = -jax.nn.softplus(-(w0_ref[...] + jnp.dot(jnp.tanh(lora), w2_ref[...], preferred_element_type=F32))) - 0.5
    decay = jnp.exp(-jnp.exp(w))
    a = _sigmoid(a0_ref[...] + jnp.dot(lora, a2_ref[...], preferred_element_type=F32))
    g = jnp.dot(_sigmoid(gl), g2_ref[...], preferred_element_type=F32)
    kk = k * kk_ref[...]
    kk = kk / jnp.maximum(jnp.sqrt(_seg_sum(kk * kk)), 1e-12)
    k_mod = k * (1.0 + (a - 1.0) * ka_ref[...])
    v_ref[...] = v
    g_ref[...] = g
    bonus_ref[...] = _seg_sum(r * k_mod * rk_ref[...]) * v
    if not sequence:
        r_ref[...] = r
        w_ref[...] = decay
        k_ref[...] = k_mod
        a_ref[...] = -kk
        b_ref[...] = kk * a
        return
    sub =lax.broadcasted_iota(jnp.int32, decay.shape, 0) % SCAN_SUB
    cp = decay
    shift = 1
    while shift < SCAN_SUB:
        cp = jnp.where(sub >= shift, cp * pltpu.roll(cp, shift, axis=0), cp)
        shift *= 2
    cpx = jnp.where(sub == 0, 1.0, pltpu.roll(cp, 1, axis=0))
    inv = 1.0 / cp
    r_ref[...] = r * cp
    k_ref[...] = k_mod * inv
    a_ref[...] = -kk * cpx
    b_ref[...] = kk * a * inv
    for n, scr in enumerate(scratch):
        scr[...] = cp[:, n * LANES:(n + 1) * LANES]
        w_ref[:, n * LANES:(n + 1) * LANES] = scr[pl.ds(SCAN_SUB - 1, tm // SCAN_SUB, stride=SCAN_SUB), :]


def _rwkv_prep(zc, prev, shift0, cw, sequence, tm):
    t = zc.shape[0]
    row = lambda w: pl.BlockSpec((tm, w), lambda i: (i, 0))
    if sequence:
        per = tm // SUBLANES
        prev_spec = pl.BlockSpec((SUBLANES, C_PAD), lambda i: (jnp.maximum(i * per - 1, 0), 0))
        prev = zc
    else:
        prev_spec = row(C_PAD)
    vec = _full((1, C_WIDTH))
    out_specs = [row(C_WIDTH)] * 8
    out_shape = [jax.ShapeDtypeStruct((t, C_WIDTH), F32)] * 8
    if sequence:
        out_specs[1] = pl.BlockSpec((tm // SCAN_SUB, C_WIDTH), lambda i: (i, 0))
        out_shape[1] = jax.ShapeDtypeStruct((t // SCAN_SUB, C_WIDTH), F32)
    return pl.pallas_call(
        functools.partial(_rwkv_prep_kernel, sequence=sequence, tm=tm), grid=(t // tm,), name="rwkv_prep",
        in_specs=[row(C_PAD), prev_spec, _full((1, C_PAD)), _full((1, C_PAD)), vec, _full((LANES, C_WIDTH)),
                  vec, _full((LANES, C_WIDTH)), _full((2 * LANES, C_WIDTH)), vec, vec, vec],
        out_specs=out_specs, out_shape=out_shape,
        scratch_shapes=[pltpu.VMEM((tm, LANES), F32)] * (C_WIDTH // LANES) if sequence else [],
        compiler_params=_cp(("parallel",)))(zc, prev, shift0, cw["mu"], cw["w0"], cw["w2"], cw["a0"], cw["a2"],
                                           cw["g2"], cw["k_k"], cw["k_a"], cw["r_k"])


SCAN_BLOCK = 256
SCAN_SUB = 16
SCAN_NSUB = SCAN_BLOCK // SCAN_SUB
SCAN_UNROLL = 16
SCAN_OPS = 3


def _scan_select():
    sel = np.zeros((SCAN_SUB // 2, LANES, 2 * LANES), np.float32)
    for p in range(SCAN_SUB // 2):
        for piece in range(3):
            for rh in range(2):
                for tl in range(2):
                    row = piece * 2 * SCAN_SUB + rh * SCAN_SUB + 2 * p + tl
                    sel[p, row, tl * LANES + rh * 64: tl * LANES + (rh + 1) * 64] = 1.0
    return jnp.asarray(sel, BF16)


def _stage_lhs(ops, base, lhs_scr):
    for n, ref in enumerate(ops):
        xs = ref[pl.ds(base, SCAN_SUB), :]
        hi = xs.astype(BF16).astype(F32)
        r1 = xs - hi
        mid = r1.astype(BF16).astype(F32)
        lo = (r1 - mid).astype(BF16).astype(F32)
        g = jnp.concatenate([hi[:, :LANES], hi[:, LANES:], mid[:, :LANES], mid[:, LANES:],
                             lo[:, :LANES], lo[:, LANES:], jnp.zeros((LANES - 6 * SCAN_SUB, LANES), F32)], axis=0)
        lhs_scr[n * LANES:(n + 1) * LANES, :] = g.T.astype(BF16)


def _stage_tiles(lhs_scr, sel_ref, xb_scr, p):
    out = jnp.dot(lhs_scr[...], sel_ref[p], preferred_element_type=F32)
    for n in range(SCAN_OPS):
        xb_scr[n, 2 * p] = out[n * LANES:(n + 1) * LANES, :LANES]
        xb_scr[n, 2 * p + 1] = out[n * LANES:(n + 1) * LANES, LANES:]


def _rwkv_scan_kernel(r_ref, pc_ref, k_ref, v_ref, a_ref, b_ref, g_ref, bonus_ref, lnw_ref, lnb_ref, sel_ref,
                      o_ref, so_ref, st_scr, lhs_a, lhs_b, xb_a, xb_b, vt_scr, vb_scr, yt_scr, yb_scr):
    @pl.when(pl.program_id(0) == 0)
    def _():
        st_scr[...] = jnp.zeros_like(st_scr)

    vtop, vbot = _rowpair(v_ref[...])
    vt_scr[...] = vtop
    vb_scr[...] = vbot
    ops = (a_ref, b_ref, k_ref)
    pairs = SCAN_UNROLL // 2
    _stage_lhs(ops, 0, lhs_a)
    for p in range(SCAN_SUB // 2):
        _stage_tiles(lhs_a, sel_ref, xb_a, p)
    _stage_lhs(ops, SCAN_SUB, lhs_b)

    def half(sc, st, lhs_cur, xb_cur, lhs_nxt, xb_nxt):
        base = pl.multiple_of(sc * SCAN_SUB, SCAN_SUB)

        def quad(q, st):
            for pp in range(pairs):
                _stage_tiles(lhs_nxt, sel_ref, xb_nxt, q * pairs + pp)
            for uu in range(SCAN_UNROLL):
                u = q * SCAN_UNROLL + uu
                t = base + u
                sa_t, sa_b = _halfsums(st * xb_cur[0, u])
                st = (st + _rowbcast(sa_t, sa_b) * xb_cur[1, u]
                      + _rowbcast(vt_scr[pl.ds(t, 1), :], vb_scr[pl.ds(t, 1), :]) * xb_cur[2, u])
                y_t, y_b = _halfsums(st * _colbcast(r_ref[pl.ds(t, 1), :]))
                yt_scr[pl.ds(t, 1), :] = y_t
                yb_scr[pl.ds(t, 1), :] = y_b
            return st

        st = lax.fori_loop(0, SCAN_SUB // SCAN_UNROLL, quad, st) * _colbcast(pc_ref[pl.ds(sc, 1), :])
        nxt2 = pl.multiple_of(jnp.minimum(sc + 2, SCAN_NSUB - 1) * SCAN_SUB, SCAN_SUB)
        _stage_lhs(ops, nxt2, lhs_cur)
        return st

    def sub2(s2, st):
        st = half(2 * s2, st, lhs_a, xb_a, lhs_b, xb_b)
        return half(2 * s2 + 1, st, lhs_b, xb_b, lhs_a, xb_a)

    st = lax.fori_loop(0, SCAN_NSUB // 2, sub2, st_scr[...])
    st_scr[...] = st
    y = _unpair(yt_scr[...], yb_scr[...])
    o_ref[...] = _wkv_post(y, lnw_ref[...], lnb_ref[...], bonus_ref[...], g_ref[...])
    heads = _t_to_heads(st, transpose=True)
    for h in range(C_HEADS):
        so_ref[h] = heads[h]


def _rwkv_scan(prep, lnw, lnb):
    r, pc, k, v, a, b, g, bonus = prep
    l = r.shape[0]
    blk = pl.BlockSpec((SCAN_BLOCK, C_WIDTH), lambda i: (i, 0))
    vec = _full((1, C_WIDTH))
    sel = _scan_select()
    return pl.pallas_call(
        _rwkv_scan_kernel, grid=(l // SCAN_BLOCK,), name="rwkv_scan",
        in_specs=[blk, pl.BlockSpec((SCAN_NSUB, C_WIDTH), lambda i: (i, 0))] + [blk] * 6 + [vec, vec, _full(sel.shape)],
        out_specs=[blk, _full((C_HEADS, C_HEAD, C_HEAD))],
        out_shape=[jax.ShapeDtypeStruct((l, C_WIDTH), F32), jax.ShapeDtypeStruct((C_HEADS, C_HEAD, C_HEAD), F32)],
        scratch_shapes=[pltpu.VMEM((LANES, LANES), F32),
                        pltpu.VMEM((SCAN_OPS * LANES, LANES), BF16), pltpu.VMEM((SCAN_OPS * LANES, LANES), BF16),
                        pltpu.VMEM((SCAN_OPS, SCAN_SUB, LANES, LANES), F32),
                        pltpu.VMEM((SCAN_OPS, SCAN_SUB, LANES, LANES), F32),
                        pltpu.VMEM((SCAN_BLOCK, LANES), F32), pltpu.VMEM((SCAN_BLOCK, LANES), F32),
                        pltpu.VMEM((SCAN_BLOCK, LANES), F32), pltpu.VMEM((SCAN_BLOCK, LANES), F32)],
        compiler_params=_cp(("arbitrary",)))(r, pc, k, v, a, b, g, bonus, lnw, lnb, sel)


def _sample_kernel(sinks_ref, gam_ref, za_ref, zr_ref, r_ref, w_ref, k_ref, v_ref, a_ref, b_ref, g_ref, bonus_ref,
                   kc_ref, vc_ref, sret_ref, swkv_ref, lnw_ref, lnb_ref,
                   oa_ref, or_ref, oc_ref, kco_ref, vco_ref, sreto_ref, swkvo_ref):
    lane =lax.broadcasted_iota(jnp.int32, (1, LANES), 1)
    row8 = lax.broadcasted_iota(jnp.int32, (SUBLANES, LANES), 0)

    za = za_ref[0]
    q = za[:, :A_WIDTH]
    knew = za[:, A_WIDTH:A_WIDTH + A_KV_WIDTH]
    vnew = za[:, A_WIDTH + A_KV_WIDTH:]
    kc, vc = kc_ref[0], vc_ref[0]
    g = A_HEADS // A_KV_HEADS
    zero = jnp.zeros((1, HEAD_DIM), F32)
    qrows = []
    for h in range(A_HEADS):
        qh = q[:, h * HEAD_DIM:(h + 1) * HEAD_DIM]
        qrows.append(jnp.concatenate([qh, zero] if h // g == 0 else [zero, qh], axis=1))
    qm = jnp.concatenate(qrows + [jnp.zeros((LANES - A_HEADS, LANES), F32)], axis=0)
    scale = HEAD_DIM ** -0.5
    s = lax.dot_general(kc, qm, (((1,), (1,)), ((), ())), preferred_element_type=F32) * scale
    kn8 = jnp.where(row8 == 0, jnp.broadcast_to(knew, (SUBLANES, LANES)), 0.0)
    s_new = lax.dot_general(kn8, qm, (((1,), (1,)), ((), ())), preferred_element_type=F32)[0:1] * scale
    sink = jnp.zeros((1, LANES), F32)
    for h in range(A_HEADS):
        sink = jnp.where(lane == h, sinks_ref[h], sink)
    m = jnp.maximum(jnp.maximum(jnp.max(s, axis=0, keepdims=True), s_new), sink)
    p = jnp.exp(s - m)
    p_new = jnp.exp(s_new - m)
    denom = jnp.sum(p, axis=0, keepdims=True) + p_new + jnp.exp(sink - m)
    p = p / denom
    p_new = p_new / denom
    pn_col = jnp.broadcast_to(p_new, (LANES, LANES)).T[:, 0:1]
    o_full = jnp.dot(p.T, vc, preferred_element_type=F32) + pn_col * vnew
    oa_ref[0] = jnp.concatenate(
        [o_full[h:h + 1, (h // g) * HEAD_DIM:(h // g + 1) * HEAD_DIM] for h in range(A_HEADS)], axis=1)
    rowk = lax.broadcasted_iota(jnp.int32, kc.shape, 0)
    last = rowk == kc.shape[0] - 1
    kco_ref[0] = jnp.where(last, jnp.broadcast_to(knew, kc.shape), pltpu.roll(kc, kc.shape[0] - 1, axis=0))
    vco_ref[0] = jnp.where(last, jnp.broadcast_to(vnew, vc.shape), pltpu.roll(vc, vc.shape[0] - 1, axis=0))

    zr = zr_ref[0]
    qk = R_HEADS * R_DK
    rq = zr[:, :qk]
    rk = zr[:, qk:2 * qk] * (R_DK ** -0.5)
    rv = zr[:, 2 * qk:2 * qk + R_WIDTH]
    rg = zr[:, 2 * qk + R_WIDTH:]
    st = _heads_to_t(sret_ref[0], transpose=False)
    rr = lax.broadcasted_iota(jnp.int32, (LANES, LANES), 0) >= 64
    cc = lax.broadcasted_iota(jnp.int32, (LANES, LANES), 1) >= 64
    gamma = jnp.where(rr, jnp.where(cc, gam_ref[3], gam_ref[1]), jnp.where(cc, gam_ref[2], gam_ref[0]))
    vt, vb = _rowpair(rv)
    st = st * gamma + _colbcast(rk) * _rowbcast(vt, vb)
    o_t, o_b = _halfsums(st * _colbcast(rq))
    o = _unpair(o_t, o_b)
    o = o * lax.rsqrt(_seg_sum(o * o) * (1.0 / R_DV) + EPS)
    or_ref[0] = o * _silu(rg)
    heads = _t_to_heads(st, transpose=False)
    for h in range(R_HEADS):
        sreto_ref[0, h] = heads[h]

    st = _heads_to_t(swkv_ref[0], transpose=True)
    vt, vb = _rowpair(v_ref[0])
    st, y_t, y_b = _wkv_step(st, _colbcast(a_ref[0]), _colbcast(w_ref[0]), _colbcast(b_ref[0]),
                             _colbcast(k_ref[0]), _colbcast(r_ref[0]), vt, vb)
    oc_ref[0] = _wkv_post(_unpair(y_t, y_b), lnw_ref[...], lnb_ref[...], bonus_ref[0], g_ref[0])
    heads = _t_to_heads(st, transpose=True)
    for h in range(C_HEADS):
        swkvo_ref[0, h] = heads[h]


def _sample_mixers(za, zr, prep, kc, vc, sret, swkv, sinks, lnw, lnb):
    b = za.shape[0]
    wb = kc.shape[1]
    assert wb <= WINDOW and PAST_LEN >= wb
    gam = jnp.exp(jnp.log1p(-jnp.exp2(-5.0 - jnp.arange(R_HEADS, dtype=F32))) * 1.0)
    tok = lambda w: pl.BlockSpec((1, 1, w), lambda i: (i, 0, 0))
    cache = pl.BlockSpec((1, wb, A_KV_WIDTH), lambda i: (i, 0, 0))
    state = pl.BlockSpec((1, 4, 64, 64), lambda i: (i, 0, 0, 0))
    smem = pl.BlockSpec(memory_space=pltpu.SMEM)
    vec = _full((1, C_WIDTH))
    r3 = lambda x: x.reshape(b, 1, x.shape[-1])
    outs = pl.pallas_call(
        _sample_kernel, grid=(b,), name="sample_mixers",
        in_specs=[smem, smem, tok(A_COLS), tok(R_COLS)] + [tok(C_WIDTH)] * 8 + [cache, cache, state, state,
                  vec, vec],
        out_specs=[tok(A_WIDTH), tok(R_WIDTH), tok(C_WIDTH), cache, cache, state, state],
        out_shape=[jax.ShapeDtypeStruct((b, 1, A_WIDTH), F32), jax.ShapeDtypeStruct((b, 1, R_WIDTH), F32),
                   jax.ShapeDtypeStruct((b, 1, C_WIDTH), F32), jax.ShapeDtypeStruct(kc.shape, F32),
                   jax.ShapeDtypeStruct(vc.shape, F32), jax.ShapeDtypeStruct(sret.shape, F32),
                   jax.ShapeDtypeStruct(swkv.shape, F32)],
        compiler_params=_cp(("parallel",)))(sinks, gam, r3(za), r3(zr), *[r3(x) for x in prep],
                                           kc, vc, sret, swkv, lnw, lnb)
    oa, orr, oc = (x.reshape(b, x.shape[-1]) for x in outs[:3])
    return (oa, orr, oc) + tuple(outs[3:])


def _outproj_kernel(oa_ref, or_ref, oc_ref, x_ref, wo_ref, gf_ref, wrt_ref, brt_ref, *rest, tm):
    x1_ref, h2_ref, route_ref = rest[-3:]
    mix =jnp.concatenate([oa_ref[...], or_ref[...], oc_ref[...]], axis=1).astype(BF16)
    x1 = x_ref[...] + jnp.dot(mix, wo_ref[...], preferred_element_type=F32)
    x1_ref[...] = x1
    h2 = _rmsnorm(x1, gf_ref[...])
    for s in range(ROW_TILES):
        h2_ref[pl.ds(s, tm, stride=ROW_TILES), :] = h2[:, s * LANES:(s + 1) * LANES]
    h_hi = h2.astype(BF16)
    h_lo = (h2 - h_hi.astype(F32)).astype(BF16)
    w_hi, w_lo = wrt_ref[0], wrt_ref[1]
    logits = (jnp.dot(h_hi, w_hi, preferred_element_type=F32) + jnp.dot(h_lo, w_hi, preferred_element_type=F32)
              + jnp.dot(h_hi, w_lo, preferred_element_type=F32)) + brt_ref[...]
    lane = lax.broadcasted_iota(jnp.int32, logits.shape, 1).astype(F32)
    big = float(LANES)
    is_g = lane < N_GROUPS
    lg = jnp.where(is_g, logits, -jnp.inf)
    m_g = jnp.max(lg, -1, keepdims=True)
    grp = jnp.min(jnp.where(lg == m_g, lane, big), -1, keepdims=True)
    p_grp = 1.0 / jnp.sum(jnp.where(is_g, jnp.exp(lg - m_g), 0.0), -1, keepdims=True)
    lo = N_GROUPS + EXPERTS_PER_GROUP * grp
    in_grp = (lane >= lo) & (lane < lo + EXPERTS_PER_GROUP)
    le = jnp.where(in_grp, logits, -jnp.inf)
    l1 = jnp.max(le, -1, keepdims=True)
    i1 = jnp.min(jnp.where(le == l1, lane, big), -1, keepdims=True)
    le2 = jnp.where(lane == i1, -jnp.inf, le)
    l2 = jnp.max(le2, -1, keepdims=True)
    i2 = jnp.min(jnp.where(le2 == l2, lane, big), -1, keepdims=True)
    e = jnp.exp(l2 - l1)
    g1 = p_grp / (1.0 + e)
    g2 = p_grp * e / (1.0 + e)
    route = jnp.where(lane == 0, i1 - N_GROUPS, jnp.where(lane == 1, i2 - N_GROUPS,
                      jnp.where(lane == 2, g1, jnp.where(lane == 3, g2, 0.0))))
    route_ref[...] = route


def _outproj(oa, orr, oc, x, wo, gf, wrt, brt, tm, t_all, first_tok, shared=None):
    t = x.shape[0]
    off = first_tok // tm
    nblk = t // tm
    steps = nblk + (1 if shared is None else 0)
    assert shared is not None or t_all == t + tm
    row = lambda w: pl.BlockSpec((tm, w), lambda i: (jnp.minimum(i, nblk - 1), 0))
    in_specs = [row(A_WIDTH), row(R_WIDTH), row(C_WIDTH), row(D_MODEL), _full(wo.shape), _full((1, D_MODEL)),
                _full(wrt.shape), _full((1, LANES))]
    args = [oa, orr, oc, x, wo, gf, wrt, brt]
    aliases = {}
    if shared is not None:
        aliases = {len(args): 1, len(args) + 1: 2}
        in_specs += [pl.BlockSpec(memory_space=pl.ANY)] * 2
        args += list(shared)
    return pl.pallas_call(
        functools.partial(_outproj_kernel, tm=tm), grid=(steps,), name="outproj_router",
        in_specs=in_specs,
        out_specs=[row(D_MODEL), pl.BlockSpec((tm * ROW_TILES, LANES), lambda i: (i + off, 0)),
                   pl.BlockSpec((tm, LANES), lambda i: (i + off, 0))],
        out_shape=[jax.ShapeDtypeStruct((t, D_MODEL), F32), jax.ShapeDtypeStruct((t_all * ROW_TILES, LANES), F32),
                   jax.ShapeDtypeStruct((t_all, LANES), F32)],
        input_output_aliases=aliases,
        compiler_params=_cp(("arbitrary",)))(*args)


def _moe_kernel(blk_exp_ref, src_ref, dst_ref, nused_ref, h2_hbm, wg_ref, wu_ref, wd_ref, yw_hbm,
                gbuf, sbuf, xbuf, wgb, wub, wdb, gsem, ssem):
    i = pl.program_id(0)
    nb = pl.num_programs(0)
    nused = nused_ref[0]
    slot = i % 2

    def tile(off):
        return pl.ds(pl.multiple_of(off, ROW_TILES), ROW_TILES)

    def gather(blk, sl):
        def body(r, _):
            pltpu.make_async_copy(h2_hbm.at[tile(src_ref[blk * MOE_ROWS + r]), :],
                                  gbuf.at[sl, tile(r * ROW_TILES), :], gsem.at[sl]).start()
            return 0
        lax.fori_loop(0, MOE_ROWS, body, 0, unroll=DMA_UNROLL)

    def gather_wait(sl):
        def body(r, _):
            pltpu.make_async_copy(h2_hbm.at[tile(0), :], gbuf.at[sl, tile(0), :], gsem.at[sl]).wait()
            return 0
        lax.fori_loop(0, MOE_ROWS, body, 0, unroll=DMA_UNROLL)

    def scatter(blk):
        def body(r, _):
            pltpu.make_async_copy(sbuf.at[tile(r * ROW_TILES), :],
                                  yw_hbm.at[tile(dst_ref[blk * MOE_ROWS + r]), :], ssem).start()
            return 0
        lax.fori_loop(0, MOE_ROWS, body, 0, unroll=DMA_UNROLL)

    def scatter_wait():
        def body(r, _):
            pltpu.make_async_copy(sbuf.at[tile(0), :], yw_hbm.at[tile(0), :], ssem).wait()
            return 0
        lax.fori_loop(0, MOE_ROWS, body, 0, unroll=DMA_UNROLL)

    @pl.when((i == 0) & (nused > 0))
    def _():
        gather(0, 0)

    @pl.when(i < nused)
    def _():
        @pl.when((i == 0) | (blk_exp_ref[i] != blk_exp_ref[jnp.maximum(i - 1, 0)]))
        def _():
            wgb[...] = wg_ref[0].astype(BF16)
            wub[...] = wu_ref[0].astype(BF16)
            wdb[...] = wd_ref[0].astype(BF16)

        gather_wait(slot)

        @pl.when(i + 1 < nused)
        def _():
            gather(i + 1, 1 - slot)

        for s in range(ROW_TILES):
            xbuf[:, s * LANES:(s + 1) * LANES] = gbuf[slot, pl.ds(s, MOE_ROWS, stride=ROW_TILES), :].astype(BF16)
        x = xbuf[...]
        hid = _silu(jnp.dot(x, wgb[...], preferred_element_type=F32)) * jnp.dot(x, wub[...], preferred_element_type=F32)
        y = jnp.dot(hid.astype(BF16), wdb[...], preferred_element_type=F32)

        @pl.when(i > 0)
        def _():
            scatter_wait()

        for s in range(ROW_TILES):
            sbuf[pl.ds(s, MOE_ROWS, stride=ROW_TILES), :] = y[:, s * LANES:(s + 1) * LANES]
        scatter(i)

    @pl.when((i == nb - 1) & (nused > 0))
    def _():
        scatter_wait()
        spare = pltpu.make_async_copy(sbuf, yw_hbm.at[pl.ds(yw_hbm.shape[0] - MOE_ROWS * ROW_TILES,
                                                            MOE_ROWS * ROW_TILES), :], ssem)
        spare.start()
        spare.wait()


def _moe(h2, route, t, wg, wu, wd):
    na = t * TOP_K
    assert na < (1 << IDX_BITS)
    flat_e = route[:t, :TOP_K].astype(jnp.int32).reshape(na)
    experts = jnp.arange(N_EXPERTS, dtype=jnp.int32)
    counts = jnp.sum(flat_e[:, None] == experts[None, :], axis=0, dtype=jnp.int32)
    npad = (-counts) % MOE_ROWS
    real_keys = (flat_e << (IDX_BITS + 1)) | jnp.arange(na, dtype=jnp.int32)
    q = jnp.arange(MOE_ROWS, dtype=jnp.int32)[None, :]
    pad_keys = jnp.where(q < npad[:, None], (experts[:, None] << (IDX_BITS + 1)) | (1 << IDX_BITS) | q,
                         jnp.iinfo(jnp.int32).max)
    n_blocks = -(-na // MOE_ROWS) + N_EXPERTS
    n_rows = n_blocks * MOE_ROWS
    filler = jnp.full((n_rows - na - N_EXPERTS * MOE_ROWS,), jnp.iinfo(jnp.int32).max, jnp.int32)
    keys = jnp.sort(jnp.concatenate([real_keys, pad_keys.reshape(-1), filler]))
    is_pad = ((keys >> IDX_BITS) & 1) == 1
    idx = keys & ((1 << IDX_BITS) - 1)
    pos = jnp.arange(n_rows, dtype=jnp.int32)
    src = jnp.where(is_pad, 0, (idx // TOP_K) * ROW_TILES).astype(jnp.int32)
    dst = jnp.where(is_pad, na + pos % MOE_ROWS, idx).astype(jnp.int32) * ROW_TILES
    blk_exp = jnp.minimum(keys[::MOE_ROWS] >> (IDX_BITS + 1), N_EXPERTS - 1).astype(jnp.int32)
    nused = (jnp.sum(counts + npad) // MOE_ROWS).astype(jnp.int32).reshape(1)
    wspec = lambda a, b: pl.BlockSpec((1, a, b), lambda i, be, s, d, nu: (be[i], 0, 0))
    return pl.pallas_call(
        _moe_kernel, name="moe_experts",
        grid_spec=pltpu.PrefetchScalarGridSpec(
            num_scalar_prefetch=4, grid=(n_blocks,),
            in_specs=[pl.BlockSpec(memory_space=pl.ANY), wspec(D_MODEL, EXPERT_FF), wspec(D_MODEL, EXPERT_FF),
                      wspec(EXPERT_FF, D_MODEL)],
            out_specs=pl.BlockSpec(memory_space=pl.ANY),
            scratch_shapes=[pltpu.VMEM((2, MOE_ROWS * ROW_TILES, LANES), F32),
                            pltpu.VMEM((MOE_ROWS * ROW_TILES, LANES), F32),
                            pltpu.VMEM((MOE_ROWS, D_MODEL), BF16),
                            pltpu.VMEM((D_MODEL, EXPERT_FF), BF16), pltpu.VMEM((D_MODEL, EXPERT_FF), BF16),
                            pltpu.VMEM((EXPERT_FF, D_MODEL), BF16),
                            pltpu.SemaphoreType.DMA((2,)), pltpu.SemaphoreType.DMA(())]),
        out_shape=jax.ShapeDtypeStruct(((na + MOE_ROWS) * ROW_TILES, LANES), F32),
        compiler_params=_cp(("arbitrary",)))(blk_exp, src, dst, nused, h2, wg, wu, wd)


def _rope_tables(pos):
    half = HEAD_DIM // 2
    inv = ROPE_THETA ** (-jnp.arange(half, dtype=F32) / half)
    ang = pos.astype(F32)[:, None] * inv[None, :]
    c, s = jnp.cos(ang), jnp.sin(ang)
    return jnp.concatenate([c, c, c, c], axis=1), jnp.concatenate([-s, s, -s, s], axis=1)


def _pad_cols(w, n):
    return jnp.pad(w, ((0, 0), (0, n - w.shape[1])))


def _pad_rows(w, lo, n):
    return jnp.pad(w, ((lo, n - lo - w.shape[0]), (0, 0)))


def kernel(x_prompt, x_sample, cache_swa_k, cache_swa_v, state_ret, state_wkv, state_shift, norm_mix, w_in, sinks,
           rwkv_mu, rwkv_w0, rwkv_w2, rwkv_a0, rwkv_a2, rwkv_g2, rwkv_k_k, rwkv_k_a, rwkv_r_k, rwkv_ln_w, rwkv_ln_b,
           w_out, norm_ffn, router_g, router_g_b, router_e, router_e_b, expert_w_gate, expert_w_up, expert_w_down,
           norm_final):
    lp = x_prompt.shape[1]
    bs = x_sample.shape[0]
    wb = cache_swa_k.shape[2]
    tm_p = 512
    xp = x_prompt.reshape(lp, D_MODEL)
    xs = x_sample.reshape(bs, D_MODEL)
    cos_p, sin_p = _rope_tables(jnp.arange(lp, dtype=jnp.int32))
    cos_s, sin_s = (jnp.broadcast_to(t, (bs, LANES))
                    for t in _rope_tables(PAST_LEN + jnp.arange(x_sample.shape[1], dtype=jnp.int32)))
    outs_p = [[] for _ in range(5)]
    outs_s = [[] for _ in range(5)]
    moe_p = moe_s = None
    row = lambda v: v.reshape(1, -1)
    for i in range(DEPTH):
        wi = w_in[i].astype(BF16)
        wa, wr, wc = wi[:, :A_COLS], wi[:, A_COLS:A_COLS + R_COLS], _pad_cols(wi[:, A_COLS + R_COLS:], C_PAD)
        cw = {
            "mu": _pad_cols(row(rwkv_mu[i]), C_PAD), "w0": row(rwkv_w0[i]), "a0": row(rwkv_a0[i]),
            "w2": _pad_rows(rwkv_w2[i], 0, LANES), "a2": _pad_rows(rwkv_a2[i], C_DECAY_LORA, LANES),
            "g2": _pad_rows(rwkv_g2[i], 0, 2 * LANES),
            "k_k": row(rwkv_k_k[i]), "k_a": row(rwkv_k_a[i]), "r_k": row(rwkv_r_k[i]),
        }
        lnw, lnb = row(rwkv_ln_w[i]), row(rwkv_ln_b[i])
        g_mix, g_ffn = row(norm_mix[i]), row(norm_ffn[i])
        wo = w_out[i].astype(BF16)
        wrt = _pad_cols(jnp.concatenate([router_g[i], router_e[i]], axis=1), LANES)
        wrt_hi = wrt.astype(BF16)
        wrt = jnp.stack([wrt_hi, (wrt - wrt_hi.astype(F32)).astype(BF16)])
        brt = _pad_cols(row(jnp.concatenate([router_g_b[i], router_e_b[i]])), LANES)

        xn, za, zr, zc = _inproj(xp, moe_p, g_mix, wa, wr, wc, cos_p, sin_p, tm_p)
        xp = xp if xn is None else xn
        oa = _swa_prompt(za, sinks[i])
        orr, s_ret = _ret_prompt(zr)
        prep = _rwkv_prep(zc, None, jnp.zeros((1, C_PAD), F32), cw, True, tm_p)
        oc, s_wkv = _rwkv_scan(prep, lnw, lnb)
        xp, h2, route = _outproj(oa, orr, oc, xp, wo, g_ffn, wrt, brt, tm_p, lp + tm_p, 0)
        nk = min(WINDOW, lp)
        outs_p[0].append(za[lp - nk:, A_WIDTH:A_WIDTH + A_KV_WIDTH].reshape(1, nk, A_KV_HEADS, HEAD_DIM))
        outs_p[1].append(za[lp - nk:, A_WIDTH + A_KV_WIDTH:].reshape(1, nk, A_KV_HEADS, HEAD_DIM))
        outs_p[2].append(s_ret[None])
        outs_p[3].append(s_wkv[None])
        outs_p[4].append(zc[lp - 1:, :C_COLS])

        xn, za, zr, zc = _inproj(xs, moe_s, g_mix, wa, wr, wc, cos_s, sin_s, bs)
        xs = xs if xn is None else xn
        prep = _rwkv_prep(zc, _pad_cols(state_shift[i], C_PAD), jnp.zeros((1, C_PAD), F32), cw, False, bs)
        oa, orr, oc, kc_new, vc_new, sret_new, swkv_new = _sample_mixers(
            za, zr, prep, cache_swa_k[i].reshape(bs, wb, A_KV_WIDTH), cache_swa_v[i].reshape(bs, wb, A_KV_WIDTH),
            state_ret[i], state_wkv[i], sinks[i], lnw, lnb)
        xs, h2, route = _outproj(oa, orr, oc, xs, wo, g_ffn, wrt, brt, bs, lp + tm_p, lp, shared=(h2, route))
        yw = _moe(h2, route, lp + bs, expert_w_gate[i], expert_w_up[i], expert_w_down[i])
        moe_p, moe_s = (yw, route, 0), (yw, route, lp)
        outs_s[0].append(kc_new.reshape(bs, wb, A_KV_HEADS, HEAD_DIM))
        outs_s[1].append(vc_new.reshape(bs, wb, A_KV_HEADS, HEAD_DIM))
        outs_s[2].append(sret_new)
        outs_s[3].append(swkv_new)
        outs_s[4].append(zc[:, :C_COLS])

    gfin = row(norm_final)
    y_prompt = _final(xp, moe_p, gfin, tm_p).reshape(x_prompt.shape)
    y_sample = _final(xs, moe_s, gfin, bs).reshape(x_sample.shape)
    sp = [jnp.stack(o) for o in outs_p]
    ss = [jnp.stack(o) for o in outs_s]
    return (y_prompt, y_sample, sp[0], sp[1], sp[2], sp[3], sp[4], ss[0], ss[1], ss[2], ss[3], ss[4])
```

```python
import functools

import numpy as np
import jax
import jax.numpy as jnp
from jax import lax
from jax.experimental import pallas as pl
from jax.experimental.pallas import tpu as pltpu

F32 = jnp.float32
BF16 = jnp.bfloat16

D_MODEL = 1024
DEPTH = 2
PAST_LEN = 16384
A_HEADS, A_KV_HEADS, HEAD_DIM, WINDOW = 8, 2, 64, 128
ROPE_THETA = 10000.0
A_WIDTH = A_HEADS * HEAD_DIM
A_KV_WIDTH = A_KV_HEADS * HEAD_DIM
A_COLS = A_WIDTH + 2 * A_KV_WIDTH
R_HEADS, R_DK, R_DV, R_CHUNK = 4, 64, 64, 128
R_WIDTH = R_HEADS * R_DV
R_COLS = 2 * R_HEADS * R_DK + 2 * R_WIDTH
C_HEADS, C_HEAD = 4, 64
C_WIDTH = C_HEADS * C_HEAD
C_DECAY_LORA, C_ICLR_LORA, C_GATE_LORA = 64, 64, 160
C_COLS = 3 * C_WIDTH + C_DECAY_LORA + C_ICLR_LORA + C_GATE_LORA
C_PAD = 1152
C_GN_EPS = 64e-5
N_GROUPS, EXPERTS_PER_GROUP, TOP_K, EXPERT_FF = 4, 8, 2, 512
N_EXPERTS = N_GROUPS * EXPERTS_PER_GROUP
EPS = 1e-6

LANES = 128
SUBLANES = 8
ROW_TILES = D_MODEL // LANES
MOE_ROWS = 256
DMA_UNROLL = 8
IDX_BITS = 19
VMEM_LIMIT = 56 * 1024 * 1024


def _cp(sem, vmem=VMEM_LIMIT):
    return pltpu.CompilerParams(dimension_semantics=sem, vmem_limit_bytes=vmem)


def _full(shape):
    n = len(shape)
    return pl.BlockSpec(shape, lambda *_: (0,) * n)


def _rmsnorm(x, g):
    return x * lax.rsqrt(jnp.mean(x * x, -1, keepdims=True) + EPS) * g


def _silu(x):
    return x * (1.0 / (1.0 + jnp.exp(-x)))


def _sigmoid(x):
    return 1.0 / (1.0 + jnp.exp(-x))


def _rope(x, cos, sin_signed):
    w = x.shape[-1]
    lane = lax.broadcasted_iota(jnp.int32, x.shape, x.ndim - 1)
    first = (lane % HEAD_DIM) < (HEAD_DIM // 2)
    swapped = jnp.where(first, pltpu.roll(x, w - HEAD_DIM // 2, axis=x.ndim - 1),
                        pltpu.roll(x, HEAD_DIM // 2, axis=x.ndim - 1))
    return x * cos + swapped * sin_signed


def _tile_lanes(x, n):
    return jnp.concatenate([x] * n, axis=-1) if n > 1 else x


def _seg_sum(x, seg=64):
    outs = []
    for h in range(x.shape[-1] // seg):
        s = jnp.sum(x[:, h * seg:(h + 1) * seg], axis=-1, keepdims=True)
        outs.append(jnp.broadcast_to(s, (x.shape[0], seg)))
    return jnp.concatenate(outs, axis=-1)


def _read_rows(ref, tm, first, stride):
    return jnp.concatenate([ref[pl.ds(first + s, tm, stride=stride), :] for s in range(ROW_TILES)], axis=1)


def _colbcast(x):
    z = jnp.concatenate([jnp.broadcast_to(x[:, :LANES], (64, LANES)),
                         jnp.broadcast_to(x[:, LANES:], (64, LANES))], axis=0)
    return z.T


def _rowpair(x):
    top = jnp.concatenate([x[:, 0:64], x[:, 128:192]], axis=1)
    bot = jnp.concatenate([x[:, 64:128], x[:, 192:256]], axis=1)
    return top, bot


def _unpair(top, bot):
    return jnp.concatenate([top[:, :64], bot[:, :64], top[:, 64:], bot[:, 64:]], axis=1)


def _rowbcast(top, bot):
    return jnp.concatenate([jnp.broadcast_to(top, (64, LANES)), jnp.broadcast_to(bot, (64, LANES))], axis=0)


def _halfsums(p):
    return jnp.sum(p[:64], axis=0, keepdims=True), jnp.sum(p[64:], axis=0, keepdims=True)


def _wkv_step(st, ab, wb, bb, kb, rb, vtop, vbot):
    sa_t, sa_b = _halfsums(st * ab)
    st = st * wb + _rowbcast(sa_t, sa_b) * bb + _rowbcast(vtop, vbot) * kb
    y_t, y_b = _halfsums(st * rb)
    return st, y_t, y_b


def _wkv_post(y, lnw, lnb, bonus, g):
    mu = _seg_sum(y) * (1.0 / C_HEAD)
    d = y - mu
    var = _seg_sum(d * d) * (1.0 / C_HEAD)
    yn = d * lax.rsqrt(var + C_GN_EPS)
    return (yn * lnw + lnb + bonus) * g


def _heads_to_t(s4, transpose):
    if transpose:
        m = jnp.concatenate([jnp.concatenate([s4[0], s4[1]], axis=1),
                             jnp.concatenate([s4[2], s4[3]], axis=1)], axis=0)
        return m.T
    return jnp.concatenate([jnp.concatenate([s4[0], s4[2]], axis=1),
                            jnp.concatenate([s4[1], s4[3]], axis=1)], axis=0)


def _t_to_heads(st, transpose):
    if transpose:
        m = st.T
        return [m[0:64, 0:64], m[0:64, 64:128], m[64:128, 0:64], m[64:128, 64:128]]
    return [st[0:64, 0:64], st[64:128, 0:64], st[0:64, 64:128], st[64:128, 64:128]]


def _inproj_kernel(*refs, combine, tm):
    if combine:
        x_ref, yw_ref, route_ref, g_ref, wa_ref, wr_ref, wc_ref, cos_ref, sin_ref, xo_ref, za_ref, zr_ref, zc_ref = refs
        route = route_ref[...]
        y0 = _read_rows(yw_ref, tm, 0, 2 * ROW_TILES)
        y1 = _read_rows(yw_ref, tm, ROW_TILES, 2 * ROW_TILES)
        x = x_ref[...] + (route[:, 2:3] * y0 + route[:, 3:4] * y1)
        xo_ref[...] = x
    else:
        x_ref, g_ref, wa_ref, wr_ref, wc_ref, cos_ref, sin_ref, za_ref, zr_ref, zc_ref = refs
        x = x_ref[...]
    h = _rmsnorm(x, g_ref[...]).astype(BF16)
    cos, sin = cos_ref[...], sin_ref[...]
    na = (A_WIDTH + A_KV_WIDTH) // LANES
    nr = 2 * R_HEADS * R_DK // LANES
    za = jnp.dot(h, wa_ref[...], preferred_element_type=F32)
    za_ref[:, :na * LANES] = _rope(za[:, :na * LANES], _tile_lanes(cos, na), _tile_lanes(sin, na))
    za_ref[:, na * LANES:] = za[:, na * LANES:]
    zr = jnp.dot(h, wr_ref[...], preferred_element_type=F32)
    zr_ref[:, :nr * LANES] = _rope(zr[:, :nr * LANES], _tile_lanes(cos, nr), _tile_lanes(sin, nr))
    zr_ref[:, nr * LANES:] = zr[:, nr * LANES:]
    zc_ref[...] = jnp.dot(h, wc_ref[...], preferred_element_type=F32)


def _inproj(x, moe, g, wa, wr, wc, cos, sin, tm):
    t = x.shape[0]
    row = lambda w: pl.BlockSpec((tm, w), lambda i: (i, 0))
    wspecs = [_full((1, D_MODEL)), _full(wa.shape), _full(wr.shape), _full(wc.shape), row(LANES), row(LANES)]
    zshapes = [jax.ShapeDtypeStruct((t, A_COLS), F32), jax.ShapeDtypeStruct((t, R_COLS), F32),
               jax.ShapeDtypeStruct((t, C_PAD), F32)]
    zspecs = [row(A_COLS), row(R_COLS), row(C_PAD)]
    if moe is None:
        return (None,) + tuple(pl.pallas_call(
            functools.partial(_inproj_kernel, combine=False, tm=tm), name="inproj",
            grid=(t // tm,), in_specs=[row(D_MODEL)] + wspecs, out_specs=zspecs, out_shape=zshapes,
            compiler_params=_cp(("parallel",)))(x, g, wa, wr, wc, cos, sin))
    yw, route, first_tok = moe
    off = first_tok // tm
    return pl.pallas_call(
        functools.partial(_inproj_kernel, combine=True, tm=tm), name="combine_inproj",
        grid=(t // tm,),
        in_specs=[row(D_MODEL), pl.BlockSpec((tm * 2 * ROW_TILES, LANES), lambda i: (i + off, 0)),
                  pl.BlockSpec((tm, LANES), lambda i: (i + off, 0))] + wspecs,
        out_specs=[row(D_MODEL)] + zspecs,
        out_shape=[jax.ShapeDtypeStruct((t, D_MODEL), F32)] + zshapes,
        compiler_params=_cp(("parallel",)))(x, yw, route, g, wa, wr, wc, cos, sin)


def _final_kernel(x_ref, yw_ref, route_ref, g_ref, o_ref, *, tm):
    route = route_ref[...]
    y0 = _read_rows(yw_ref, tm, 0, 2 * ROW_TILES)
    y1 = _read_rows(yw_ref, tm, ROW_TILES, 2 * ROW_TILES)
    x = x_ref[...] + (route[:, 2:3] * y0 + route[:, 3:4] * y1)
    o_ref[...] = _rmsnorm(x, g_ref[...])


def _final(x, moe, g, tm):
    t = x.shape[0]
    yw, route, first_tok = moe
    off = first_tok // tm
    row = lambda w: pl.BlockSpec((tm, w), lambda i: (i, 0))
    return pl.pallas_call(
        functools.partial(_final_kernel, tm=tm), grid=(t // tm,), name="combine_final",
        in_specs=[row(D_MODEL), pl.BlockSpec((tm * 2 * ROW_TILES, LANES), lambda i: (i + off, 0)),
                  pl.BlockSpec((tm, LANES), lambda i: (i + off, 0)), _full((1, D_MODEL))],
        out_specs=row(D_MODEL), out_shape=jax.ShapeDtypeStruct((t, D_MODEL), F32),
        compiler_params=_cp(("parallel",)))(x, yw, route, g)


def _swa_prompt_kernel(sinks_ref, q_ref, kc_ref, kp_ref, vc_ref, vp_ref, o_ref):
    i = pl.program_id(0)
    g = A_HEADS // A_KV_HEADS
    nq = g * WINDOW
    kcat = jnp.concatenate([kp_ref[...], kc_ref[...]], axis=0)
    vcat = jnp.concatenate([vp_ref[...], vc_ref[...]], axis=0)
    kswap = pltpu.roll(kcat, HEAD_DIM, axis=1)
    vswap = pltpu.roll(vcat, HEAD_DIM, axis=1)
    upper_k = lax.broadcasted_iota(jnp.int32, kcat.shape, 1) >= HEAD_DIM
    upper_q = lax.broadcasted_iota(jnp.int32, (WINDOW, LANES), 1) >= HEAD_DIM
    r = lax.broadcasted_iota(jnp.int32, (nq, 2 * WINDOW), 0) % WINDOW
    c = lax.broadcasted_iota(jnp.int32, (nq, 2 * WINDOW), 1)
    ok = (c >= r) & (c <= r + WINDOW) & ((c >= WINDOW) | (i > 0))
    sink_col4 = c == (r + WINDOW + 1) % (2 * WINDOW)
    sink_col = sink_col4[:WINDOW]
    slabs = []
    for kv in range(A_KV_HEADS):
        kboth = jnp.where(upper_k, kswap, kcat) if kv == 0 else jnp.where(upper_k, kcat, kswap)
        v_lo = jnp.where(upper_k, 0.0, vcat if kv == 0 else vswap)
        v_hi = jnp.where(upper_k, vswap if kv == 0 else vcat, 0.0)
        vpair = jnp.concatenate([v_lo, v_hi], axis=0)
        qs = []
        for j in range(g):
            h = kv * g + j
            slab = q_ref[:, (h // 2) * LANES:(h // 2 + 1) * LANES]
            qs.append(jnp.where(upper_q, slab, 0.0) if h % 2 else jnp.where(upper_q, 0.0, slab))
        q4 = jnp.concatenate(qs, axis=0)
        s = lax.dot_general(q4, kboth, (((1,), (1,)), ((), ())), preferred_element_type=F32) * (HEAD_DIM ** -0.5)
        sink = jnp.concatenate([jnp.where(sink_col, sinks_ref[kv * g + j], -jnp.inf) for j in range(g)], axis=0)
        s = jnp.where(ok, s, sink)
        m = jnp.max(s, -1, keepdims=True)
        p = jnp.exp(s - m)
        p = jnp.where(sink_col4, 0.0, p / jnp.sum(p, -1, keepdims=True))
        for j in range(0, g, 2):
            pp = jnp.concatenate([p[j * WINDOW:(j + 1) * WINDOW], p[(j + 1) * WINDOW:(j + 2) * WINDOW]], axis=1)
            slabs.append(jnp.dot(pp, vpair, preferred_element_type=F32))
    o_ref[...] = jnp.concatenate(slabs, axis=1)


def _swa_prompt(za, sinks):
    l = za.shape[0]
    nb = l // WINDOW
    cur = lambda w, c: pl.BlockSpec((WINDOW, w), lambda i: (i, c))
    prv = lambda w, c: pl.BlockSpec((WINDOW, w), lambda i: (jnp.maximum(i - 1, 0), c))
    kcol, vcol = A_WIDTH // A_KV_WIDTH, A_WIDTH // A_KV_WIDTH + 1
    return pl.pallas_call(
        _swa_prompt_kernel, grid=(nb,), name="swa_prompt",
        in_specs=[pl.BlockSpec(memory_space=pltpu.SMEM),
                  cur(A_WIDTH, 0), cur(A_KV_WIDTH, kcol), prv(A_KV_WIDTH, kcol),
                  cur(A_KV_WIDTH, vcol), prv(A_KV_WIDTH, vcol)],
        out_specs=cur(A_WIDTH, 0),
        out_shape=jax.ShapeDtypeStruct((l, A_WIDTH), F32),
        compiler_params=_cp(("parallel",)))(sinks, za, za, za, za, za)


def _ret_prompt_kernel(cdec_ref, zr_ref, intra_ref, qdec_ref, kdec_ref, o_ref, so_ref, s_scr):
    @pl.when(pl.program_id(0) == 0)
    def _():
        s_scr[...] = jnp.zeros_like(s_scr)

    z = zr_ref[...]
    qk = R_HEADS * R_DK
    q = z[:, :qk]
    k = z[:, qk:2 * qk] * (R_DK ** -0.5)
    v = z[:, 2 * qk:2 * qk + R_WIDTH]
    gate = z[:, 2 * qk + R_WIDTH:]
    qdec = qdec_ref[...]
    kd = k * kdec_ref[...]
    outs = []
    for h in range(R_HEADS):
        sl = slice(h * R_DK, (h + 1) * R_DK)
        qh, kh, vh = q[:, sl], k[:, sl], v[:, sl]
        att = lax.dot_general(qh, kh, (((1,), (1,)), ((), ())), preferred_element_type=F32) * intra_ref[h]
        s = s_scr[h]
        o = jnp.dot(att, vh, preferred_element_type=F32) + jnp.dot(qh, s, preferred_element_type=F32) * qdec[:, sl]
        s_scr[h] = s * cdec_ref[h] + jnp.dot(kd[:, sl].T, vh, preferred_element_type=F32)
        outs.append(o * lax.rsqrt(jnp.mean(o * o, -1, keepdims=True) + EPS))
    o_ref[...] = jnp.concatenate(outs, axis=1) * _silu(gate)
    so_ref[...] = s_scr[...]


def _ret_tables(c):
    log_g = jnp.log1p(-jnp.exp2(-5.0 - jnp.arange(R_HEADS, dtype=F32)))
    idx = jnp.arange(c, dtype=F32)
    diff = idx[:, None] - idx[None, :]
    intra = jnp.where(diff >= 0, jnp.exp(log_g[:, None, None] * jnp.maximum(diff, 0.0)), 0.0)
    q_dec = jnp.exp(log_g[None, :] * (idx[:, None] + 1.0))
    k_dec = jnp.exp(log_g[None, :] * (c - 1.0 - idx[:, None]))
    c_dec = jnp.exp(log_g * c)
    return intra, jnp.repeat(q_dec, R_DK, axis=1), jnp.repeat(k_dec, R_DK, axis=1), c_dec


def _ret_prompt(zr):
    l = zr.shape[0]
    c = R_CHUNK
    intra, qdec, kdec, cdec = _ret_tables(c)
    blk = lambda w: pl.BlockSpec((c, w), lambda i: (i, 0))
    return pl.pallas_call(
        _ret_prompt_kernel, grid=(l // c,), name="ret_prompt",
        in_specs=[pl.BlockSpec(memory_space=pltpu.SMEM), blk(R_COLS),
                  _full((R_HEADS, c, c)), _full((c, R_HEADS * R_DK)), _full((c, R_HEADS * R_DK))],
        out_specs=[blk(R_WIDTH), _full((R_HEADS, R_DK, R_DV))],
        out_shape=[jax.ShapeDtypeStruct((l, R_WIDTH), F32), jax.ShapeDtypeStruct((R_HEADS, R_DK, R_DV), F32)],
        scratch_shapes=[pltpu.VMEM((R_HEADS, R_DK, R_DV), F32)],
        compiler_params=_cp(("arbitrary",)))(cdec, zr, intra, qdec, kdec)


def _rwkv_prep_kernel(zc_ref, prev_ref, shift0_ref, mu_ref, w0_ref, w2_ref, a0_ref, a2_ref, g2_ref,
                      kk_ref, ka_ref, rk_ref,
                      r_ref, w_ref, k_ref, v_ref, a_ref, b_ref, g_ref, bonus_ref, *scratch, sequence, tm):
    zc = zc_ref[...]
    if sequence:
        row = lax.broadcasted_iota(jnp.int32, zc.shape, 0)
        boundary = jnp.where(pl.program_id(0) == 0, shift0_ref[...], prev_ref[SUBLANES - 1:SUBLANES, :])
        prev = jnp.where(row == 0, jnp.broadcast_to(boundary, zc.shape), pltpu.roll(zc, 1, axis=0))
    else:
        prev = prev_ref[...]
    zs = zc + mu_ref[...] * (prev - zc)
    o1, o2, o3 = C_WIDTH, 2 * C_WIDTH, 3 * C_WIDTH
    r, k, v = zs[:, :o1], zs[:, o1:o2], zs[:, o2:o3]
    lora = zs[:, o3:o3 + LANES]
    gl = zs[:, o3 + LANES:]
    w = -jax.nn.softplus(-(w0_ref[...] + jnp.dot(jnp.tanh(lora), w2_ref[...], preferred_element_type=F32))) - 0.5
    decay = jnp.exp(-jnp.exp(w))
    a = _sigmoid(a0_ref[...] + jnp.dot(lora, a2_ref[...], preferred_element_type=F32))
    g = jnp.dot(_sigmoid(gl), g2_ref[...], preferred_element_type=F32)
    kk = k * kk_ref[...]
    kk = kk / jnp.maximum(jnp.sqrt(_seg_sum(kk * kk)), 1e-12)
    k_mod = k * (1.0 + (a - 1.0) * ka_ref[...])
    v_ref[...] = v
    g_ref[...] = g
    bonus_ref[...] = _seg_sum(r * k_mod * rk_ref[...]) * v
    if not sequence:
        r_ref[...] = r
        w_ref[...] = decay
        k_ref[...] = k_mod
        a_ref[...] = -kk
        b_ref[...] = kk * a
        return
    sub =lax.broadcasted_iota(jnp.int32, decay.shape, 0) % SCAN_SUB
    cp = decay
    shift = 1
    while shift < SCAN_SUB:
        cp = jnp.where(sub >= shift, cp * pltpu.roll(cp, shift, axis=0), cp)
        shift *= 2
    cpx = jnp.where(sub == 0, 1.0, pltpu.roll(cp, 1, axis=0))
    inv = 1.0 / cp
    r_ref[...] = r * cp
    k_ref[...] = k_mod * inv
    a_ref[...] = -kk * cpx
    b_ref[...] = kk * a * inv
    for n, scr in enumerate(scratch):
        scr[...] = cp[:, n * LANES:(n + 1) * LANES]
        w_ref[:, n * LANES:(n + 1) * LANES] = scr[pl.ds(SCAN_SUB - 1, tm // SCAN_SUB, stride=SCAN_SUB), :]


def _rwkv_prep(zc, prev, shift0, cw, sequence, tm):
    t = zc.shape[0]
    row = lambda w: pl.BlockSpec((tm, w), lambda i: (i, 0))
    if sequence:
        per = tm // SUBLANES
        prev_spec = pl.BlockSpec((SUBLANES, C_PAD), lambda i: (jnp.maximum(i * per - 1, 0), 0))
        prev = zc
    else:
        prev_spec = row(C_PAD)
    vec = _full((1, C_WIDTH))
    out_specs = [row(C_WIDTH)] * 8
    out_shape = [jax.ShapeDtypeStruct((t, C_WIDTH), F32)] * 8
    if sequence:
        out_specs[1] = pl.BlockSpec((tm // SCAN_SUB, C_WIDTH), lambda i: (i, 0))
        out_shape[1] = jax.ShapeDtypeStruct((t // SCAN_SUB, C_WIDTH), F32)
    return pl.pallas_call(
        functools.partial(_rwkv_prep_kernel, sequence=sequence, tm=tm), grid=(t // tm,), name="rwkv_prep",
        in_specs=[row(C_PAD), prev_spec, _full((1, C_PAD)), _full((1, C_PAD)), vec, _full((LANES, C_WIDTH)),
                  vec, _full((LANES, C_WIDTH)), _full((2 * LANES, C_WIDTH)), vec, vec, vec],
        out_specs=out_specs, out_shape=out_shape,
        scratch_shapes=[pltpu.VMEM((tm, LANES), F32)] * (C_WIDTH // LANES) if sequence else [],
        compiler_params=_cp(("parallel",)))(zc, prev, shift0, cw["mu"], cw["w0"], cw["w2"], cw["a0"], cw["a2"],
                                           cw["g2"], cw["k_k"], cw["k_a"], cw["r_k"])


SCAN_BLOCK = 256
SCAN_SUB = 16
SCAN_NSUB = SCAN_BLOCK // SCAN_SUB
SCAN_UNROLL = 16
SCAN_OPS = 2


def _scan_select():
    sel = np.zeros((SCAN_SUB // 2, LANES, 2 * LANES), np.float32)
    for p in range(SCAN_SUB // 2):
        for piece in range(3):
            for rh in range(2):
                for tl in range(2):
                    row = piece * 2 * SCAN_SUB + rh * SCAN_SUB + 2 * p + tl
                    sel[p, row, tl * LANES + rh * 64: tl * LANES + (rh + 1) * 64] = 1.0
    return jnp.asarray(sel, BF16)


def _stage_lhs(ops, base, lhs_scr):
    for n, ref in enumerate(ops):
        xs = ref[pl.ds(base, SCAN_SUB), :]
        hi = xs.astype(BF16).astype(F32)
        r1 = xs - hi
        mid = r1.astype(BF16).astype(F32)
        lo = (r1 - mid).astype(BF16).astype(F32)
        g = jnp.concatenate([hi[:, :LANES], hi[:, LANES:], mid[:, :LANES], mid[:, LANES:],
                             lo[:, :LANES], lo[:, LANES:], jnp.zeros((LANES - 6 * SCAN_SUB, LANES), F32)], axis=0)
        lhs_scr[n * LANES:(n + 1) * LANES, :] = g.T.astype(BF16)


def _stage_tiles(lhs_scr, sel_ref, xb_scr, p):
    out = jnp.dot(lhs_scr[...], sel_ref[p], preferred_element_type=F32)
    for n in range(SCAN_OPS):
        xb_scr[n, 2 * p] = out[n * LANES:(n + 1) * LANES, :LANES]
        xb_scr[n, 2 * p + 1] = out[n * LANES:(n + 1) * LANES, LANES:]


def _rwkv_scan_kernel(r_ref, pc_ref, k_ref, v_ref, a_ref, b_ref, g_ref, bonus_ref, lnw_ref, lnb_ref, sel_ref,
                      o_ref, so_ref, st_scr, lhs_a, lhs_b, xb_a, xb_b, vt_scr, vb_scr, yt_scr, yb_scr):
    @pl.when(pl.program_id(0) == 0)
    def _():
        st_scr[...] = jnp.zeros_like(st_scr)

    vtop, vbot = _rowpair(v_ref[...])
    vt_scr[...] = vtop
    vb_scr[...] = vbot
    ops = (a_ref, b_ref)
    pairs = SCAN_UNROLL // 2
    _stage_lhs(ops, 0, lhs_a)
    for p in range(SCAN_SUB // 2):
        _stage_tiles(lhs_a, sel_ref, xb_a, p)
    _stage_lhs(ops, SCAN_SUB, lhs_b)

    def half(sc, st, lhs_cur, xb_cur, lhs_nxt, xb_nxt):
        base = pl.multiple_of(sc * SCAN_SUB, SCAN_SUB)

        def quad(q, st):
            for pp in range(pairs):
                _stage_tiles(lhs_nxt, sel_ref, xb_nxt, q * pairs + pp)
            for uu in range(SCAN_UNROLL):
                u = q * SCAN_UNROLL + uu
                t = base + u
                sa_t, sa_b = _halfsums(st * xb_cur[0, u])
                st = (st + _rowbcast(sa_t, sa_b) * xb_cur[1, u]
                      + _rowbcast(vt_scr[pl.ds(t, 1), :], vb_scr[pl.ds(t, 1), :]) * _colbcast(k_ref[pl.ds(t, 1), :]))
                y_t, y_b = _halfsums(st * _colbcast(r_ref[pl.ds(t, 1), :]))
                yt_scr[pl.ds(t, 1), :] = y_t
                yb_scr[pl.ds(t, 1), :] = y_b
            return st

        st = lax.fori_loop(0, SCAN_SUB // SCAN_UNROLL, quad, st) * _colbcast(pc_ref[pl.ds(sc, 1), :])
        nxt2 = pl.multiple_of(jnp.minimum(sc + 2, SCAN_NSUB - 1) * SCAN_SUB, SCAN_SUB)
        _stage_lhs(ops, nxt2, lhs_cur)
        return st

    def sub2(s2, st):
        st = half(2 * s2, st, lhs_a, xb_a, lhs_b, xb_b)
        return half(2 * s2 + 1, st, lhs_b, xb_b, lhs_a, xb_a)

    st = lax.fori_loop(0, SCAN_NSUB // 2, sub2, st_scr[...])
    st_scr[...] = st
    y = _unpair(yt_scr[...], yb_scr[...])
    o_ref[...] = _wkv_post(y, lnw_ref[...], lnb_ref[...], bonus_ref[...], g_ref[...])
    heads = _t_to_heads(st, transpose=True)
    for h in range(C_HEADS):
        so_ref[h] = heads[h]


def _rwkv_scan(prep, lnw, lnb):
    r, pc, k, v, a, b, g, bonus = prep
    l = r.shape[0]
    blk = pl.BlockSpec((SCAN_BLOCK, C_WIDTH), lambda i: (i, 0))
    vec = _full((1, C_WIDTH))
    sel = _scan_select()
    return pl.pallas_call(
        _rwkv_scan_kernel, grid=(l // SCAN_BLOCK,), name="rwkv_scan",
        in_specs=[blk, pl.BlockSpec((SCAN_NSUB, C_WIDTH), lambda i: (i, 0))] + [blk] * 6 + [vec, vec, _full(sel.shape)],
        out_specs=[blk, _full((C_HEADS, C_HEAD, C_HEAD))],
        out_shape=[jax.ShapeDtypeStruct((l, C_WIDTH), F32), jax.ShapeDtypeStruct((C_HEADS, C_HEAD, C_HEAD), F32)],
        scratch_shapes=[pltpu.VMEM((LANES, LANES), F32),
                        pltpu.VMEM((SCAN_OPS * LANES, LANES), BF16), pltpu.VMEM((SCAN_OPS * LANES, LANES), BF16),
                        pltpu.VMEM((SCAN_OPS, SCAN_SUB, LANES, LANES), F32),
                        pltpu.VMEM((SCAN_OPS, SCAN_SUB, LANES, LANES), F32),
                        pltpu.VMEM((SCAN_BLOCK, LANES), F32), pltpu.VMEM((SCAN_BLOCK, LANES), F32),
                        pltpu.VMEM((SCAN_BLOCK, LANES), F32), pltpu.VMEM((SCAN_BLOCK, LANES), F32)],
        compiler_params=_cp(("arbitrary",)))(r, pc, k, v, a, b, g, bonus, lnw, lnb, sel)


def _sample_kernel(sinks_ref, gam_ref, za_ref, zr_ref, r_ref, w_ref, k_ref, v_ref, a_ref, b_ref, g_ref, bonus_ref,
                   kc_ref, vc_ref, sret_ref, swkv_ref, lnw_ref, lnb_ref,
                   oa_ref, or_ref, oc_ref, kco_ref, vco_ref, sreto_ref, swkvo_ref):
    lane =lax.broadcasted_iota(jnp.int32, (1, LANES), 1)
    row8 = lax.broadcasted_iota(jnp.int32, (SUBLANES, LANES), 0)

    za = za_ref[0]
    q = za[:, :A_WIDTH]
    knew = za[:, A_WIDTH:A_WIDTH + A_KV_WIDTH]
    vnew = za[:, A_WIDTH + A_KV_WIDTH:]
    kc, vc = kc_ref[0], vc_ref[0]
    g = A_HEADS // A_KV_HEADS
    zero = jnp.zeros((1, HEAD_DIM), F32)
    qrows = []
    for h in range(A_HEADS):
        qh = q[:, h * HEAD_DIM:(h + 1) * HEAD_DIM]
        qrows.append(jnp.concatenate([qh, zero] if h // g == 0 else [zero, qh], axis=1))
    qm = jnp.concatenate(qrows + [jnp.zeros((LANES - A_HEADS, LANES), F32)], axis=0)
    scale = HEAD_DIM ** -0.5
    s = lax.dot_general(kc, qm, (((1,), (1,)), ((), ())), preferred_element_type=F32) * scale
    kn8 = jnp.where(row8 == 0, jnp.broadcast_to(knew, (SUBLANES, LANES)), 0.0)
    s_new = lax.dot_general(kn8, qm, (((1,), (1,)), ((), ())), preferred_element_type=F32)[0:1] * scale
    sink = jnp.zeros((1, LANES), F32)
    for h in range(A_HEADS):
        sink = jnp.where(lane == h, sinks_ref[h], sink)
    m = jnp.maximum(jnp.maximum(jnp.max(s, axis=0, keepdims=True), s_new), sink)
    p = jnp.exp(s - m)
    p_new = jnp.exp(s_new - m)
    denom = jnp.sum(p, axis=0, keepdims=True) + p_new + jnp.exp(sink - m)
    p = p / denom
    p_new = p_new / denom
    pn_col = jnp.broadcast_to(p_new, (LANES, LANES)).T[:, 0:1]
    o_full = jnp.dot(p.T, vc, preferred_element_type=F32) + pn_col * vnew
    oa_ref[0] = jnp.concatenate(
        [o_full[h:h + 1, (h // g) * HEAD_DIM:(h // g + 1) * HEAD_DIM] for h in range(A_HEADS)], axis=1)
    rowk = lax.broadcasted_iota(jnp.int32, kc.shape, 0)
    last = rowk == kc.shape[0] - 1
    kco_ref[0] = jnp.where(last, jnp.broadcast_to(knew, kc.shape), pltpu.roll(kc, kc.shape[0] - 1, axis=0))
    vco_ref[0] = jnp.where(last, jnp.broadcast_to(vnew, vc.shape), pltpu.roll(vc, vc.shape[0] - 1, axis=0))

    zr = zr_ref[0]
    qk = R_HEADS * R_DK
    rq = zr[:, :qk]
    rk = zr[:, qk:2 * qk] * (R_DK ** -0.5)
    rv = zr[:, 2 * qk:2 * qk + R_WIDTH]
    rg = zr[:, 2 * qk + R_WIDTH:]
    st = _heads_to_t(sret_ref[0], transpose=False)
    rr = lax.broadcasted_iota(jnp.int32, (LANES, LANES), 0) >= 64
    cc = lax.broadcasted_iota(jnp.int32, (LANES, LANES), 1) >= 64
    gamma = jnp.where(rr, jnp.where(cc, gam_ref[3], gam_ref[1]), jnp.where(cc, gam_ref[2], gam_ref[0]))
    vt, vb = _rowpair(rv)
    st = st * gamma + _colbcast(rk) * _rowbcast(vt, vb)
    o_t, o_b = _halfsums(st * _colbcast(rq))
    o = _unpair(o_t, o_b)
    o = o * lax.rsqrt(_seg_sum(o * o) * (1.0 / R_DV) + EPS)
    or_ref[0] = o * _silu(rg)
    heads = _t_to_heads(st, transpose=False)
    for h in range(R_HEADS):
        sreto_ref[0, h] = heads[h]

    st = _heads_to_t(swkv_ref[0], transpose=True)
    vt, vb = _rowpair(v_ref[0])
    st, y_t, y_b = _wkv_step(st, _colbcast(a_ref[0]), _colbcast(w_ref[0]), _colbcast(b_ref[0]),
                             _colbcast(k_ref[0]), _colbcast(r_ref[0]), vt, vb)
    oc_ref[0] = _wkv_post(_unpair(y_t, y_b), lnw_ref[...], lnb_ref[...], bonus_ref[0], g_ref[0])
    heads = _t_to_heads(st, transpose=True)
    for h in range(C_HEADS):
        swkvo_ref[0, h] = heads[h]


def _sample_mixers(za, zr, prep, kc, vc, sret, swkv, sinks, lnw, lnb):
    b = za.shape[0]
    wb = kc.shape[1]
    assert wb <= WINDOW and PAST_LEN >= wb
    gam = jnp.exp(jnp.log1p(-jnp.exp2(-5.0 - jnp.arange(R_HEADS, dtype=F32))) * 1.0)
    tok = lambda w: pl.BlockSpec((1, 1, w), lambda i: (i, 0, 0))
    cache = pl.BlockSpec((1, wb, A_KV_WIDTH), lambda i: (i, 0, 0))
    state = pl.BlockSpec((1, 4, 64, 64), lambda i: (i, 0, 0, 0))
    smem = pl.BlockSpec(memory_space=pltpu.SMEM)
    vec = _full((1, C_WIDTH))
    r3 = lambda x: x.reshape(b, 1, x.shape[-1])
    outs = pl.pallas_call(
        _sample_kernel, grid=(b,), name="sample_mixers",
        in_specs=[smem, smem, tok(A_COLS), tok(R_COLS)] + [tok(C_WIDTH)] * 8 + [cache, cache, state, state,
                  vec, vec],
        out_specs=[tok(A_WIDTH), tok(R_WIDTH), tok(C_WIDTH), cache, cache, state, state],
        out_shape=[jax.ShapeDtypeStruct((b, 1, A_WIDTH), F32), jax.ShapeDtypeStruct((b, 1, R_WIDTH), F32),
                   jax.ShapeDtypeStruct((b, 1, C_WIDTH), F32), jax.ShapeDtypeStruct(kc.shape, F32),
                   jax.ShapeDtypeStruct(vc.shape, F32), jax.ShapeDtypeStruct(sret.shape, F32),
                   jax.ShapeDtypeStruct(swkv.shape, F32)],
        compiler_params=_cp(("parallel",)))(sinks, gam, r3(za), r3(zr), *[r3(x) for x in prep],
                                           kc, vc, sret, swkv, lnw, lnb)
    oa, orr, oc = (x.reshape(b, x.shape[-1]) for x in outs[:3])
    return (oa, orr, oc) + tuple(outs[3:])


def _outproj_kernel(oa_ref, or_ref, oc_ref, x_ref, wo_ref, gf_ref, wrt_ref, brt_ref, *rest, tm):
    x1_ref, h2_ref, route_ref = rest[-3:]
    mix =jnp.concatenate([oa_ref[...], or_ref[...], oc_ref[...]], axis=1).astype(BF16)
    x1 = x_ref[...] + jnp.dot(mix, wo_ref[...], preferred_element_type=F32)
    x1_ref[...] = x1
    h2 = _rmsnorm(x1, gf_ref[...])
    for s in range(ROW_TILES):
        h2_ref[pl.ds(s, tm, stride=ROW_TILES), :] = h2[:, s * LANES:(s + 1) * LANES]
    h_hi = h2.astype(BF16)
    h_lo = (h2 - h_hi.astype(F32)).astype(BF16)
    w_hi, w_lo = wrt_ref[0], wrt_ref[1]
    logits = (jnp.dot(h_hi, w_hi, preferred_element_type=F32) + jnp.dot(h_lo, w_hi, preferred_element_type=F32)
              + jnp.dot(h_hi, w_lo, preferred_element_type=F32)) + brt_ref[...]
    lane = lax.broadcasted_iota(jnp.int32, logits.shape, 1).astype(F32)
    big = float(LANES)
    is_g = lane < N_GROUPS
    lg = jnp.where(is_g, logits, -jnp.inf)
    m_g = jnp.max(lg, -1, keepdims=True)
    grp = jnp.min(jnp.where(lg == m_g, lane, big), -1, keepdims=True)
    p_grp = 1.0 / jnp.sum(jnp.where(is_g, jnp.exp(lg - m_g), 0.0), -1, keepdims=True)
    lo = N_GROUPS + EXPERTS_PER_GROUP * grp
    in_grp = (lane >= lo) & (lane < lo + EXPERTS_PER_GROUP)
    le = jnp.where(in_grp, logits, -jnp.inf)
    l1 = jnp.max(le, -1, keepdims=True)
    i1 = jnp.min(jnp.where(le == l1, lane, big), -1, keepdims=True)
    le2 = jnp.where(lane == i1, -jnp.inf, le)
    l2 = jnp.max(le2, -1, keepdims=True)
    i2 = jnp.min(jnp.where(le2 == l2, lane, big), -1, keepdims=True)
    e = jnp.exp(l2 - l1)
    g1 = p_grp / (1.0 + e)
    g2 = p_grp * e / (1.0 + e)
    route = jnp.where(lane == 0, i1 - N_GROUPS, jnp.where(lane == 1, i2 - N_GROUPS,
                      jnp.where(lane == 2, g1, jnp.where(lane == 3, g2, 0.0))))
    route_ref[...] = route


def _outproj(oa, orr, oc, x, wo, gf, wrt, brt, tm, t_all, first_tok, shared=None):
    t = x.shape[0]
    off = first_tok // tm
    nblk = t // tm
    steps = nblk + (1 if shared is None else 0)
    assert shared is not None or t_all == t + tm
    row = lambda w: pl.BlockSpec((tm, w), lambda i: (jnp.minimum(i, nblk - 1), 0))
    in_specs = [row(A_WIDTH), row(R_WIDTH), row(C_WIDTH), row(D_MODEL), _full(wo.shape), _full((1, D_MODEL)),
                _full(wrt.shape), _full((1, LANES))]
    args = [oa, orr, oc, x, wo, gf, wrt, brt]
    aliases = {}
    if shared is not None:
        aliases = {len(args): 1, len(args) + 1: 2}
        in_specs += [pl.BlockSpec(memory_space=pl.ANY)] * 2
        args += list(shared)
    return pl.pallas_call(
        functools.partial(_outproj_kernel, tm=tm), grid=(steps,), name="outproj_router",
        in_specs=in_specs,
        out_specs=[row(D_MODEL), pl.BlockSpec((tm * ROW_TILES, LANES), lambda i: (i + off, 0)),
                   pl.BlockSpec((tm, LANES), lambda i: (i + off, 0))],
        out_shape=[jax.ShapeDtypeStruct((t, D_MODEL), F32), jax.ShapeDtypeStruct((t_all * ROW_TILES, LANES), F32),
                   jax.ShapeDtypeStruct((t_all, LANES), F32)],
        input_output_aliases=aliases,
        compiler_params=_cp(("arbitrary",)))(*args)


def _moe_kernel(blk_exp_ref, src_ref, dst_ref, grp_ref, nused_ref, h2_hbm, wg_ref, wu_ref, wd_ref, yw_hbm,
                gbuf, sbuf, xbuf, wgb, wub, wdb, gsem, ssem):
    i = pl.program_id(0)
    nb = pl.num_programs(0)
    nused = nused_ref[0]
    slot = i % 2

    def tile(off):
        return pl.ds(pl.multiple_of(off, ROW_TILES), ROW_TILES)

    def rows(blk, fn):
        def body(g, _):
            for j in range(DMA_UNROLL):
                fn(g * DMA_UNROLL + j)
            return 0
        lax.fori_loop(0, grp_ref[blk], body, 0)

    def gather(blk, sl):
        rows(blk, lambda r: pltpu.make_async_copy(h2_hbm.at[tile(src_ref[blk * MOE_ROWS + r]), :],
                                                  gbuf.at[sl, tile(r * ROW_TILES), :], gsem.at[sl]).start())

    def gather_wait(blk, sl):
        rows(blk, lambda r: pltpu.make_async_copy(h2_hbm.at[tile(0), :], gbuf.at[sl, tile(0), :],
                                                  gsem.at[sl]).wait())

    def scatter(blk):
        rows(blk, lambda r: pltpu.make_async_copy(sbuf.at[tile(r * ROW_TILES), :],
                                                  yw_hbm.at[tile(dst_ref[blk * MOE_ROWS + r]), :], ssem).start())

    def scatter_wait(blk):
        rows(blk, lambda r: pltpu.make_async_copy(sbuf.at[tile(0), :], yw_hbm.at[tile(0), :], ssem).wait())

    @pl.when((i == 0) & (nused > 0))
    def _():
        gbuf[...] = jnp.zeros_like(gbuf)
        gather(0, 0)

    @pl.when(i < nused)
    def _():
        @pl.when((i == 0) | (blk_exp_ref[i] != blk_exp_ref[jnp.maximum(i - 1, 0)]))
        def _():
            wgb[...] = wg_ref[0].astype(BF16)
            wub[...] = wu_ref[0].astype(BF16)
            wdb[...] = wd_ref[0].astype(BF16)

        gather_wait(i, slot)

        @pl.when(i + 1 < nused)
        def _():
            gather(i + 1, 1 - slot)

        for s in range(ROW_TILES):
            xbuf[:, s * LANES:(s + 1) * LANES] = gbuf[slot, pl.ds(s, MOE_ROWS, stride=ROW_TILES), :].astype(BF16)
        x = xbuf[...]
        hid = _silu(jnp.dot(x, wgb[...], preferred_element_type=F32)) * jnp.dot(x, wub[...], preferred_element_type=F32)
        y = jnp.dot(hid.astype(BF16), wdb[...], preferred_element_type=F32)

        @pl.when(i > 0)
        def _():
            scatter_wait(i - 1)

        for s in range(ROW_TILES):
            sbuf[pl.ds(s, MOE_ROWS, stride=ROW_TILES), :] = y[:, s * LANES:(s + 1) * LANES]
        scatter(i)

    @pl.when((i == nb - 1) & (nused > 0))
    def _():
        scatter_wait(nused - 1)
        spare = pltpu.make_async_copy(sbuf, yw_hbm.at[pl.ds(yw_hbm.shape[0] - MOE_ROWS * ROW_TILES,
                                                            MOE_ROWS * ROW_TILES), :], ssem)
        spare.start()
        spare.wait()


def _moe(h2, route, t, wg, wu, wd):
    na = t * TOP_K
    assert na < (1 << IDX_BITS)
    flat_e = route[:t, :TOP_K].astype(jnp.int32).reshape(na)
    experts = jnp.arange(N_EXPERTS, dtype=jnp.int32)
    counts = jnp.sum(flat_e[:, None] == experts[None, :], axis=0, dtype=jnp.int32)
    npad = (-counts) % MOE_ROWS
    real_keys = (flat_e << (IDX_BITS + 1)) | jnp.arange(na, dtype=jnp.int32)
    q = jnp.arange(MOE_ROWS, dtype=jnp.int32)[None, :]
    pad_keys = jnp.where(q < npad[:, None], (experts[:, None] << (IDX_BITS + 1)) | (1 << IDX_BITS) | q,
                         jnp.iinfo(jnp.int32).max)
    n_blocks = -(-na // MOE_ROWS) + N_EXPERTS
    n_rows = n_blocks * MOE_ROWS
    filler = jnp.full((n_rows - na - N_EXPERTS * MOE_ROWS,), jnp.iinfo(jnp.int32).max, jnp.int32)
    keys = jnp.sort(jnp.concatenate([real_keys, pad_keys.reshape(-1), filler]))
    is_pad = ((keys >> IDX_BITS) & 1) == 1
    idx = keys & ((1 << IDX_BITS) - 1)
    pos = jnp.arange(n_rows, dtype=jnp.int32)
    src = jnp.where(is_pad, 0, (idx // TOP_K) * ROW_TILES).astype(jnp.int32)
    dst = jnp.where(is_pad, na + pos % MOE_ROWS, idx).astype(jnp.int32) * ROW_TILES
    blk_exp = jnp.minimum(keys[::MOE_ROWS] >> (IDX_BITS + 1), N_EXPERTS - 1).astype(jnp.int32)
    nused = (jnp.sum(counts + npad) // MOE_ROWS).astype(jnp.int32).reshape(1)
    real_rows = jnp.sum((~is_pad).reshape(n_blocks, MOE_ROWS), axis=1, dtype=jnp.int32)
    grp = (real_rows + DMA_UNROLL - 1) // DMA_UNROLL
    wspec = lambda a, b: pl.BlockSpec((1, a, b), lambda i, be, s, d, gr, nu: (be[i], 0, 0))
    return pl.pallas_call(
        _moe_kernel, name="moe_experts",
        grid_spec=pltpu.PrefetchScalarGridSpec(
            num_scalar_prefetch=5, grid=(n_blocks,),
            in_specs=[pl.BlockSpec(memory_space=pl.ANY), wspec(D_MODEL, EXPERT_FF), wspec(D_MODEL, EXPERT_FF),
                      wspec(EXPERT_FF, D_MODEL)],
            out_specs=pl.BlockSpec(memory_space=pl.ANY),
            scratch_shapes=[pltpu.VMEM((2, MOE_ROWS * ROW_TILES, LANES), F32),
                            pltpu.VMEM((MOE_ROWS * ROW_TILES, LANES), F32),
                            pltpu.VMEM((MOE_ROWS, D_MODEL), BF16),
                            pltpu.VMEM((D_MODEL, EXPERT_FF), BF16), pltpu.VMEM((D_MODEL, EXPERT_FF), BF16),
                            pltpu.VMEM((EXPERT_FF, D_MODEL), BF16),
                            pltpu.SemaphoreType.DMA((2,)), pltpu.SemaphoreType.DMA(())]),
        out_shape=jax.ShapeDtypeStruct(((na + MOE_ROWS) * ROW_TILES, LANES), F32),
        compiler_params=_cp(("arbitrary",)))(blk_exp, src, dst, grp, nused, h2, wg, wu, wd)


def _rope_tables(pos):
    half = HEAD_DIM // 2
    inv = ROPE_THETA ** (-jnp.arange(half, dtype=F32) / half)
    ang = pos.astype(F32)[:, None] * inv[None, :]
    c, s = jnp.cos(ang), jnp.sin(ang)
    return jnp.concatenate([c, c, c, c], axis=1), jnp.concatenate([-s, s, -s, s], axis=1)


def _pad_cols(w, n):
    return jnp.pad(w, ((0, 0), (0, n - w.shape[1])))


def _pad_rows(w, lo, n):
    return jnp.pad(w, ((lo, n - lo - w.shape[0]), (0, 0)))


def kernel(x_prompt, x_sample, cache_swa_k, cache_swa_v, state_ret, state_wkv, state_shift, norm_mix, w_in, sinks,
           rwkv_mu, rwkv_w0, rwkv_w2, rwkv_a0, rwkv_a2, rwkv_g2, rwkv_k_k, rwkv_k_a, rwkv_r_k, rwkv_ln_w, rwkv_ln_b,
           w_out, norm_ffn, router_g, router_g_b, router_e, router_e_b, expert_w_gate, expert_w_up, expert_w_down,
           norm_final):
    lp = x_prompt.shape[1]
    bs = x_sample.shape[0]
    wb = cache_swa_k.shape[2]
    tm_p = 512
    xp = x_prompt.reshape(lp, D_MODEL)
    xs = x_sample.reshape(bs, D_MODEL)
    cos_p, sin_p = _rope_tables(jnp.arange(lp, dtype=jnp.int32))
    cos_s, sin_s = (jnp.broadcast_to(t, (bs, LANES))
                    for t in _rope_tables(PAST_LEN + jnp.arange(x_sample.shape[1], dtype=jnp.int32)))
    outs_p = [[] for _ in range(5)]
    outs_s = [[] for _ in range(5)]
    moe_p = moe_s = None
    row = lambda v: v.reshape(1, -1)
    for i in range(DEPTH):
        wi = w_in[i].astype(BF16)
        wa, wr, wc = wi[:, :A_COLS], wi[:, A_COLS:A_COLS + R_COLS], _pad_cols(wi[:, A_COLS + R_COLS:], C_PAD)
        cw = {
            "mu": _pad_cols(row(rwkv_mu[i]), C_PAD), "w0": row(rwkv_w0[i]), "a0": row(rwkv_a0[i]),
            "w2": _pad_rows(rwkv_w2[i], 0, LANES), "a2": _pad_rows(rwkv_a2[i], C_DECAY_LORA, LANES),
            "g2": _pad_rows(rwkv_g2[i], 0, 2 * LANES),
            "k_k": row(rwkv_k_k[i]), "k_a": row(rwkv_k_a[i]), "r_k": row(rwkv_r_k[i]),
        }
        lnw, lnb = row(rwkv_ln_w[i]), row(rwkv_ln_b[i])
        g_mix, g_ffn = row(norm_mix[i]), row(norm_ffn[i])
        wo = w_out[i].astype(BF16)
        wrt = _pad_cols(jnp.concatenate([router_g[i], router_e[i]], axis=1), LANES)
        wrt_hi = wrt.astype(BF16)
        wrt = jnp.stack([wrt_hi, (wrt - wrt_hi.astype(F32)).astype(BF16)])
        brt = _pad_cols(row(jnp.concatenate([router_g_b[i], router_e_b[i]])), LANES)

        xn, za, zr, zc = _inproj(xp, moe_p, g_mix, wa, wr, wc, cos_p, sin_p, tm_p)
        xp = xp if xn is None else xn
        oa = _swa_prompt(za, sinks[i])
        orr, s_ret = _ret_prompt(zr)
        prep = _rwkv_prep(zc, None, jnp.zeros((1, C_PAD), F32), cw, True, tm_p)
        oc, s_wkv = _rwkv_scan(prep, lnw, lnb)
        xp, h2, route = _outproj(oa, orr, oc, xp, wo, g_ffn, wrt, brt, tm_p, lp + tm_p, 0)
        nk = min(WINDOW, lp)
        outs_p[0].append(za[lp - nk:, A_WIDTH:A_WIDTH + A_KV_WIDTH].reshape(1, nk, A_KV_HEADS, HEAD_DIM))
        outs_p[1].append(za[lp - nk:, A_WIDTH + A_KV_WIDTH:].reshape(1, nk, A_KV_HEADS, HEAD_DIM))
        outs_p[2].append(s_ret[None])
        outs_p[3].append(s_wkv[None])
        outs_p[4].append(zc[lp - 1:, :C_COLS])

        xn, za, zr, zc = _inproj(xs, moe_s, g_mix, wa, wr, wc, cos_s, sin_s, bs)
        xs = xs if xn is None else xn
        prep = _rwkv_prep(zc, _pad_cols(state_shift[i], C_PAD), jnp.zeros((1, C_PAD), F32), cw, False, bs)
        oa, orr, oc, kc_new, vc_new, sret_new, swkv_new = _sample_mixers(
            za, zr, prep, cache_swa_k[i].reshape(bs, wb, A_KV_WIDTH), cache_swa_v[i].reshape(bs, wb, A_KV_WIDTH),
            state_ret[i], state_wkv[i], sinks[i], lnw, lnb)
        xs, h2, route = _outproj(oa, orr, oc, xs, wo, g_ffn, wrt, brt, bs, lp + tm_p, lp, shared=(h2, route))
        yw = _moe(h2, route, lp + bs, expert_w_gate[i], expert_w_up[i], expert_w_down[i])
        moe_p, moe_s = (yw, route, 0), (yw, route, lp)
        outs_s[0].append(kc_new.reshape(bs, wb, A_KV_HEADS, HEAD_DIM))
        outs_s[1].append(vc_new.reshape(bs, wb, A_KV_HEADS, HEAD_DIM))
        outs_s[2].append(sret_new)
        outs_s[3].append(swkv_new)
        outs_s[4].append(zc[:, :C_COLS])

    gfin = row(norm_final)
    y_prompt = _final(xp, moe_p, gfin, tm_p).reshape(x_prompt.shape)
    y_sample = _final(xs, moe_s, gfin, bs).reshape(x_sample.shape)
    sp = [jnp.stack(o) for o in outs_p]
    ss = [jnp.stack(o) for o in outs_s]
    return (y_prompt, y_sample, sp[0], sp[1], sp[2], sp[3], sp[4], ss[0], ss[1], ss[2], ss[3], ss[4])
```

```python
import functools

import numpy as np
import jax
import jax.numpy as jnp
from jax import lax
from jax.experimental import pallas as pl
from jax.experimental.pallas import tpu as pltpu

F32 = jnp.float32
BF16 = jnp.bfloat16

D_MODEL = 1024
DEPTH = 2
PAST_LEN = 16384
A_HEADS, A_KV_HEADS, HEAD_DIM, WINDOW = 8, 2, 64, 128
ROPE_THETA = 10000.0
A_WIDTH = A_HEADS * HEAD_DIM
A_KV_WIDTH = A_KV_HEADS * HEAD_DIM
A_COLS = A_WIDTH + 2 * A_KV_WIDTH
R_HEADS, R_DK, R_DV, R_CHUNK = 4, 64, 64, 128
R_WIDTH = R_HEADS * R_DV
R_COLS = 2 * R_HEADS * R_DK + 2 * R_WIDTH
C_HEADS, C_HEAD = 4, 64
C_WIDTH = C_HEADS * C_HEAD
C_DECAY_LORA, C_ICLR_LORA, C_GATE_LORA = 64, 64, 160
C_COLS = 3 * C_WIDTH + C_DECAY_LORA + C_ICLR_LORA + C_GATE_LORA
C_PAD = 1152
C_GN_EPS = 64e-5
N_GROUPS, EXPERTS_PER_GROUP, TOP_K, EXPERT_FF = 4, 8, 2, 512
N_EXPERTS = N_GROUPS * EXPERTS_PER_GROUP
EPS = 1e-6

LANES = 128
SUBLANES = 8
ROW_TILES = D_MODEL // LANES
MOE_ROWS = 256
DMA_UNROLL = 8
IDX_BITS = 19
VMEM_LIMIT = 56 * 1024 * 1024


def _cp(sem, vmem=VMEM_LIMIT):
    return pltpu.CompilerParams(dimension_semantics=sem, vmem_limit_bytes=vmem)


def _full(shape):
    n = len(shape)
    return pl.BlockSpec(shape, lambda *_: (0,) * n)


def _rmsnorm(x, g):
    return x * lax.rsqrt(jnp.mean(x * x, -1, keepdims=True) + EPS) * g


def _silu(x):
    return x * (1.0 / (1.0 + jnp.exp(-x)))


def _sigmoid(x):
    return 1.0 / (1.0 + jnp.exp(-x))


def _rope(x, cos, sin_signed):
    w = x.shape[-1]
    lane = lax.broadcasted_iota(jnp.int32, x.shape, x.ndim - 1)
    first = (lane % HEAD_DIM) < (HEAD_DIM // 2)
    swapped = jnp.where(first, pltpu.roll(x, w - HEAD_DIM // 2, axis=x.ndim - 1),
                        pltpu.roll(x, HEAD_DIM // 2, axis=x.ndim - 1))
    return x * cos + swapped * sin_signed


def _tile_lanes(x, n):
    return jnp.concatenate([x] * n, axis=-1) if n > 1 else x


def _seg_sum(x, seg=64):
    outs = []
    for h in range(x.shape[-1] // seg):
        s = jnp.sum(x[:, h * seg:(h + 1) * seg], axis=-1, keepdims=True)
        outs.append(jnp.broadcast_to(s, (x.shape[0], seg)))
    return jnp.concatenate(outs, axis=-1)


def _read_rows(ref, tm, first, stride):
    return jnp.concatenate([ref[pl.ds(first + s, tm, stride=stride), :] for s in range(ROW_TILES)], axis=1)


def _colbcast(x):
    z = jnp.concatenate([jnp.broadcast_to(x[:, :LANES], (64, LANES)),
                         jnp.broadcast_to(x[:, LANES:], (64, LANES))], axis=0)
    return z.T


def _rowpair(x):
    top = jnp.concatenate([x[:, 0:64], x[:, 128:192]], axis=1)
    bot = jnp.concatenate([x[:, 64:128], x[:, 192:256]], axis=1)
    return top, bot


def _unpair(top, bot):
    return jnp.concatenate([top[:, :64], bot[:, :64], top[:, 64:], bot[:, 64:]], axis=1)


def _rowbcast(top, bot):
    return jnp.concatenate([jnp.broadcast_to(top, (64, LANES)), jnp.broadcast_to(bot, (64, LANES))], axis=0)


def _halfsums(p):
    return jnp.sum(p[:64], axis=0, keepdims=True), jnp.sum(p[64:], axis=0, keepdims=True)


def _wkv_step(st, ab, wb, bb, kb, rb, vtop, vbot):
    sa_t, sa_b = _halfsums(st * ab)
    st = st * wb + _rowbcast(sa_t, sa_b) * bb + _rowbcast(vtop, vbot) * kb
    y_t, y_b = _halfsums(st * rb)
    return st, y_t, y_b


def _wkv_post(y, lnw, lnb, bonus, g):
    mu = _seg_sum(y) * (1.0 / C_HEAD)
    d = y - mu
    var = _seg_sum(d * d) * (1.0 / C_HEAD)
    yn = d * lax.rsqrt(var + C_GN_EPS)
    return (yn * lnw + lnb + bonus) * g


def _heads_to_t(s4, transpose):
    if transpose:
        m = jnp.concatenate([jnp.concatenate([s4[0], s4[1]], axis=1),
                             jnp.concatenate([s4[2], s4[3]], axis=1)], axis=0)
        return m.T
    return jnp.concatenate([jnp.concatenate([s4[0], s4[2]], axis=1),
                            jnp.concatenate([s4[1], s4[3]], axis=1)], axis=0)


def _t_to_heads(st, transpose):
    if transpose:
        m = st.T
        return [m[0:64, 0:64], m[0:64, 64:128], m[64:128, 0:64], m[64:128, 64:128]]
    return [st[0:64, 0:64], st[64:128, 0:64], st[0:64, 64:128], st[64:128, 64:128]]


def _inproj_kernel(*refs, combine, tm):
    if combine:
        x_ref, yw_ref, route_ref, g_ref, wa_ref, wr_ref, wc_ref, cos_ref, sin_ref, xo_ref, za_ref, zr_ref, zc_ref = refs
        route = route_ref[...]
        y0 = _read_rows(yw_ref, tm, 0, 2 * ROW_TILES)
        y1 = _read_rows(yw_ref, tm, ROW_TILES, 2 * ROW_TILES)
        x = x_ref[...] + (route[:, 2:3] * y0 + route[:, 3:4] * y1)
        xo_ref[...] = x
    else:
        x_ref, g_ref, wa_ref, wr_ref, wc_ref, cos_ref, sin_ref, za_ref, zr_ref, zc_ref = refs
        x = x_ref[...]
    h = _rmsnorm(x, g_ref[...]).astype(BF16)
    cos, sin = cos_ref[...], sin_ref[...]
    na = (A_WIDTH + A_KV_WIDTH) // LANES
    nr = 2 * R_HEADS * R_DK // LANES
    za = jnp.dot(h, wa_ref[...], preferred_element_type=F32)
    za_ref[:, :na * LANES] = _rope(za[:, :na * LANES], _tile_lanes(cos, na), _tile_lanes(sin, na))
    za_ref[:, na * LANES:] = za[:, na * LANES:]
    zr = jnp.dot(h, wr_ref[...], preferred_element_type=F32)
    zr_ref[:, :nr * LANES] = _rope(zr[:, :nr * LANES], _tile_lanes(cos, nr), _tile_lanes(sin, nr))
    zr_ref[:, nr * LANES:] = zr[:, nr * LANES:]
    zc_ref[...] = jnp.dot(h, wc_ref[...], preferred_element_type=F32)


def _inproj(x, moe, g, wa, wr, wc, cos, sin, tm):
    t = x.shape[0]
    row = lambda w: pl.BlockSpec((tm, w), lambda i: (i, 0))
    wspecs = [_full((1, D_MODEL)), _full(wa.shape), _full(wr.shape), _full(wc.shape), row(LANES), row(LANES)]
    zshapes = [jax.ShapeDtypeStruct((t, A_COLS), F32), jax.ShapeDtypeStruct((t, R_COLS), F32),
               jax.ShapeDtypeStruct((t, C_PAD), F32)]
    zspecs = [row(A_COLS), row(R_COLS), row(C_PAD)]
    if moe is None:
        return (None,) + tuple(pl.pallas_call(
            functools.partial(_inproj_kernel, combine=False, tm=tm), name="inproj",
            grid=(t // tm,), in_specs=[row(D_MODEL)] + wspecs, out_specs=zspecs, out_shape=zshapes,
            compiler_params=_cp(("parallel",)))(x, g, wa, wr, wc, cos, sin))
    yw, route, first_tok = moe
    off = first_tok // tm
    return pl.pallas_call(
        functools.partial(_inproj_kernel, combine=True, tm=tm), name="combine_inproj",
        grid=(t // tm,),
        in_specs=[row(D_MODEL), pl.BlockSpec((tm * 2 * ROW_TILES, LANES), lambda i: (i + off, 0)),
                  pl.BlockSpec((tm, LANES), lambda i: (i + off, 0))] + wspecs,
        out_specs=[row(D_MODEL)] + zspecs,
        out_shape=[jax.ShapeDtypeStruct((t, D_MODEL), F32)] + zshapes,
        compiler_params=_cp(("parallel",)))(x, yw, route, g, wa, wr, wc, cos, sin)


def _final_kernel(x_ref, yw_ref, route_ref, g_ref, o_ref, *, tm):
    route = route_ref[...]
    y0 = _read_rows(yw_ref, tm, 0, 2 * ROW_TILES)
    y1 = _read_rows(yw_ref, tm, ROW_TILES, 2 * ROW_TILES)
    x = x_ref[...] + (route[:, 2:3] * y0 + route[:, 3:4] * y1)
    o_ref[...] = _rmsnorm(x, g_ref[...])


def _final(x, moe, g, tm):
    t = x.shape[0]
    yw, route, first_tok = moe
    off = first_tok // tm
    row = lambda w: pl.BlockSpec((tm, w), lambda i: (i, 0))
    return pl.pallas_call(
        functools.partial(_final_kernel, tm=tm), grid=(t // tm,), name="combine_final",
        in_specs=[row(D_MODEL), pl.BlockSpec((tm * 2 * ROW_TILES, LANES), lambda i: (i + off, 0)),
                  pl.BlockSpec((tm, LANES), lambda i: (i + off, 0)), _full((1, D_MODEL))],
        out_specs=row(D_MODEL), out_shape=jax.ShapeDtypeStruct((t, D_MODEL), F32),
        compiler_params=_cp(("parallel",)))(x, yw, route, g)


def _swa_prompt_kernel(sinks_ref, q_ref, kc_ref, kp_ref, vc_ref, vp_ref, o_ref):
    i = pl.program_id(0)
    g = A_HEADS // A_KV_HEADS
    nq = g * WINDOW
    kcat = jnp.concatenate([kp_ref[...], kc_ref[...]], axis=0)
    vcat = jnp.concatenate([vp_ref[...], vc_ref[...]], axis=0)
    kswap = pltpu.roll(kcat, HEAD_DIM, axis=1)
    vswap = pltpu.roll(vcat, HEAD_DIM, axis=1)
    upper_k = lax.broadcasted_iota(jnp.int32, kcat.shape, 1) >= HEAD_DIM
    upper_q = lax.broadcasted_iota(jnp.int32, (WINDOW, LANES), 1) >= HEAD_DIM
    r = lax.broadcasted_iota(jnp.int32, (nq, 2 * WINDOW), 0) % WINDOW
    c = lax.broadcasted_iota(jnp.int32, (nq, 2 * WINDOW), 1)
    ok = (c >= r) & (c <= r + WINDOW) & ((c >= WINDOW) | (i > 0))
    sink_col4 = c == (r + WINDOW + 1) % (2 * WINDOW)
    sink_col = sink_col4[:WINDOW]
    slabs = []
    for kv in range(A_KV_HEADS):
        kboth = jnp.where(upper_k, kswap, kcat) if kv == 0 else jnp.where(upper_k, kcat, kswap)
        v_lo = jnp.where(upper_k, 0.0, vcat if kv == 0 else vswap)
        v_hi = jnp.where(upper_k, vswap if kv == 0 else vcat, 0.0)
        vpair = jnp.concatenate([v_lo, v_hi], axis=0)
        qs = []
        for j in range(g):
            h = kv * g + j
            slab = q_ref[:, (h // 2) * LANES:(h // 2 + 1) * LANES]
            qs.append(jnp.where(upper_q, slab, 0.0) if h % 2 else jnp.where(upper_q, 0.0, slab))
        q4 = jnp.concatenate(qs, axis=0)
        s = lax.dot_general(q4, kboth, (((1,), (1,)), ((), ())), preferred_element_type=F32) * (HEAD_DIM ** -0.5)
        sink = jnp.concatenate([jnp.where(sink_col, sinks_ref[kv * g + j], -jnp.inf) for j in range(g)], axis=0)
        s = jnp.where(ok, s, sink)
        m = jnp.max(s, -1, keepdims=True)
        p = jnp.exp(s - m)
        p = jnp.where(sink_col4, 0.0, p / jnp.sum(p, -1, keepdims=True))
        for j in range(0, g, 2):
            pp = jnp.concatenate([p[j * WINDOW:(j + 1) * WINDOW], p[(j + 1) * WINDOW:(j + 2) * WINDOW]], axis=1)
            slabs.append(jnp.dot(pp, vpair, preferred_element_type=F32))
    o_ref[...] = jnp.concatenate(slabs, axis=1)


def _swa_prompt(za, sinks):
    l = za.shape[0]
    nb = l // WINDOW
    cur = lambda w, c: pl.BlockSpec((WINDOW, w), lambda i: (i, c))
    prv = lambda w, c: pl.BlockSpec((WINDOW, w), lambda i: (jnp.maximum(i - 1, 0), c))
    kcol, vcol = A_WIDTH // A_KV_WIDTH, A_WIDTH // A_KV_WIDTH + 1
    return pl.pallas_call(
        _swa_prompt_kernel, grid=(nb,), name="swa_prompt",
        in_specs=[pl.BlockSpec(memory_space=pltpu.SMEM),
                  cur(A_WIDTH, 0), cur(A_KV_WIDTH, kcol), prv(A_KV_WIDTH, kcol),
                  cur(A_KV_WIDTH, vcol), prv(A_KV_WIDTH, vcol)],
        out_specs=cur(A_WIDTH, 0),
        out_shape=jax.ShapeDtypeStruct((l, A_WIDTH), F32),
        compiler_params=_cp(("parallel",)))(sinks, za, za, za, za, za)


def _ret_prompt_kernel(cdec_ref, zr_ref, intra_ref, qdec_ref, kdec_ref, o_ref, so_ref, s_scr):
    @pl.when(pl.program_id(0) == 0)
    def _():
        s_scr[...] = jnp.zeros_like(s_scr)

    z = zr_ref[...]
    qk = R_HEADS * R_DK
    q = z[:, :qk]
    k = z[:, qk:2 * qk] * (R_DK ** -0.5)
    v = z[:, 2 * qk:2 * qk + R_WIDTH]
    gate = z[:, 2 * qk + R_WIDTH:]
    qdec = qdec_ref[...]
    kd = k * kdec_ref[...]
    outs = []
    for h in range(R_HEADS):
        sl = slice(h * R_DK, (h + 1) * R_DK)
        qh, kh, vh = q[:, sl], k[:, sl], v[:, sl]
        att = lax.dot_general(qh, kh, (((1,), (1,)), ((), ())), preferred_element_type=F32) * intra_ref[h]
        s = s_scr[h]
        o = jnp.dot(att, vh, preferred_element_type=F32) + jnp.dot(qh, s, preferred_element_type=F32) * qdec[:, sl]
        s_scr[h] = s * cdec_ref[h] + jnp.dot(kd[:, sl].T, vh, preferred_element_type=F32)
        outs.append(o * lax.rsqrt(jnp.mean(o * o, -1, keepdims=True) + EPS))
    o_ref[...] = jnp.concatenate(outs, axis=1) * _silu(gate)
    so_ref[...] = s_scr[...]


def _ret_tables(c):
    log_g = jnp.log1p(-jnp.exp2(-5.0 - jnp.arange(R_HEADS, dtype=F32)))
    idx = jnp.arange(c, dtype=F32)
    diff = idx[:, None] - idx[None, :]
    intra = jnp.where(diff >= 0, jnp.exp(log_g[:, None, None] * jnp.maximum(diff, 0.0)), 0.0)
    q_dec = jnp.exp(log_g[None, :] * (idx[:, None] + 1.0))
    k_dec = jnp.exp(log_g[None, :] * (c - 1.0 - idx[:, None]))
    c_dec = jnp.exp(log_g * c)
    return intra, jnp.repeat(q_dec, R_DK, axis=1), jnp.repeat(k_dec, R_DK, axis=1), c_dec


def _ret_prompt(zr):
    l = zr.shape[0]
    c = R_CHUNK
    intra, qdec, kdec, cdec = _ret_tables(c)
    blk = lambda w: pl.BlockSpec((c, w), lambda i: (i, 0))
    return pl.pallas_call(
        _ret_prompt_kernel, grid=(l // c,), name="ret_prompt",
        in_specs=[pl.BlockSpec(memory_space=pltpu.SMEM), blk(R_COLS),
                  _full((R_HEADS, c, c)), _full((c, R_HEADS * R_DK)), _full((c, R_HEADS * R_DK))],
        out_specs=[blk(R_WIDTH), _full((R_HEADS, R_DK, R_DV))],
        out_shape=[jax.ShapeDtypeStruct((l, R_WIDTH), F32), jax.ShapeDtypeStruct((R_HEADS, R_DK, R_DV), F32)],
        scratch_shapes=[pltpu.VMEM((R_HEADS, R_DK, R_DV), F32)],
        compiler_params=_cp(("arbitrary",)))(cdec, zr, intra, qdec, kdec)


def _rwkv_prep_kernel(zc_ref, prev_ref, shift0_ref, mu_ref, w0_ref, w2_ref, a0_ref, a2_ref, g2_ref,
                      kk_ref, ka_ref, rk_ref,
                      r_ref, w_ref, k_ref, v_ref, a_ref, b_ref, g_ref, bonus_ref, *scratch, sequence, tm):
    zc = zc_ref[...]
    if sequence:
        row = lax.broadcasted_iota(jnp.int32, zc.shape, 0)
        boundary = jnp.where(pl.program_id(0) == 0, shift0_ref[...], prev_ref[SUBLANES - 1:SUBLANES, :])
        prev = jnp.where(row == 0, jnp.broadcast_to(boundary, zc.shape), pltpu.roll(zc, 1, axis=0))
    else:
        prev = prev_ref[...]
    zs = zc + mu_ref[...] * (prev - zc)
    o1, o2, o3 = C_WIDTH, 2 * C_WIDTH, 3 * C_WIDTH
    r, k, v = zs[:, :o1], zs[:, o1:o2], zs[:, o2:o3]
    lora = zs[:, o3:o3 + LANES]
    gl = zs[:, o3 + LANES:]
    w = -jax.nn.softplus(-(w0_ref[...] + jnp.dot(jnp.tanh(lora), w2_ref[...], preferred_element_type=F32))) - 0.5
    decay = jnp.exp(-jnp.exp(w))
    a = _sigmoid(a0_ref[...] + jnp.dot(lora, a2_ref[...], preferred_element_type=F32))
    g = jnp.dot(_sigmoid(gl), g2_ref[...], preferred_element_type=F32)
    kk = k * kk_ref[...]
    kk = kk / jnp.maximum(jnp.sqrt(_seg_sum(kk * kk)), 1e-12)
    k_mod = k * (1.0 + (a - 1.0) * ka_ref[...])
    v_ref[...] = v
    g_ref[...] = g
    bonus_ref[...] = _seg_sum(r * k_mod * rk_ref[...]) * v
    if not sequence:
        r_ref[...] = r
        w_ref[...] = decay
        k_ref[...] = k_mod
        a_ref[...] = -kk
        b_ref[...] = kk * a
        return
    sub =lax.broadcasted_iota(jnp.int32, decay.shape, 0) % SCAN_SUB
    cp = decay
    shift = 1
    while shift < SCAN_SUB:
        cp = jnp.where(sub >= shift, cp * pltpu.roll(cp, shift, axis=0), cp)
        shift *= 2
    cpx = jnp.where(sub == 0, 1.0, pltpu.roll(cp, 1, axis=0))
    inv = 1.0 / cp
    r_ref[...] = r * cp
    k_ref[...] = k_mod * inv
    a_ref[...] = -kk * cpx
    b_ref[...] = kk * a * inv
    for n, scr in enumerate(scratch):
        scr[...] = cp[:, n * LANES:(n + 1) * LANES]
        w_ref[:, n * LANES:(n + 1) * LANES] = scr[pl.ds(SCAN_SUB - 1, tm // SCAN_SUB, stride=SCAN_SUB), :]


def _rwkv_prep(zc, prev, shift0, cw, sequence, tm):
    t = zc.shape[0]
    row = lambda w: pl.BlockSpec((tm, w), lambda i: (i, 0))
    if sequence:
        per = tm // SUBLANES
        prev_spec = pl.BlockSpec((SUBLANES, C_PAD), lambda i: (jnp.maximum(i * per - 1, 0), 0))
        prev = zc
    else:
        prev_spec = row(C_PAD)
    vec = _full((1, C_WIDTH))
    out_specs = [row(C_WIDTH)] * 8
    out_shape = [jax.ShapeDtypeStruct((t, C_WIDTH), F32)] * 8
    if sequence:
        out_specs[1] = pl.BlockSpec((tm // SCAN_SUB, C_WIDTH), lambda i: (i, 0))
        out_shape[1] = jax.ShapeDtypeStruct((t // SCAN_SUB, C_WIDTH), F32)
    return pl.pallas_call(
        functools.partial(_rwkv_prep_kernel, sequence=sequence, tm=tm), grid=(t // tm,), name="rwkv_prep",
        in_specs=[row(C_PAD), prev_spec, _full((1, C_PAD)), _full((1, C_PAD)), vec, _full((LANES, C_WIDTH)),
                  vec, _full((LANES, C_WIDTH)), _full((2 * LANES, C_WIDTH)), vec, vec, vec],
        out_specs=out_specs, out_shape=out_shape,
        scratch_shapes=[pltpu.VMEM((tm, LANES), F32)] * (C_WIDTH // LANES) if sequence else [],
        compiler_params=_cp(("parallel",)))(zc, prev, shift0, cw["mu"], cw["w0"], cw["w2"], cw["a0"], cw["a2"],
                                           cw["g2"], cw["k_k"], cw["k_a"], cw["r_k"])


SCAN_BLOCK = 256
SCAN_SUB = 16
SCAN_NSUB = SCAN_BLOCK // SCAN_SUB
SCAN_UNROLL = 16
SCAN_OPS = 3
SCAN_GROUP = 4


def _scan_select():
    sel = np.zeros((SCAN_SUB // SCAN_GROUP, LANES, SCAN_GROUP * LANES), np.float32)
    for p in range(SCAN_SUB // SCAN_GROUP):
        for piece in range(3):
            for rh in range(2):
                for tl in range(SCAN_GROUP):
                    row = piece * 2 * SCAN_SUB + rh * SCAN_SUB + SCAN_GROUP * p + tl
                    sel[p, row, tl * LANES + rh * 64: tl * LANES + (rh + 1) * 64] = 1.0
    return jnp.asarray(sel, BF16)


def _stage_lhs(ops, base, lhs_scr):
    for n, ref in enumerate(ops):
        xs = ref[pl.ds(base, SCAN_SUB), :]
        hi = xs.astype(BF16).astype(F32)
        r1 = xs - hi
        mid = r1.astype(BF16).astype(F32)
        lo = (r1 - mid).astype(BF16).astype(F32)
        g = jnp.concatenate([hi[:, :LANES], hi[:, LANES:], mid[:, :LANES], mid[:, LANES:],
                             lo[:, :LANES], lo[:, LANES:], jnp.zeros((LANES - 6 * SCAN_SUB, LANES), F32)], axis=0)
        lhs_scr[n * LANES:(n + 1) * LANES, :] = g.T.astype(BF16)


def _stage_tiles(lhs_scr, sel_ref, xb_scr, p):
    out = jnp.dot(lhs_scr[...], sel_ref[p], preferred_element_type=F32)
    for n in range(SCAN_OPS):
        for tl in range(SCAN_GROUP):
            xb_scr[n, SCAN_GROUP * p + tl] = out[n * LANES:(n + 1) * LANES, tl * LANES:(tl + 1) * LANES]


def _rwkv_scan_kernel(r_ref, pc_ref, k_ref, v_ref, a_ref, b_ref, g_ref, bonus_ref, lnw_ref, lnb_ref, sel_ref,
                      o_ref, so_ref, st_scr, lhs_a, lhs_b, xb_a, xb_b, vt_scr, vb_scr, yt_scr, yb_scr):
    @pl.when(pl.program_id(0) == 0)
    def _():
        st_scr[...] = jnp.zeros_like(st_scr)

    vtop, vbot = _rowpair(v_ref[...])
    vt_scr[...] = vtop
    vb_scr[...] = vbot
    ops = (a_ref, b_ref, k_ref)
    pairs = SCAN_UNROLL // SCAN_GROUP
    _stage_lhs(ops, 0, lhs_a)
    for p in range(SCAN_SUB // SCAN_GROUP):
        _stage_tiles(lhs_a, sel_ref, xb_a, p)
    _stage_lhs(ops, SCAN_SUB, lhs_b)

    def half(sc, st, lhs_cur, xb_cur, lhs_nxt, xb_nxt):
        base = pl.multiple_of(sc * SCAN_SUB, SCAN_SUB)

        def quad(q, st):
            for pp in range(pairs):
                _stage_tiles(lhs_nxt, sel_ref, xb_nxt, q * pairs + pp)
            for uu in range(SCAN_UNROLL):
                u = q * SCAN_UNROLL + uu
                t = base + u
                sa_t, sa_b = _halfsums(st * xb_cur[0, u])
                st = (st + _rowbcast(sa_t, sa_b) * xb_cur[1, u]
                      + _rowbcast(vt_scr[pl.ds(t, 1), :], vb_scr[pl.ds(t, 1), :]) * xb_cur[2, u])
                y_t, y_b = _halfsums(st * _colbcast(r_ref[pl.ds(t, 1), :]))
                yt_scr[pl.ds(t, 1), :] = y_t
                yb_scr[pl.ds(t, 1), :] = y_b
            return st

        st = lax.fori_loop(0, SCAN_SUB // SCAN_UNROLL, quad, st) * _colbcast(pc_ref[pl.ds(sc, 1), :])
        nxt2 = pl.multiple_of(jnp.minimum(sc + 2, SCAN_NSUB - 1) * SCAN_SUB, SCAN_SUB)
        _stage_lhs(ops, nxt2, lhs_cur)
        return st

    def sub2(s2, st):
        st = half(2 * s2, st, lhs_a, xb_a, lhs_b, xb_b)
        return half(2 * s2 + 1, st, lhs_b, xb_b, lhs_a, xb_a)

    st = lax.fori_loop(0, SCAN_NSUB // 2, sub2, st_scr[...])
    st_scr[...] = st
    y = _unpair(yt_scr[...], yb_scr[...])
    o_ref[...] = _wkv_post(y, lnw_ref[...], lnb_ref[...], bonus_ref[...], g_ref[...])
    heads = _t_to_heads(st, transpose=True)
    for h in range(C_HEADS):
        so_ref[h] = heads[h]


def _rwkv_scan(prep, lnw, lnb):
    r, pc, k, v, a, b, g, bonus = prep
    l = r.shape[0]
    blk = pl.BlockSpec((SCAN_BLOCK, C_WIDTH), lambda i: (i, 0))
    vec = _full((1, C_WIDTH))
    sel = _scan_select()
    return pl.pallas_call(
        _rwkv_scan_kernel, grid=(l // SCAN_BLOCK,), name="rwkv_scan",
        in_specs=[blk, pl.BlockSpec((SCAN_NSUB, C_WIDTH), lambda i: (i, 0))] + [blk] * 6 + [vec, vec, _full(sel.shape)],
        out_specs=[blk, _full((C_HEADS, C_HEAD, C_HEAD))],
        out_shape=[jax.ShapeDtypeStruct((l, C_WIDTH), F32), jax.ShapeDtypeStruct((C_HEADS, C_HEAD, C_HEAD), F32)],
        scratch_shapes=[pltpu.VMEM((LANES, LANES), F32),
                        pltpu.VMEM((SCAN_OPS * LANES, LANES), BF16), pltpu.VMEM((SCAN_OPS * LANES, LANES), BF16),
                        pltpu.VMEM((SCAN_OPS, SCAN_SUB, LANES, LANES), F32),
                        pltpu.VMEM((SCAN_OPS, SCAN_SUB, LANES, LANES), F32),
                        pltpu.VMEM((SCAN_BLOCK, LANES), F32), pltpu.VMEM((SCAN_BLOCK, LANES), F32),
                        pltpu.VMEM((SCAN_BLOCK, LANES), F32), pltpu.VMEM((SCAN_BLOCK, LANES), F32)],
        compiler_params=_cp(("arbitrary",)))(r, pc, k, v, a, b, g, bonus, lnw, lnb, sel)


def _sample_kernel(sinks_ref, gam_ref, za_ref, zr_ref, r_ref, w_ref, k_ref, v_ref, a_ref, b_ref, g_ref, bonus_ref,
                   kc_ref, vc_ref, sret_ref, swkv_ref, lnw_ref, lnb_ref,
                   oa_ref, or_ref, oc_ref, kco_ref, vco_ref, sreto_ref, swkvo_ref):
    lane =lax.broadcasted_iota(jnp.int32, (1, LANES), 1)
    row8 = lax.broadcasted_iota(jnp.int32, (SUBLANES, LANES), 0)

    za = za_ref[0]
    q = za[:, :A_WIDTH]
    knew = za[:, A_WIDTH:A_WIDTH + A_KV_WIDTH]
    vnew = za[:, A_WIDTH + A_KV_WIDTH:]
    kc, vc = kc_ref[0], vc_ref[0]
    g = A_HEADS // A_KV_HEADS
    zero = jnp.zeros((1, HEAD_DIM), F32)
    qrows = []
    for h in range(A_HEADS):
        qh = q[:, h * HEAD_DIM:(h + 1) * HEAD_DIM]
        qrows.append(jnp.concatenate([qh, zero] if h // g == 0 else [zero, qh], axis=1))
    qm = jnp.concatenate(qrows + [jnp.zeros((LANES - A_HEADS, LANES), F32)], axis=0)
    scale = HEAD_DIM ** -0.5
    s = lax.dot_general(kc, qm, (((1,), (1,)), ((), ())), preferred_element_type=F32) * scale
    kn8 = jnp.where(row8 == 0, jnp.broadcast_to(knew, (SUBLANES, LANES)), 0.0)
    s_new = lax.dot_general(kn8, qm, (((1,), (1,)), ((), ())), preferred_element_type=F32)[0:1] * scale
    sink = jnp.zeros((1, LANES), F32)
    for h in range(A_HEADS):
        sink = jnp.where(lane == h, sinks_ref[h], sink)
    m = jnp.maximum(jnp.maximum(jnp.max(s, axis=0, keepdims=True), s_new), sink)
    p = jnp.exp(s - m)
    p_new = jnp.exp(s_new - m)
    denom = jnp.sum(p, axis=0, keepdims=True) + p_new + jnp.exp(sink - m)
    p = p / denom
    p_new = p_new / denom
    pn_col = jnp.broadcast_to(p_new, (LANES, LANES)).T[:, 0:1]
    o_full = jnp.dot(p.T, vc, preferred_element_type=F32) + pn_col * vnew
    oa_ref[0] = jnp.concatenate(
        [o_full[h:h + 1, (h // g) * HEAD_DIM:(h // g + 1) * HEAD_DIM] for h in range(A_HEADS)], axis=1)
    rowk = lax.broadcasted_iota(jnp.int32, kc.shape, 0)
    last = rowk == kc.shape[0] - 1
    kco_ref[0] = jnp.where(last, jnp.broadcast_to(knew, kc.shape), pltpu.roll(kc, kc.shape[0] - 1, axis=0))
    vco_ref[0] = jnp.where(last, jnp.broadcast_to(vnew, vc.shape), pltpu.roll(vc, vc.shape[0] - 1, axis=0))

    zr = zr_ref[0]
    qk = R_HEADS * R_DK
    rq = zr[:, :qk]
    rk = zr[:, qk:2 * qk] * (R_DK ** -0.5)
    rv = zr[:, 2 * qk:2 * qk + R_WIDTH]
    rg = zr[:, 2 * qk + R_WIDTH:]
    st = _heads_to_t(sret_ref[0], transpose=False)
    rr = lax.broadcasted_iota(jnp.int32, (LANES, LANES), 0) >= 64
    cc = lax.broadcasted_iota(jnp.int32, (LANES, LANES), 1) >= 64
    gamma = jnp.where(rr, jnp.where(cc, gam_ref[3], gam_ref[1]), jnp.where(cc, gam_ref[2], gam_ref[0]))
    vt, vb = _rowpair(rv)
    st = st * gamma + _colbcast(rk) * _rowbcast(vt, vb)
    o_t, o_b = _halfsums(st * _colbcast(rq))
    o = _unpair(o_t, o_b)
    o = o * lax.rsqrt(_seg_sum(o * o) * (1.0 / R_DV) + EPS)
    or_ref[0] = o * _silu(rg)
    heads = _t_to_heads(st, transpose=False)
    for h in range(R_HEADS):
        sreto_ref[0, h] = heads[h]

    st = _heads_to_t(swkv_ref[0], transpose=True)
    vt, vb = _rowpair(v_ref[0])
    st, y_t, y_b = _wkv_step(st, _colbcast(a_ref[0]), _colbcast(w_ref[0]), _colbcast(b_ref[0]),
                             _colbcast(k_ref[0]), _colbcast(r_ref[0]), vt, vb)
    oc_ref[0] = _wkv_post(_unpair(y_t, y_b), lnw_ref[...], lnb_ref[...], bonus_ref[0], g_ref[0])
    heads = _t_to_heads(st, transpose=True)
    for h in range(C_HEADS):
        swkvo_ref[0, h] = heads[h]


def _sample_mixers(za, zr, prep, kc, vc, sret, swkv, sinks, lnw, lnb):
    b = za.shape[0]
    wb = kc.shape[1]
    assert wb <= WINDOW and PAST_LEN >= wb
    gam = jnp.exp(jnp.log1p(-jnp.exp2(-5.0 - jnp.arange(R_HEADS, dtype=F32))) * 1.0)
    tok = lambda w: pl.BlockSpec((1, 1, w), lambda i: (i, 0, 0))
    cache = pl.BlockSpec((1, wb, A_KV_WIDTH), lambda i: (i, 0, 0))
    state = pl.BlockSpec((1, 4, 64, 64), lambda i: (i, 0, 0, 0))
    smem = pl.BlockSpec(memory_space=pltpu.SMEM)
    vec = _full((1, C_WIDTH))
    r3 = lambda x: x.reshape(b, 1, x.shape[-1])
    outs = pl.pallas_call(
        _sample_kernel, grid=(b,), name="sample_mixers",
        in_specs=[smem, smem, tok(A_COLS), tok(R_COLS)] + [tok(C_WIDTH)] * 8 + [cache, cache, state, state,
                  vec, vec],
        out_specs=[tok(A_WIDTH), tok(R_WIDTH), tok(C_WIDTH), cache, cache, state, state],
        out_shape=[jax.ShapeDtypeStruct((b, 1, A_WIDTH), F32), jax.ShapeDtypeStruct((b, 1, R_WIDTH), F32),
                   jax.ShapeDtypeStruct((b, 1, C_WIDTH), F32), jax.ShapeDtypeStruct(kc.shape, F32),
                   jax.ShapeDtypeStruct(vc.shape, F32), jax.ShapeDtypeStruct(sret.shape, F32),
                   jax.ShapeDtypeStruct(swkv.shape, F32)],
        compiler_params=_cp(("parallel",)))(sinks, gam, r3(za), r3(zr), *[r3(x) for x in prep],
                                           kc, vc, sret, swkv, lnw, lnb)
    oa, orr, oc = (x.reshape(b, x.shape[-1]) for x in outs[:3])
    return (oa, orr, oc) + tuple(outs[3:])


def _outproj_kernel(oa_ref, or_ref, oc_ref, x_ref, wo_ref, gf_ref, wrt_ref, brt_ref, *rest, tm):
    x1_ref, h2_ref, route_ref = rest[-3:]
    mix =jnp.concatenate([oa_ref[...], or_ref[...], oc_ref[...]], axis=1).astype(BF16)
    x1 = x_ref[...] + jnp.dot(mix, wo_ref[...], preferred_element_type=F32)
    x1_ref[...] = x1
    h2 = _rmsnorm(x1, gf_ref[...])
    for s in range(ROW_TILES):
        h2_ref[pl.ds(s, tm, stride=ROW_TILES), :] = h2[:, s * LANES:(s + 1) * LANES]
    h_hi = h2.astype(BF16)
    h_lo = (h2 - h_hi.astype(F32)).astype(BF16)
    w_hi, w_lo = wrt_ref[0], wrt_ref[1]
    logits = (jnp.dot(h_hi, w_hi, preferred_element_type=F32) + jnp.dot(h_lo, w_hi, preferred_element_type=F32)
              + jnp.dot(h_hi, w_lo, preferred_element_type=F32)) + brt_ref[...]
    lane = lax.broadcasted_iota(jnp.int32, logits.shape, 1).astype(F32)
    big = float(LANES)
    is_g = lane < N_GROUPS
    lg = jnp.where(is_g, logits, -jnp.inf)
    m_g = jnp.max(lg, -1, keepdims=True)
    grp = jnp.min(jnp.where(lg == m_g, lane, big), -1, keepdims=True)
    p_grp = 1.0 / jnp.sum(jnp.where(is_g, jnp.exp(lg - m_g), 0.0), -1, keepdims=True)
    lo = N_GROUPS + EXPERTS_PER_GROUP * grp
    in_grp = (lane >= lo) & (lane < lo + EXPERTS_PER_GROUP)
    le = jnp.where(in_grp, logits, -jnp.inf)
    l1 = jnp.max(le, -1, keepdims=True)
    i1 = jnp.min(jnp.where(le == l1, lane, big), -1, keepdims=True)
    le2 = jnp.where(lane == i1, -jnp.inf, le)
    l2 = jnp.max(le2, -1, keepdims=True)
    i2 = jnp.min(jnp.where(le2 == l2, lane, big), -1, keepdims=True)
    e = jnp.exp(l2 - l1)
    g1 = p_grp / (1.0 + e)
    g2 = p_grp * e / (1.0 + e)
    route = jnp.where(lane == 0, i1 - N_GROUPS, jnp.where(lane == 1, i2 - N_GROUPS,
                      jnp.where(lane == 2, g1, jnp.where(lane == 3, g2, 0.0))))
    route_ref[...] = route


def _outproj(oa, orr, oc, x, wo, gf, wrt, brt, tm, t_all, first_tok, shared=None):
    t = x.shape[0]
    off = first_tok // tm
    nblk = t // tm
    steps = nblk + (1 if shared is None else 0)
    assert shared is not None or t_all == t + tm
    row = lambda w: pl.BlockSpec((tm, w), lambda i: (jnp.minimum(i, nblk - 1), 0))
    in_specs = [row(A_WIDTH), row(R_WIDTH), row(C_WIDTH), row(D_MODEL), _full(wo.shape), _full((1, D_MODEL)),
                _full(wrt.shape), _full((1, LANES))]
    args = [oa, orr, oc, x, wo, gf, wrt, brt]
    aliases = {}
    if shared is not None:
        aliases = {len(args): 1, len(args) + 1: 2}
        in_specs += [pl.BlockSpec(memory_space=pl.ANY)] * 2
        args += list(shared)
    return pl.pallas_call(
        functools.partial(_outproj_kernel, tm=tm), grid=(steps,), name="outproj_router",
        in_specs=in_specs,
        out_specs=[row(D_MODEL), pl.BlockSpec((tm * ROW_TILES, LANES), lambda i: (i + off, 0)),
                   pl.BlockSpec((tm, LANES), lambda i: (i + off, 0))],
        out_shape=[jax.ShapeDtypeStruct((t, D_MODEL), F32), jax.ShapeDtypeStruct((t_all * ROW_TILES, LANES), F32),
                   jax.ShapeDtypeStruct((t_all, LANES), F32)],
        input_output_aliases=aliases,
        compiler_params=_cp(("arbitrary",)))(*args)


def _moe_kernel(blk_exp_ref, src_ref, dst_ref, grp_ref, nused_ref, h2_hbm, wg_ref, wu_ref, wd_ref, yw_hbm,
                gbuf, sbuf, xbuf, wgb, wub, wdb, gsem, ssem):
    i = pl.program_id(0)
    nb = pl.num_programs(0)
    nused = nused_ref[0]
    slot = i % 2

    def tile(off):
        return pl.ds(pl.multiple_of(off, ROW_TILES), ROW_TILES)

    def rows(blk, fn):
        def body(g, _):
            for j in range(DMA_UNROLL):
                fn(g * DMA_UNROLL + j)
            return 0
        lax.fori_loop(0, grp_ref[blk], body, 0)

    def gather(blk, sl):
        rows(blk, lambda r: pltpu.make_async_copy(h2_hbm.at[tile(src_ref[blk * MOE_ROWS + r]), :],
                                                  gbuf.at[sl, tile(r * ROW_TILES), :], gsem.at[sl]).start())

    def gather_wait(blk, sl):
        rows(blk, lambda r: pltpu.make_async_copy(h2_hbm.at[tile(0), :], gbuf.at[sl, tile(0), :],
                                                  gsem.at[sl]).wait())

    def scatter(blk):
        rows(blk, lambda r: pltpu.make_async_copy(sbuf.at[tile(r * ROW_TILES), :],
                                                  yw_hbm.at[tile(dst_ref[blk * MOE_ROWS + r]), :], ssem).start())

    def scatter_wait(blk):
        rows(blk, lambda r: pltpu.make_async_copy(sbuf.at[tile(0), :], yw_hbm.at[tile(0), :], ssem).wait())

    @pl.when((i == 0) & (nused > 0))
    def _():
        gbuf[...] = jnp.zeros_like(gbuf)
        gather(0, 0)

    @pl.when(i < nused)
    def _():
        @pl.when((i == 0) | (blk_exp_ref[i] != blk_exp_ref[jnp.maximum(i - 1, 0)]))
        def _():
            wgb[...] = wg_ref[0].astype(BF16)
            wub[...] = wu_ref[0].astype(BF16)
            wdb[...] = wd_ref[0].astype(BF16)

        gather_wait(i, slot)

        @pl.when(i + 1 < nused)
        def _():
            gather(i + 1, 1 - slot)

        for s in range(ROW_TILES):
            xbuf[:, s * LANES:(s + 1) * LANES] = gbuf[slot, pl.ds(s, MOE_ROWS, stride=ROW_TILES), :].astype(BF16)
        x = xbuf[...]
        hid = _silu(jnp.dot(x, wgb[...], preferred_element_type=F32)) * jnp.dot(x, wub[...], preferred_element_type=F32)
        y = jnp.dot(hid.astype(BF16), wdb[...], preferred_element_type=F32)

        @pl.when(i > 0)
        def _():
            scatter_wait(i - 1)

        for s in range(ROW_TILES):
            sbuf[pl.ds(s, MOE_ROWS, stride=ROW_TILES), :] = y[:, s * LANES:(s + 1) * LANES]
        scatter(i)

    @pl.when((i == nb - 1) & (nused > 0))
    def _():
        scatter_wait(nused - 1)
        spare = pltpu.make_async_copy(sbuf, yw_hbm.at[pl.ds(yw_hbm.shape[0] - MOE_ROWS * ROW_TILES,
                                                            MOE_ROWS * ROW_TILES), :], ssem)
        spare.start()
        spare.wait()


def _moe(h2, route, t, wg, wu, wd):
    na = t * TOP_K
    assert na < (1 << IDX_BITS)
    flat_e = route[:t, :TOP_K].astype(jnp.int32).reshape(na)
    experts = jnp.arange(N_EXPERTS, dtype=jnp.int32)
    counts = jnp.sum(flat_e[:, None] == experts[None, :], axis=0, dtype=jnp.int32)
    npad = (-counts) % MOE_ROWS
    real_keys = (flat_e << (IDX_BITS + 1)) | jnp.arange(na, dtype=jnp.int32)
    q = jnp.arange(MOE_ROWS, dtype=jnp.int32)[None, :]
    pad_keys = jnp.where(q < npad[:, None], (experts[:, None] << (IDX_BITS + 1)) | (1 << IDX_BITS) | q,
                         jnp.iinfo(jnp.int32).max)
    n_blocks = -(-na // MOE_ROWS) + N_EXPERTS
    n_rows = n_blocks * MOE_ROWS
    filler = jnp.full((n_rows - na - N_EXPERTS * MOE_ROWS,), jnp.iinfo(jnp.int32).max, jnp.int32)
    keys = jnp.sort(jnp.concatenate([real_keys, pad_keys.reshape(-1), filler]))
    is_pad = ((keys >> IDX_BITS) & 1) == 1
    idx = keys & ((1 << IDX_BITS) - 1)
    pos = jnp.arange(n_rows, dtype=jnp.int32)
    src = jnp.where(is_pad, 0, (idx // TOP_K) * ROW_TILES).astype(jnp.int32)
    dst = jnp.where(is_pad, na + pos % MOE_ROWS, idx).astype(jnp.int32) * ROW_TILES
    blk_exp = jnp.minimum(keys[::MOE_ROWS] >> (IDX_BITS + 1), N_EXPERTS - 1).astype(jnp.int32)
    nused = (jnp.sum(counts + npad) // MOE_ROWS).astype(jnp.int32).reshape(1)
    real_rows = jnp.sum((~is_pad).reshape(n_blocks, MOE_ROWS), axis=1, dtype=jnp.int32)
    grp = (real_rows + DMA_UNROLL - 1) // DMA_UNROLL
    wspec = lambda a, b: pl.BlockSpec((1, a, b), lambda i, be, s, d, gr, nu: (be[i], 0, 0))
    return pl.pallas_call(
        _moe_kernel, name="moe_experts",
        grid_spec=pltpu.PrefetchScalarGridSpec(
            num_scalar_prefetch=5, grid=(n_blocks,),
            in_specs=[pl.BlockSpec(memory_space=pl.ANY), wspec(D_MODEL, EXPERT_FF), wspec(D_MODEL, EXPERT_FF),
                      wspec(EXPERT_FF, D_MODEL)],
            out_specs=pl.BlockSpec(memory_space=pl.ANY),
            scratch_shapes=[pltpu.VMEM((2, MOE_ROWS * ROW_TILES, LANES), F32),
                            pltpu.VMEM((MOE_ROWS * ROW_TILES, LANES), F32),
                            pltpu.VMEM((MOE_ROWS, D_MODEL), BF16),
                            pltpu.VMEM((D_MODEL, EXPERT_FF), BF16), pltpu.VMEM((D_MODEL, EXPERT_FF), BF16),
                            pltpu.VMEM((EXPERT_FF, D_MODEL), BF16),
                            pltpu.SemaphoreType.DMA((2,)), pltpu.SemaphoreType.DMA(())]),
        out_shape=jax.ShapeDtypeStruct(((na + MOE_ROWS) * ROW_TILES, LANES), F32),
        compiler_params=_cp(("arbitrary",)))(blk_exp, src, dst, grp, nused, h2, wg, wu, wd)


def _rope_tables(pos):
    half = HEAD_DIM // 2
    inv = ROPE_THETA ** (-jnp.arange(half, dtype=F32) / half)
    ang = pos.astype(F32)[:, None] * inv[None, :]
    c, s = jnp.cos(ang), jnp.sin(ang)
    return jnp.concatenate([c, c, c, c], axis=1), jnp.concatenate([-s, s, -s, s], axis=1)


def _pad_cols(w, n):
    return jnp.pad(w, ((0, 0), (0, n - w.shape[1])))


def _pad_rows(w, lo, n):
    return jnp.pad(w, ((lo, n - lo - w.shape[0]), (0, 0)))


def kernel(x_prompt, x_sample, cache_swa_k, cache_swa_v, state_ret, state_wkv, state_shift, norm_mix, w_in, sinks,
           rwkv_mu, rwkv_w0, rwkv_w2, rwkv_a0, rwkv_a2, rwkv_g2, rwkv_k_k, rwkv_k_a, rwkv_r_k, rwkv_ln_w, rwkv_ln_b,
           w_out, norm_ffn, router_g, router_g_b, router_e, router_e_b, expert_w_gate, expert_w_up, expert_w_down,
           norm_final):
    lp = x_prompt.shape[1]
    bs = x_sample.shape[0]
    wb = cache_swa_k.shape[2]
    tm_p = 512
    xp = x_prompt.reshape(lp, D_MODEL)
    xs = x_sample.reshape(bs, D_MODEL)
    cos_p, sin_p = _rope_tables(jnp.arange(lp, dtype=jnp.int32))
    cos_s, sin_s = (jnp.broadcast_to(t, (bs, LANES))
                    for t in _rope_tables(PAST_LEN + jnp.arange(x_sample.shape[1], dtype=jnp.int32)))
    outs_p = [[] for _ in range(5)]
    outs_s = [[] for _ in range(5)]
    moe_p = moe_s = None
    row = lambda v: v.reshape(1, -1)
    for i in range(DEPTH):
        wi = w_in[i].astype(BF16)
        wa, wr, wc = wi[:, :A_COLS], wi[:, A_COLS:A_COLS + R_COLS], _pad_cols(wi[:, A_COLS + R_COLS:], C_PAD)
        cw = {
            "mu": _pad_cols(row(rwkv_mu[i]), C_PAD), "w0": row(rwkv_w0[i]), "a0": row(rwkv_a0[i]),
            "w2": _pad_rows(rwkv_w2[i], 0, LANES), "a2": _pad_rows(rwkv_a2[i], C_DECAY_LORA, LANES),
            "g2": _pad_rows(rwkv_g2[i], 0, 2 * LANES),
            "k_k": row(rwkv_k_k[i]), "k_a": row(rwkv_k_a[i]), "r_k": row(rwkv_r_k[i]),
        }
        lnw, lnb = row(rwkv_ln_w[i]), row(rwkv_ln_b[i])
        g_mix, g_ffn = row(norm_mix[i]), row(norm_ffn[i])
        wo = w_out[i].astype(BF16)
        wrt = _pad_cols(jnp.concatenate([router_g[i], router_e[i]], axis=1), LANES)
        wrt_hi = wrt.astype(BF16)
        wrt = jnp.stack([wrt_hi, (wrt - wrt_hi.astype(F32)).astype(BF16)])
        brt = _pad_cols(row(jnp.concatenate([router_g_b[i], router_e_b[i]])), LANES)

        xn, za, zr, zc = _inproj(xp, moe_p, g_mix, wa, wr, wc, cos_p, sin_p, tm_p)
        xp = xp if xn is None else xn
        oa = _swa_prompt(za, sinks[i])
        orr, s_ret = _ret_prompt(zr)
        prep = _rwkv_prep(zc, None, jnp.zeros((1, C_PAD), F32), cw, True, tm_p)
        oc, s_wkv = _rwkv_scan(prep, lnw, lnb)
        xp, h2, route = _outproj(oa, orr, oc, xp, wo, g_ffn, wrt, brt, tm_p, lp + tm_p, 0)
        nk = min(WINDOW, lp)
        outs_p[0].append(za[lp - nk:, A_WIDTH:A_WIDTH + A_KV_WIDTH].reshape(1, nk, A_KV_HEADS, HEAD_DIM))
        outs_p[1].append(za[lp - nk:, A_WIDTH + A_KV_WIDTH:].reshape(1, nk, A_KV_HEADS, HEAD_DIM))
        outs_p[2].append(s_ret[None])
        outs_p[3].append(s_wkv[None])
        outs_p[4].append(zc[lp - 1:, :C_COLS])

        xn, za, zr, zc = _inproj(xs, moe_s, g_mix, wa, wr, wc, cos_s, sin_s, bs)
        xs = xs if xn is None else xn
        prep = _rwkv_prep(zc, _pad_cols(state_shift[i], C_PAD), jnp.zeros((1, C_PAD), F32), cw, False, bs)
        oa, orr, oc, kc_new, vc_new, sret_new, swkv_new = _sample_mixers(
            za, zr, prep, cache_swa_k[i].reshape(bs, wb, A_KV_WIDTH), cache_swa_v[i].reshape(bs, wb, A_KV_WIDTH),
            state_ret[i], state_wkv[i], sinks[i], lnw, lnb)
        xs, h2, route = _outproj(oa, orr, oc, xs, wo, g_ffn, wrt, brt, bs, lp + tm_p, lp, shared=(h2, route))
        yw = _moe(h2, route, lp + bs, expert_w_gate[i], expert_w_up[i], expert_w_down[i])
        moe_p, moe_s = (yw, route, 0), (yw, route, lp)
        outs_s[0].append(kc_new.reshape(bs, wb, A_KV_HEADS, HEAD_DIM))
        outs_s[1].append(vc_new.reshape(bs, wb, A_KV_HEADS, HEAD_DIM))
        outs_s[2].append(sret_new)
        outs_s[3].append(swkv_new)
        outs_s[4].append(zc[:, :C_COLS])

    gfin = row(norm_final)
    y_prompt = _final(xp, moe_p, gfin, tm_p).reshape(x_prompt.shape)
    y_sample = _final(xs, moe_s, gfin, bs).reshape(x_sample.shape)
    sp = [jnp.stack(o) for o in outs_p]
    ss = [jnp.stack(o) for o in outs_s]
    return (y_prompt, y_sample, sp[0], sp[1], sp[2], sp[3], sp[4], ss[0], ss[1], ss[2], ss[3], ss[4])
```

```python
import functools

import numpy as np
import jax
import jax.numpy as jnp
from jax import lax
from jax.experimental import pallas as pl
from jax.experimental.pallas import tpu as pltpu

F32 = jnp.float32
BF16 = jnp.bfloat16

D_MODEL = 1024
DEPTH = 2
PAST_LEN = 16384
A_HEADS, A_KV_HEADS, HEAD_DIM, WINDOW = 8, 2, 64, 128
ROPE_THETA = 10000.0
A_WIDTH = A_HEADS * HEAD_DIM
A_KV_WIDTH = A_KV_HEADS * HEAD_DIM
A_COLS = A_WIDTH + 2 * A_KV_WIDTH
R_HEADS, R_DK, R_DV, R_CHUNK = 4, 64, 64, 128
R_WIDTH = R_HEADS * R_DV
R_COLS = 2 * R_HEADS * R_DK + 2 * R_WIDTH
C_HEADS, C_HEAD = 4, 64
C_WIDTH = C_HEADS * C_HEAD
C_DECAY_LORA, C_ICLR_LORA, C_GATE_LORA = 64, 64, 160
C_COLS = 3 * C_WIDTH + C_DECAY_LORA + C_ICLR_LORA + C_GATE_LORA
C_PAD = 1152
C_GN_EPS = 64e-5
N_GROUPS, EXPERTS_PER_GROUP, TOP_K, EXPERT_FF = 4, 8, 2, 512
N_EXPERTS = N_GROUPS * EXPERTS_PER_GROUP
EPS = 1e-6

LANES = 128
SUBLANES = 8
ROW_TILES = D_MODEL // LANES
MOE_ROWS = 256
DMA_UNROLL = 8
IDX_BITS = 19
VMEM_LIMIT = 56 * 1024 * 1024


def _cp(sem, vmem=VMEM_LIMIT):
    return pltpu.CompilerParams(dimension_semantics=sem, vmem_limit_bytes=vmem)


def _full(shape):
    n = len(shape)
    return pl.BlockSpec(shape, lambda *_: (0,) * n)


def _rmsnorm(x, g):
    return x * lax.rsqrt(jnp.mean(x * x, -1, keepdims=True) + EPS) * g


def _silu(x):
    return x * (1.0 / (1.0 + jnp.exp(-x)))


def _sigmoid(x):
    return 1.0 / (1.0 + jnp.exp(-x))


def _rope(x, cos, sin_signed):
    w = x.shape[-1]
    lane = lax.broadcasted_iota(jnp.int32, x.shape, x.ndim - 1)
    first = (lane % HEAD_DIM) < (HEAD_DIM // 2)
    swapped = jnp.where(first, pltpu.roll(x, w - HEAD_DIM // 2, axis=x.ndim - 1),
                        pltpu.roll(x, HEAD_DIM // 2, axis=x.ndim - 1))
    return x * cos + swapped * sin_signed


def _tile_lanes(x, n):
    return jnp.concatenate([x] * n, axis=-1) if n > 1 else x


def _seg_sum(x, seg=64):
    outs = []
    for h in range(x.shape[-1] // seg):
        s = jnp.sum(x[:, h * seg:(h + 1) * seg], axis=-1, keepdims=True)
        outs.append(jnp.broadcast_to(s, (x.shape[0], seg)))
    return jnp.concatenate(outs, axis=-1)


def _read_rows(ref, tm, first, stride):
    return jnp.concatenate([ref[pl.ds(first + s, tm, stride=stride), :] for s in range(ROW_TILES)], axis=1)


def _colbcast(x):
    z = jnp.concatenate([jnp.broadcast_to(x[:, :LANES], (64, LANES)),
                         jnp.broadcast_to(x[:, LANES:], (64, LANES))], axis=0)
    return z.T


def _rowpair(x):
    top = jnp.concatenate([x[:, 0:64], x[:, 128:192]], axis=1)
    bot = jnp.concatenate([x[:, 64:128], x[:, 192:256]], axis=1)
    return top, bot


def _unpair(top, bot):
    return jnp.concatenate([top[:, :64], bot[:, :64], top[:, 64:], bot[:, 64:]], axis=1)


def _rowbcast(top, bot):
    return jnp.concatenate([jnp.broadcast_to(top, (64, LANES)), jnp.broadcast_to(bot, (64, LANES))], axis=0)


def _halfsums(p):
    return jnp.sum(p[:64], axis=0, keepdims=True), jnp.sum(p[64:], axis=0, keepdims=True)


def _wkv_step(st, ab, wb, bb, kb, rb, vtop, vbot):
    sa_t, sa_b = _halfsums(st * ab)
    st = st * wb + _rowbcast(sa_t, sa_b) * bb + _rowbcast(vtop, vbot) * kb
    y_t, y_b = _halfsums(st * rb)
    return st, y_t, y_b


def _wkv_post(y, lnw, lnb, bonus, g):
    mu = _seg_sum(y) * (1.0 / C_HEAD)
    d = y - mu
    var = _seg_sum(d * d) * (1.0 / C_HEAD)
    yn = d * lax.rsqrt(var + C_GN_EPS)
    return (yn * lnw + lnb + bonus) * g


def _heads_to_t(s4, transpose):
    if transpose:
        m = jnp.concatenate([jnp.concatenate([s4[0], s4[1]], axis=1),
                             jnp.concatenate([s4[2], s4[3]], axis=1)], axis=0)
        return m.T
    return jnp.concatenate([jnp.concatenate([s4[0], s4[2]], axis=1),
                            jnp.concatenate([s4[1], s4[3]], axis=1)], axis=0)


def _t_to_heads(st, transpose):
    if transpose:
        m = st.T
        return [m[0:64, 0:64], m[0:64, 64:128], m[64:128, 0:64], m[64:128, 64:128]]
    return [st[0:64, 0:64], st[64:128, 0:64], st[0:64, 64:128], st[64:128, 64:128]]


def _inproj_kernel(*refs, combine, tm):
    if combine:
        x_ref, yw_ref, route_ref, g_ref, wa_ref, wr_ref, wc_ref, cos_ref, sin_ref, xo_ref, za_ref, zr_ref, zc_ref = refs
        route = route_ref[...]
        y0 = _read_rows(yw_ref, tm, 0, 2 * ROW_TILES)
        y1 = _read_rows(yw_ref, tm, ROW_TILES, 2 * ROW_TILES)
        x = x_ref[...] + (route[:, 2:3] * y0 + route[:, 3:4] * y1)
        xo_ref[...] = x
    else:
        x_ref, g_ref, wa_ref, wr_ref, wc_ref, cos_ref, sin_ref, za_ref, zr_ref, zc_ref = refs
        x = x_ref[...]
    h = _rmsnorm(x, g_ref[...]).astype(BF16)
    cos, sin = cos_ref[...], sin_ref[...]
    na = (A_WIDTH + A_KV_WIDTH) // LANES
    nr = 2 * R_HEADS * R_DK // LANES
    za = jnp.dot(h, wa_ref[...], preferred_element_type=F32)
    za_ref[:, :na * LANES] = _rope(za[:, :na * LANES], _tile_lanes(cos, na), _tile_lanes(sin, na))
    za_ref[:, na * LANES:] = za[:, na * LANES:]
    zr = jnp.dot(h, wr_ref[...], preferred_element_type=F32)
    zr_ref[:, :nr * LANES] = _rope(zr[:, :nr * LANES], _tile_lanes(cos, nr), _tile_lanes(sin, nr))
    zr_ref[:, nr * LANES:] = zr[:, nr * LANES:]
    zc_ref[...] = jnp.dot(h, wc_ref[...], preferred_element_type=F32)


def _inproj(x, moe, g, wa, wr, wc, cos, sin, tm):
    t = x.shape[0]
    row = lambda w: pl.BlockSpec((tm, w), lambda i: (i, 0))
    wspecs = [_full((1, D_MODEL)), _full(wa.shape), _full(wr.shape), _full(wc.shape), row(LANES), row(LANES)]
    zshapes = [jax.ShapeDtypeStruct((t, A_COLS), F32), jax.ShapeDtypeStruct((t, R_COLS), F32),
               jax.ShapeDtypeStruct((t, C_PAD), F32)]
    zspecs = [row(A_COLS), row(R_COLS), row(C_PAD)]
    if moe is None:
        return (None,) + tuple(pl.pallas_call(
            functools.partial(_inproj_kernel, combine=False, tm=tm), name="inproj",
            grid=(t // tm,), in_specs=[row(D_MODEL)] + wspecs, out_specs=zspecs, out_shape=zshapes,
            compiler_params=_cp(("parallel",)))(x, g, wa, wr, wc, cos, sin))
    yw, route, first_tok = moe
    off = first_tok // tm
    return pl.pallas_call(
        functools.partial(_inproj_kernel, combine=True, tm=tm), name="combine_inproj",
        grid=(t // tm,),
        in_specs=[row(D_MODEL), pl.BlockSpec((tm * 2 * ROW_TILES, LANES), lambda i: (i + off, 0)),
                  pl.BlockSpec((tm, LANES), lambda i: (i + off, 0))] + wspecs,
        out_specs=[row(D_MODEL)] + zspecs,
        out_shape=[jax.ShapeDtypeStruct((t, D_MODEL), F32)] + zshapes,
        compiler_params=_cp(("parallel",)))(x, yw, route, g, wa, wr, wc, cos, sin)


def _final_kernel(x_ref, yw_ref, route_ref, g_ref, o_ref, *, tm):
    route = route_ref[...]
    y0 = _read_rows(yw_ref, tm, 0, 2 * ROW_TILES)
    y1 = _read_rows(yw_ref, tm, ROW_TILES, 2 * ROW_TILES)
    x = x_ref[...] + (route[:, 2:3] * y0 + route[:, 3:4] * y1)
    o_ref[...] = _rmsnorm(x, g_ref[...])


def _final(x, moe, g, tm):
    t = x.shape[0]
    yw, route, first_tok = moe
    off = first_tok // tm
    row = lambda w: pl.BlockSpec((tm, w), lambda i: (i, 0))
    return pl.pallas_call(
        functools.partial(_final_kernel, tm=tm), grid=(t // tm,), name="combine_final",
        in_specs=[row(D_MODEL), pl.BlockSpec((tm * 2 * ROW_TILES, LANES), lambda i: (i + off, 0)),
                  pl.BlockSpec((tm, LANES), lambda i: (i + off, 0)), _full((1, D_MODEL))],
        out_specs=row(D_MODEL), out_shape=jax.ShapeDtypeStruct((t, D_MODEL), F32),
        compiler_params=_cp(("parallel",)))(x, yw, route, g)


def _swa_prompt_kernel(sinks_ref, q_ref, kc_ref, kp_ref, vc_ref, vp_ref, o_ref):
    i = pl.program_id(0)
    g = A_HEADS // A_KV_HEADS
    nq = g * WINDOW
    kcat = jnp.concatenate([kp_ref[...], kc_ref[...]], axis=0)
    vcat = jnp.concatenate([vp_ref[...], vc_ref[...]], axis=0)
    kswap = pltpu.roll(kcat, HEAD_DIM, axis=1)
    vswap = pltpu.roll(vcat, HEAD_DIM, axis=1)
    upper_k = lax.broadcasted_iota(jnp.int32, kcat.shape, 1) >= HEAD_DIM
    upper_q = lax.broadcasted_iota(jnp.int32, (WINDOW, LANES), 1) >= HEAD_DIM
    r = lax.broadcasted_iota(jnp.int32, (nq, 2 * WINDOW), 0) % WINDOW
    c = lax.broadcasted_iota(jnp.int32, (nq, 2 * WINDOW), 1)
    ok = (c >= r) & (c <= r + WINDOW) & ((c >= WINDOW) | (i > 0))
    sink_col4 = c == (r + WINDOW + 1) % (2 * WINDOW)
    sink_col = sink_col4[:WINDOW]
    slabs = []
    for kv in range(A_KV_HEADS):
        kboth = jnp.where(upper_k, kswap, kcat) if kv == 0 else jnp.where(upper_k, kcat, kswap)
        v_lo = jnp.where(upper_k, 0.0, vcat if kv == 0 else vswap)
        v_hi = jnp.where(upper_k, vswap if kv == 0 else vcat, 0.0)
        vpair = jnp.concatenate([v_lo, v_hi], axis=0)
        qs = []
        for j in range(g):
            h = kv * g + j
            slab = q_ref[:, (h // 2) * LANES:(h // 2 + 1) * LANES]
            qs.append(jnp.where(upper_q, slab, 0.0) if h % 2 else jnp.where(upper_q, 0.0, slab))
        q4 = jnp.concatenate(qs, axis=0)
        s = lax.dot_general(q4, kboth, (((1,), (1,)), ((), ())), preferred_element_type=F32) * (HEAD_DIM ** -0.5)
        sink = jnp.concatenate([jnp.where(sink_col, sinks_ref[kv * g + j], -jnp.inf) for j in range(g)], axis=0)
        s = jnp.where(ok, s, sink)
        m = jnp.max(s, -1, keepdims=True)
        p = jnp.exp(s - m)
        p = jnp.where(sink_col4, 0.0, p / jnp.sum(p, -1, keepdims=True))
        for j in range(0, g, 2):
            pp = jnp.concatenate([p[j * WINDOW:(j + 1) * WINDOW], p[(j + 1) * WINDOW:(j + 2) * WINDOW]], axis=1)
            slabs.append(jnp.dot(pp, vpair, preferred_element_type=F32))
    o_ref[...] = jnp.concatenate(slabs, axis=1)


def _swa_prompt(za, sinks):
    l = za.shape[0]
    nb = l // WINDOW
    cur = lambda w, c: pl.BlockSpec((WINDOW, w), lambda i: (i, c))
    prv = lambda w, c: pl.BlockSpec((WINDOW, w), lambda i: (jnp.maximum(i - 1, 0), c))
    kcol, vcol = A_WIDTH // A_KV_WIDTH, A_WIDTH // A_KV_WIDTH + 1
    return pl.pallas_call(
        _swa_prompt_kernel, grid=(nb,), name="swa_prompt",
        in_specs=[pl.BlockSpec(memory_space=pltpu.SMEM),
                  cur(A_WIDTH, 0), cur(A_KV_WIDTH, kcol), prv(A_KV_WIDTH, kcol),
                  cur(A_KV_WIDTH, vcol), prv(A_KV_WIDTH, vcol)],
        out_specs=cur(A_WIDTH, 0),
        out_shape=jax.ShapeDtypeStruct((l, A_WIDTH), F32),
        compiler_params=_cp(("parallel",)))(sinks, za, za, za, za, za)


def _ret_prompt_kernel(cdec_ref, zr_ref, intra_ref, qdec_ref, kdec_ref, o_ref, so_ref, s_scr):
    @pl.when(pl.program_id(0) == 0)
    def _():
        s_scr[...] = jnp.zeros_like(s_scr)

    z = zr_ref[...]
    qk = R_HEADS * R_DK
    q = z[:, :qk]
    k = z[:, qk:2 * qk] * (R_DK ** -0.5)
    v = z[:, 2 * qk:2 * qk + R_WIDTH]
    gate = z[:, 2 * qk + R_WIDTH:]
    qdec = qdec_ref[...]
    kd = k * kdec_ref[...]
    outs = []
    for h in range(R_HEADS):
        sl = slice(h * R_DK, (h + 1) * R_DK)
        qh, kh, vh = q[:, sl], k[:, sl], v[:, sl]
        att = lax.dot_general(qh, kh, (((1,), (1,)), ((), ())), preferred_element_type=F32) * intra_ref[h]
        s = s_scr[h]
        o = jnp.dot(att, vh, preferred_element_type=F32) + jnp.dot(qh, s, preferred_element_type=F32) * qdec[:, sl]
        s_scr[h] = s * cdec_ref[h] + jnp.dot(kd[:, sl].T, vh, preferred_element_type=F32)
        outs.append(o * lax.rsqrt(jnp.mean(o * o, -1, keepdims=True) + EPS))
    o_ref[...] = jnp.concatenate(outs, axis=1) * _silu(gate)
    so_ref[...] = s_scr[...]


def _ret_tables(c):
    log_g = jnp.log1p(-jnp.exp2(-5.0 - jnp.arange(R_HEADS, dtype=F32)))
    idx = jnp.arange(c, dtype=F32)
    diff = idx[:, None] - idx[None, :]
    intra = jnp.where(diff >= 0, jnp.exp(log_g[:, None, None] * jnp.maximum(diff, 0.0)), 0.0)
    q_dec = jnp.exp(log_g[None, :] * (idx[:, None] + 1.0))
    k_dec = jnp.exp(log_g[None, :] * (c - 1.0 - idx[:, None]))
    c_dec = jnp.exp(log_g * c)
    return intra, jnp.repeat(q_dec, R_DK, axis=1), jnp.repeat(k_dec, R_DK, axis=1), c_dec


def _ret_prompt(zr):
    l = zr.shape[0]
    c = R_CHUNK
    intra, qdec, kdec, cdec = _ret_tables(c)
    blk = lambda w: pl.BlockSpec((c, w), lambda i: (i, 0))
    return pl.pallas_call(
        _ret_prompt_kernel, grid=(l // c,), name="ret_prompt",
        in_specs=[pl.BlockSpec(memory_space=pltpu.SMEM), blk(R_COLS),
                  _full((R_HEADS, c, c)), _full((c, R_HEADS * R_DK)), _full((c, R_HEADS * R_DK))],
        out_specs=[blk(R_WIDTH), _full((R_HEADS, R_DK, R_DV))],
        out_shape=[jax.ShapeDtypeStruct((l, R_WIDTH), F32), jax.ShapeDtypeStruct((R_HEADS, R_DK, R_DV), F32)],
        scratch_shapes=[pltpu.VMEM((R_HEADS, R_DK, R_DV), F32)],
        compiler_params=_cp(("arbitrary",)))(cdec, zr, intra, qdec, kdec)


def _rwkv_prep_kernel(zc_ref, prev_ref, shift0_ref, mu_ref, w0_ref, w2_ref, a0_ref, a2_ref, g2_ref,
                      kk_ref, ka_ref, rk_ref,
                      r_ref, w_ref, k_ref, v_ref, a_ref, b_ref, g_ref, bonus_ref, *scratch, sequence, tm):
    zc = zc_ref[...]
    if sequence:
        row = lax.broadcasted_iota(jnp.int32, zc.shape, 0)
        boundary = jnp.where(pl.program_id(0) == 0, shift0_ref[...], prev_ref[SUBLANES - 1:SUBLANES, :])
        prev = jnp.where(row == 0, jnp.broadcast_to(boundary, zc.shape), pltpu.roll(zc, 1, axis=0))
    else:
        prev = prev_ref[...]
    zs = zc + mu_ref[...] * (prev - zc)
    o1, o2, o3 = C_WIDTH, 2 * C_WIDTH, 3 * C_WIDTH
    r, k, v = zs[:, :o1], zs[:, o1:o2], zs[:, o2:o3]
    lora = zs[:, o3:o3 + LANES]
    gl = zs[:, o3 + LANES:]
    w = -jax.nn.softplus(-(w0_ref[...] + jnp.dot(jnp.tanh(lora), w2_ref[...], preferred_element_type=F32))) - 0.5
    decay = jnp.exp(-jnp.exp(w))
    a = _sigmoid(a0_ref[...] + jnp.dot(lora, a2_ref[...], preferred_element_type=F32))
    g = jnp.dot(_sigmoid(gl), g2_ref[...], preferred_element_type=F32)
    kk = k * kk_ref[...]
    kk = kk / jnp.maximum(jnp.sqrt(_seg_sum(kk * kk)), 1e-12)
    k_mod = k * (1.0 + (a - 1.0) * ka_ref[...])
    v_ref[...] = v
    g_ref[...] = g
    bonus_ref[...] = _seg_sum(r * k_mod * rk_ref[...]) * v
    if not sequence:
        r_ref[...] = r
        w_ref[...] = decay
        k_ref[...] = k_mod
        a_ref[...] = -kk
        b_ref[...] = kk * a
        return
    sub =lax.broadcasted_iota(jnp.int32, decay.shape, 0) % SCAN_SUB
    cp = decay
    shift = 1
    while shift < SCAN_SUB:
        cp = jnp.where(sub >= shift, cp * pltpu.roll(cp, shift, axis=0), cp)
        shift *= 2
    cpx = jnp.where(sub == 0, 1.0, pltpu.roll(cp, 1, axis=0))
    inv = 1.0 / cp
    r_ref[...] = r * cp
    k_ref[...] = k_mod * inv
    a_ref[...] = -kk * cpx
    b_ref[...] = kk * a * inv
    for n, scr in enumerate(scratch):
        scr[...] = cp[:, n * LANES:(n + 1) * LANES]
        w_ref[:, n * LANES:(n + 1) * LANES] = scr[pl.ds(SCAN_SUB - 1, tm // SCAN_SUB, stride=SCAN_SUB), :]


def _rwkv_prep(zc, prev, shift0, cw, sequence, tm):
    t = zc.shape[0]
    row = lambda w: pl.BlockSpec((tm, w), lambda i: (i, 0))
    if sequence:
        per = tm // SUBLANES
        prev_spec = pl.BlockSpec((SUBLANES, C_PAD), lambda i: (jnp.maximum(i * per - 1, 0), 0))
        prev = zc
    else:
        prev_spec = row(C_PAD)
    vec = _full((1, C_WIDTH))
    out_specs = [row(C_WIDTH)] * 8
    out_shape = [jax.ShapeDtypeStruct((t, C_WIDTH), F32)] * 8
    if sequence:
        out_specs[1] = pl.BlockSpec((tm // SCAN_SUB, C_WIDTH), lambda i: (i, 0))
        out_shape[1] = jax.ShapeDtypeStruct((t // SCAN_SUB, C_WIDTH), F32)
    return pl.pallas_call(
        functools.partial(_rwkv_prep_kernel, sequence=sequence, tm=tm), grid=(t // tm,), name="rwkv_prep",
        in_specs=[row(C_PAD), prev_spec, _full((1, C_PAD)), _full((1, C_PAD)), vec, _full((LANES, C_WIDTH)),
                  vec, _full((LANES, C_WIDTH)), _full((2 * LANES, C_WIDTH)), vec, vec, vec],
        out_specs=out_specs, out_shape=out_shape,
        scratch_shapes=[pltpu.VMEM((tm, LANES), F32)] * (C_WIDTH // LANES) if sequence else [],
        compiler_params=_cp(("parallel",)))(zc, prev, shift0, cw["mu"], cw["w0"], cw["w2"], cw["a0"], cw["a2"],
                                           cw["g2"], cw["k_k"], cw["k_a"], cw["r_k"])


SCAN_BLOCK = 256
SCAN_SUB = 16
SCAN_NSUB = SCAN_BLOCK // SCAN_SUB
SCAN_UNROLL = 16
SCAN_OPS = 3
SCAN_GROUP = 4


def _scan_select(every):
    ntok = SCAN_GROUP // every
    sel = np.zeros((SCAN_SUB // SCAN_GROUP, LANES, ntok * LANES), np.float32)
    for p in range(SCAN_SUB // SCAN_GROUP):
        for piece in range(3):
            for rh in range(2):
                for tl in range(ntok):
                    row = piece * 2 * SCAN_SUB + rh * SCAN_SUB + SCAN_GROUP * p + every * tl
                    sel[p, row, tl * LANES + rh * 64: tl * LANES + (rh + 1) * 64] = 1.0
    return jnp.asarray(sel, BF16)


def _stage_lhs(ops, base, lhs_scr):
    for n, ref in enumerate(ops):
        xs = ref[pl.ds(base, SCAN_SUB), :]
        hi = xs.astype(BF16).astype(F32)
        r1 = xs - hi
        mid = r1.astype(BF16).astype(F32)
        lo = (r1 - mid).astype(BF16).astype(F32)
        g = jnp.concatenate([hi[:, :LANES], hi[:, LANES:], mid[:, :LANES], mid[:, LANES:],
                             lo[:, :LANES], lo[:, LANES:], jnp.zeros((LANES - 6 * SCAN_SUB, LANES), F32)], axis=0)
        lhs_scr[n * LANES:(n + 1) * LANES, :] = g.T.astype(BF16)


def _stage_tiles(lhs_scr, sel_ref, selk_ref, xb_scr, p):
    out = jnp.dot(lhs_scr[:2 * LANES, :], sel_ref[p], preferred_element_type=F32)
    for n in range(2):
        for tl in range(SCAN_GROUP):
            xb_scr[n, SCAN_GROUP * p + tl] = out[n * LANES:(n + 1) * LANES, tl * LANES:(tl + 1) * LANES]
    outk = jnp.dot(lhs_scr[2 * LANES:, :], selk_ref[p], preferred_element_type=F32)
    for j in range(SCAN_GROUP // 2):
        xb_scr[2, SCAN_GROUP * p + 2 * j] = outk[:, j * LANES:(j + 1) * LANES]


def _rwkv_scan_kernel(r_ref, pc_ref, k_ref, v_ref, a_ref, b_ref, g_ref, bonus_ref, lnw_ref, lnb_ref, sel_ref, selk_ref,
                      o_ref, so_ref, st_scr, lhs_a, lhs_b, xb_a, xb_b, vt_scr, vb_scr, yt_scr, yb_scr):
    @pl.when(pl.program_id(0) == 0)
    def _():
        st_scr[...] = jnp.zeros_like(st_scr)

    vtop, vbot = _rowpair(v_ref[...])
    vt_scr[...] = vtop
    vb_scr[...] = vbot
    ops = (a_ref, b_ref, k_ref)
    pairs = SCAN_UNROLL // SCAN_GROUP
    _stage_lhs(ops, 0, lhs_a)
    for p in range(SCAN_SUB // SCAN_GROUP):
        _stage_tiles(lhs_a, sel_ref, selk_ref, xb_a, p)
    _stage_lhs(ops, SCAN_SUB, lhs_b)

    def half(sc, st, lhs_cur, xb_cur, lhs_nxt, xb_nxt):
        base = pl.multiple_of(sc * SCAN_SUB, SCAN_SUB)

        def quad(q, st):
            for pp in range(pairs):
                _stage_tiles(lhs_nxt, sel_ref, selk_ref, xb_nxt, q * pairs + pp)
            for uu in range(SCAN_UNROLL):
                u = q * SCAN_UNROLL + uu
                t = base + u
                sa_t, sa_b = _halfsums(st * xb_cur[0, u])
                kb = xb_cur[2, u] if uu % 2 == 0 else _colbcast(k_ref[pl.ds(t, 1), :])
                st = (st + _rowbcast(sa_t, sa_b) * xb_cur[1, u]
                      + _rowbcast(vt_scr[pl.ds(t, 1), :], vb_scr[pl.ds(t, 1), :]) * kb)
                y_t, y_b = _halfsums(st * _colbcast(r_ref[pl.ds(t, 1), :]))
                yt_scr[pl.ds(t, 1), :] = y_t
                yb_scr[pl.ds(t, 1), :] = y_b
            return st

        st = lax.fori_loop(0, SCAN_SUB // SCAN_UNROLL, quad, st) * _colbcast(pc_ref[pl.ds(sc, 1), :])
        nxt2 = pl.multiple_of(jnp.minimum(sc + 2, SCAN_NSUB - 1) * SCAN_SUB, SCAN_SUB)
        _stage_lhs(ops, nxt2, lhs_cur)
        return st

    def sub2(s2, st):
        st = half(2 * s2, st, lhs_a, xb_a, lhs_b, xb_b)
        return half(2 * s2 + 1, st, lhs_b, xb_b, lhs_a, xb_a)

    st = lax.fori_loop(0, SCAN_NSUB // 2, sub2, st_scr[...])
    st_scr[...] = st
    y = _unpair(yt_scr[...], yb_scr[...])
    o_ref[...] = _wkv_post(y, lnw_ref[...], lnb_ref[...], bonus_ref[...], g_ref[...])
    heads = _t_to_heads(st, transpose=True)
    for h in range(C_HEADS):
        so_ref[h] = heads[h]


def _rwkv_scan(prep, lnw, lnb):
    r, pc, k, v, a, b, g, bonus = prep
    l = r.shape[0]
    blk = pl.BlockSpec((SCAN_BLOCK, C_WIDTH), lambda i: (i, 0))
    vec = _full((1, C_WIDTH))
    sel = _scan_select(1)
    selk = _scan_select(2)
    return pl.pallas_call(
        _rwkv_scan_kernel, grid=(l // SCAN_BLOCK,), name="rwkv_scan",
        in_specs=[blk, pl.BlockSpec((SCAN_NSUB, C_WIDTH), lambda i: (i, 0))] + [blk] * 6 + [vec, vec, _full(sel.shape), _full(selk.shape)],
        out_specs=[blk, _full((C_HEADS, C_HEAD, C_HEAD))],
        out_shape=[jax.ShapeDtypeStruct((l, C_WIDTH), F32), jax.ShapeDtypeStruct((C_HEADS, C_HEAD, C_HEAD), F32)],
        scratch_shapes=[pltpu.VMEM((LANES, LANES), F32),
                        pltpu.VMEM((SCAN_OPS * LANES, LANES), BF16), pltpu.VMEM((SCAN_OPS * LANES, LANES), BF16),
                        pltpu.VMEM((SCAN_OPS, SCAN_SUB, LANES, LANES), F32),
                        pltpu.VMEM((SCAN_OPS, SCAN_SUB, LANES, LANES), F32),
                        pltpu.VMEM((SCAN_BLOCK, LANES), F32), pltpu.VMEM((SCAN_BLOCK, LANES), F32),
                        pltpu.VMEM((SCAN_BLOCK, LANES), F32), pltpu.VMEM((SCAN_BLOCK, LANES), F32)],
        compiler_params=_cp(("arbitrary",)))(r, pc, k, v, a, b, g, bonus, lnw, lnb, sel, selk)


def _sample_kernel(sinks_ref, gam_ref, za_ref, zr_ref, r_ref, w_ref, k_ref, v_ref, a_ref, b_ref, g_ref, bonus_ref,
                   kc_ref, vc_ref, sret_ref, swkv_ref, lnw_ref, lnb_ref,
                   oa_ref, or_ref, oc_ref, kco_ref, vco_ref, sreto_ref, swkvo_ref):
    lane =lax.broadcasted_iota(jnp.int32, (1, LANES), 1)
    row8 = lax.broadcasted_iota(jnp.int32, (SUBLANES, LANES), 0)

    za = za_ref[0]
    q = za[:, :A_WIDTH]
    knew = za[:, A_WIDTH:A_WIDTH + A_KV_WIDTH]
    vnew = za[:, A_WIDTH + A_KV_WIDTH:]
    kc, vc = kc_ref[0], vc_ref[0]
    g = A_HEADS // A_KV_HEADS
    zero = jnp.zeros((1, HEAD_DIM), F32)
    qrows = []
    for h in range(A_HEADS):
        qh = q[:, h * HEAD_DIM:(h + 1) * HEAD_DIM]
        qrows.append(jnp.concatenate([qh, zero] if h // g == 0 else [zero, qh], axis=1))
    qm = jnp.concatenate(qrows + [jnp.zeros((LANES - A_HEADS, LANES), F32)], axis=0)
    scale = HEAD_DIM ** -0.5
    s = lax.dot_general(kc, qm, (((1,), (1,)), ((), ())), preferred_element_type=F32) * scale
    kn8 = jnp.where(row8 == 0, jnp.broadcast_to(knew, (SUBLANES, LANES)), 0.0)
    s_new = lax.dot_general(kn8, qm, (((1,), (1,)), ((), ())), preferred_element_type=F32)[0:1] * scale
    sink = jnp.zeros((1, LANES), F32)
    for h in range(A_HEADS):
        sink = jnp.where(lane == h, sinks_ref[h], sink)
    m = jnp.maximum(jnp.maximum(jnp.max(s, axis=0, keepdims=True), s_new), sink)
    p = jnp.exp(s - m)
    p_new = jnp.exp(s_new - m)
    denom = jnp.sum(p, axis=0, keepdims=True) + p_new + jnp.exp(sink - m)
    p = p / denom
    p_new = p_new / denom
    pn_col = jnp.broadcast_to(p_new, (LANES, LANES)).T[:, 0:1]
    o_full = jnp.dot(p.T, vc, preferred_element_type=F32) + pn_col * vnew
    oa_ref[0] = jnp.concatenate(
        [o_full[h:h + 1, (h // g) * HEAD_DIM:(h // g + 1) * HEAD_DIM] for h in range(A_HEADS)], axis=1)
    rowk = lax.broadcasted_iota(jnp.int32, kc.shape, 0)
    last = rowk == kc.shape[0] - 1
    kco_ref[0] = jnp.where(last, jnp.broadcast_to(knew, kc.shape), pltpu.roll(kc, kc.shape[0] - 1, axis=0))
    vco_ref[0] = jnp.where(last, jnp.broadcast_to(vnew, vc.shape), pltpu.roll(vc, vc.shape[0] - 1, axis=0))

    zr = zr_ref[0]
    qk = R_HEADS * R_DK
    rq = zr[:, :qk]
    rk = zr[:, qk:2 * qk] * (R_DK ** -0.5)
    rv = zr[:, 2 * qk:2 * qk + R_WIDTH]
    rg = zr[:, 2 * qk + R_WIDTH:]
    st = _heads_to_t(sret_ref[0], transpose=False)
    rr = lax.broadcasted_iota(jnp.int32, (LANES, LANES), 0) >= 64
    cc = lax.broadcasted_iota(jnp.int32, (LANES, LANES), 1) >= 64
    gamma = jnp.where(rr, jnp.where(cc, gam_ref[3], gam_ref[1]), jnp.where(cc, gam_ref[2], gam_ref[0]))
    vt, vb = _rowpair(rv)
    st = st * gamma + _colbcast(rk) * _rowbcast(vt, vb)
    o_t, o_b = _halfsums(st * _colbcast(rq))
    o = _unpair(o_t, o_b)
    o = o * lax.rsqrt(_seg_sum(o * o) * (1.0 / R_DV) + EPS)
    or_ref[0] = o * _silu(rg)
    heads = _t_to_heads(st, transpose=False)
    for h in range(R_HEADS):
        sreto_ref[0, h] = heads[h]

    st = _heads_to_t(swkv_ref[0], transpose=True)
    vt, vb = _rowpair(v_ref[0])
    st, y_t, y_b = _wkv_step(st, _colbcast(a_ref[0]), _colbcast(w_ref[0]), _colbcast(b_ref[0]),
                             _colbcast(k_ref[0]), _colbcast(r_ref[0]), vt, vb)
    oc_ref[0] = _wkv_post(_unpair(y_t, y_b), lnw_ref[...], lnb_ref[...], bonus_ref[0], g_ref[0])
    heads = _t_to_heads(st, transpose=True)
    for h in range(C_HEADS):
        swkvo_ref[0, h] = heads[h]


def _sample_mixers(za, zr, prep, kc, vc, sret, swkv, sinks, lnw, lnb):
    b = za.shape[0]
    wb = kc.shape[1]
    assert wb <= WINDOW and PAST_LEN >= wb
    gam = jnp.exp(jnp.log1p(-jnp.exp2(-5.0 - jnp.arange(R_HEADS, dtype=F32))) * 1.0)
    tok = lambda w: pl.BlockSpec((1, 1, w), lambda i: (i, 0, 0))
    cache = pl.BlockSpec((1, wb, A_KV_WIDTH), lambda i: (i, 0, 0))
    state = pl.BlockSpec((1, 4, 64, 64), lambda i: (i, 0, 0, 0))
    smem = pl.BlockSpec(memory_space=pltpu.SMEM)
    vec = _full((1, C_WIDTH))
    r3 = lambda x: x.reshape(b, 1, x.shape[-1])
    outs = pl.pallas_call(
        _sample_kernel, grid=(b,), name="sample_mixers",
        in_specs=[smem, smem, tok(A_COLS), tok(R_COLS)] + [tok(C_WIDTH)] * 8 + [cache, cache, state, state,
                  vec, vec],
        out_specs=[tok(A_WIDTH), tok(R_WIDTH), tok(C_WIDTH), cache, cache, state, state],
        out_shape=[jax.ShapeDtypeStruct((b, 1, A_WIDTH), F32), jax.ShapeDtypeStruct((b, 1, R_WIDTH), F32),
                   jax.ShapeDtypeStruct((b, 1, C_WIDTH), F32), jax.ShapeDtypeStruct(kc.shape, F32),
                   jax.ShapeDtypeStruct(vc.shape, F32), jax.ShapeDtypeStruct(sret.shape, F32),
                   jax.ShapeDtypeStruct(swkv.shape, F32)],
        compiler_params=_cp(("parallel",)))(sinks, gam, r3(za), r3(zr), *[r3(x) for x in prep],
                                           kc, vc, sret, swkv, lnw, lnb)
    oa, orr, oc = (x.reshape(b, x.shape[-1]) for x in outs[:3])
    return (oa, orr, oc) + tuple(outs[3:])


def _outproj_kernel(oa_ref, or_ref, oc_ref, x_ref, wo_ref, gf_ref, wrt_ref, brt_ref, *rest, tm):
    x1_ref, h2_ref, route_ref = rest[-3:]
    mix =jnp.concatenate([oa_ref[...], or_ref[...], oc_ref[...]], axis=1).astype(BF16)
    x1 = x_ref[...] + jnp.dot(mix, wo_ref[...], preferred_element_type=F32)
    x1_ref[...] = x1
    h2 = _rmsnorm(x1, gf_ref[...])
    for s in range(ROW_TILES):
        h2_ref[pl.ds(s, tm, stride=ROW_TILES), :] = h2[:, s * LANES:(s + 1) * LANES]
    h_hi = h2.astype(BF16)
    h_lo = (h2 - h_hi.astype(F32)).astype(BF16)
    w_hi, w_lo = wrt_ref[0], wrt_ref[1]
    logits = (jnp.dot(h_hi, w_hi, preferred_element_type=F32) + jnp.dot(h_lo, w_hi, preferred_element_type=F32)
              + jnp.dot(h_hi, w_lo, preferred_element_type=F32)) + brt_ref[...]
    lane = lax.broadcasted_iota(jnp.int32, logits.shape, 1).astype(F32)
    big = float(LANES)
    is_g = lane < N_GROUPS
    lg = jnp.where(is_g, logits, -jnp.inf)
    m_g = jnp.max(lg, -1, keepdims=True)
    grp = jnp.min(jnp.where(lg == m_g, lane, big), -1, keepdims=True)
    p_grp = 1.0 / jnp.sum(jnp.where(is_g, jnp.exp(lg - m_g), 0.0), -1, keepdims=True)
    lo = N_GROUPS + EXPERTS_PER_GROUP * grp
    in_grp = (lane >= lo) & (lane < lo + EXPERTS_PER_GROUP)
    le = jnp.where(in_grp, logits, -jnp.inf)
    l1 = jnp.max(le, -1, keepdims=True)
    i1 = jnp.min(jnp.where(le == l1, lane, big), -1, keepdims=True)
    le2 = jnp.where(lane == i1, -jnp.inf, le)
    l2 = jnp.max(le2, -1, keepdims=True)
    i2 = jnp.min(jnp.where(le2 == l2, lane, big), -1, keepdims=True)
    e = jnp.exp(l2 - l1)
    g1 = p_grp / (1.0 + e)
    g2 = p_grp * e / (1.0 + e)
    route = jnp.where(lane == 0, i1 - N_GROUPS, jnp.where(lane == 1, i2 - N_GROUPS,
                      jnp.where(lane == 2, g1, jnp.where(lane == 3, g2, 0.0))))
    route_ref[...] = route


def _outproj(oa, orr, oc, x, wo, gf, wrt, brt, tm, t_all, first_tok, shared=None):
    t = x.shape[0]
    off = first_tok // tm
    nblk = t // tm
    steps = nblk + (1 if shared is None else 0)
    assert shared is not None or t_all == t + tm
    row = lambda w: pl.BlockSpec((tm, w), lambda i: (jnp.minimum(i, nblk - 1), 0))
    in_specs = [row(A_WIDTH), row(R_WIDTH), row(C_WIDTH), row(D_MODEL), _full(wo.shape), _full((1, D_MODEL)),
                _full(wrt.shape), _full((1, LANES))]
    args = [oa, orr, oc, x, wo, gf, wrt, brt]
    aliases = {}
    if shared is not None:
        aliases = {len(args): 1, len(args) + 1: 2}
        in_specs += [pl.BlockSpec(memory_space=pl.ANY)] * 2
        args += list(shared)
    return pl.pallas_call(
        functools.partial(_outproj_kernel, tm=tm), grid=(steps,), name="outproj_router",
        in_specs=in_specs,
        out_specs=[row(D_MODEL), pl.BlockSpec((tm * ROW_TILES, LANES), lambda i: (i + off, 0)),
                   pl.BlockSpec((tm, LANES), lambda i: (i + off, 0))],
        out_shape=[jax.ShapeDtypeStruct((t, D_MODEL), F32), jax.ShapeDtypeStruct((t_all * ROW_TILES, LANES), F32),
                   jax.ShapeDtypeStruct((t_all, LANES), F32)],
        input_output_aliases=aliases,
        compiler_params=_cp(("arbitrary",)))(*args)


def _moe_kernel(blk_exp_ref, src_ref, dst_ref, grp_ref, nused_ref, h2_hbm, wg_ref, wu_ref, wd_ref, yw_hbm,
                gbuf, sbuf, xbuf, wgb, wub, wdb, gsem, ssem):
    i = pl.program_id(0)
    nb = pl.num_programs(0)
    nused = nused_ref[0]
    slot = i % 2

    def tile(off):
        return pl.ds(pl.multiple_of(off, ROW_TILES), ROW_TILES)

    def rows(blk, fn):
        def body(g, _):
            for j in range(DMA_UNROLL):
                fn(g * DMA_UNROLL + j)
            return 0
        lax.fori_loop(0, grp_ref[blk], body, 0)

    def gather(blk, sl):
        rows(blk, lambda r: pltpu.make_async_copy(h2_hbm.at[tile(src_ref[blk * MOE_ROWS + r]), :],
                                                  gbuf.at[sl, tile(r * ROW_TILES), :], gsem.at[sl]).start())

    def gather_wait(blk, sl):
        rows(blk, lambda r: pltpu.make_async_copy(h2_hbm.at[tile(0), :], gbuf.at[sl, tile(0), :],
                                                  gsem.at[sl]).wait())

    def scatter(blk):
        rows(blk, lambda r: pltpu.make_async_copy(sbuf.at[tile(r * ROW_TILES), :],
                                                  yw_hbm.at[tile(dst_ref[blk * MOE_ROWS + r]), :], ssem).start())

    def scatter_wait(blk):
        rows(blk, lambda r: pltpu.make_async_copy(sbuf.at[tile(0), :], yw_hbm.at[tile(0), :], ssem).wait())

    @pl.when((i == 0) & (nused > 0))
    def _():
        gbuf[...] = jnp.zeros_like(gbuf)
        gather(0, 0)

    @pl.when(i < nused)
    def _():
        @pl.when((i == 0) | (blk_exp_ref[i] != blk_exp_ref[jnp.maximum(i - 1, 0)]))
        def _():
            wgb[...] = wg_ref[0].astype(BF16)
            wub[...] = wu_ref[0].astype(BF16)
            wdb[...] = wd_ref[0].astype(BF16)

        gather_wait(i, slot)

        @pl.when(i + 1 < nused)
        def _():
            gather(i + 1, 1 - slot)

        for s in range(ROW_TILES):
            xbuf[:, s * LANES:(s + 1) * LANES] = gbuf[slot, pl.ds(s, MOE_ROWS, stride=ROW_TILES), :].astype(BF16)
        x = xbuf[...]
        hid = _silu(jnp.dot(x, wgb[...], preferred_element_type=F32)) * jnp.dot(x, wub[...], preferred_element_type=F32)
        y = jnp.dot(hid.astype(BF16), wdb[...], preferred_element_type=F32)

        @pl.when(i > 0)
        def _():
            scatter_wait(i - 1)

        for s in range(ROW_TILES):
            sbuf[pl.ds(s, MOE_ROWS, stride=ROW_TILES), :] = y[:, s * LANES:(s + 1) * LANES]
        scatter(i)

    @pl.when((i == nb - 1) & (nused > 0))
    def _():
        scatter_wait(nused - 1)
        spare = pltpu.make_async_copy(sbuf, yw_hbm.at[pl.ds(yw_hbm.shape[0] - MOE_ROWS * ROW_TILES,
                                                            MOE_ROWS * ROW_TILES), :], ssem)
        spare.start()
        spare.wait()


def _moe(h2, route, t, wg, wu, wd):
    na = t * TOP_K
    assert na < (1 << IDX_BITS)
    flat_e = route[:t, :TOP_K].astype(jnp.int32).reshape(na)
    experts = jnp.arange(N_EXPERTS, dtype=jnp.int32)
    counts = jnp.sum(flat_e[:, None] == experts[None, :], axis=0, dtype=jnp.int32)
    npad = (-counts) % MOE_ROWS
    real_keys = (flat_e << (IDX_BITS + 1)) | jnp.arange(na, dtype=jnp.int32)
    q = jnp.arange(MOE_ROWS, dtype=jnp.int32)[None, :]
    pad_keys = jnp.where(q < npad[:, None], (experts[:, None] << (IDX_BITS + 1)) | (1 << IDX_BITS) | q,
                         jnp.iinfo(jnp.int32).max)
    n_blocks = -(-na // MOE_ROWS) + N_EXPERTS
    n_rows = n_blocks * MOE_ROWS
    filler = jnp.full((n_rows - na - N_EXPERTS * MOE_ROWS,), jnp.iinfo(jnp.int32).max, jnp.int32)
    keys = jnp.sort(jnp.concatenate([real_keys, pad_keys.reshape(-1), filler]))
    is_pad = ((keys >> IDX_BITS) & 1) == 1
    idx = keys & ((1 << IDX_BITS) - 1)
    pos = jnp.arange(n_rows, dtype=jnp.int32)
    src = jnp.where(is_pad, 0, (idx // TOP_K) * ROW_TILES).astype(jnp.int32)
    dst = jnp.where(is_pad, na + pos % MOE_ROWS, idx).astype(jnp.int32) * ROW_TILES
    blk_exp = jnp.minimum(keys[::MOE_ROWS] >> (IDX_BITS + 1), N_EXPERTS - 1).astype(jnp.int32)
    nused = (jnp.sum(counts + npad) // MOE_ROWS).astype(jnp.int32).reshape(1)
    real_rows = jnp.sum((~is_pad).reshape(n_blocks, MOE_ROWS), axis=1, dtype=jnp.int32)
    grp = (real_rows + DMA_UNROLL - 1) // DMA_UNROLL
    wspec = lambda a, b: pl.BlockSpec((1, a, b), lambda i, be, s, d, gr, nu: (be[i], 0, 0))
    return pl.pallas_call(
        _moe_kernel, name="moe_experts",
        grid_spec=pltpu.PrefetchScalarGridSpec(
            num_scalar_prefetch=5, grid=(n_blocks,),
            in_specs=[pl.BlockSpec(memory_space=pl.ANY), wspec(D_MODEL, EXPERT_FF), wspec(D_MODEL, EXPERT_FF),
                      wspec(EXPERT_FF, D_MODEL)],
            out_specs=pl.BlockSpec(memory_space=pl.ANY),
            scratch_shapes=[pltpu.VMEM((2, MOE_ROWS * ROW_TILES, LANES), F32),
                            pltpu.VMEM((MOE_ROWS * ROW_TILES, LANES), F32),
                            pltpu.VMEM((MOE_ROWS, D_MODEL), BF16),
                            pltpu.VMEM((D_MODEL, EXPERT_FF), BF16), pltpu.VMEM((D_MODEL, EXPERT_FF), BF16),
                            pltpu.VMEM((EXPERT_FF, D_MODEL), BF16),
                            pltpu.SemaphoreType.DMA((2,)), pltpu.SemaphoreType.DMA(())]),
        out_shape=jax.ShapeDtypeStruct(((na + MOE_ROWS) * ROW_TILES, LANES), F32),
        compiler_params=_cp(("arbitrary",)))(blk_exp, src, dst, grp, nused, h2, wg, wu, wd)


def _rope_tables(pos):
    half = HEAD_DIM // 2
    inv = ROPE_THETA ** (-jnp.arange(half, dtype=F32) / half)
    ang = pos.astype(F32)[:, None] * inv[None, :]
    c, s = jnp.cos(ang), jnp.sin(ang)
    return jnp.concatenate([c, c, c, c], axis=1), jnp.concatenate([-s, s, -s, s], axis=1)


def _pad_cols(w, n):
    return jnp.pad(w, ((0, 0), (0, n - w.shape[1])))


def _pad_rows(w, lo, n):
    return jnp.pad(w, ((lo, n - lo - w.shape[0]), (0, 0)))


def kernel(x_prompt, x_sample, cache_swa_k, cache_swa_v, state_ret, state_wkv, state_shift, norm_mix, w_in, sinks,
           rwkv_mu, rwkv_w0, rwkv_w2, rwkv_a0, rwkv_a2, rwkv_g2, rwkv_k_k, rwkv_k_a, rwkv_r_k, rwkv_ln_w, rwkv_ln_b,
           w_out, norm_ffn, router_g, router_g_b, router_e, router_e_b, expert_w_gate, expert_w_up, expert_w_down,
           norm_final):
    lp = x_prompt.shape[1]
    bs = x_sample.shape[0]
    wb = cache_swa_k.shape[2]
    tm_p = 512
    xp = x_prompt.reshape(lp, D_MODEL)
    xs = x_sample.reshape(bs, D_MODEL)
    cos_p, sin_p = _rope_tables(jnp.arange(lp, dtype=jnp.int32))
    cos_s, sin_s = (jnp.broadcast_to(t, (bs, LANES))
                    for t in _rope_tables(PAST_LEN + jnp.arange(x_sample.shape[1], dtype=jnp.int32)))
    outs_p = [[] for _ in range(5)]
    outs_s = [[] for _ in range(5)]
    moe_p = moe_s = None
    row = lambda v: v.reshape(1, -1)
    for i in range(DEPTH):
        wi = w_in[i].astype(BF16)
        wa, wr, wc = wi[:, :A_COLS], wi[:, A_COLS:A_COLS + R_COLS], _pad_cols(wi[:, A_COLS + R_COLS:], C_PAD)
        cw = {
            "mu": _pad_cols(row(rwkv_mu[i]), C_PAD), "w0": row(rwkv_w0[i]), "a0": row(rwkv_a0[i]),
            "w2": _pad_rows(rwkv_w2[i], 0, LANES), "a2": _pad_rows(rwkv_a2[i], C_DECAY_LORA, LANES),
            "g2": _pad_rows(rwkv_g2[i], 0, 2 * LANES),
            "k_k": row(rwkv_k_k[i]), "k_a": row(rwkv_k_a[i]), "r_k": row(rwkv_r_k[i]),
        }
        lnw, lnb = row(rwkv_ln_w[i]), row(rwkv_ln_b[i])
        g_mix, g_ffn = row(norm_mix[i]), row(norm_ffn[i])
        wo = w_out[i].astype(BF16)
        wrt = _pad_cols(jnp.concatenate([router_g[i], router_e[i]], axis=1), LANES)
        wrt_hi = wrt.astype(BF16)
        wrt = jnp.stack([wrt_hi, (wrt - wrt_hi.astype(F32)).astype(BF16)])
        brt = _pad_cols(row(jnp.concatenate([router_g_b[i], router_e_b[i]])), LANES)

        xn, za, zr, zc = _inproj(xp, moe_p, g_mix, wa, wr, wc, cos_p, sin_p, tm_p)
        xp = xp if xn is None else xn
        oa = _swa_prompt(za, sinks[i])
        orr, s_ret = _ret_prompt(zr)
        prep = _rwkv_prep(zc, None, jnp.zeros((1, C_PAD), F32), cw, True, tm_p)
        oc, s_wkv = _rwkv_scan(prep, lnw, lnb)
        xp, h2, route = _outproj(oa, orr, oc, xp, wo, g_ffn, wrt, brt, tm_p, lp + tm_p, 0)
        nk = min(WINDOW, lp)
        outs_p[0].append(za[lp - nk:, A_WIDTH:A_WIDTH + A_KV_WIDTH].reshape(1, nk, A_KV_HEADS, HEAD_DIM))
        outs_p[1].append(za[lp - nk:, A_WIDTH + A_KV_WIDTH:].reshape(1, nk, A_KV_HEADS, HEAD_DIM))
        outs_p[2].append(s_ret[None])
        outs_p[3].append(s_wkv[None])
        outs_p[4].append(zc[lp - 1:, :C_COLS])

        xn, za, zr, zc = _inproj(xs, moe_s, g_mix, wa, wr, wc, cos_s, sin_s, bs)
        xs = xs if xn is None else xn
        prep = _rwkv_prep(zc, _pad_cols(state_shift[i], C_PAD), jnp.zeros((1, C_PAD), F32), cw, False, bs)
        oa, orr, oc, kc_new, vc_new, sret_new, swkv_new = _sample_mixers(
            za, zr, prep, cache_swa_k[i].reshape(bs, wb, A_KV_WIDTH), cache_swa_v[i].reshape(bs, wb, A_KV_WIDTH),
            state_ret[i], state_wkv[i], sinks[i], lnw, lnb)
        xs, h2, route = _outproj(oa, orr, oc, xs, wo, g_ffn, wrt, brt, bs, lp + tm_p, lp, shared=(h2, route))
        yw = _moe(h2, route, lp + bs, expert_w_gate[i], expert_w_up[i], expert_w_down[i])
        moe_p, moe_s = (yw, route, 0), (yw, route, lp)
        outs_s[0].append(kc_new.reshape(bs, wb, A_KV_HEADS, HEAD_DIM))
        outs_s[1].append(vc_new.reshape(bs, wb, A_KV_HEADS, HEAD_DIM))
        outs_s[2].append(sret_new)
        outs_s[3].append(swkv_new)
        outs_s[4].append(zc[:, :C_COLS])

    gfin = row(norm_final)
    y_prompt = _final(xp, moe_p, gfin, tm_p).reshape(x_prompt.shape)
    y_sample = _final(xs, moe_s, gfin, bs).reshape(x_sample.shape)
    sp = [jnp.stack(o) for o in outs_p]
    ss = [jnp.stack(o) for o in outs_s]
    return (y_prompt, y_sample, sp[0], sp[1], sp[2], sp[3], sp[4], ss[0], ss[1], ss[2], ss[3], ss[4])
```

```python
import functools

import numpy as np
import jax
import jax.numpy as jnp
from jax import lax
from jax.experimental import pallas as pl
from jax.experimental.pallas import tpu as pltpu

F32 = jnp.float32
BF16 = jnp.bfloat16

D_MODEL = 1024
DEPTH = 2
PAST_LEN = 16384
A_HEADS, A_KV_HEADS, HEAD_DIM, WINDOW = 8, 2, 64, 128
ROPE_THETA = 10000.0
A_WIDTH = A_HEADS * HEAD_DIM
A_KV_WIDTH = A_KV_HEADS * HEAD_DIM
A_COLS = A_WIDTH + 2 * A_KV_WIDTH
R_HEADS, R_DK, R_DV, R_CHUNK = 4, 64, 64, 128
R_WIDTH = R_HEADS * R_DV
R_COLS = 2 * R_HEADS * R_DK + 2 * R_WIDTH
C_HEADS, C_HEAD = 4, 64
C_WIDTH = C_HEADS * C_HEAD
C_DECAY_LORA, C_ICLR_LORA, C_GATE_LORA = 64, 64, 160
C_COLS = 3 * C_WIDTH + C_DECAY_LORA + C_ICLR_LORA + C_GATE_LORA
C_PAD = 1152
C_GN_EPS = 64e-5
N_GROUPS, EXPERTS_PER_GROUP, TOP_K, EXPERT_FF = 4, 8, 2, 512
N_EXPERTS = N_GROUPS * EXPERTS_PER_GROUP
EPS = 1e-6

LANES = 128
SUBLANES = 8
ROW_TILES = D_MODEL // LANES
MOE_ROWS = 256
DMA_UNROLL = 8
IDX_BITS = 19
VMEM_LIMIT = 56 * 1024 * 1024


def _cp(sem, vmem=VMEM_LIMIT):
    return pltpu.CompilerParams(dimension_semantics=sem, vmem_limit_bytes=vmem)


def _full(shape):
    n = len(shape)
    return pl.BlockSpec(shape, lambda *_: (0,) * n)


def _rmsnorm(x, g):
    return x * lax.rsqrt(jnp.mean(x * x, -1, keepdims=True) + EPS) * g


def _silu(x):
    return x * (1.0 / (1.0 + jnp.exp(-x)))


def _sigmoid(x):
    return 1.0 / (1.0 + jnp.exp(-x))


def _rope(x, cos, sin_signed):
    w = x.shape[-1]
    lane = lax.broadcasted_iota(jnp.int32, x.shape, x.ndim - 1)
    first = (lane % HEAD_DIM) < (HEAD_DIM // 2)
    swapped = jnp.where(first, pltpu.roll(x, w - HEAD_DIM // 2, axis=x.ndim - 1),
                        pltpu.roll(x, HEAD_DIM // 2, axis=x.ndim - 1))
    return x * cos + swapped * sin_signed


def _tile_lanes(x, n):
    return jnp.concatenate([x] * n, axis=-1) if n > 1 else x


def _seg_sum(x, seg=64):
    outs = []
    for h in range(x.shape[-1] // seg):
        s = jnp.sum(x[:, h * seg:(h + 1) * seg], axis=-1, keepdims=True)
        outs.append(jnp.broadcast_to(s, (x.shape[0], seg)))
    return jnp.concatenate(outs, axis=-1)


def _read_rows(ref, tm, first, stride):
    return jnp.concatenate([ref[pl.ds(first + s, tm, stride=stride), :] for s in range(ROW_TILES)], axis=1)


def _colbcast(x):
    z = jnp.concatenate([jnp.broadcast_to(x[:, :LANES], (64, LANES)),
                         jnp.broadcast_to(x[:, LANES:], (64, LANES))], axis=0)
    return z.T


def _rowpair(x):
    top = jnp.concatenate([x[:, 0:64], x[:, 128:192]], axis=1)
    bot = jnp.concatenate([x[:, 64:128], x[:, 192:256]], axis=1)
    return top, bot


def _unpair(top, bot):
    return jnp.concatenate([top[:, :64], bot[:, :64], top[:, 64:], bot[:, 64:]], axis=1)


def _rowbcast(top, bot):
    return jnp.concatenate([jnp.broadcast_to(top, (64, LANES)), jnp.broadcast_to(bot, (64, LANES))], axis=0)


def _halfsums(p):
    return jnp.sum(p[:64], axis=0, keepdims=True), jnp.sum(p[64:], axis=0, keepdims=True)


def _wkv_step(st, ab, wb, bb, kb, rb, vtop, vbot):
    sa_t, sa_b = _halfsums(st * ab)
    st = st * wb + _rowbcast(sa_t, sa_b) * bb + _rowbcast(vtop, vbot) * kb
    y_t, y_b = _halfsums(st * rb)
    return st, y_t, y_b


def _wkv_post(y, lnw, lnb, bonus, g):
    mu = _seg_sum(y) * (1.0 / C_HEAD)
    d = y - mu
    var = _seg_sum(d * d) * (1.0 / C_HEAD)
    yn = d * lax.rsqrt(var + C_GN_EPS)
    return (yn * lnw + lnb + bonus) * g


def _heads_to_t(s4, transpose):
    if transpose:
        m = jnp.concatenate([jnp.concatenate([s4[0], s4[1]], axis=1),
                             jnp.concatenate([s4[2], s4[3]], axis=1)], axis=0)
        return m.T
    return jnp.concatenate([jnp.concatenate([s4[0], s4[2]], axis=1),
                            jnp.concatenate([s4[1], s4[3]], axis=1)], axis=0)


def _t_to_heads(st, transpose):
    if transpose:
        m = st.T
        return [m[0:64, 0:64], m[0:64, 64:128], m[64:128, 0:64], m[64:128, 64:128]]
    return [st[0:64, 0:64], st[64:128, 0:64], st[0:64, 64:128], st[64:128, 64:128]]


def _inproj_kernel(*refs, combine, tm):
    if combine:
        x_ref, yw_ref, route_ref, g_ref, wa_ref, wr_ref, wc_ref, cos_ref, sin_ref, xo_ref, za_ref, zr_ref, zc_ref = refs
        route = route_ref[...]
        y0 = _read_rows(yw_ref, tm, 0, 2 * ROW_TILES)
        y1 = _read_rows(yw_ref, tm, ROW_TILES, 2 * ROW_TILES)
        x = x_ref[...] + (route[:, 2:3] * y0 + route[:, 3:4] * y1)
        xo_ref[...] = x
    else:
        x_ref, g_ref, wa_ref, wr_ref, wc_ref, cos_ref, sin_ref, za_ref, zr_ref, zc_ref = refs
        x = x_ref[...]
    h = _rmsnorm(x, g_ref[...]).astype(BF16)
    cos, sin = cos_ref[...], sin_ref[...]
    na = (A_WIDTH + A_KV_WIDTH) // LANES
    nr = 2 * R_HEADS * R_DK // LANES
    za = jnp.dot(h, wa_ref[...], preferred_element_type=F32)
    za_ref[:, :na * LANES] = _rope(za[:, :na * LANES], _tile_lanes(cos, na), _tile_lanes(sin, na))
    za_ref[:, na * LANES:] = za[:, na * LANES:]
    zr = jnp.dot(h, wr_ref[...], preferred_element_type=F32)
    zr_ref[:, :nr * LANES] = _rope(zr[:, :nr * LANES], _tile_lanes(cos, nr), _tile_lanes(sin, nr))
    zr_ref[:, nr * LANES:] = zr[:, nr * LANES:]
    zc_ref[...] = jnp.dot(h, wc_ref[...], preferred_element_type=F32)


def _inproj(x, moe, g, wa, wr, wc, cos, sin, tm):
    t = x.shape[0]
    row = lambda w: pl.BlockSpec((tm, w), lambda i: (i, 0))
    wspecs = [_full((1, D_MODEL)), _full(wa.shape), _full(wr.shape), _full(wc.shape), row(LANES), row(LANES)]
    zshapes = [jax.ShapeDtypeStruct((t, A_COLS), F32), jax.ShapeDtypeStruct((t, R_COLS), F32),
               jax.ShapeDtypeStruct((t, C_PAD), F32)]
    zspecs = [row(A_COLS), row(R_COLS), row(C_PAD)]
    if moe is None:
        return (None,) + tuple(pl.pallas_call(
            functools.partial(_inproj_kernel, combine=False, tm=tm), name="inproj",
            grid=(t // tm,), in_specs=[row(D_MODEL)] + wspecs, out_specs=zspecs, out_shape=zshapes,
            compiler_params=_cp(("parallel",)))(x, g, wa, wr, wc, cos, sin))
    yw, route, first_tok = moe
    off = first_tok // tm
    return pl.pallas_call(
        functools.partial(_inproj_kernel, combine=True, tm=tm), name="combine_inproj",
        grid=(t // tm,),
        in_specs=[row(D_MODEL), pl.BlockSpec((tm * 2 * ROW_TILES, LANES), lambda i: (i + off, 0)),
                  pl.BlockSpec((tm, LANES), lambda i: (i + off, 0))] + wspecs,
        out_specs=[row(D_MODEL)] + zspecs,
        out_shape=[jax.ShapeDtypeStruct((t, D_MODEL), F32)] + zshapes,
        compiler_params=_cp(("parallel",)))(x, yw, route, g, wa, wr, wc, cos, sin)


def _final_kernel(x_ref, yw_ref, route_ref, g_ref, o_ref, *, tm):
    route = route_ref[...]
    y0 = _read_rows(yw_ref, tm, 0, 2 * ROW_TILES)
    y1 = _read_rows(yw_ref, tm, ROW_TILES, 2 * ROW_TILES)
    x = x_ref[...] + (route[:, 2:3] * y0 + route[:, 3:4] * y1)
    o_ref[...] = _rmsnorm(x, g_ref[...])


def _final(x, moe, g, tm):
    t = x.shape[0]
    yw, route, first_tok = moe
    off = first_tok // tm
    row = lambda w: pl.BlockSpec((tm, w), lambda i: (i, 0))
    return pl.pallas_call(
        functools.partial(_final_kernel, tm=tm), grid=(t // tm,), name="combine_final",
        in_specs=[row(D_MODEL), pl.BlockSpec((tm * 2 * ROW_TILES, LANES), lambda i: (i + off, 0)),
                  pl.BlockSpec((tm, LANES), lambda i: (i + off, 0)), _full((1, D_MODEL))],
        out_specs=row(D_MODEL), out_shape=jax.ShapeDtypeStruct((t, D_MODEL), F32),
        compiler_params=_cp(("parallel",)))(x, yw, route, g)


def _swa_prompt_kernel(sinks_ref, q_ref, kc_ref, kp_ref, vc_ref, vp_ref, o_ref):
    i = pl.program_id(0)
    g = A_HEADS // A_KV_HEADS
    nq = g * WINDOW
    kcat = jnp.concatenate([kp_ref[...], kc_ref[...]], axis=0)
    vcat = jnp.concatenate([vp_ref[...], vc_ref[...]], axis=0)
    kswap = pltpu.roll(kcat, HEAD_DIM, axis=1)
    vswap = pltpu.roll(vcat, HEAD_DIM, axis=1)
    upper_k = lax.broadcasted_iota(jnp.int32, kcat.shape, 1) >= HEAD_DIM
    upper_q = lax.broadcasted_iota(jnp.int32, (WINDOW, LANES), 1) >= HEAD_DIM
    r = lax.broadcasted_iota(jnp.int32, (nq, 2 * WINDOW), 0) % WINDOW
    c = lax.broadcasted_iota(jnp.int32, (nq, 2 * WINDOW), 1)
    ok = (c >= r) & (c <= r + WINDOW) & ((c >= WINDOW) | (i > 0))
    sink_col4 = c == (r + WINDOW + 1) % (2 * WINDOW)
    sink_col = sink_col4[:WINDOW]
    slabs = []
    for kv in range(A_KV_HEADS):
        kboth = jnp.where(upper_k, kswap, kcat) if kv == 0 else jnp.where(upper_k, kcat, kswap)
        v_lo = jnp.where(upper_k, 0.0, vcat if kv == 0 else vswap)
        v_hi = jnp.where(upper_k, vswap if kv == 0 else vcat, 0.0)
        vpair = jnp.concatenate([v_lo, v_hi], axis=0)
        qs = []
        for j in range(g):
            h = kv * g + j
            slab = q_ref[:, (h // 2) * LANES:(h // 2 + 1) * LANES]
            qs.append(jnp.where(upper_q, slab, 0.0) if h % 2 else jnp.where(upper_q, 0.0, slab))
        q4 = jnp.concatenate(qs, axis=0)
        s = lax.dot_general(q4, kboth, (((1,), (1,)), ((), ())), preferred_element_type=F32) * (HEAD_DIM ** -0.5)
        sink = jnp.concatenate([jnp.where(sink_col, sinks_ref[kv * g + j], -jnp.inf) for j in range(g)], axis=0)
        s = jnp.where(ok, s, sink)
        m = jnp.max(s, -1, keepdims=True)
        p = jnp.exp(s - m)
        p = jnp.where(sink_col4, 0.0, p / jnp.sum(p, -1, keepdims=True))
        for j in range(0, g, 2):
            pp = jnp.concatenate([p[j * WINDOW:(j + 1) * WINDOW], p[(j + 1) * WINDOW:(j + 2) * WINDOW]], axis=1)
            slabs.append(jnp.dot(pp, vpair, preferred_element_type=F32))
    o_ref[...] = jnp.concatenate(slabs, axis=1)


def _swa_prompt(za, sinks):
    l = za.shape[0]
    nb = l // WINDOW
    cur = lambda w, c: pl.BlockSpec((WINDOW, w), lambda i: (i, c))
    prv = lambda w, c: pl.BlockSpec((WINDOW, w), lambda i: (jnp.maximum(i - 1, 0), c))
    kcol, vcol = A_WIDTH // A_KV_WIDTH, A_WIDTH // A_KV_WIDTH + 1
    return pl.pallas_call(
        _swa_prompt_kernel, grid=(nb,), name="swa_prompt",
        in_specs=[pl.BlockSpec(memory_space=pltpu.SMEM),
                  cur(A_WIDTH, 0), cur(A_KV_WIDTH, kcol), prv(A_KV_WIDTH, kcol),
                  cur(A_KV_WIDTH, vcol), prv(A_KV_WIDTH, vcol)],
        out_specs=cur(A_WIDTH, 0),
        out_shape=jax.ShapeDtypeStruct((l, A_WIDTH), F32),
        compiler_params=_cp(("parallel",)))(sinks, za, za, za, za, za)


def _ret_prompt_kernel(cdec_ref, zr_ref, intra_ref, qdec_ref, kdec_ref, o_ref, so_ref, s_scr):
    @pl.when(pl.program_id(0) == 0)
    def _():
        s_scr[...] = jnp.zeros_like(s_scr)

    z = zr_ref[...]
    qk = R_HEADS * R_DK
    q = z[:, :qk]
    k = z[:, qk:2 * qk] * (R_DK ** -0.5)
    v = z[:, 2 * qk:2 * qk + R_WIDTH]
    gate = z[:, 2 * qk + R_WIDTH:]
    qdec = qdec_ref[...]
    kd = k * kdec_ref[...]
    outs = []
    for h in range(R_HEADS):
        sl = slice(h * R_DK, (h + 1) * R_DK)
        qh, kh, vh = q[:, sl], k[:, sl], v[:, sl]
        att = lax.dot_general(qh, kh, (((1,), (1,)), ((), ())), preferred_element_type=F32) * intra_ref[h]
        s = s_scr[h]
        o = jnp.dot(att, vh, preferred_element_type=F32) + jnp.dot(qh, s, preferred_element_type=F32) * qdec[:, sl]
        s_scr[h] = s * cdec_ref[h] + jnp.dot(kd[:, sl].T, vh, preferred_element_type=F32)
        outs.append(o * lax.rsqrt(jnp.mean(o * o, -1, keepdims=True) + EPS))
    o_ref[...] = jnp.concatenate(outs, axis=1) * _silu(gate)
    so_ref[...] = s_scr[...]


def _ret_tables(c):
    log_g = jnp.log1p(-jnp.exp2(-5.0 - jnp.arange(R_HEADS, dtype=F32)))
    idx = jnp.arange(c, dtype=F32)
    diff = idx[:, None] - idx[None, :]
    intra = jnp.where(diff >= 0, jnp.exp(log_g[:, None, None] * jnp.maximum(diff, 0.0)), 0.0)
    q_dec = jnp.exp(log_g[None, :] * (idx[:, None] + 1.0))
    k_dec = jnp.exp(log_g[None, :] * (c - 1.0 - idx[:, None]))
    c_dec = jnp.exp(log_g * c)
    return intra, jnp.repeat(q_dec, R_DK, axis=1), jnp.repeat(k_dec, R_DK, axis=1), c_dec


def _ret_prompt(zr):
    l = zr.shape[0]
    c = R_CHUNK
    intra, qdec, kdec, cdec = _ret_tables(c)
    blk = lambda w: pl.BlockSpec((c, w), lambda i: (i, 0))
    return pl.pallas_call(
        _ret_prompt_kernel, grid=(l // c,), name="ret_prompt",
        in_specs=[pl.BlockSpec(memory_space=pltpu.SMEM), blk(R_COLS),
                  _full((R_HEADS, c, c)), _full((c, R_HEADS * R_DK)), _full((c, R_HEADS * R_DK))],
        out_specs=[blk(R_WIDTH), _full((R_HEADS, R_DK, R_DV))],
        out_shape=[jax.ShapeDtypeStruct((l, R_WIDTH), F32), jax.ShapeDtypeStruct((R_HEADS, R_DK, R_DV), F32)],
        scratch_shapes=[pltpu.VMEM((R_HEADS, R_DK, R_DV), F32)],
        compiler_params=_cp(("arbitrary",)))(cdec, zr, intra, qdec, kdec)


def _rwkv_prep_kernel(zc_ref, prev_ref, shift0_ref, mu_ref, w0_ref, w2_ref, a0_ref, a2_ref, g2_ref,
                      kk_ref, ka_ref, rk_ref,
                      r_ref, w_ref, k_ref, v_ref, a_ref, b_ref, g_ref, bonus_ref, *scratch, sequence, tm):
    zc = zc_ref[...]
    if sequence:
        row = lax.broadcasted_iota(jnp.int32, zc.shape, 0)
        boundary = jnp.where(pl.program_id(0) == 0, shift0_ref[...], prev_ref[SUBLANES - 1:SUBLANES, :])
        prev = jnp.where(row == 0, jnp.broadcast_to(boundary, zc.shape), pltpu.roll(zc, 1, axis=0))
    else:
        prev = prev_ref[...]
    zs = zc + mu_ref[...] * (prev - zc)
    o1, o2, o3 = C_WIDTH, 2 * C_WIDTH, 3 * C_WIDTH
    r, k, v = zs[:, :o1], zs[:, o1:o2], zs[:, o2:o3]
    lora = zs[:, o3:o3 + LANES]
    gl = zs[:, o3 + LANES:]
    w = -jax.nn.softplus(-(w0_ref[...] + jnp.dot(jnp.tanh(lora), w2_ref[...], preferred_element_type=F32))) - 0.5
    decay = jnp.exp(-jnp.exp(w))
    a = _sigmoid(a0_ref[...] + jnp.dot(lora, a2_ref[...], preferred_element_type=F32))
    g = jnp.dot(_sigmoid(gl), g2_ref[...], preferred_element_type=F32)
    kk = k * kk_ref[...]
    kk = kk / jnp.maximum(jnp.sqrt(_seg_sum(kk * kk)), 1e-12)
    k_mod = k * (1.0 + (a - 1.0) * ka_ref[...])
    v_ref[...] = v
    g_ref[...] = g
    bonus_ref[...] = _seg_sum(r * k_mod * rk_ref[...]) * v
    if not sequence:
        r_ref[...] = r
        w_ref[...] = decay
        k_ref[...] = k_mod
        a_ref[...] = -kk
        b_ref[...] = kk * a
        return
    sub =lax.broadcasted_iota(jnp.int32, decay.shape, 0) % SCAN_SUB
    cp = decay
    shift = 1
    while shift < SCAN_SUB:
        cp = jnp.where(sub >= shift, cp * pltpu.roll(cp, shift, axis=0), cp)
        shift *= 2
    cpx = jnp.where(sub == 0, 1.0, pltpu.roll(cp, 1, axis=0))
    inv = 1.0 / cp
    r_ref[...] = r * cp
    k_ref[...] = k_mod * inv
    a_ref[...] = -kk * cpx
    b_ref[...] = kk * a * inv
    for n, scr in enumerate(scratch):
        scr[...] = cp[:, n * LANES:(n + 1) * LANES]
        w_ref[:, n * LANES:(n + 1) * LANES] = scr[pl.ds(SCAN_SUB - 1, tm // SCAN_SUB, stride=SCAN_SUB), :]


def _rwkv_prep(zc, prev, shift0, cw, sequence, tm):
    t = zc.shape[0]
    row = lambda w: pl.BlockSpec((tm, w), lambda i: (i, 0))
    if sequence:
        per = tm // SUBLANES
        prev_spec = pl.BlockSpec((SUBLANES, C_PAD), lambda i: (jnp.maximum(i * per - 1, 0), 0))
        prev = zc
    else:
        prev_spec = row(C_PAD)
    vec = _full((1, C_WIDTH))
    out_specs = [row(C_WIDTH)] * 8
    out_shape = [jax.ShapeDtypeStruct((t, C_WIDTH), F32)] * 8
    if sequence:
        out_specs[1] = pl.BlockSpec((tm // SCAN_SUB, C_WIDTH), lambda i: (i, 0))
        out_shape[1] = jax.ShapeDtypeStruct((t // SCAN_SUB, C_WIDTH), F32)
    return pl.pallas_call(
        functools.partial(_rwkv_prep_kernel, sequence=sequence, tm=tm), grid=(t // tm,), name="rwkv_prep",
        in_specs=[row(C_PAD), prev_spec, _full((1, C_PAD)), _full((1, C_PAD)), vec, _full((LANES, C_WIDTH)),
                  vec, _full((LANES, C_WIDTH)), _full((2 * LANES, C_WIDTH)), vec, vec, vec],
        out_specs=out_specs, out_shape=out_shape,
        scratch_shapes=[pltpu.VMEM((tm, LANES), F32)] * (C_WIDTH // LANES) if sequence else [],
        compiler_params=_cp(("parallel",)))(zc, prev, shift0, cw["mu"], cw["w0"], cw["w2"], cw["a0"], cw["a2"],
                                           cw["g2"], cw["k_k"], cw["k_a"], cw["r_k"])


SCAN_BLOCK = 1024
SCAN_SUB = 16
SCAN_NSUB = SCAN_BLOCK // SCAN_SUB
SCAN_UNROLL = 16
SCAN_OPS = 3
SCAN_GROUP = 4


def _scan_select():
    sel = np.zeros((SCAN_SUB // SCAN_GROUP, LANES, SCAN_GROUP * LANES), np.float32)
    for p in range(SCAN_SUB // SCAN_GROUP):
        for piece in range(3):
            for rh in range(2):
                for tl in range(SCAN_GROUP):
                    row = piece * 2 * SCAN_SUB + rh * SCAN_SUB + SCAN_GROUP * p + tl
                    sel[p, row, tl * LANES + rh * 64: tl * LANES + (rh + 1) * 64] = 1.0
    return jnp.asarray(sel, BF16)


def _stage_lhs(ops, base, lhs_scr):
    for n, ref in enumerate(ops):
        xs = ref[pl.ds(base, SCAN_SUB), :]
        hi = xs.astype(BF16).astype(F32)
        r1 = xs - hi
        mid = r1.astype(BF16).astype(F32)
        lo = (r1 - mid).astype(BF16).astype(F32)
        g = jnp.concatenate([hi[:, :LANES], hi[:, LANES:], mid[:, :LANES], mid[:, LANES:],
                             lo[:, :LANES], lo[:, LANES:], jnp.zeros((LANES - 6 * SCAN_SUB, LANES), F32)], axis=0)
        lhs_scr[n * LANES:(n + 1) * LANES, :] = g.T.astype(BF16)


def _stage_tiles(lhs_scr, sel_ref, xb_scr, p):
    out = jnp.dot(lhs_scr[...], sel_ref[p], preferred_element_type=F32)
    for n in range(SCAN_OPS):
        for tl in range(SCAN_GROUP):
            xb_scr[n, SCAN_GROUP * p + tl] = out[n * LANES:(n + 1) * LANES, tl * LANES:(tl + 1) * LANES]


def _rwkv_scan_kernel(r_ref, pc_ref, k_ref, v_ref, a_ref, b_ref, g_ref, bonus_ref, lnw_ref, lnb_ref, sel_ref,
                      o_ref, so_ref, st_scr, lhs_a, lhs_b, xb_a, xb_b, vt_scr, vb_scr, yt_scr, yb_scr):
    @pl.when(pl.program_id(0) == 0)
    def _():
        st_scr[...] = jnp.zeros_like(st_scr)

    vtop, vbot = _rowpair(v_ref[...])
    vt_scr[...] = vtop
    vb_scr[...] = vbot
    ops = (a_ref, b_ref, k_ref)
    pairs = SCAN_UNROLL // SCAN_GROUP
    _stage_lhs(ops, 0, lhs_a)
    for p in range(SCAN_SUB // SCAN_GROUP):
        _stage_tiles(lhs_a, sel_ref, xb_a, p)
    _stage_lhs(ops, SCAN_SUB, lhs_b)

    def half(sc, st, lhs_cur, xb_cur, lhs_nxt, xb_nxt):
        base = pl.multiple_of(sc * SCAN_SUB, SCAN_SUB)

        def quad(q, st):
            for pp in range(pairs):
                _stage_tiles(lhs_nxt, sel_ref, xb_nxt, q * pairs + pp)
            for uu in range(SCAN_UNROLL):
                u = q * SCAN_UNROLL + uu
                t = base + u
                sa_t, sa_b = _halfsums(st * xb_cur[0, u])
                st = (st + _rowbcast(sa_t, sa_b) * xb_cur[1, u]
                      + _rowbcast(vt_scr[pl.ds(t, 1), :], vb_scr[pl.ds(t, 1), :]) * xb_cur[2, u])
                y_t, y_b = _halfsums(st * _colbcast(r_ref[pl.ds(t, 1), :]))
                yt_scr[pl.ds(t, 1), :] = y_t
                yb_scr[pl.ds(t, 1), :] = y_b
            return st

        st = lax.fori_loop(0, SCAN_SUB // SCAN_UNROLL, quad, st) * _colbcast(pc_ref[pl.ds(sc, 1), :])
        nxt2 = pl.multiple_of(jnp.minimum(sc + 2, SCAN_NSUB - 1) * SCAN_SUB, SCAN_SUB)
        _stage_lhs(ops, nxt2, lhs_cur)
        return st

    def sub2(s2, st):
        st = half(2 * s2, st, lhs_a, xb_a, lhs_b, xb_b)
        return half(2 * s2 + 1, st, lhs_b, xb_b, lhs_a, xb_a)

    st = lax.fori_loop(0, SCAN_NSUB // 2, sub2, st_scr[...])
    st_scr[...] = st
    y = _unpair(yt_scr[...], yb_scr[...])
    o_ref[...] = _wkv_post(y, lnw_ref[...], lnb_ref[...], bonus_ref[...], g_ref[...])
    heads = _t_to_heads(st, transpose=True)
    for h in range(C_HEADS):
        so_ref[h] = heads[h]


def _rwkv_scan(prep, lnw, lnb):
    r, pc, k, v, a, b, g, bonus = prep
    l = r.shape[0]
    blk = pl.BlockSpec((SCAN_BLOCK, C_WIDTH), lambda i: (i, 0))
    vec = _full((1, C_WIDTH))
    sel = _scan_select()
    return pl.pallas_call(
        _rwkv_scan_kernel, grid=(l // SCAN_BLOCK,), name="rwkv_scan",
        in_specs=[blk, pl.BlockSpec((SCAN_NSUB, C_WIDTH), lambda i: (i, 0))] + [blk] * 6 + [vec, vec, _full(sel.shape)],
        out_specs=[blk, _full((C_HEADS, C_HEAD, C_HEAD))],
        out_shape=[jax.ShapeDtypeStruct((l, C_WIDTH), F32), jax.ShapeDtypeStruct((C_HEADS, C_HEAD, C_HEAD), F32)],
        scratch_shapes=[pltpu.VMEM((LANES, LANES), F32),
                        pltpu.VMEM((SCAN_OPS * LANES, LANES), BF16), pltpu.VMEM((SCAN_OPS * LANES, LANES), BF16),
                        pltpu.VMEM((SCAN_OPS, SCAN_SUB, LANES, LANES), F32),
                        pltpu.VMEM((SCAN_OPS, SCAN_SUB, LANES, LANES), F32),
                        pltpu.VMEM((SCAN_BLOCK, LANES), F32), pltpu.VMEM((SCAN_BLOCK, LANES), F32),
                        pltpu.VMEM((SCAN_BLOCK, LANES), F32), pltpu.VMEM((SCAN_BLOCK, LANES), F32)],
        compiler_params=_cp(("arbitrary",)))(r, pc, k, v, a, b, g, bonus, lnw, lnb, sel)


def _sample_kernel(sinks_ref, gam_ref, za_ref, zr_ref, r_ref, w_ref, k_ref, v_ref, a_ref, b_ref, g_ref, bonus_ref,
                   kc_ref, vc_ref, sret_ref, swkv_ref, lnw_ref, lnb_ref,
                   oa_ref, or_ref, oc_ref, kco_ref, vco_ref, sreto_ref, swkvo_ref):
    lane =lax.broadcasted_iota(jnp.int32, (1, LANES), 1)
    row8 = lax.broadcasted_iota(jnp.int32, (SUBLANES, LANES), 0)

    za = za_ref[0]
    q = za[:, :A_WIDTH]
    knew = za[:, A_WIDTH:A_WIDTH + A_KV_WIDTH]
    vnew = za[:, A_WIDTH + A_KV_WIDTH:]
    kc, vc = kc_ref[0], vc_ref[0]
    g = A_HEADS // A_KV_HEADS
    zero = jnp.zeros((1, HEAD_DIM), F32)
    qrows = []
    for h in range(A_HEADS):
        qh = q[:, h * HEAD_DIM:(h + 1) * HEAD_DIM]
        qrows.append(jnp.concatenate([qh, zero] if h // g == 0 else [zero, qh], axis=1))
    qm = jnp.concatenate(qrows + [jnp.zeros((LANES - A_HEADS, LANES), F32)], axis=0)
    scale = HEAD_DIM ** -0.5
    s = lax.dot_general(kc, qm, (((1,), (1,)), ((), ())), preferred_element_type=F32) * scale
    kn8 = jnp.where(row8 == 0, jnp.broadcast_to(knew, (SUBLANES, LANES)), 0.0)
    s_new = lax.dot_general(kn8, qm, (((1,), (1,)), ((), ())), preferred_element_type=F32)[0:1] * scale
    sink = jnp.zeros((1, LANES), F32)
    for h in range(A_HEADS):
        sink = jnp.where(lane == h, sinks_ref[h], sink)
    m = jnp.maximum(jnp.maximum(jnp.max(s, axis=0, keepdims=True), s_new), sink)
    p = jnp.exp(s - m)
    p_new = jnp.exp(s_new - m)
    denom = jnp.sum(p, axis=0, keepdims=True) + p_new + jnp.exp(sink - m)
    p = p / denom
    p_new = p_new / denom
    pn_col = jnp.broadcast_to(p_new, (LANES, LANES)).T[:, 0:1]
    o_full = jnp.dot(p.T, vc, preferred_element_type=F32) + pn_col * vnew
    oa_ref[0] = jnp.concatenate(
        [o_full[h:h + 1, (h // g) * HEAD_DIM:(h // g + 1) * HEAD_DIM] for h in range(A_HEADS)], axis=1)
    rowk = lax.broadcasted_iota(jnp.int32, kc.shape, 0)
    last = rowk == kc.shape[0] - 1
    kco_ref[0] = jnp.where(last, jnp.broadcast_to(knew, kc.shape), pltpu.roll(kc, kc.shape[0] - 1, axis=0))
    vco_ref[0] = jnp.where(last, jnp.broadcast_to(vnew, vc.shape), pltpu.roll(vc, vc.shape[0] - 1, axis=0))

    zr = zr_ref[0]
    qk = R_HEADS * R_DK
    rq = zr[:, :qk]
    rk = zr[:, qk:2 * qk] * (R_DK ** -0.5)
    rv = zr[:, 2 * qk:2 * qk + R_WIDTH]
    rg = zr[:, 2 * qk + R_WIDTH:]
    st = _heads_to_t(sret_ref[0], transpose=False)
    rr = lax.broadcasted_iota(jnp.int32, (LANES, LANES), 0) >= 64
    cc = lax.broadcasted_iota(jnp.int32, (LANES, LANES), 1) >= 64
    gamma = jnp.where(rr, jnp.where(cc, gam_ref[3], gam_ref[1]), jnp.where(cc, gam_ref[2], gam_ref[0]))
    vt, vb = _rowpair(rv)
    st = st * gamma + _colbcast(rk) * _rowbcast(vt, vb)
    o_t, o_b = _halfsums(st * _colbcast(rq))
    o = _unpair(o_t, o_b)
    o = o * lax.rsqrt(_seg_sum(o * o) * (1.0 / R_DV) + EPS)
    or_ref[0] = o * _silu(rg)
    heads = _t_to_heads(st, transpose=False)
    for h in range(R_HEADS):
        sreto_ref[0, h] = heads[h]

    st = _heads_to_t(swkv_ref[0], transpose=True)
    vt, vb = _rowpair(v_ref[0])
    st, y_t, y_b = _wkv_step(st, _colbcast(a_ref[0]), _colbcast(w_ref[0]), _colbcast(b_ref[0]),
                             _colbcast(k_ref[0]), _colbcast(r_ref[0]), vt, vb)
    oc_ref[0] = _wkv_post(_unpair(y_t, y_b), lnw_ref[...], lnb_ref[...], bonus_ref[0], g_ref[0])
    heads = _t_to_heads(st, transpose=True)
    for h in range(C_HEADS):
        swkvo_ref[0, h] = heads[h]


def _sample_mixers(za, zr, prep, kc, vc, sret, swkv, sinks, lnw, lnb):
    b = za.shape[0]
    wb = kc.shape[1]
    assert wb <= WINDOW and PAST_LEN >= wb
    gam = jnp.exp(jnp.log1p(-jnp.exp2(-5.0 - jnp.arange(R_HEADS, dtype=F32))) * 1.0)
    tok = lambda w: pl.BlockSpec((1, 1, w), lambda i: (i, 0, 0))
    cache = pl.BlockSpec((1, wb, A_KV_WIDTH), lambda i: (i, 0, 0))
    state = pl.BlockSpec((1, 4, 64, 64), lambda i: (i, 0, 0, 0))
    smem = pl.BlockSpec(memory_space=pltpu.SMEM)
    vec = _full((1, C_WIDTH))
    r3 = lambda x: x.reshape(b, 1, x.shape[-1])
    outs = pl.pallas_call(
        _sample_kernel, grid=(b,), name="sample_mixers",
        in_specs=[smem, smem, tok(A_COLS), tok(R_COLS)] + [tok(C_WIDTH)] * 8 + [cache, cache, state, state,
                  vec, vec],
        out_specs=[tok(A_WIDTH), tok(R_WIDTH), tok(C_WIDTH), cache, cache, state, state],
        out_shape=[jax.ShapeDtypeStruct((b, 1, A_WIDTH), F32), jax.ShapeDtypeStruct((b, 1, R_WIDTH), F32),
                   jax.ShapeDtypeStruct((b, 1, C_WIDTH), F32), jax.ShapeDtypeStruct(kc.shape, F32),
                   jax.ShapeDtypeStruct(vc.shape, F32), jax.ShapeDtypeStruct(sret.shape, F32),
                   jax.ShapeDtypeStruct(swkv.shape, F32)],
        compiler_params=_cp(("parallel",)))(sinks, gam, r3(za), r3(zr), *[r3(x) for x in prep],
                                           kc, vc, sret, swkv, lnw, lnb)
    oa, orr, oc = (x.reshape(b, x.shape[-1]) for x in outs[:3])
    return (oa, orr, oc) + tuple(outs[3:])


def _outproj_kernel(oa_ref, or_ref, oc_ref, x_ref, wo_ref, gf_ref, wrt_ref, brt_ref, *rest, tm):
    x1_ref, h2_ref, route_ref = rest[-3:]
    mix =jnp.concatenate([oa_ref[...], or_ref[...], oc_ref[...]], axis=1).astype(BF16)
    x1 = x_ref[...] + jnp.dot(mix, wo_ref[...], preferred_element_type=F32)
    x1_ref[...] = x1
    h2 = _rmsnorm(x1, gf_ref[...])
    for s in range(ROW_TILES):
        h2_ref[pl.ds(s, tm, stride=ROW_TILES), :] = h2[:, s * LANES:(s + 1) * LANES]
    h_hi = h2.astype(BF16)
    h_lo = (h2 - h_hi.astype(F32)).astype(BF16)
    w_hi, w_lo = wrt_ref[0], wrt_ref[1]
    logits = (jnp.dot(h_hi, w_hi, preferred_element_type=F32) + jnp.dot(h_lo, w_hi, preferred_element_type=F32)
              + jnp.dot(h_hi, w_lo, preferred_element_type=F32)) + brt_ref[...]
    lane = lax.broadcasted_iota(jnp.int32, logits.shape, 1).astype(F32)
    big = float(LANES)
    is_g = lane < N_GROUPS
    lg = jnp.where(is_g, logits, -jnp.inf)
    m_g = jnp.max(lg, -1, keepdims=True)
    grp = jnp.min(jnp.where(lg == m_g, lane, big), -1, keepdims=True)
    p_grp = 1.0 / jnp.sum(jnp.where(is_g, jnp.exp(lg - m_g), 0.0), -1, keepdims=True)
    lo = N_GROUPS + EXPERTS_PER_GROUP * grp
    in_grp = (lane >= lo) & (lane < lo + EXPERTS_PER_GROUP)
    le = jnp.where(in_grp, logits, -jnp.inf)
    l1 = jnp.max(le, -1, keepdims=True)
    i1 = jnp.min(jnp.where(le == l1, lane, big), -1, keepdims=True)
    le2 = jnp.where(lane == i1, -jnp.inf, le)
    l2 = jnp.max(le2, -1, keepdims=True)
    i2 = jnp.min(jnp.where(le2 == l2, lane, big), -1, keepdims=True)
    e = jnp.exp(l2 - l1)
    g1 = p_grp / (1.0 + e)
    g2 = p_grp * e / (1.0 + e)
    route = jnp.where(lane == 0, i1 - N_GROUPS, jnp.where(lane == 1, i2 - N_GROUPS,
                      jnp.where(lane == 2, g1, jnp.where(lane == 3, g2, 0.0))))
    route_ref[...] = route


def _outproj(oa, orr, oc, x, wo, gf, wrt, brt, tm, t_all, first_tok, shared=None):
    t = x.shape[0]
    off = first_tok // tm
    nblk = t // tm
    steps = nblk + (1 if shared is None else 0)
    assert shared is not None or t_all == t + tm
    row = lambda w: pl.BlockSpec((tm, w), lambda i: (jnp.minimum(i, nblk - 1), 0))
    in_specs = [row(A_WIDTH), row(R_WIDTH), row(C_WIDTH), row(D_MODEL), _full(wo.shape), _full((1, D_MODEL)),
                _full(wrt.shape), _full((1, LANES))]
    args = [oa, orr, oc, x, wo, gf, wrt, brt]
    aliases = {}
    if shared is not None:
        aliases = {len(args): 1, len(args) + 1: 2}
        in_specs += [pl.BlockSpec(memory_space=pl.ANY)] * 2
        args += list(shared)
    return pl.pallas_call(
        functools.partial(_outproj_kernel, tm=tm), grid=(steps,), name="outproj_router",
        in_specs=in_specs,
        out_specs=[row(D_MODEL), pl.BlockSpec((tm * ROW_TILES, LANES), lambda i: (i + off, 0)),
                   pl.BlockSpec((tm, LANES), lambda i: (i + off, 0))],
        out_shape=[jax.ShapeDtypeStruct((t, D_MODEL), F32), jax.ShapeDtypeStruct((t_all * ROW_TILES, LANES), F32),
                   jax.ShapeDtypeStruct((t_all, LANES), F32)],
        input_output_aliases=aliases,
        compiler_params=_cp(("arbitrary",)))(*args)


def _moe_kernel(blk_exp_ref, src_ref, dst_ref, grp_ref, nused_ref, h2_hbm, wg_ref, wu_ref, wd_ref, yw_hbm,
                gbuf, sbuf, xbuf, wgb, wub, wdb, gsem, ssem):
    i = pl.program_id(0)
    nb = pl.num_programs(0)
    nused = nused_ref[0]
    slot = i % 2

    def tile(off):
        return pl.ds(pl.multiple_of(off, ROW_TILES), ROW_TILES)

    def rows(blk, fn):
        def body(g, _):
            for j in range(DMA_UNROLL):
                fn(g * DMA_UNROLL + j)
            return 0
        lax.fori_loop(0, grp_ref[blk], body, 0)

    def gather(blk, sl):
        rows(blk, lambda r: pltpu.make_async_copy(h2_hbm.at[tile(src_ref[blk * MOE_ROWS + r]), :],
                                                  gbuf.at[sl, tile(r * ROW_TILES), :], gsem.at[sl]).start())

    def gather_wait(blk, sl):
        rows(blk, lambda r: pltpu.make_async_copy(h2_hbm.at[tile(0), :], gbuf.at[sl, tile(0), :],
                                                  gsem.at[sl]).wait())

    def scatter(blk):
        rows(blk, lambda r: pltpu.make_async_copy(sbuf.at[tile(r * ROW_TILES), :],
                                                  yw_hbm.at[tile(dst_ref[blk * MOE_ROWS + r]), :], ssem).start())

    def scatter_wait(blk):
        rows(blk, lambda r: pltpu.make_async_copy(sbuf.at[tile(0), :], yw_hbm.at[tile(0), :], ssem).wait())

    @pl.when((i == 0) & (nused > 0))
    def _():
        gbuf[...] = jnp.zeros_like(gbuf)
        gather(0, 0)

    @pl.when(i < nused)
    def _():
        @pl.when((i == 0) | (blk_exp_ref[i] != blk_exp_ref[jnp.maximum(i - 1, 0)]))
        def _():
            wgb[...] = wg_ref[0].astype(BF16)
            wub[...] = wu_ref[0].astype(BF16)
            wdb[...] = wd_ref[0].astype(BF16)

        gather_wait(i, slot)

        @pl.when(i + 1 < nused)
        def _():
            gather(i + 1, 1 - slot)

        for s in range(ROW_TILES):
            xbuf[:, s * LANES:(s + 1) * LANES] = gbuf[slot, pl.ds(s, MOE_ROWS, stride=ROW_TILES), :].astype(BF16)
        x = xbuf[...]
        hid = _silu(jnp.dot(x, wgb[...], preferred_element_type=F32)) * jnp.dot(x, wub[...], preferred_element_type=F32)
        y = jnp.dot(hid.astype(BF16), wdb[...], preferred_element_type=F32)

        @pl.when(i > 0)
        def _():
            scatter_wait(i - 1)

        for s in range(ROW_TILES):
            sbuf[pl.ds(s, MOE_ROWS, stride=ROW_TILES), :] = y[:, s * LANES:(s + 1) * LANES]
        scatter(i)

    @pl.when((i == nb - 1) & (nused > 0))
    def _():
        scatter_wait(nused - 1)
        spare = pltpu.make_async_copy(sbuf, yw_hbm.at[pl.ds(yw_hbm.shape[0] - MOE_ROWS * ROW_TILES,
                                                            MOE_ROWS * ROW_TILES), :], ssem)
        spare.start()
        spare.wait()


def _moe(h2, route, t, wg, wu, wd):
    na = t * TOP_K
    assert na < (1 << IDX_BITS)
    flat_e = route[:t, :TOP_K].astype(jnp.int32).reshape(na)
    experts = jnp.arange(N_EXPERTS, dtype=jnp.int32)
    counts = jnp.sum(flat_e[:, None] == experts[None, :], axis=0, dtype=jnp.int32)
    npad = (-counts) % MOE_ROWS
    real_keys = (flat_e << (IDX_BITS + 1)) | jnp.arange(na, dtype=jnp.int32)
    q = jnp.arange(MOE_ROWS, dtype=jnp.int32)[None, :]
    pad_keys = jnp.where(q < npad[:, None], (experts[:, None] << (IDX_BITS + 1)) | (1 << IDX_BITS) | q,
                         jnp.iinfo(jnp.int32).max)
    n_blocks = -(-na // MOE_ROWS) + N_EXPERTS
    n_rows = n_blocks * MOE_ROWS
    filler = jnp.full((n_rows - na - N_EXPERTS * MOE_ROWS,), jnp.iinfo(jnp.int32).max, jnp.int32)
    keys = jnp.sort(jnp.concatenate([real_keys, pad_keys.reshape(-1), filler]))
    is_pad = ((keys >> IDX_BITS) & 1) == 1
    idx = keys & ((1 << IDX_BITS) - 1)
    pos = jnp.arange(n_rows, dtype=jnp.int32)
    src = jnp.where(is_pad, 0, (idx // TOP_K) * ROW_TILES).astype(jnp.int32)
    dst = jnp.where(is_pad, na + pos % MOE_ROWS, idx).astype(jnp.int32) * ROW_TILES
    blk_exp = jnp.minimum(keys[::MOE_ROWS] >> (IDX_BITS + 1), N_EXPERTS - 1).astype(jnp.int32)
    nused = (jnp.sum(counts + npad) // MOE_ROWS).astype(jnp.int32).reshape(1)
    real_rows = jnp.sum((~is_pad).reshape(n_blocks, MOE_ROWS), axis=1, dtype=jnp.int32)
    grp = (real_rows + DMA_UNROLL - 1) // DMA_UNROLL
    wspec = lambda a, b: pl.BlockSpec((1, a, b), lambda i, be, s, d, gr, nu: (be[i], 0, 0))
    return pl.pallas_call(
        _moe_kernel, name="moe_experts",
        grid_spec=pltpu.PrefetchScalarGridSpec(
            num_scalar_prefetch=5, grid=(n_blocks,),
            in_specs=[pl.BlockSpec(memory_space=pl.ANY), wspec(D_MODEL, EXPERT_FF), wspec(D_MODEL, EXPERT_FF),
                      wspec(EXPERT_FF, D_MODEL)],
            out_specs=pl.BlockSpec(memory_space=pl.ANY),
            scratch_shapes=[pltpu.VMEM((2, MOE_ROWS * ROW_TILES, LANES), F32),
                            pltpu.VMEM((MOE_ROWS * ROW_TILES, LANES), F32),
                            pltpu.VMEM((MOE_ROWS, D_MODEL), BF16),
                            pltpu.VMEM((D_MODEL, EXPERT_FF), BF16), pltpu.VMEM((D_MODEL, EXPERT_FF), BF16),
                            pltpu.VMEM((EXPERT_FF, D_MODEL), BF16),
                            pltpu.SemaphoreType.DMA((2,)), pltpu.SemaphoreType.DMA(())]),
        out_shape=jax.ShapeDtypeStruct(((na + MOE_ROWS) * ROW_TILES, LANES), F32),
        compiler_params=_cp(("arbitrary",)))(blk_exp, src, dst, grp, nused, h2, wg, wu, wd)


def _rope_tables(pos):
    half = HEAD_DIM // 2
    inv = ROPE_THETA ** (-jnp.arange(half, dtype=F32) / half)
    ang = pos.astype(F32)[:, None] * inv[None, :]
    c, s = jnp.cos(ang), jnp.sin(ang)
    return jnp.concatenate([c, c, c, c], axis=1), jnp.concatenate([-s, s, -s, s], axis=1)


def _pad_cols(w, n):
    return jnp.pad(w, ((0, 0), (0, n - w.shape[1])))


def _pad_rows(w, lo, n):
    return jnp.pad(w, ((lo, n - lo - w.shape[0]), (0, 0)))


def kernel(x_prompt, x_sample, cache_swa_k, cache_swa_v, state_ret, state_wkv, state_shift, norm_mix, w_in, sinks,
           rwkv_mu, rwkv_w0, rwkv_w2, rwkv_a0, rwkv_a2, rwkv_g2, rwkv_k_k, rwkv_k_a, rwkv_r_k, rwkv_ln_w, rwkv_ln_b,
           w_out, norm_ffn, router_g, router_g_b, router_e, router_e_b, expert_w_gate, expert_w_up, expert_w_down,
           norm_final):
    lp = x_prompt.shape[1]
    bs = x_sample.shape[0]
    wb = cache_swa_k.shape[2]
    tm_p = 512
    xp = x_prompt.reshape(lp, D_MODEL)
    xs = x_sample.reshape(bs, D_MODEL)
    cos_p, sin_p = _rope_tables(jnp.arange(lp, dtype=jnp.int32))
    cos_s, sin_s = (jnp.broadcast_to(t, (bs, LANES))
                    for t in _rope_tables(PAST_LEN + jnp.arange(x_sample.shape[1], dtype=jnp.int32)))
    outs_p = [[] for _ in range(5)]
    outs_s = [[] for _ in range(5)]
    moe_p = moe_s = None
    row = lambda v: v.reshape(1, -1)
    for i in range(DEPTH):
        wi = w_in[i].astype(BF16)
        wa, wr, wc = wi[:, :A_COLS], wi[:, A_COLS:A_COLS + R_COLS], _pad_cols(wi[:, A_COLS + R_COLS:], C_PAD)
        cw = {
            "mu": _pad_cols(row(rwkv_mu[i]), C_PAD), "w0": row(rwkv_w0[i]), "a0": row(rwkv_a0[i]),
            "w2": _pad_rows(rwkv_w2[i], 0, LANES), "a2": _pad_rows(rwkv_a2[i], C_DECAY_LORA, LANES),
            "g2": _pad_rows(rwkv_g2[i], 0, 2 * LANES),
            "k_k": row(rwkv_k_k[i]), "k_a": row(rwkv_k_a[i]), "r_k": row(rwkv_r_k[i]),
        }
        lnw, lnb = row(rwkv_ln_w[i]), row(rwkv_ln_b[i])
        g_mix, g_ffn = row(norm_mix[i]), row(norm_ffn[i])
        wo = w_out[i].astype(BF16)
        wrt = _pad_cols(jnp.concatenate([router_g[i], router_e[i]], axis=1), LANES)
        wrt_hi = wrt.astype(BF16)
        wrt = jnp.stack([wrt_hi, (wrt - wrt_hi.astype(F32)).astype(BF16)])
        brt = _pad_cols(row(jnp.concatenate([router_g_b[i], router_e_b[i]])), LANES)

        xn, za, zr, zc = _inproj(xp, moe_p, g_mix, wa, wr, wc, cos_p, sin_p, tm_p)
        xp = xp if xn is None else xn
        oa = _swa_prompt(za, sinks[i])
        orr, s_ret = _ret_prompt(zr)
        prep = _rwkv_prep(zc, None, jnp.zeros((1, C_PAD), F32), cw, True, tm_p)
        oc, s_wkv = _rwkv_scan(prep, lnw, lnb)
        xp, h2, route = _outproj(oa, orr, oc, xp, wo, g_ffn, wrt, brt, tm_p, lp + tm_p, 0)
        nk = min(WINDOW, lp)
        outs_p[0].append(za[lp - nk:, A_WIDTH:A_WIDTH + A_KV_WIDTH].reshape(1, nk, A_KV_HEADS, HEAD_DIM))
        outs_p[1].append(za[lp - nk:, A_WIDTH + A_KV_WIDTH:].reshape(1, nk, A_KV_HEADS, HEAD_DIM))
        outs_p[2].append(s_ret[None])
        outs_p[3].append(s_wkv[None])
        outs_p[4].append(zc[lp - 1:, :C_COLS])

        xn, za, zr, zc = _inproj(xs, moe_s, g_mix, wa, wr, wc, cos_s, sin_s, bs)
        xs = xs if xn is None else xn
        prep = _rwkv_prep(zc, _pad_cols(state_shift[i], C_PAD), jnp.zeros((1, C_PAD), F32), cw, False, bs)
        oa, orr, oc, kc_new, vc_new, sret_new, swkv_new = _sample_mixers(
            za, zr, prep, cache_swa_k[i].reshape(bs, wb, A_KV_WIDTH), cache_swa_v[i].reshape(bs, wb, A_KV_WIDTH),
            state_ret[i], state_wkv[i], sinks[i], lnw, lnb)
        xs, h2, route = _outproj(oa, orr, oc, xs, wo, g_ffn, wrt, brt, bs, lp + tm_p, lp, shared=(h2, route))
        yw = _moe(h2, route, lp + bs, expert_w_gate[i], expert_w_up[i], expert_w_down[i])
        moe_p, moe_s = (yw, route, 0), (yw, route, lp)
        outs_s[0].append(kc_new.reshape(bs, wb, A_KV_HEADS, HEAD_DIM))
        outs_s[1].append(vc_new.reshape(bs, wb, A_KV_HEADS, HEAD_DIM))
        outs_s[2].append(sret_new)
        outs_s[3].append(swkv_new)
        outs_s[4].append(zc[:, :C_COLS])

    gfin = row(norm_final)
    y_prompt = _final(xp, moe_p, gfin, tm_p).reshape(x_prompt.shape)
    y_sample = _final(xs, moe_s, gfin, bs).reshape(x_sample.shape)
    sp = [jnp.stack(o) for o in outs_p]
    ss = [jnp.stack(o) for o in outs_s]
    return (y_prompt, y_sample, sp[0], sp[1], sp[2], sp[3], sp[4], ss[0], ss[1], ss[2], ss[3], ss[4])
```

```python
import functools

import numpy as np
import jax
import jax.numpy as jnp
from jax import lax
from jax.experimental import pallas as pl
from jax.experimental.pallas import tpu as pltpu

F32 = jnp.float32
BF16 = jnp.bfloat16

D_MODEL = 1024
DEPTH = 2
PAST_LEN = 16384
A_HEADS, A_KV_HEADS, HEAD_DIM, WINDOW = 8, 2, 64, 128
ROPE_THETA = 10000.0
A_WIDTH = A_HEADS * HEAD_DIM
A_KV_WIDTH = A_KV_HEADS * HEAD_DIM
A_COLS = A_WIDTH + 2 * A_KV_WIDTH
R_HEADS, R_DK, R_DV, R_CHUNK = 4, 64, 64, 128
R_WIDTH = R_HEADS * R_DV
R_COLS = 2 * R_HEADS * R_DK + 2 * R_WIDTH
C_HEADS, C_HEAD = 4, 64
C_WIDTH = C_HEADS * C_HEAD
C_DECAY_LORA, C_ICLR_LORA, C_GATE_LORA = 64, 64, 160
C_COLS = 3 * C_WIDTH + C_DECAY_LORA + C_ICLR_LORA + C_GATE_LORA
C_PAD = 1152
C_GN_EPS = 64e-5
N_GROUPS, EXPERTS_PER_GROUP, TOP_K, EXPERT_FF = 4, 8, 2, 512
N_EXPERTS = N_GROUPS * EXPERTS_PER_GROUP
EPS = 1e-6

LANES = 128
SUBLANES = 8
ROW_TILES = D_MODEL // LANES
MOE_ROWS = 256
DMA_UNROLL = 8
IDX_BITS = 19
VMEM_LIMIT = 56 * 1024 * 1024


def _cp(sem, vmem=VMEM_LIMIT):
    return pltpu.CompilerParams(dimension_semantics=sem, vmem_limit_bytes=vmem)


def _full(shape):
    n = len(shape)
    return pl.BlockSpec(shape, lambda *_: (0,) * n)


def _rmsnorm(x, g):
    return x * lax.rsqrt(jnp.mean(x * x, -1, keepdims=True) + EPS) * g


def _silu(x):
    return x * (1.0 / (1.0 + jnp.exp(-x)))


def _sigmoid(x):
    return 1.0 / (1.0 + jnp.exp(-x))


def _rope(x, cos, sin_signed):
    w = x.shape[-1]
    lane = lax.broadcasted_iota(jnp.int32, x.shape, x.ndim - 1)
    first = (lane % HEAD_DIM) < (HEAD_DIM // 2)
    swapped = jnp.where(first, pltpu.roll(x, w - HEAD_DIM // 2, axis=x.ndim - 1),
                        pltpu.roll(x, HEAD_DIM // 2, axis=x.ndim - 1))
    return x * cos + swapped * sin_signed


def _tile_lanes(x, n):
    return jnp.concatenate([x] * n, axis=-1) if n > 1 else x


def _seg_sum(x, seg=64):
    outs = []
    for h in range(x.shape[-1] // seg):
        s = jnp.sum(x[:, h * seg:(h + 1) * seg], axis=-1, keepdims=True)
        outs.append(jnp.broadcast_to(s, (x.shape[0], seg)))
    return jnp.concatenate(outs, axis=-1)


def _read_rows(ref, tm, first, stride):
    return jnp.concatenate([ref[pl.ds(first + s, tm, stride=stride), :] for s in range(ROW_TILES)], axis=1)


def _colbcast(x):
    z = jnp.concatenate([jnp.broadcast_to(x[:, :LANES], (64, LANES)),
                         jnp.broadcast_to(x[:, LANES:], (64, LANES))], axis=0)
    return z.T


def _rowpair(x):
    top = jnp.concatenate([x[:, 0:64], x[:, 128:192]], axis=1)
    bot = jnp.concatenate([x[:, 64:128], x[:, 192:256]], axis=1)
    return top, bot


def _unpair(top, bot):
    return jnp.concatenate([top[:, :64], bot[:, :64], top[:, 64:], bot[:, 64:]], axis=1)


def _rowbcast(top, bot):
    return jnp.concatenate([jnp.broadcast_to(top, (64, LANES)), jnp.broadcast_to(bot, (64, LANES))], axis=0)


def _halfsums(p):
    return jnp.sum(p[:64], axis=0, keepdims=True), jnp.sum(p[64:], axis=0, keepdims=True)


def _wkv_step(st, ab, wb, bb, kb, rb, vtop, vbot):
    sa_t, sa_b = _halfsums(st * ab)
    st = st * wb + _rowbcast(sa_t, sa_b) * bb + _rowbcast(vtop, vbot) * kb
    y_t, y_b = _halfsums(st * rb)
    return st, y_t, y_b


def _wkv_post(y, lnw, lnb, bonus, g):
    mu = _seg_sum(y) * (1.0 / C_HEAD)
    d = y - mu
    var = _seg_sum(d * d) * (1.0 / C_HEAD)
    yn = d * lax.rsqrt(var + C_GN_EPS)
    return (yn * lnw + lnb + bonus) * g


def _heads_to_t(s4, transpose):
    if transpose:
        m = jnp.concatenate([jnp.concatenate([s4[0], s4[1]], axis=1),
                             jnp.concatenate([s4[2], s4[3]], axis=1)], axis=0)
        return m.T
    return jnp.concatenate([jnp.concatenate([s4[0], s4[2]], axis=1),
                            jnp.concatenate([s4[1], s4[3]], axis=1)], axis=0)


def _t_to_heads(st, transpose):
    if transpose:
        m = st.T
        return [m[0:64, 0:64], m[0:64, 64:128], m[64:128, 0:64], m[64:128, 64:128]]
    return [st[0:64, 0:64], st[64:128, 0:64], st[0:64, 64:128], st[64:128, 64:128]]


def _inproj_kernel(*refs, combine, tm):
    if combine:
        x_ref, yw_ref, route_ref, g_ref, wa_ref, wr_ref, wc_ref, cos_ref, sin_ref, xo_ref, za_ref, zr_ref, zc_ref = refs
        route = route_ref[...]
        y0 = _read_rows(yw_ref, tm, 0, 2 * ROW_TILES)
        y1 = _read_rows(yw_ref, tm, ROW_TILES, 2 * ROW_TILES)
        x = x_ref[...] + (route[:, 2:3] * y0 + route[:, 3:4] * y1)
        xo_ref[...] = x
    else:
        x_ref, g_ref, wa_ref, wr_ref, wc_ref, cos_ref, sin_ref, za_ref, zr_ref, zc_ref = refs
        x = x_ref[...]
    h = _rmsnorm(x, g_ref[...]).astype(BF16)
    cos, sin = cos_ref[...], sin_ref[...]
    na = (A_WIDTH + A_KV_WIDTH) // LANES
    nr = 2 * R_HEADS * R_DK // LANES
    za = jnp.dot(h, wa_ref[...], preferred_element_type=F32)
    za_ref[:, :na * LANES] = _rope(za[:, :na * LANES], _tile_lanes(cos, na), _tile_lanes(sin, na))
    za_ref[:, na * LANES:] = za[:, na * LANES:]
    zr = jnp.dot(h, wr_ref[...], preferred_element_type=F32)
    zr_ref[:, :nr * LANES] = _rope(zr[:, :nr * LANES], _tile_lanes(cos, nr), _tile_lanes(sin, nr))
    zr_ref[:, nr * LANES:] = zr[:, nr * LANES:]
    zc_ref[...] = jnp.dot(h, wc_ref[...], preferred_element_type=F32)


def _inproj(x, moe, g, wa, wr, wc, cos, sin, tm):
    t = x.shape[0]
    row = lambda w: pl.BlockSpec((tm, w), lambda i: (i, 0))
    wspecs = [_full((1, D_MODEL)), _full(wa.shape), _full(wr.shape), _full(wc.shape), row(LANES), row(LANES)]
    zshapes = [jax.ShapeDtypeStruct((t, A_COLS), F32), jax.ShapeDtypeStruct((t, R_COLS), F32),
               jax.ShapeDtypeStruct((t, C_PAD), F32)]
    zspecs = [row(A_COLS), row(R_COLS), row(C_PAD)]
    if moe is None:
        return (None,) + tuple(pl.pallas_call(
            functools.partial(_inproj_kernel, combine=False, tm=tm), name="inproj",
            grid=(t // tm,), in_specs=[row(D_MODEL)] + wspecs, out_specs=zspecs, out_shape=zshapes,
            compiler_params=_cp(("parallel",)))(x, g, wa, wr, wc, cos, sin))
    yw, route, first_tok = moe
    off = first_tok // tm
    return pl.pallas_call(
        functools.partial(_inproj_kernel, combine=True, tm=tm), name="combine_inproj",
        grid=(t // tm,),
        in_specs=[row(D_MODEL), pl.BlockSpec((tm * 2 * ROW_TILES, LANES), lambda i: (i + off, 0)),
                  pl.BlockSpec((tm, LANES), lambda i: (i + off, 0))] + wspecs,
        out_specs=[row(D_MODEL)] + zspecs,
        out_shape=[jax.ShapeDtypeStruct((t, D_MODEL), F32)] + zshapes,
        compiler_params=_cp(("parallel",)))(x, yw, route, g, wa, wr, wc, cos, sin)


def _final_kernel(x_ref, yw_ref, route_ref, g_ref, o_ref, *, tm):
    route = route_ref[...]
    y0 = _read_rows(yw_ref, tm, 0, 2 * ROW_TILES)
    y1 = _read_rows(yw_ref, tm, ROW_TILES, 2 * ROW_TILES)
    x = x_ref[...] + (route[:, 2:3] * y0 + route[:, 3:4] * y1)
    o_ref[...] = _rmsnorm(x, g_ref[...])


def _final(x, moe, g, tm):
    t = x.shape[0]
    yw, route, first_tok = moe
    off = first_tok // tm
    row = lambda w: pl.BlockSpec((tm, w), lambda i: (i, 0))
    return pl.pallas_call(
        functools.partial(_final_kernel, tm=tm), grid=(t // tm,), name="combine_final",
        in_specs=[row(D_MODEL), pl.BlockSpec((tm * 2 * ROW_TILES, LANES), lambda i: (i + off, 0)),
                  pl.BlockSpec((tm, LANES), lambda i: (i + off, 0)), _full((1, D_MODEL))],
        out_specs=row(D_MODEL), out_shape=jax.ShapeDtypeStruct((t, D_MODEL), F32),
        compiler_params=_cp(("parallel",)))(x, yw, route, g)


def _swa_prompt_kernel(sinks_ref, q_ref, kc_ref, kp_ref, vc_ref, vp_ref, o_ref):
    i = pl.program_id(0)
    g = A_HEADS // A_KV_HEADS
    nq = g * WINDOW
    kcat = jnp.concatenate([kp_ref[...], kc_ref[...]], axis=0)
    vcat = jnp.concatenate([vp_ref[...], vc_ref[...]], axis=0)
    kswap = pltpu.roll(kcat, HEAD_DIM, axis=1)
    vswap = pltpu.roll(vcat, HEAD_DIM, axis=1)
    upper_k = lax.broadcasted_iota(jnp.int32, kcat.shape, 1) >= HEAD_DIM
    upper_q = lax.broadcasted_iota(jnp.int32, (WINDOW, LANES), 1) >= HEAD_DIM
    r = lax.broadcasted_iota(jnp.int32, (nq, 2 * WINDOW), 0) % WINDOW
    c = lax.broadcasted_iota(jnp.int32, (nq, 2 * WINDOW), 1)
    ok = (c >= r) & (c <= r + WINDOW) & ((c >= WINDOW) | (i > 0))
    sink_col4 = c == (r + WINDOW + 1) % (2 * WINDOW)
    sink_col = sink_col4[:WINDOW]
    slabs = []
    for kv in range(A_KV_HEADS):
        kboth = jnp.where(upper_k, kswap, kcat) if kv == 0 else jnp.where(upper_k, kcat, kswap)
        v_lo = jnp.where(upper_k, 0.0, vcat if kv == 0 else vswap)
        v_hi = jnp.where(upper_k, vswap if kv == 0 else vcat, 0.0)
        vpair = jnp.concatenate([v_lo, v_hi], axis=0)
        qs = []
        for j in range(g):
            h = kv * g + j
            slab = q_ref[:, (h // 2) * LANES:(h // 2 + 1) * LANES]
            qs.append(jnp.where(upper_q, slab, 0.0) if h % 2 else jnp.where(upper_q, 0.0, slab))
        q4 = jnp.concatenate(qs, axis=0)
        s = lax.dot_general(q4, kboth, (((1,), (1,)), ((), ())), preferred_element_type=F32) * (HEAD_DIM ** -0.5)
        sink = jnp.concatenate([jnp.where(sink_col, sinks_ref[kv * g + j], -jnp.inf) for j in range(g)], axis=0)
        s = jnp.where(ok, s, sink)
        m = jnp.max(s, -1, keepdims=True)
        p = jnp.exp(s - m)
        p = jnp.where(sink_col4, 0.0, p / jnp.sum(p, -1, keepdims=True))
        for j in range(0, g, 2):
            pp = jnp.concatenate([p[j * WINDOW:(j + 1) * WINDOW], p[(j + 1) * WINDOW:(j + 2) * WINDOW]], axis=1)
            slabs.append(jnp.dot(pp, vpair, preferred_element_type=F32))
    o_ref[...] = jnp.concatenate(slabs, axis=1)


def _swa_prompt(za, sinks):
    l = za.shape[0]
    nb = l // WINDOW
    cur = lambda w, c: pl.BlockSpec((WINDOW, w), lambda i: (i, c))
    prv = lambda w, c: pl.BlockSpec((WINDOW, w), lambda i: (jnp.maximum(i - 1, 0), c))
    kcol, vcol = A_WIDTH // A_KV_WIDTH, A_WIDTH // A_KV_WIDTH + 1
    return pl.pallas_call(
        _swa_prompt_kernel, grid=(nb,), name="swa_prompt",
        in_specs=[pl.BlockSpec(memory_space=pltpu.SMEM),
                  cur(A_WIDTH, 0), cur(A_KV_WIDTH, kcol), prv(A_KV_WIDTH, kcol),
                  cur(A_KV_WIDTH, vcol), prv(A_KV_WIDTH, vcol)],
        out_specs=cur(A_WIDTH, 0),
        out_shape=jax.ShapeDtypeStruct((l, A_WIDTH), F32),
        compiler_params=_cp(("parallel",)))(sinks, za, za, za, za, za)


RET_STEP_CHUNKS = 4


def _ret_prompt_kernel(cdec_ref, zr_ref, intra_ref, qdec_ref, kdec_ref, o_ref, so_ref, s_scr):
    @pl.when(pl.program_id(0) == 0)
    def _():
        s_scr[...] = jnp.zeros_like(s_scr)

    qk = R_HEADS * R_DK
    qdec = qdec_ref[...]
    kdec = kdec_ref[...]
    for ci in range(RET_STEP_CHUNKS):
        rows = slice(ci * R_CHUNK, (ci + 1) * R_CHUNK)
        z = zr_ref[rows, :]
        q = z[:, :qk]
        k = z[:, qk:2 * qk] * (R_DK ** -0.5)
        v = z[:, 2 * qk:2 * qk + R_WIDTH]
        gate = z[:, 2 * qk + R_WIDTH:]
        kd = k * kdec
        outs = []
        for h in range(R_HEADS):
            sl = slice(h * R_DK, (h + 1) * R_DK)
            qh, kh, vh = q[:, sl], k[:, sl], v[:, sl]
            att = lax.dot_general(qh, kh, (((1,), (1,)), ((), ())), preferred_element_type=F32) * intra_ref[h]
            s = s_scr[h]
            o = jnp.dot(att, vh, preferred_element_type=F32) + jnp.dot(qh, s, preferred_element_type=F32) * qdec[:, sl]
            s_scr[h] = s * cdec_ref[h] + jnp.dot(kd[:, sl].T, vh, preferred_element_type=F32)
            outs.append(o * lax.rsqrt(jnp.mean(o * o, -1, keepdims=True) + EPS))
        o_ref[rows, :] = jnp.concatenate(outs, axis=1) * _silu(gate)
    so_ref[...] = s_scr[...]


def _ret_tables(c):
    log_g = jnp.log1p(-jnp.exp2(-5.0 - jnp.arange(R_HEADS, dtype=F32)))
    idx = jnp.arange(c, dtype=F32)
    diff = idx[:, None] - idx[None, :]
    intra = jnp.where(diff >= 0, jnp.exp(log_g[:, None, None] * jnp.maximum(diff, 0.0)), 0.0)
    q_dec = jnp.exp(log_g[None, :] * (idx[:, None] + 1.0))
    k_dec = jnp.exp(log_g[None, :] * (c - 1.0 - idx[:, None]))
    c_dec = jnp.exp(log_g * c)
    return intra, jnp.repeat(q_dec, R_DK, axis=1), jnp.repeat(k_dec, R_DK, axis=1), c_dec


def _ret_prompt(zr):
    l = zr.shape[0]
    c = R_CHUNK
    intra, qdec, kdec, cdec = _ret_tables(c)
    blk = lambda w: pl.BlockSpec((RET_STEP_CHUNKS * c, w), lambda i: (i, 0))
    return pl.pallas_call(
        _ret_prompt_kernel, grid=(l // (RET_STEP_CHUNKS * c),), name="ret_prompt",
        in_specs=[pl.BlockSpec(memory_space=pltpu.SMEM), blk(R_COLS),
                  _full((R_HEADS, c, c)), _full((c, R_HEADS * R_DK)), _full((c, R_HEADS * R_DK))],
        out_specs=[blk(R_WIDTH), _full((R_HEADS, R_DK, R_DV))],
        out_shape=[jax.ShapeDtypeStruct((l, R_WIDTH), F32), jax.ShapeDtypeStruct((R_HEADS, R_DK, R_DV), F32)],
        scratch_shapes=[pltpu.VMEM((R_HEADS, R_DK, R_DV), F32)],
        compiler_params=_cp(("arbitrary",)))(cdec, zr, intra, qdec, kdec)


def _rwkv_prep_kernel(zc_ref, prev_ref, shift0_ref, mu_ref, w0_ref, w2_ref, a0_ref, a2_ref, g2_ref,
                      kk_ref, ka_ref, rk_ref,
                      r_ref, w_ref, k_ref, v_ref, a_ref, b_ref, g_ref, bonus_ref, *scratch, sequence, tm):
    zc = zc_ref[...]
    if sequence:
        row = lax.broadcasted_iota(jnp.int32, zc.shape, 0)
        boundary = jnp.where(pl.program_id(0) == 0, shift0_ref[...], prev_ref[SUBLANES - 1:SUBLANES, :])
        prev = jnp.where(row == 0, jnp.broadcast_to(boundary, zc.shape), pltpu.roll(zc, 1, axis=0))
    else:
        prev = prev_ref[...]
    zs = zc + mu_ref[...] * (prev - zc)
    o1, o2, o3 = C_WIDTH, 2 * C_WIDTH, 3 * C_WIDTH
    r, k, v = zs[:, :o1], zs[:, o1:o2], zs[:, o2:o3]
    lora = zs[:, o3:o3 + LANES]
    gl = zs[:, o3 + LANES:]
    w = -jax.nn.softplus(-(w0_ref[...] + jnp.dot(jnp.tanh(lora), w2_ref[...], preferred_element_type=F32))) - 0.5
    decay = jnp.exp(-jnp.exp(w))
    a = _sigmoid(a0_ref[...] + jnp.dot(lora, a2_ref[...], preferred_element_type=F32))
    g = jnp.dot(_sigmoid(gl), g2_ref[...], preferred_element_type=F32)
    kk = k * kk_ref[...]
    kk = kk / jnp.maximum(jnp.sqrt(_seg_sum(kk * kk)), 1e-12)
    k_mod = k * (1.0 + (a - 1.0) * ka_ref[...])
    v_ref[...] = v
    g_ref[...] = g
    bonus_ref[...] = _seg_sum(r * k_mod * rk_ref[...]) * v
    if not sequence:
        r_ref[...] = r
        w_ref[...] = decay
        k_ref[...] = k_mod
        a_ref[...] = -kk
        b_ref[...] = kk * a
        return
    sub =lax.broadcasted_iota(jnp.int32, decay.shape, 0) % SCAN_SUB
    cp = decay
    shift = 1
    while shift < SCAN_SUB:
        cp = jnp.where(sub >= shift, cp * pltpu.roll(cp, shift, axis=0), cp)
        shift *= 2
    cpx = jnp.where(sub == 0, 1.0, pltpu.roll(cp, 1, axis=0))
    inv = 1.0 / cp
    r_ref[...] = r * cp
    k_ref[...] = k_mod * inv
    a_ref[...] = -kk * cpx
    b_ref[...] = kk * a * inv
    for n, scr in enumerate(scratch):
        scr[...] = cp[:, n * LANES:(n + 1) * LANES]
        w_ref[:, n * LANES:(n + 1) * LANES] = scr[pl.ds(SCAN_SUB - 1, tm // SCAN_SUB, stride=SCAN_SUB), :]


def _rwkv_prep(zc, prev, shift0, cw, sequence, tm):
    t = zc.shape[0]
    row = lambda w: pl.BlockSpec((tm, w), lambda i: (i, 0))
    if sequence:
        per = tm // SUBLANES
        prev_spec = pl.BlockSpec((SUBLANES, C_PAD), lambda i: (jnp.maximum(i * per - 1, 0), 0))
        prev = zc
    else:
        prev_spec = row(C_PAD)
    vec = _full((1, C_WIDTH))
    out_specs = [row(C_WIDTH)] * 8
    out_shape = [jax.ShapeDtypeStruct((t, C_WIDTH), F32)] * 8
    if sequence:
        out_specs[1] = pl.BlockSpec((tm // SCAN_SUB, C_WIDTH), lambda i: (i, 0))
        out_shape[1] = jax.ShapeDtypeStruct((t // SCAN_SUB, C_WIDTH), F32)
    return pl.pallas_call(
        functools.partial(_rwkv_prep_kernel, sequence=sequence, tm=tm), grid=(t // tm,), name="rwkv_prep",
        in_specs=[row(C_PAD), prev_spec, _full((1, C_PAD)), _full((1, C_PAD)), vec, _full((LANES, C_WIDTH)),
                  vec, _full((LANES, C_WIDTH)), _full((2 * LANES, C_WIDTH)), vec, vec, vec],
        out_specs=out_specs, out_shape=out_shape,
        scratch_shapes=[pltpu.VMEM((tm, LANES), F32)] * (C_WIDTH // LANES) if sequence else [],
        compiler_params=_cp(("parallel",)))(zc, prev, shift0, cw["mu"], cw["w0"], cw["w2"], cw["a0"], cw["a2"],
                                           cw["g2"], cw["k_k"], cw["k_a"], cw["r_k"])


SCAN_BLOCK = 1024
SCAN_SUB = 16
SCAN_NSUB = SCAN_BLOCK // SCAN_SUB
SCAN_UNROLL = 16
SCAN_OPS = 3
SCAN_GROUP = 4


def _scan_select():
    sel = np.zeros((SCAN_SUB // SCAN_GROUP, LANES, SCAN_GROUP * LANES), np.float32)
    for p in range(SCAN_SUB // SCAN_GROUP):
        for piece in range(3):
            for rh in range(2):
                for tl in range(SCAN_GROUP):
                    row = piece * 2 * SCAN_SUB + rh * SCAN_SUB + SCAN_GROUP * p + tl
                    sel[p, row, tl * LANES + rh * 64: tl * LANES + (rh + 1) * 64] = 1.0
    return jnp.asarray(sel, BF16)


def _stage_lhs(ops, base, lhs_scr):
    for n, ref in enumerate(ops):
        xs = ref[pl.ds(base, SCAN_SUB), :]
        hi = xs.astype(BF16).astype(F32)
        r1 = xs - hi
        mid = r1.astype(BF16).astype(F32)
        lo = (r1 - mid).astype(BF16).astype(F32)
        g = jnp.concatenate([hi[:, :LANES], hi[:, LANES:], mid[:, :LANES], mid[:, LANES:],
                             lo[:, :LANES], lo[:, LANES:], jnp.zeros((LANES - 6 * SCAN_SUB, LANES), F32)], axis=0)
        lhs_scr[n * LANES:(n + 1) * LANES, :] = g.T.astype(BF16)


def _stage_tiles(lhs_scr, sel_ref, xb_scr, p):
    out = jnp.dot(lhs_scr[...], sel_ref[p], preferred_element_type=F32)
    for n in range(SCAN_OPS):
        for tl in range(SCAN_GROUP):
            xb_scr[n, SCAN_GROUP * p + tl] = out[n * LANES:(n + 1) * LANES, tl * LANES:(tl + 1) * LANES]


def _rwkv_scan_kernel(r_ref, pc_ref, k_ref, v_ref, a_ref, b_ref, g_ref, bonus_ref, lnw_ref, lnb_ref, sel_ref,
                      o_ref, so_ref, st_scr, lhs_a, lhs_b, xb_a, xb_b, vt_scr, vb_scr, yt_scr, yb_scr):
    @pl.when(pl.program_id(0) == 0)
    def _():
        st_scr[...] = jnp.zeros_like(st_scr)

    vtop, vbot = _rowpair(v_ref[...])
    vt_scr[...] = vtop
    vb_scr[...] = vbot
    ops = (a_ref, b_ref, k_ref)
    pairs = SCAN_UNROLL // SCAN_GROUP
    _stage_lhs(ops, 0, lhs_a)
    for p in range(SCAN_SUB // SCAN_GROUP):
        _stage_tiles(lhs_a, sel_ref, xb_a, p)
    _stage_lhs(ops, SCAN_SUB, lhs_b)

    def half(sc, st, lhs_cur, xb_cur, lhs_nxt, xb_nxt):
        base = pl.multiple_of(sc * SCAN_SUB, SCAN_SUB)

        def quad(q, st):
            for pp in range(pairs):
                _stage_tiles(lhs_nxt, sel_ref, xb_nxt, q * pairs + pp)
            for uu in range(SCAN_UNROLL):
                u = q * SCAN_UNROLL + uu
                t = base + u
                sa_t, sa_b = _halfsums(st * xb_cur[0, u])
                st = (st + _rowbcast(sa_t, sa_b) * xb_cur[1, u]
                      + _rowbcast(vt_scr[pl.ds(t, 1), :], vb_scr[pl.ds(t, 1), :]) * xb_cur[2, u])
                y_t, y_b = _halfsums(st * _colbcast(r_ref[pl.ds(t, 1), :]))
                yt_scr[pl.ds(t, 1), :] = y_t
                yb_scr[pl.ds(t, 1), :] = y_b
            return st

        st = lax.fori_loop(0, SCAN_SUB // SCAN_UNROLL, quad, st) * _colbcast(pc_ref[pl.ds(sc, 1), :])
        nxt2 = pl.multiple_of(jnp.minimum(sc + 2, SCAN_NSUB - 1) * SCAN_SUB, SCAN_SUB)
        _stage_lhs(ops, nxt2, lhs_cur)
        return st

    def sub2(s2, st):
        st = half(2 * s2, st, lhs_a, xb_a, lhs_b, xb_b)
        return half(2 * s2 + 1, st, lhs_b, xb_b, lhs_a, xb_a)

    st = lax.fori_loop(0, SCAN_NSUB // 2, sub2, st_scr[...])
    st_scr[...] = st
    y = _unpair(yt_scr[...], yb_scr[...])
    o_ref[...] = _wkv_post(y, lnw_ref[...], lnb_ref[...], bonus_ref[...], g_ref[...])
    heads = _t_to_heads(st, transpose=True)
    for h in range(C_HEADS):
        so_ref[h] = heads[h]


def _rwkv_scan(prep, lnw, lnb):
    r, pc, k, v, a, b, g, bonus = prep
    l = r.shape[0]
    blk = pl.BlockSpec((SCAN_BLOCK, C_WIDTH), lambda i: (i, 0))
    vec = _full((1, C_WIDTH))
    sel = _scan_select()
    return pl.pallas_call(
        _rwkv_scan_kernel, grid=(l // SCAN_BLOCK,), name="rwkv_scan",
        in_specs=[blk, pl.BlockSpec((SCAN_NSUB, C_WIDTH), lambda i: (i, 0))] + [blk] * 6 + [vec, vec, _full(sel.shape)],
        out_specs=[blk, _full((C_HEADS, C_HEAD, C_HEAD))],
        out_shape=[jax.ShapeDtypeStruct((l, C_WIDTH), F32), jax.ShapeDtypeStruct((C_HEADS, C_HEAD, C_HEAD), F32)],
        scratch_shapes=[pltpu.VMEM((LANES, LANES), F32),
                        pltpu.VMEM((SCAN_OPS * LANES, LANES), BF16), pltpu.VMEM((SCAN_OPS * LANES, LANES), BF16),
                        pltpu.VMEM((SCAN_OPS, SCAN_SUB, LANES, LANES), F32),
                        pltpu.VMEM((SCAN_OPS, SCAN_SUB, LANES, LANES), F32),
                        pltpu.VMEM((SCAN_BLOCK, LANES), F32), pltpu.VMEM((SCAN_BLOCK, LANES), F32),
                        pltpu.VMEM((SCAN_BLOCK, LANES), F32), pltpu.VMEM((SCAN_BLOCK, LANES), F32)],
        compiler_params=_cp(("arbitrary",)))(r, pc, k, v, a, b, g, bonus, lnw, lnb, sel)


SAMPLE_STEP = 4


def _sample_kernel(*refs):
    shared_in, per_seq_in, shared_tail, outs = refs[:2], refs[2:16], refs[16:18], refs[18:]
    for bi in range(SAMPLE_STEP):
        one = lambda r: r.at[pl.ds(bi, 1)]
        _sample_one(*shared_in, *map(one, per_seq_in), *shared_tail, *map(one, outs))


def _sample_one(sinks_ref, gam_ref, za_ref, zr_ref, r_ref, w_ref, k_ref, v_ref, a_ref, b_ref, g_ref, bonus_ref,
                   kc_ref, vc_ref, sret_ref, swkv_ref, lnw_ref, lnb_ref,
                   oa_ref, or_ref, oc_ref, kco_ref, vco_ref, sreto_ref, swkvo_ref):
    lane =lax.broadcasted_iota(jnp.int32, (1, LANES), 1)
    row8 = lax.broadcasted_iota(jnp.int32, (SUBLANES, LANES), 0)

    za = za_ref[0]
    q = za[:, :A_WIDTH]
    knew = za[:, A_WIDTH:A_WIDTH + A_KV_WIDTH]
    vnew = za[:, A_WIDTH + A_KV_WIDTH:]
    kc, vc = kc_ref[0], vc_ref[0]
    g = A_HEADS // A_KV_HEADS
    zero = jnp.zeros((1, HEAD_DIM), F32)
    qrows = []
    for h in range(A_HEADS):
        qh = q[:, h * HEAD_DIM:(h + 1) * HEAD_DIM]
        qrows.append(jnp.concatenate([qh, zero] if h // g == 0 else [zero, qh], axis=1))
    qm = jnp.concatenate(qrows + [jnp.zeros((LANES - A_HEADS, LANES), F32)], axis=0)
    scale = HEAD_DIM ** -0.5
    s = lax.dot_general(kc, qm, (((1,), (1,)), ((), ())), preferred_element_type=F32) * scale
    kn8 = jnp.where(row8 == 0, jnp.broadcast_to(knew, (SUBLANES, LANES)), 0.0)
    s_new = lax.dot_general(kn8, qm, (((1,), (1,)), ((), ())), preferred_element_type=F32)[0:1] * scale
    sink = jnp.zeros((1, LANES), F32)
    for h in range(A_HEADS):
        sink = jnp.where(lane == h, sinks_ref[h], sink)
    m = jnp.maximum(jnp.maximum(jnp.max(s, axis=0, keepdims=True), s_new), sink)
    p = jnp.exp(s - m)
    p_new = jnp.exp(s_new - m)
    denom = jnp.sum(p, axis=0, keepdims=True) + p_new + jnp.exp(sink - m)
    p = p / denom
    p_new = p_new / denom
    pn_col = jnp.broadcast_to(p_new, (LANES, LANES)).T[:, 0:1]
    o_full = jnp.dot(p.T, vc, preferred_element_type=F32) + pn_col * vnew
    oa_ref[0] = jnp.concatenate(
        [o_full[h:h + 1, (h // g) * HEAD_DIM:(h // g + 1) * HEAD_DIM] for h in range(A_HEADS)], axis=1)
    rowk = lax.broadcasted_iota(jnp.int32, kc.shape, 0)
    last = rowk == kc.shape[0] - 1
    kco_ref[0] = jnp.where(last, jnp.broadcast_to(knew, kc.shape), pltpu.roll(kc, kc.shape[0] - 1, axis=0))
    vco_ref[0] = jnp.where(last, jnp.broadcast_to(vnew, vc.shape), pltpu.roll(vc, vc.shape[0] - 1, axis=0))

    zr = zr_ref[0]
    qk = R_HEADS * R_DK
    rq = zr[:, :qk]
    rk = zr[:, qk:2 * qk] * (R_DK ** -0.5)
    rv = zr[:, 2 * qk:2 * qk + R_WIDTH]
    rg = zr[:, 2 * qk + R_WIDTH:]
    st = _heads_to_t(sret_ref[0], transpose=False)
    rr = lax.broadcasted_iota(jnp.int32, (LANES, LANES), 0) >= 64
    cc = lax.broadcasted_iota(jnp.int32, (LANES, LANES), 1) >= 64
    gamma = jnp.where(rr, jnp.where(cc, gam_ref[3], gam_ref[1]), jnp.where(cc, gam_ref[2], gam_ref[0]))
    vt, vb = _rowpair(rv)
    st = st * gamma + _colbcast(rk) * _rowbcast(vt, vb)
    o_t, o_b = _halfsums(st * _colbcast(rq))
    o = _unpair(o_t, o_b)
    o = o * lax.rsqrt(_seg_sum(o * o) * (1.0 / R_DV) + EPS)
    or_ref[0] = o * _silu(rg)
    heads = _t_to_heads(st, transpose=False)
    for h in range(R_HEADS):
        sreto_ref[0, h] = heads[h]

    st = _heads_to_t(swkv_ref[0], transpose=True)
    vt, vb = _rowpair(v_ref[0])
    st, y_t, y_b = _wkv_step(st, _colbcast(a_ref[0]), _colbcast(w_ref[0]), _colbcast(b_ref[0]),
                             _colbcast(k_ref[0]), _colbcast(r_ref[0]), vt, vb)
    oc_ref[0] = _wkv_post(_unpair(y_t, y_b), lnw_ref[...], lnb_ref[...], bonus_ref[0], g_ref[0])
    heads = _t_to_heads(st, transpose=True)
    for h in range(C_HEADS):
        swkvo_ref[0, h] = heads[h]


def _sample_mixers(za, zr, prep, kc, vc, sret, swkv, sinks, lnw, lnb):
    b = za.shape[0]
    wb = kc.shape[1]
    assert wb <= WINDOW and PAST_LEN >= wb
    gam = jnp.exp(jnp.log1p(-jnp.exp2(-5.0 - jnp.arange(R_HEADS, dtype=F32))) * 1.0)
    tok = lambda w: pl.BlockSpec((SAMPLE_STEP, 1, w), lambda i: (i, 0, 0))
    cache = pl.BlockSpec((SAMPLE_STEP, wb, A_KV_WIDTH), lambda i: (i, 0, 0))
    state = pl.BlockSpec((SAMPLE_STEP, 4, 64, 64), lambda i: (i, 0, 0, 0))
    smem = pl.BlockSpec(memory_space=pltpu.SMEM)
    vec = _full((1, C_WIDTH))
    r3 = lambda x: x.reshape(b, 1, x.shape[-1])
    outs = pl.pallas_call(
        _sample_kernel, grid=(b // SAMPLE_STEP,), name="sample_mixers",
        in_specs=[smem, smem, tok(A_COLS), tok(R_COLS)] + [tok(C_WIDTH)] * 8 + [cache, cache, state, state,
                  vec, vec],
        out_specs=[tok(A_WIDTH), tok(R_WIDTH), tok(C_WIDTH), cache, cache, state, state],
        out_shape=[jax.ShapeDtypeStruct((b, 1, A_WIDTH), F32), jax.ShapeDtypeStruct((b, 1, R_WIDTH), F32),
                   jax.ShapeDtypeStruct((b, 1, C_WIDTH), F32), jax.ShapeDtypeStruct(kc.shape, F32),
                   jax.ShapeDtypeStruct(vc.shape, F32), jax.ShapeDtypeStruct(sret.shape, F32),
                   jax.ShapeDtypeStruct(swkv.shape, F32)],
        compiler_params=_cp(("parallel",)))(sinks, gam, r3(za), r3(zr), *[r3(x) for x in prep],
                                           kc, vc, sret, swkv, lnw, lnb)
    oa, orr, oc = (x.reshape(b, x.shape[-1]) for x in outs[:3])
    return (oa, orr, oc) + tuple(outs[3:])


def _outproj_kernel(oa_ref, or_ref, oc_ref, x_ref, wo_ref, gf_ref, wrt_ref, brt_ref, *rest, tm):
    x1_ref, h2_ref, route_ref = rest[-3:]
    mix =jnp.concatenate([oa_ref[...], or_ref[...], oc_ref[...]], axis=1).astype(BF16)
    x1 = x_ref[...] + jnp.dot(mix, wo_ref[...], preferred_element_type=F32)
    x1_ref[...] = x1
    h2 = _rmsnorm(x1, gf_ref[...])
    for s in range(ROW_TILES):
        h2_ref[pl.ds(s, tm, stride=ROW_TILES), :] = h2[:, s * LANES:(s + 1) * LANES]
    h_hi = h2.astype(BF16)
    h_lo = (h2 - h_hi.astype(F32)).astype(BF16)
    w_hi, w_lo = wrt_ref[0], wrt_ref[1]
    logits = (jnp.dot(h_hi, w_hi, preferred_element_type=F32) + jnp.dot(h_lo, w_hi, preferred_element_type=F32)
              + jnp.dot(h_hi, w_lo, preferred_element_type=F32)) + brt_ref[...]
    lane = lax.broadcasted_iota(jnp.int32, logits.shape, 1).astype(F32)
    big = float(LANES)
    is_g = lane < N_GROUPS
    lg = jnp.where(is_g, logits, -jnp.inf)
    m_g = jnp.max(lg, -1, keepdims=True)
    grp = jnp.min(jnp.where(lg == m_g, lane, big), -1, keepdims=True)
    p_grp = 1.0 / jnp.sum(jnp.where(is_g, jnp.exp(lg - m_g), 0.0), -1, keepdims=True)
    lo = N_GROUPS + EXPERTS_PER_GROUP * grp
    in_grp = (lane >= lo) & (lane < lo + EXPERTS_PER_GROUP)
    le = jnp.where(in_grp, logits, -jnp.inf)
    l1 = jnp.max(le, -1, keepdims=True)
    i1 = jnp.min(jnp.where(le == l1, lane, big), -1, keepdims=True)
    le2 = jnp.where(lane == i1, -jnp.inf, le)
    l2 = jnp.max(le2, -1, keepdims=True)
    i2 = jnp.min(jnp.where(le2 == l2, lane, big), -1, keepdims=True)
    e = jnp.exp(l2 - l1)
    g1 = p_grp / (1.0 + e)
    g2 = p_grp * e / (1.0 + e)
    route = jnp.where(lane == 0, i1 - N_GROUPS, jnp.where(lane == 1, i2 - N_GROUPS,
                      jnp.where(lane == 2, g1, jnp.where(lane == 3, g2, 0.0))))
    route_ref[...] = route


def _outproj(oa, orr, oc, x, wo, gf, wrt, brt, tm, t_all, first_tok, shared=None):
    t = x.shape[0]
    off = first_tok // tm
    nblk = t // tm
    steps = nblk + (1 if shared is None else 0)
    assert shared is not None or t_all == t + tm
    row = lambda w: pl.BlockSpec((tm, w), lambda i: (jnp.minimum(i, nblk - 1), 0))
    in_specs = [row(A_WIDTH), row(R_WIDTH), row(C_WIDTH), row(D_MODEL), _full(wo.shape), _full((1, D_MODEL)),
                _full(wrt.shape), _full((1, LANES))]
    args = [oa, orr, oc, x, wo, gf, wrt, brt]
    aliases = {}
    if shared is not None:
        aliases = {len(args): 1, len(args) + 1: 2}
        in_specs += [pl.BlockSpec(memory_space=pl.ANY)] * 2
        args += list(shared)
    return pl.pallas_call(
        functools.partial(_outproj_kernel, tm=tm), grid=(steps,), name="outproj_router",
        in_specs=in_specs,
        out_specs=[row(D_MODEL), pl.BlockSpec((tm * ROW_TILES, LANES), lambda i: (i + off, 0)),
                   pl.BlockSpec((tm, LANES), lambda i: (i + off, 0))],
        out_shape=[jax.ShapeDtypeStruct((t, D_MODEL), F32), jax.ShapeDtypeStruct((t_all * ROW_TILES, LANES), F32),
                   jax.ShapeDtypeStruct((t_all, LANES), F32)],
        input_output_aliases=aliases,
        compiler_params=_cp(("arbitrary",)))(*args)


def _moe_kernel(blk_exp_ref, src_ref, dst_ref, grp_ref, nused_ref, h2_hbm, wg_ref, wu_ref, wd_ref, yw_hbm,
                gbuf, sbuf, xbuf, wgb, wub, wdb, gsem, ssem):
    i = pl.program_id(0)
    nb = pl.num_programs(0)
    nused = nused_ref[0]
    slot = i % 2

    def tile(off):
        return pl.ds(pl.multiple_of(off, ROW_TILES), ROW_TILES)

    def rows(blk, fn):
        def body(g, _):
            for j in range(DMA_UNROLL):
                fn(g * DMA_UNROLL + j)
            return 0
        lax.fori_loop(0, grp_ref[blk], body, 0)

    def gather(blk, sl):
        rows(blk, lambda r: pltpu.make_async_copy(h2_hbm.at[tile(src_ref[blk * MOE_ROWS + r]), :],
                                                  gbuf.at[sl, tile(r * ROW_TILES), :], gsem.at[sl]).start())

    def gather_wait(blk, sl):
        rows(blk, lambda r: pltpu.make_async_copy(h2_hbm.at[tile(0), :], gbuf.at[sl, tile(0), :],
                                                  gsem.at[sl]).wait())

    def scatter(blk):
        rows(blk, lambda r: pltpu.make_async_copy(sbuf.at[tile(r * ROW_TILES), :],
                                                  yw_hbm.at[tile(dst_ref[blk * MOE_ROWS + r]), :], ssem).start())

    def scatter_wait(blk):
        rows(blk, lambda r: pltpu.make_async_copy(sbuf.at[tile(0), :], yw_hbm.at[tile(0), :], ssem).wait())

    @pl.when((i == 0) & (nused > 0))
    def _():
        gbuf[...] = jnp.zeros_like(gbuf)
        gather(0, 0)

    @pl.when(i < nused)
    def _():
        @pl.when((i == 0) | (blk_exp_ref[i] != blk_exp_ref[jnp.maximum(i - 1, 0)]))
        def _():
            wgb[...] = wg_ref[0].astype(BF16)
            wub[...] = wu_ref[0].astype(BF16)
            wdb[...] = wd_ref[0].astype(BF16)

        gather_wait(i, slot)

        @pl.when(i + 1 < nused)
        def _():
            gather(i + 1, 1 - slot)

        for s in range(ROW_TILES):
            xbuf[:, s * LANES:(s + 1) * LANES] = gbuf[slot, pl.ds(s, MOE_ROWS, stride=ROW_TILES), :].astype(BF16)
        x = xbuf[...]
        hid = _silu(jnp.dot(x, wgb[...], preferred_element_type=F32)) * jnp.dot(x, wub[...], preferred_element_type=F32)
        y = jnp.dot(hid.astype(BF16), wdb[...], preferred_element_type=F32)

        @pl.when(i > 0)
        def _():
            scatter_wait(i - 1)

        for s in range(ROW_TILES):
            sbuf[pl.ds(s, MOE_ROWS, stride=ROW_TILES), :] = y[:, s * LANES:(s + 1) * LANES]
        scatter(i)

    @pl.when((i == nb - 1) & (nused > 0))
    def _():
        scatter_wait(nused - 1)
        spare = pltpu.make_async_copy(sbuf, yw_hbm.at[pl.ds(yw_hbm.shape[0] - MOE_ROWS * ROW_TILES,
                                                            MOE_ROWS * ROW_TILES), :], ssem)
        spare.start()
        spare.wait()


def _moe(h2, route, t, wg, wu, wd):
    na = t * TOP_K
    assert na < (1 << IDX_BITS)
    flat_e = route[:t, :TOP_K].astype(jnp.int32).reshape(na)
    experts = jnp.arange(N_EXPERTS, dtype=jnp.int32)
    counts = jnp.sum(flat_e[:, None] == experts[None, :], axis=0, dtype=jnp.int32)
    npad = (-counts) % MOE_ROWS
    real_keys = (flat_e << (IDX_BITS + 1)) | jnp.arange(na, dtype=jnp.int32)
    q = jnp.arange(MOE_ROWS, dtype=jnp.int32)[None, :]
    pad_keys = jnp.where(q < npad[:, None], (experts[:, None] << (IDX_BITS + 1)) | (1 << IDX_BITS) | q,
                         jnp.iinfo(jnp.int32).max)
    n_blocks = -(-na // MOE_ROWS) + N_EXPERTS
    n_rows = n_blocks * MOE_ROWS
    filler = jnp.full((n_rows - na - N_EXPERTS * MOE_ROWS,), jnp.iinfo(jnp.int32).max, jnp.int32)
    keys = jnp.sort(jnp.concatenate([real_keys, pad_keys.reshape(-1), filler]))
    is_pad = ((keys >> IDX_BITS) & 1) == 1
    idx = keys & ((1 << IDX_BITS) - 1)
    pos = jnp.arange(n_rows, dtype=jnp.int32)
    src = jnp.where(is_pad, 0, (idx // TOP_K) * ROW_TILES).astype(jnp.int32)
    dst = jnp.where(is_pad, na + pos % MOE_ROWS, idx).astype(jnp.int32) * ROW_TILES
    blk_exp = jnp.minimum(keys[::MOE_ROWS] >> (IDX_BITS + 1), N_EXPERTS - 1).astype(jnp.int32)
    nused = (jnp.sum(counts + npad) // MOE_ROWS).astype(jnp.int32).reshape(1)
    real_rows = jnp.sum((~is_pad).reshape(n_blocks, MOE_ROWS), axis=1, dtype=jnp.int32)
    grp = (real_rows + DMA_UNROLL - 1) // DMA_UNROLL
    wspec = lambda a, b: pl.BlockSpec((1, a, b), lambda i, be, s, d, gr, nu: (be[i], 0, 0))
    return pl.pallas_call(
        _moe_kernel, name="moe_experts",
        grid_spec=pltpu.PrefetchScalarGridSpec(
            num_scalar_prefetch=5, grid=(n_blocks,),
            in_specs=[pl.BlockSpec(memory_space=pl.ANY), wspec(D_MODEL, EXPERT_FF), wspec(D_MODEL, EXPERT_FF),
                      wspec(EXPERT_FF, D_MODEL)],
            out_specs=pl.BlockSpec(memory_space=pl.ANY),
            scratch_shapes=[pltpu.VMEM((2, MOE_ROWS * ROW_TILES, LANES), F32),
                            pltpu.VMEM((MOE_ROWS * ROW_TILES, LANES), F32),
                            pltpu.VMEM((MOE_ROWS, D_MODEL), BF16),
                            pltpu.VMEM((D_MODEL, EXPERT_FF), BF16), pltpu.VMEM((D_MODEL, EXPERT_FF), BF16),
                            pltpu.VMEM((EXPERT_FF, D_MODEL), BF16),
                            pltpu.SemaphoreType.DMA((2,)), pltpu.SemaphoreType.DMA(())]),
        out_shape=jax.ShapeDtypeStruct(((na + MOE_ROWS) * ROW_TILES, LANES), F32),
        compiler_params=_cp(("arbitrary",)))(blk_exp, src, dst, grp, nused, h2, wg, wu, wd)


def _rope_tables(pos):
    half = HEAD_DIM // 2
    inv = ROPE_THETA ** (-jnp.arange(half, dtype=F32) / half)
    ang = pos.astype(F32)[:, None] * inv[None, :]
    c, s = jnp.cos(ang), jnp.sin(ang)
    return jnp.concatenate([c, c, c, c], axis=1), jnp.concatenate([-s, s, -s, s], axis=1)


def _pad_cols(w, n):
    return jnp.pad(w, ((0, 0), (0, n - w.shape[1])))


def _pad_rows(w, lo, n):
    return jnp.pad(w, ((lo, n - lo - w.shape[0]), (0, 0)))


def kernel(x_prompt, x_sample, cache_swa_k, cache_swa_v, state_ret, state_wkv, state_shift, norm_mix, w_in, sinks,
           rwkv_mu, rwkv_w0, rwkv_w2, rwkv_a0, rwkv_a2, rwkv_g2, rwkv_k_k, rwkv_k_a, rwkv_r_k, rwkv_ln_w, rwkv_ln_b,
           w_out, norm_ffn, router_g, router_g_b, router_e, router_e_b, expert_w_gate, expert_w_up, expert_w_down,
           norm_final):
    lp = x_prompt.shape[1]
    bs = x_sample.shape[0]
    wb = cache_swa_k.shape[2]
    tm_p = 512
    xp = x_prompt.reshape(lp, D_MODEL)
    xs = x_sample.reshape(bs, D_MODEL)
    cos_p, sin_p = _rope_tables(jnp.arange(lp, dtype=jnp.int32))
    cos_s, sin_s = (jnp.broadcast_to(t, (bs, LANES))
                    for t in _rope_tables(PAST_LEN + jnp.arange(x_sample.shape[1], dtype=jnp.int32)))
    outs_p = [[] for _ in range(5)]
    outs_s = [[] for _ in range(5)]
    moe_p = moe_s = None
    row = lambda v: v.reshape(1, -1)
    for i in range(DEPTH):
        wi = w_in[i].astype(BF16)
        wa, wr, wc = wi[:, :A_COLS], wi[:, A_COLS:A_COLS + R_COLS], _pad_cols(wi[:, A_COLS + R_COLS:], C_PAD)
        cw = {
            "mu": _pad_cols(row(rwkv_mu[i]), C_PAD), "w0": row(rwkv_w0[i]), "a0": row(rwkv_a0[i]),
            "w2": _pad_rows(rwkv_w2[i], 0, LANES), "a2": _pad_rows(rwkv_a2[i], C_DECAY_LORA, LANES),
            "g2": _pad_rows(rwkv_g2[i], 0, 2 * LANES),
            "k_k": row(rwkv_k_k[i]), "k_a": row(rwkv_k_a[i]), "r_k": row(rwkv_r_k[i]),
        }
        lnw, lnb = row(rwkv_ln_w[i]), row(rwkv_ln_b[i])
        g_mix, g_ffn = row(norm_mix[i]), row(norm_ffn[i])
        wo = w_out[i].astype(BF16)
        wrt = _pad_cols(jnp.concatenate([router_g[i], router_e[i]], axis=1), LANES)
        wrt_hi = wrt.astype(BF16)
        wrt = jnp.stack([wrt_hi, (wrt - wrt_hi.astype(F32)).astype(BF16)])
        brt = _pad_cols(row(jnp.concatenate([router_g_b[i], router_e_b[i]])), LANES)

        xn, za, zr, zc = _inproj(xp, moe_p, g_mix, wa, wr, wc, cos_p, sin_p, tm_p)
        xp = xp if xn is None else xn
        oa = _swa_prompt(za, sinks[i])
        orr, s_ret = _ret_prompt(zr)
        prep = _rwkv_prep(zc, None, jnp.zeros((1, C_PAD), F32), cw, True, tm_p)
        oc, s_wkv = _rwkv_scan(prep, lnw, lnb)
        xp, h2, route = _outproj(oa, orr, oc, xp, wo, g_ffn, wrt, brt, tm_p, lp + tm_p, 0)
        nk = min(WINDOW, lp)
        outs_p[0].append(za[lp - nk:, A_WIDTH:A_WIDTH + A_KV_WIDTH].reshape(1, nk, A_KV_HEADS, HEAD_DIM))
        outs_p[1].append(za[lp - nk:, A_WIDTH + A_KV_WIDTH:].reshape(1, nk, A_KV_HEADS, HEAD_DIM))
        outs_p[2].append(s_ret[None])
        outs_p[3].append(s_wkv[None])
        outs_p[4].append(zc[lp - 1:, :C_COLS])

        xn, za, zr, zc = _inproj(xs, moe_s, g_mix, wa, wr, wc, cos_s, sin_s, bs)
        xs = xs if xn is None else xn
        prep = _rwkv_prep(zc, _pad_cols(state_shift[i], C_PAD), jnp.zeros((1, C_PAD), F32), cw, False, bs)
        oa, orr, oc, kc_new, vc_new, sret_new, swkv_new = _sample_mixers(
            za, zr, prep, cache_swa_k[i].reshape(bs, wb, A_KV_WIDTH), cache_swa_v[i].reshape(bs, wb, A_KV_WIDTH),
            state_ret[i], state_wkv[i], sinks[i], lnw, lnb)
        xs, h2, route = _outproj(oa, orr, oc, xs, wo, g_ffn, wrt, brt, bs, lp + tm_p, lp, shared=(h2, route))
        yw = _moe(h2, route, lp + bs, expert_w_gate[i], expert_w_up[i], expert_w_down[i])
        moe_p, moe_s = (yw, route, 0), (yw, route, lp)
        outs_s[0].append(kc_new.reshape(bs, wb, A_KV_HEADS, HEAD_DIM))
        outs_s[1].append(vc_new.reshape(bs, wb, A_KV_HEADS, HEAD_DIM))
        outs_s[2].append(sret_new)
        outs_s[3].append(swkv_new)
        outs_s[4].append(zc[:, :C_COLS])

    gfin = row(norm_final)
    y_prompt = _final(xp, moe_p, gfin, tm_p).reshape(x_prompt.shape)
    y_sample = _final(xs, moe_s, gfin, bs).reshape(x_sample.shape)
    sp = [jnp.stack(o) for o in outs_p]
    ss = [jnp.stack(o) for o in outs_s]
    return (y_prompt, y_sample, sp[0], sp[1], sp[2], sp[3], sp[4], ss[0], ss[1], ss[2], ss[3], ss[4])
```
